```python
import jax, jax.numpy as jnp
from jax import lax
import numpy as np

D_MODEL = 1024
BATCH = 2
SEQ = 8192
DEPTH = 1
DEC_BATCH = 128
DEC_SEQ = 4
PAST_LEN = 16384
PAGE_SIZE = 128

A_WIDTH = D_MODEL // 2
A_GROUPS = 8
A_GROUP_DIM = A_WIDTH // A_GROUPS
CHUNK = 128
N_HEADS = 8
QK_NOPE = 64
QK_ROPE = 32
V_HEAD = 64
Q_LORA = 384
KV_LORA = 256
ROPE_THETA = 10000.0
Q_BLOCK = 128
ATTN_SCALE = (QK_NOPE + QK_ROPE) ** -0.5
N_BRANCH = 2
N_EXPERTS = 256
TOP_K = 8
N_GROUPS = 8
TOPK_GROUPS = 4
EXPERT_DIM = 256
SHARED_DIM = 256
ROUTED_SCALE = 2.5
EXPERT_BLOCK = 64
EPS = 1e-6

IN_OFFSETS = (A_WIDTH, 2 * A_WIDTH, 2 * A_WIDTH + Q_LORA, 2 * A_WIDTH + Q_LORA + KV_LORA,
              2 * A_WIDTH + Q_LORA + KV_LORA + QK_ROPE)
N_IN = 2 * A_WIDTH + Q_LORA + KV_LORA + QK_ROPE + N_BRANCH * D_MODEL

kernel_name = 'hybrid_gmlp_mla_moe_step'


def rmsnorm(x, g):
    xf = x.astype(jnp.float32)
    y = xf * lax.rsqrt(jnp.mean(xf * xf, axis=-1, keepdims=True) + EPS)
    return (y * g.astype(jnp.float32)).astype(x.dtype)


def rope(x, pos):
    half = QK_ROPE // 2
    freqs = ROPE_THETA ** (-jnp.arange(half, dtype=jnp.float32) / half)
    ang = pos.astype(jnp.float32)[:, None] * freqs
    ang = ang.reshape(ang.shape[:1] + (1,) * (x.ndim - 3) + (half,))
    cos, sin = jnp.cos(ang).astype(x.dtype), jnp.sin(ang).astype(x.dtype)
    x1, x2 = x[..., :half], x[..., half:]
    return jnp.concatenate([x1 * cos - x2 * sin, x1 * sin + x2 * cos], axis=-1)


def chunk_mlp(u, v, w_s, b_s):
    b, s, _ = v.shape
    n_chunks = -(-s // CHUNK)
    pad = n_chunks * CHUNK - s
    vp = jnp.pad(v, ((0, 0), (0, pad), (0, 0))).reshape(b, n_chunks, CHUNK, A_GROUPS, A_GROUP_DIM)
    causal = jnp.tril(jnp.ones((CHUNK, CHUNK), dtype=bool))
    w = jnp.where(causal, w_s, 0)
    sp = jnp.einsum('gts,bnsgc->bntgc', w, vp) + jnp.transpose(b_s)[None, None, :, :, None]
    sp = sp.reshape(b, n_chunks * CHUNK, A_WIDTH)[:, :s]
    return u * sp


def mla_prompt(q_nope, q_rope, ckv, kr, w_uk, w_uv):
    k_nope = jnp.einsum('bkr,rhd->bkhd', ckv, w_uk)
    val = jnp.einsum('bkr,rhd->bkhd', ckv, w_uv)
    outs = []
    for i in range(q_nope.shape[1] // Q_BLOCK):
        lo, hi = i * Q_BLOCK, (i + 1) * Q_BLOCK
        s = (jnp.einsum('bqhd,bkhd->bhqk', q_nope[:, lo:hi], k_nope[:, :hi])
             + jnp.einsum('bqhd,bkd->bhqk', q_rope[:, lo:hi], kr[:, :hi])).astype(jnp.float32) * ATTN_SCALE
        causal = (lo + jnp.arange(Q_BLOCK))[:, None] >= jnp.arange(hi)[None, :]
        p = jax.nn.softmax(jnp.where(causal, s, -jnp.inf), axis=-1).astype(val.dtype)
        outs.append(jnp.einsum('bhqk,bkhd->bqhd', p, val[:, :hi]))
    return jnp.concatenate(outs, axis=1)


def mla_sample(q_nope, q_rope, ckv_new, kr_new, ckv_past, kr_past, w_uk, w_uv):
    n_new, n_past = q_nope.shape[1], ckv_past.shape[1]
    q_lat = jnp.einsum('bqhd,rhd->bqhr', q_nope, w_uk)
    s_past = (jnp.einsum('bqhr,bkr->bhqk', q_lat, ckv_past)
              + jnp.einsum('bqhd,bkd->bhqk', q_rope, kr_past)).astype(jnp.float32) * ATTN_SCALE
    s_new = (jnp.einsum('bqhr,bkr->bhqk', q_lat, ckv_new)
             + jnp.einsum('bqhd,bkd->bhqk', q_rope, kr_new)).astype(jnp.float32) * ATTN_SCALE
    causal = jnp.tril(jnp.ones((n_new, n_new), dtype=bool))
    s = jnp.concatenate([s_past, jnp.where(causal, s_new, -jnp.inf)], axis=-1)
    p = jax.nn.softmax(s, axis=-1).astype(ckv_new.dtype)
    o_lat = (jnp.einsum('bhqk,bkr->bqhr', p[..., :n_past], ckv_past)
             + jnp.einsum('bhqk,bkr->bqhr', p[..., n_past:], ckv_new))
    return jnp.einsum('bqhr,rhd->bqhd', o_lat, w_uv)


def token_mixer(h, p, pos, past):
    b, s, _ = h.shape
    u, v, cq, ckv, kr, gates = jnp.split(h @ p['w_in'], IN_OFFSETS, axis=-1)
    u = jax.nn.gelu(u)
    v = rmsnorm(jax.nn.gelu(v), p['a_vnorm_g'])
    y_a = chunk_mlp(u, v, p['w_spatial'], p['b_spatial']) @ p['w_out_a']
    q = jnp.einsum('bsr,rhd->bshd', rmsnorm(cq, p['q_norm_g']), p['w_uq'])
    q_nope, q_rope = q[..., :QK_NOPE], rope(q[..., QK_NOPE:], pos)
    ckv = rmsnorm(ckv, p['kv_norm_g'])
    kr = rope(kr, pos)
    if past is None:
        o = mla_prompt(q_nope, q_rope, ckv, kr, p['w_uk'], p['w_uv'])
    else:
        o = mla_sample(q_nope, q_rope, ckv, kr, past[0], past[1], p['w_uk'], p['w_uv'])
    y_b = o.reshape(b, s, N_HEADS * V_HEAD) @ p['w_out_b']
    g_a, g_b = jnp.split(jax.nn.sigmoid(gates), N_BRANCH, axis=-1)
    y = (g_a * y_a + g_b * y_b) @ p['w_o']
    return y, (ckv, kr, v)


def route(xf, w_router, router_bias):
    t = xf.shape[0]
    scores = jax.nn.sigmoid((xf @ w_router).astype(jnp.float32))
    sel = scores + router_bias.astype(jnp.float32)
    grp = sel.reshape(t, N_GROUPS, N_EXPERTS // N_GROUPS)
    grp_score = lax.top_k(grp, 2)[0].sum(-1)
    _, top_g = lax.top_k(grp_score, TOPK_GROUPS)
    g_mask = jax.nn.one_hot(top_g, N_GROUPS, dtype=jnp.float32).sum(-2) > 0
    e_mask = jnp.repeat(g_mask, N_EXPERTS // N_GROUPS, axis=-1)
    _, idx = lax.top_k(jnp.where(e_mask, sel, -jnp.inf), TOP_K)
    w = jnp.take_along_axis(scores, idx, axis=-1)
    w = w / jnp.sum(w, axis=-1, keepdims=True) * ROUTED_SCALE
    return idx, w


def moe(h, p):
    shp = h.shape
    xf = h.reshape(-1, D_MODEL)
    t = xf.shape[0]
    idx, wts = route(xf, p['w_router'], p['router_bias'])
    n_assign = t * TOP_K
    flat_e = idx.reshape(-1)
    order = jnp.argsort(flat_e)
    e_sorted = flat_e[order]
    tok_sorted = (order // TOP_K).astype(jnp.int32)
    w_sorted = wts.reshape(-1)[order].astype(h.dtype)
    counts = jnp.bincount(flat_e, length=N_EXPERTS)
    padded = (counts + EXPERT_BLOCK - 1) // EXPERT_BLOCK * EXPERT_BLOCK
    pad_end = jnp.cumsum(padded)
    pad_start = pad_end - padded
    grp_start = jnp.cumsum(counts) - counts
    dest = pad_start[e_sorted] + jnp.arange(n_assign) - grp_start[e_sorted]
    n_blocks = -(-n_assign // EXPERT_BLOCK) + N_EXPERTS
    n_slots = n_blocks * EXPERT_BLOCK
    slot_tok = jnp.full((n_slots,), t, jnp.int32).at[dest].set(tok_sorted)
    slot_w = jnp.zeros((n_slots,), h.dtype).at[dest].set(w_sorted)
    block_e = jnp.minimum(jnp.searchsorted(pad_end, jnp.arange(n_blocks) * EXPERT_BLOCK, side='right'),
                          N_EXPERTS - 1)
    x_pad = jnp.concatenate([xf, jnp.zeros((1, D_MODEL), xf.dtype)], axis=0)
    xb = x_pad[slot_tok].reshape(n_blocks, EXPERT_BLOCK, D_MODEL)
    w_gu, w_dn = p['w_exp_gu'], p['w_exp_down']

    def expert_block(args):
        xblk, e = args
        g, u = jnp.split(xblk @ w_gu[e], 2, axis=-1)
        return (jax.nn.silu(g) * u) @ w_dn[e]

    yb = lax.map(expert_block, (xb, block_e)).reshape(n_slots, D_MODEL)
    y = jax.ops.segment_sum(yb * slot_w[:, None], slot_tok, num_segments=t + 1)[:t]
    gs, us = jnp.split(xf @ p['w_sh_gu'], 2, axis=-1)
    y = y + (jax.nn.silu(gs) * us) @ p['w_sh_down']
    return y.reshape(shp)


def layer(x, c, p, pos, past):
    mod = (jax.nn.silu(c) @ p['w_ada'] + p['b_ada'])[:, None, :]
    sh1, sc1, gt1, sh2, sc2, gt2 = jnp.split(mod, 6, axis=-1)
    h = rmsnorm(x, p['norm1_g']) * (1 + sc1) + sh1
    y, state = token_mixer(h, p, pos, past)
    x = x + gt1 * y
    h = rmsnorm(x, p['norm2_g']) * (1 + sc2) + sh2
    x = x + gt2 * moe(h, p)
    return x, state


def setup_inputs(seed: int = 0) -> dict:
    key = jax.random.key(seed)
    ks = list(jax.random.split(key, 40))

    def nrm(i, shape, scale):
        return scale * jax.random.normal(ks[i], shape, jnp.float32)

    def gain(i, shape):
        return 1.0 + nrm(i, shape, 0.02)

    n_pages = PAST_LEN // PAGE_SIZE
    n_used = DEC_BATCH * n_pages
    n_pool = n_used + n_used // 4
    page_table = jax.random.permutation(ks[0], n_pool)[:n_used].reshape(DEC_BATCH, n_pages).astype(jnp.int32)
    return {
        'x_prompt': nrm(1, (BATCH, SEQ, D_MODEL), 1.0),
        'x_sample': nrm(2, (DEC_BATCH, DEC_SEQ, D_MODEL), 1.0),
        'cache_ckv': nrm(3, (DEPTH, n_pool, PAGE_SIZE, KV_LORA), 1.0),
        'cache_krope': nrm(4, (DEPTH, n_pool, PAGE_SIZE, QK_ROPE), 1.0),
        'page_table': page_table,
        'c_prompt': nrm(5, (BATCH, D_MODEL), 1.0),
        'c_sample': nrm(6, (DEC_BATCH, D_MODEL), 1.0),
        'w_ada': nrm(7, (DEPTH, D_MODEL, 6 * D_MODEL), 0.5 * D_MODEL ** -0.5),
        'b_ada': nrm(8, (DEPTH, 6 * D_MODEL), 0.01),
        'norm1_g': gain(9, (DEPTH, D_MODEL)),
        'norm2_g': gain(10, (DEPTH, D_MODEL)),
        'w_in': nrm(11, (DEPTH, D_MODEL, N_IN), D_MODEL ** -0.5),
        'a_vnorm_g': gain(12, (DEPTH, A_WIDTH)),
        'w_spatial': nrm(13, (DEPTH, A_GROUPS, CHUNK, CHUNK), CHUNK ** -0.5),
        'b_spatial': gain(14, (DEPTH, A_GROUPS, CHUNK)),
        'w_out_a': nrm(15, (DEPTH, A_WIDTH, D_MODEL), A_WIDTH ** -0.5),
        'q_norm_g': gain(16, (DEPTH, Q_LORA)),
        'w_uq': nrm(17, (DEPTH, Q_LORA, N_HEADS, QK_NOPE + QK_ROPE), Q_LORA ** -0.5),
        'kv_norm_g': gain(18, (DEPTH, KV_LORA)),
        'w_uk': nrm(19, (DEPTH, KV_LORA, N_HEADS, QK_NOPE), KV_LORA ** -0.5),
        'w_uv': nrm(20, (DEPTH, KV_LORA, N_HEADS, V_HEAD), KV_LORA ** -0.5),
        'w_out_b': nrm(21, (DEPTH, N_HEADS * V_HEAD, D_MODEL), (N_HEADS * V_HEAD) ** -0.5),
        'w_o': nrm(22, (DEPTH, D_MODEL, D_MODEL), D_MODEL ** -0.5),
        'w_router': nrm(23, (DEPTH, D_MODEL, N_EXPERTS), D_MODEL ** -0.5),
        'router_bias': nrm(24, (DEPTH, N_EXPERTS), 0.01),
        'w_exp_gu': nrm(25, (DEPTH, N_EXPERTS, D_MODEL, 2 * EXPERT_DIM), D_MODEL ** -0.5),
        'w_exp_down': nrm(26, (DEPTH, N_EXPERTS, EXPERT_DIM, D_MODEL), EXPERT_DIM ** -0.5),
        'w_sh_gu': nrm(27, (DEPTH, D_MODEL, 2 * SHARED_DIM), D_MODEL ** -0.5),
        'w_sh_down': nrm(28, (DEPTH, SHARED_DIM, D_MODEL), SHARED_DIM ** -0.5),
        'final_norm_g': gain(29, (D_MODEL,)),
    }


def reference(x_prompt, x_sample, cache_ckv, cache_krope, page_table, c_prompt, c_sample,
              w_ada, b_ada, norm1_g, norm2_g, w_in, a_vnorm_g, w_spatial, b_spatial, w_out_a,
              q_norm_g, w_uq, kv_norm_g, w_uk, w_uv, w_out_b, w_o, w_router, router_bias,
              w_exp_gu, w_exp_down, w_sh_gu, w_sh_down, final_norm_g):
    dec_b, n_pages = page_table.shape
    n_past = n_pages * PAGE_SIZE
    pos_p = jnp.arange(x_prompt.shape[1])
    pos_s = n_past + jnp.arange(x_sample.shape[1])
    xp, xs = x_prompt, x_sample
    ckv_p, kr_p, ckv_s, kr_s, v_s = [], [], [], [], []
    for l in range(DEPTH):
        p = {'w_ada': w_ada[l], 'b_ada': b_ada[l], 'norm1_g': norm1_g[l], 'norm2_g': norm2_g[l],
             'w_in': w_in[l], 'a_vnorm_g': a_vnorm_g[l], 'w_spatial': w_spatial[l],
             'b_spatial': b_spatial[l], 'w_out_a': w_out_a[l], 'q_norm_g': q_norm_g[l],
             'w_uq': w_uq[l], 'kv_norm_g': kv_norm_g[l], 'w_uk': w_uk[l], 'w_uv': w_uv[l],
             'w_out_b': w_out_b[l], 'w_o': w_o[l], 'w_router': w_router[l],
             'router_bias': router_bias[l], 'w_exp_gu': w_exp_gu[l], 'w_exp_down': w_exp_down[l],
             'w_sh_gu': w_sh_gu[l], 'w_sh_down': w_sh_down[l]}
        xp, (ck, kr, _) = layer(xp, c_prompt, p, pos_p, None)
        ckv_p.append(ck)
        kr_p.append(kr)
        past_ckv = cache_ckv[l][page_table].reshape(dec_b, n_past, KV_LORA)
        past_kr = cache_krope[l][page_table].reshape(dec_b, n_past, QK_ROPE)
        xs, (ck, kr, vv) = layer(xs, c_sample, p, pos_s, (past_ckv, past_kr))
        ckv_s.append(ck)
        kr_s.append(kr)
        v_s.append(vv)
    y_prompt = rmsnorm(xp, final_norm_g)
    y_sample = rmsnorm(xs, final_norm_g)
    new_ckv_prompt = jnp.stack(ckv_p, axis=0)
    new_krope_prompt = jnp.stack(kr_p, axis=0)
    new_ckv_sample = jnp.stack(ckv_s, axis=0)
    new_krope_sample = jnp.stack(kr_s, axis=0)
    new_chunk_v_sample = jnp.stack(v_s, axis=0)
    return (y_prompt, y_sample, new_ckv_prompt, new_krope_prompt, new_ckv_sample, new_krope_sample, new_chunk_v_sample)
```

```python
import functools

import jax
import jax.numpy as jnp
from jax import lax
from jax.experimental import pallas as pl
from jax.experimental.pallas import tpu as pltpu

F32 = jnp.float32
BF16 = jnp.bfloat16

D_MODEL = 1024
A_WIDTH = D_MODEL // 2
A_GROUPS = 8
A_GROUP_DIM = A_WIDTH // A_GROUPS
CHUNK = 128
N_HEADS = 8
QK_NOPE = 64
QK_ROPE = 32
ROPE_HALF = QK_ROPE // 2
V_HEAD = 64
Q_LORA = 384
KV_LORA = 256
ROPE_THETA = 10000.0
ATTN_SCALE = (QK_NOPE + QK_ROPE) ** -0.5
N_EXPERTS = 256
TOP_K = 8
N_GROUPS = 8
GROUP_SIZE = N_EXPERTS // N_GROUPS
TOPK_GROUPS = 4
EXPERT_DIM = 256
SHARED_DIM = 256
ROUTED_SCALE = 2.5
EPS = 1e-6

LANES = 128
HEAD_PAD = 128
ROW_TILE = 256
ATTN_KV_TILE = 256
ATTN_Q_STEP = 512
RANK_TILE = 512
MOE_BLOCK = 128
PAGES_PER_STEP = 16
VMEM_LIMIT = 48 * 1024 * 1024

IN_U, IN_V, IN_CQ, IN_CKV, IN_KR, IN_END = 0, 512, 1024, 1408, 1664, 1792

NT_DIMS = (((1,), (1,)), ((), ()))
TN_DIMS = (((0,), (0,)), ((), ()))


def _params(sem, vmem=VMEM_LIMIT):
    return pltpu.CompilerParams(dimension_semantics=sem, vmem_limit_bytes=vmem)


def _rms(x, g):
    return x * lax.rsqrt(jnp.mean(x * x, axis=-1, keepdims=True) + EPS) * g


def _gelu(x):
    return 0.5 * x * (1.0 + jnp.tanh(0.7978845608028654 * (x + 0.044715 * (x * x * x))))


def _sigmoid(x):
    return 1.0 / (1.0 + jnp.exp(-x))


def _silu(x):
    return x * _sigmoid(x)


def _ada_kernel(c_ref, w_ref, b_ref, o_ref):
    s = _silu(c_ref[...]).astype(BF16)
    o_ref[...] = jnp.dot(s, w_ref[...], preferred_element_type=F32) + b_ref[...]


def _ada(c, w, b):
    rows, n = c.shape[0], w.shape[1]
    tn = 1536
    return pl.pallas_call(
        _ada_kernel,
        grid=(n // tn,),
        in_specs=[pl.BlockSpec((rows, D_MODEL), lambda j: (0, 0)),
                  pl.BlockSpec((D_MODEL, tn), lambda j: (0, j)),
                  pl.BlockSpec((1, tn), lambda j: (0, j))],
        out_specs=pl.BlockSpec((rows, tn), lambda j: (0, j)),
        out_shape=jax.ShapeDtypeStruct((rows, n), F32),
        compiler_params=_params(("parallel",)),
        name="ada",
    )(c, w, b)


def _in_kernel(x_ref, sh_ref, sc_ref, g1_ref, w_ref, vg_ref, qg_ref, kg_ref, cos_ref, sin_ref,
               wc_ref, bc_ref, a_ref, v_ref, cq_ref, ckv_ref, kr_ref):
    tm = x_ref.shape[0]
    h = _rms(x_ref[...], g1_ref[...]) * (1.0 + sc_ref[...]) + sh_ref[...]
    z = jnp.dot(h.astype(BF16), w_ref[...], preferred_element_type=F32)
    u = _gelu(z[:, IN_U:IN_V])
    v = _rms(_gelu(z[:, IN_V:IN_CQ]), vg_ref[...])
    v_ref[...] = v
    cq_ref[...] = _rms(z[:, IN_CQ:IN_CKV], qg_ref[...]).astype(BF16)
    ckv_ref[...] = _rms(z[:, IN_CKV:IN_KR], kg_ref[...])
    kr = z[:, IN_KR:IN_END]
    lane = lax.broadcasted_iota(jnp.int32, kr.shape, 1)
    swapped = jnp.where(lane < ROPE_HALF, pltpu.roll(kr, LANES - ROPE_HALF, 1), pltpu.roll(kr, ROPE_HALF, 1))
    kr_rot = kr * cos_ref[...] + swapped * sin_ref[...]
    kr_ref[...] = kr_rot[:, :QK_ROPE]
    vb = v.astype(BF16)
    half = A_WIDTH // 2
    grp_of_lane = lax.broadcasted_iota(jnp.int32, (CHUNK, half), 1) // A_GROUP_DIM
    for ci in range(tm // CHUNK):
        rows = slice(ci * CHUNK, (ci + 1) * CHUNK)
        parts = []
        for q in range(2):
            vq = vb[rows, q * half:(q + 1) * half]
            acc = None
            for gg in range(A_GROUPS // 2):
                vm = jnp.where(grp_of_lane == gg, vq, jnp.zeros_like(vq))
                part = jnp.dot(wc_ref[q * (A_GROUPS // 2) + gg], vm, preferred_element_type=F32)
                acc = part if acc is None else acc + part
            parts.append(acc)
        sp = jnp.concatenate(parts, axis=1) + bc_ref[...]
        a_ref[rows, :] = (u[rows, :] * sp).astype(BF16)


def _in_proj(x_all, sh1, sc1, g1, w5, vg, qg, kg, cos_kr, sin_kr, wc, bc, grp_of_tile, n_prompt_tiles):
    t_all = x_all.shape[0]
    tm = ROW_TILE
    n_tiles = t_all // tm
    row = lambda i: (i, 0)
    fixed2 = lambda i: (0, 0)
    kind = lambda i: (jnp.where(i < n_prompt_tiles, 0, 1), 0, 0)
    kind4 = lambda i: (jnp.where(i < n_prompt_tiles, 0, 1), 0, 0, 0)
    mod = lambda i: (grp_of_tile(i), 0, 0)
    return pl.pallas_call(
        _in_kernel,
        grid=(n_tiles,),
        in_specs=[pl.BlockSpec((tm, D_MODEL), row),
                  pl.BlockSpec((None, tm, D_MODEL), mod),
                  pl.BlockSpec((None, tm, D_MODEL), mod),
                  pl.BlockSpec((1, D_MODEL), fixed2),
                  pl.BlockSpec((D_MODEL, IN_END), fixed2),
                  pl.BlockSpec((1, A_WIDTH), fixed2),
                  pl.BlockSpec((1, Q_LORA), fixed2),
                  pl.BlockSpec((1, KV_LORA), fixed2),
                  pl.BlockSpec((tm, LANES), row),
                  pl.BlockSpec((tm, LANES), row),
                  pl.BlockSpec((None, A_GROUPS, CHUNK, CHUNK), kind4),
                  pl.BlockSpec((None, CHUNK, A_WIDTH), kind)],
        out_specs=[pl.BlockSpec((tm, A_WIDTH), row),
                   pl.BlockSpec((tm, A_WIDTH), row),
                   pl.BlockSpec((tm, Q_LORA), row),
                   pl.BlockSpec((tm, KV_LORA), row),
                   pl.BlockSpec((tm, QK_ROPE), row)],
        out_shape=[jax.ShapeDtypeStruct((t_all, A_WIDTH), BF16),
                   jax.ShapeDtypeStruct((t_all, A_WIDTH), F32),
                   jax.ShapeDtypeStruct((t_all, Q_LORA), BF16),
                   jax.ShapeDtypeStruct((t_all, KV_LORA), F32),
                   jax.ShapeDtypeStruct((t_all, QK_ROPE), F32)],
        compiler_params=_params(("parallel",)),
        name="in_proj",
    )(x_all, sh1, sc1, g1, w5, vg, qg, kg, cos_kr, sin_kr, wc, bc)


def _qkv_kernel(cq_ref, ckv_ref, kr_ref, wuqt_ref, wukp_ref, place_ref, wuvt_ref, cos_ref, sin_ref,
                qt_ref, k_ref, vt_ref):
    tm = cq_ref.shape[0]
    ckv = ckv_ref[...].astype(BF16)
    kr = kr_ref[...].astype(BF16)
    qt = lax.dot_general(wuqt_ref[...], cq_ref[...], NT_DIMS, preferred_element_type=F32)
    c = cos_ref[...]
    s = sin_ref[...]
    pad = jnp.zeros((HEAD_PAD - QK_NOPE - QK_ROPE, tm), F32)
    for h in range(N_HEADS):
        blk = qt[h * HEAD_PAD:(h + 1) * HEAD_PAD]
        x1 = blk[QK_NOPE:QK_NOPE + ROPE_HALF]
        x2 = blk[QK_NOPE + ROPE_HALF:QK_NOPE + QK_ROPE]
        full = jnp.concatenate([blk[:QK_NOPE], x1 * c - x2 * s, x1 * s + x2 * c, pad], axis=0)
        qt_ref[h] = (full * ATTN_SCALE).astype(BF16)
    k = (jnp.dot(ckv, wukp_ref[...], preferred_element_type=F32)
         + jnp.dot(kr, place_ref[...], preferred_element_type=F32))
    for h in range(N_HEADS):
        k_ref[h] = k[:, h * HEAD_PAD:(h + 1) * HEAD_PAD].astype(BF16)
    vt = lax.dot_general(wuvt_ref[...], ckv, NT_DIMS, preferred_element_type=F32)
    for h in range(N_HEADS):
        vt_ref[h] = vt[h * V_HEAD:(h + 1) * V_HEAD].astype(BF16)


def _qkv(cq_all, ckv_all, kr_all, wuqt, wukp, place, wuvt, cos_t, sin_t, batch, seq):
    tm = ATTN_KV_TILE
    nk = seq // tm
    row = lambda b, j: (b * nk + j, 0)
    fixed2 = lambda b, j: (0, 0)
    tab = lambda b, j: (0, j)
    return pl.pallas_call(
        _qkv_kernel,
        grid=(batch, nk),
        in_specs=[pl.BlockSpec((tm, Q_LORA), row),
                  pl.BlockSpec((tm, KV_LORA), row),
                  pl.BlockSpec((tm, QK_ROPE), row),
                  pl.BlockSpec(wuqt.shape, fixed2),
                  pl.BlockSpec(wukp.shape, fixed2),
                  pl.BlockSpec(place.shape, fixed2),
                  pl.BlockSpec(wuvt.shape, fixed2),
                  pl.BlockSpec((ROPE_HALF, tm), tab),
                  pl.BlockSpec((ROPE_HALF, tm), tab)],
        out_specs=[pl.BlockSpec((None, N_HEADS, HEAD_PAD, tm), lambda b, j: (b, 0, 0, j)),
                   pl.BlockSpec((None, N_HEADS, tm, HEAD_PAD), lambda b, j: (b, 0, j, 0)),
                   pl.BlockSpec((None, N_HEADS, None, V_HEAD, tm), lambda b, j: (b, 0, j, 0, 0))],
        out_shape=[jax.ShapeDtypeStruct((batch, N_HEADS, HEAD_PAD, seq), BF16),
                   jax.ShapeDtypeStruct((batch, N_HEADS, seq, HEAD_PAD), BF16),
                   jax.ShapeDtypeStruct((batch, N_HEADS, nk, V_HEAD, tm), BF16)],
        compiler_params=_params(("parallel", "parallel")),
        name="qkv",
    )(cq_all, ckv_all, kr_all, wuqt, wukp, place, wuvt, cos_t, sin_t)


def _attn_kernel(qt_ref, k_ref, vt_ref, o_ref):
    tk = ATTN_KV_TILE
    n_sub = o_ref.shape[1] // tk
    qi = pl.program_id(2)

    def step(j, carry, qt, q_lo, masked):
        m, l, acc = carry
        s = jnp.dot(k_ref[j], qt, preferred_element_type=F32)
        if masked:
            key = j * tk + lax.broadcasted_iota(jnp.int32, s.shape, 0)
            qry = q_lo + lax.broadcasted_iota(jnp.int32, s.shape, 1)
            s = jnp.where(key <= qry, s, -jnp.inf)
        m_new = jnp.maximum(m, jnp.max(s, axis=0, keepdims=True))
        alpha = jnp.exp(m - m_new)
        p = jnp.exp(s - m_new)
        l = alpha * l + jnp.sum(p, axis=0, keepdims=True)
        acc = alpha * acc + jnp.dot(vt_ref[j], p.astype(BF16), preferred_element_type=F32)
        return m_new, l, acc

    for sub in range(n_sub):
        qt = qt_ref[:, sub * tk:(sub + 1) * tk]
        jd = qi * n_sub + sub
        q_lo = jd * tk
        init = (jnp.full((1, tk), -jnp.inf, F32), jnp.zeros((1, tk), F32), jnp.zeros((V_HEAD, tk), F32))
        carry = lax.fori_loop(0, jd, lambda j, c: step(j, c, qt, q_lo, False), init)
        m, l, acc = step(jd, carry, qt, q_lo, True)
        o_ref[:, sub * tk:(sub + 1) * tk] = (acc / l).astype(BF16)


def _attn(qt, k, vt):
    batch, _, _, seq = qt.shape
    tk = ATTN_KV_TILE
    nk = seq // tk
    k5 = k.reshape(batch, N_HEADS, nk, tk, HEAD_PAD)
    return pl.pallas_call(
        _attn_kernel,
        grid=(batch, N_HEADS, seq // ATTN_Q_STEP),
        in_specs=[pl.BlockSpec((None, None, HEAD_PAD, ATTN_Q_STEP), lambda b, h, i: (b, h, 0, i)),
                  pl.BlockSpec((None, None, nk, tk, HEAD_PAD), lambda b, h, i: (b, h, 0, 0, 0)),
                  pl.BlockSpec((None, None, nk, V_HEAD, tk), lambda b, h, i: (b, h, 0, 0, 0))],
        out_specs=pl.BlockSpec((None, V_HEAD, ATTN_Q_STEP), lambda b, h, i: (b, h, i)),
        out_shape=jax.ShapeDtypeStruct((batch, N_HEADS * V_HEAD, seq), BF16),
        compiler_params=_params(("parallel", "parallel", "parallel")),
        name="attn",
    )(qt, k5, vt)


def _sq_kernel(cq_ref, wq_ref, wukb_ref, cos_ref, sin_ref, ql_ref, r1_ref, r2_ref):
    n_nope = N_HEADS * QK_NOPE
    q = jnp.dot(cq_ref[...], wq_ref[...], preferred_element_type=F32)
    x1 = q[:, n_nope:n_nope + LANES]
    x2 = q[:, n_nope + LANES:]
    c = cos_ref[...]
    s = sin_ref[...]
    r1_ref[...] = ((x1 * c - x2 * s) * ATTN_SCALE).astype(BF16)
    r2_ref[...] = ((x1 * s + x2 * c) * ATTN_SCALE).astype(BF16)
    ql = jnp.dot(q[:, :n_nope].astype(BF16), wukb_ref[...], preferred_element_type=F32)
    ql_ref[...] = (ql * ATTN_SCALE).astype(BF16)


def _sample_q(cq_s, wq_s, wuk_blk, cos_s, sin_s):
    ts = cq_s.shape[0]
    full = lambda shape: pl.BlockSpec(shape, lambda i: (0,) * len(shape))
    return pl.pallas_call(
        _sq_kernel,
        grid=(1,),
        in_specs=[full(cq_s.shape), full(wq_s.shape), full(wuk_blk.shape), full(cos_s.shape), full(sin_s.shape)],
        out_specs=[full((ts, N_HEADS * KV_LORA)), full((ts, LANES)), full((ts, LANES))],
        out_shape=[jax.ShapeDtypeStruct((ts, N_HEADS * KV_LORA), BF16),
                   jax.ShapeDtypeStruct((ts, LANES), BF16),
                   jax.ShapeDtypeStruct((ts, LANES), BF16)],
        compiler_params=_params(("arbitrary",)),
        name="sample_q",
    )(cq_s, wq_s, wuk_blk, cos_s, sin_s)


def _sattn_kernel(pt_ref, *refs):
    npg = PAGES_PER_STEP
    ckv_pages = refs[:npg]
    kr_pages = refs[npg:2 * npg]
    ql_ref, qr_ref, cnew_ref, knew_ref, o_ref, kbuf, rbuf, m_ref, l_ref, acc_ref = refs[2 * npg:]
    j = pl.program_id(1)
    n_new = cnew_ref.shape[0]

    @pl.when(j == 0)
    def _():
        m_ref[...] = jnp.full(m_ref.shape, -jnp.inf, F32)
        l_ref[...] = jnp.zeros(l_ref.shape, F32)
        acc_ref[...] = jnp.zeros(acc_ref.shape, F32)

    for p in range(npg):
        kbuf[p * CHUNK:(p + 1) * CHUNK, :] = ckv_pages[p][...].astype(BF16)
        rbuf[p * CHUNK:(p + 1) * CHUNK, :] = kr_pages[p][...].astype(BF16)

    ql = ql_ref[...]
    qr = qr_ref[...]

    def update(s, vals):
        m_old = m_ref[...]
        m_new = jnp.maximum(m_old, jnp.max(s, axis=-1, keepdims=True))
        alpha = jnp.exp(m_old - m_new)
        p = jnp.exp(s - m_new)
        l_ref[...] = alpha * l_ref[...] + jnp.sum(p, axis=-1, keepdims=True)
        acc_ref[...] = alpha * acc_ref[...] + jnp.dot(p.astype(BF16), vals, preferred_element_type=F32)
        m_ref[...] = m_new

    keys = kbuf[...]
    s = (lax.dot_general(ql, keys, NT_DIMS, preferred_element_type=F32)
         + lax.dot_general(qr, rbuf[...], NT_DIMS, preferred_element_type=F32))
    update(s, keys)

    @pl.when(j == pl.num_programs(1) - 1)
    def _():
        cnew = cnew_ref[...].astype(BF16)
        s_new = (lax.dot_general(ql, cnew, NT_DIMS, preferred_element_type=F32)
                 + lax.dot_general(qr, knew_ref[...].astype(BF16), NT_DIMS, preferred_element_type=F32))
        q_pos = lax.broadcasted_iota(jnp.int32, s_new.shape, 0) // N_HEADS
        k_pos = lax.broadcasted_iota(jnp.int32, s_new.shape, 1)
        update(jnp.where(k_pos <= q_pos, s_new, -jnp.inf), cnew)
        o_ref[...] = acc_ref[...] / l_ref[...]


def _sample_attn(page_table, cache_ckv, cache_kr, ql, qr, cnew, knew):
    dec_b, n_pages = page_table.shape
    npg = PAGES_PER_STEP
    n_steps = n_pages // npg
    rows = ql.shape[1]
    n_new = cnew.shape[1]

    def page_map(p):
        return lambda b, j, pt: (pt[b * n_pages + j * npg + p], 0, 0)

    per_b = lambda b, j, pt: (b, 0, 0)
    in_specs = ([pl.BlockSpec((None, CHUNK, KV_LORA), page_map(p)) for p in range(npg)]
                + [pl.BlockSpec((None, CHUNK, QK_ROPE), page_map(p)) for p in range(npg)]
                + [pl.BlockSpec((None, rows, KV_LORA), per_b),
                   pl.BlockSpec((None, rows, QK_ROPE), per_b),
                   pl.BlockSpec((None, n_new, KV_LORA), per_b),
                   pl.BlockSpec((None, n_new, QK_ROPE), per_b)])
    return pl.pallas_call(
        _sattn_kernel,
        grid_spec=pltpu.PrefetchScalarGridSpec(
            num_scalar_prefetch=1,
            grid=(dec_b, n_steps),
            in_specs=in_specs,
            out_specs=pl.BlockSpec((None, rows, KV_LORA), per_b),
            scratch_shapes=[pltpu.VMEM((npg * CHUNK, KV_LORA), BF16),
                            pltpu.VMEM((npg * CHUNK, QK_ROPE), BF16),
                            pltpu.VMEM((rows, 1), F32),
                            pltpu.VMEM((rows, 1), F32),
                            pltpu.VMEM((rows, KV_LORA), F32)]),
        out_shape=jax.ShapeDtypeStruct((dec_b, rows, KV_LORA), F32),
        compiler_params=_params(("parallel", "arbitrary")),
        name="sample_attn",
    )(page_table.reshape(-1), *([cache_ckv] * npg), *([cache_kr] * npg), ql, qr, cnew, knew)


def _mm_kernel(x_ref, w_ref, o_ref):
    o_ref[...] = jnp.dot(x_ref[...].astype(BF16), w_ref[...], preferred_element_type=F32).astype(o_ref.dtype)


def _matmul(x, w, out_dtype):
    m, n = x.shape[0], w.shape[1]
    full = lambda shape: pl.BlockSpec(shape, lambda i: (0,) * len(shape))
    return pl.pallas_call(
        _mm_kernel,
        grid=(1,),
        in_specs=[full(x.shape), full(w.shape)],
        out_specs=full((m, n)),
        out_shape=jax.ShapeDtypeStruct((m, n), out_dtype),
        compiler_params=_params(("arbitrary",)),
        name="matmul",
    )(x, w)


def _merge_kernel(n_prompt_tiles, x_ref, sh1_ref, sc1_ref, gt1_ref, sh2_ref, sc2_ref, g1_ref, g2_ref,
                  a_ref, ot_ref, os_ref, wg_ref, woa_ref, wob_ref, wo_ref, x1_ref, h2_ref, yb_ref):
    i = pl.program_id(0)
    x = x_ref[...]
    h = _rms(x, g1_ref[...]) * (1.0 + sc1_ref[...]) + sh1_ref[...]
    gates = _sigmoid(jnp.dot(h.astype(BF16), wg_ref[...], preferred_element_type=F32))
    y_a = jnp.dot(a_ref[...], woa_ref[...], preferred_element_type=F32)

    @pl.when(i < n_prompt_tiles)
    def _():
        yb_ref[...] = lax.dot_general(ot_ref[...], wob_ref[...], TN_DIMS, preferred_element_type=F32)

    @pl.when(i >= n_prompt_tiles)
    def _():
        yb_ref[...] = jnp.dot(os_ref[...], wob_ref[...], preferred_element_type=F32)

    z = gates[:, :D_MODEL] * y_a + gates[:, D_MODEL:] * yb_ref[...]
    y = jnp.dot(z.astype(BF16), wo_ref[...], preferred_element_type=F32)
    x1 = x + gt1_ref[...] * y
    x1_ref[...] = x1
    h2_ref[...] = (_rms(x1, g2_ref[...]) * (1.0 + sc2_ref[...]) + sh2_ref[...]).astype(BF16)


def _merge(x_all, mods, g1, g2, a_all, ot, o_s, wg, woa, wob, wo, grp_of_tile, n_prompt_tiles):
    t_all = x_all.shape[0]
    tm = ROW_TILE
    seq = ot.shape[2]
    tpb = seq // tm
    row = lambda i: (i, 0)
    fixed2 = lambda i: (0, 0)
    mod = lambda i: (grp_of_tile(i), 0, 0)

    def ot_map(i):
        ic = jnp.minimum(i, n_prompt_tiles - 1)
        return (ic // tpb, 0, ic % tpb)

    os_map = lambda i: (jnp.maximum(i - n_prompt_tiles, 0), 0)
    mod_spec = pl.BlockSpec((None, tm, D_MODEL), mod)
    return pl.pallas_call(
        functools.partial(_merge_kernel, n_prompt_tiles),
        grid=(t_all // tm,),
        in_specs=[pl.BlockSpec((tm, D_MODEL), row),
                  mod_spec, mod_spec, mod_spec, mod_spec, mod_spec,
                  pl.BlockSpec((1, D_MODEL), fixed2),
                  pl.BlockSpec((1, D_MODEL), fixed2),
                  pl.BlockSpec((tm, A_WIDTH), row),
                  pl.BlockSpec((None, N_HEADS * V_HEAD, tm), ot_map),
                  pl.BlockSpec((tm, N_HEADS * V_HEAD), os_map),
                  pl.BlockSpec(wg.shape, fixed2),
                  pl.BlockSpec(woa.shape, fixed2),
                  pl.BlockSpec(wob.shape, fixed2),
                  pl.BlockSpec(wo.shape, fixed2)],
        out_specs=[pl.BlockSpec((tm, D_MODEL), row), pl.BlockSpec((tm, D_MODEL), row)],
        out_shape=[jax.ShapeDtypeStruct((t_all, D_MODEL), F32),
                   jax.ShapeDtypeStruct((t_all, D_MODEL), BF16)],
        scratch_shapes=[pltpu.VMEM((tm, D_MODEL), F32)],
        compiler_params=_params(("parallel",)),
        name="merge",
    )(x_all, *mods, g1, g2, a_all, ot, o_s, wg, woa, wob, wo)


def _first_argmax(v, rows):
    mx = jnp.max(v, axis=0, keepdims=True)
    idx = jnp.min(jnp.where(v == mx, rows, v.shape[0]), axis=0, keepdims=True)
    return mx, idx


def _route_kernel(x1_ref, sh2_ref, sc2_ref, g2_ref, whi_ref, wlo_ref, bias_ref, idx_ref, wt_ref):
    h2 = _rms(x1_ref[...], g2_ref[...]) * (1.0 + sc2_ref[...]) + sh2_ref[...]
    hi = h2.astype(BF16)
    lo = (h2 - hi.astype(F32)).astype(BF16)
    whi = whi_ref[...]
    logits = (lax.dot_general(whi, hi, NT_DIMS, preferred_element_type=F32)
              + lax.dot_general(whi, lo, NT_DIMS, preferred_element_type=F32)
              + lax.dot_general(wlo_ref[...], hi, NT_DIMS, preferred_element_type=F32))
    scores = _sigmoid(logits)
    sel = scores + bias_ref[...]
    tm = sel.shape[1]
    neg = -jnp.inf
    rows_g = lax.broadcasted_iota(jnp.int32, (GROUP_SIZE, tm), 0)
    gscore = []
    for g in range(N_GROUPS):
        blk = sel[g * GROUP_SIZE:(g + 1) * GROUP_SIZE]
        m1, i1 = _first_argmax(blk, rows_g)
        m2 = jnp.max(jnp.where(rows_g == i1, neg, blk), axis=0, keepdims=True)
        gscore.append(m1 + m2)
    gs = jnp.concatenate(gscore, axis=0)
    rows_8 = lax.broadcasted_iota(jnp.int32, gs.shape, 0)
    chosen = jnp.zeros(gs.shape, jnp.int32)
    for _ in range(TOPK_GROUPS):
        _, gi = _first_argmax(gs, rows_8)
        hit = rows_8 == gi
        chosen = jnp.where(hit, 1, chosen)
        gs = jnp.where(hit, neg, gs)
    cand = jnp.concatenate(
        [jnp.where(chosen[g:g + 1] > 0, sel[g * GROUP_SIZE:(g + 1) * GROUP_SIZE], neg) for g in range(N_GROUPS)],
        axis=0)
    rows_e = lax.broadcasted_iota(jnp.int32, cand.shape, 0)
    idxs, wts = [], []
    for _ in range(TOP_K):
        _, ei = _first_argmax(cand, rows_e)
        hit = rows_e == ei
        idxs.append(ei)
        wts.append(jnp.sum(jnp.where(hit, scores, 0.0), axis=0, keepdims=True))
        cand = jnp.where(hit, neg, cand)
    w = jnp.concatenate(wts, axis=0)
    idx_ref[...] = jnp.concatenate(idxs, axis=0)
    wt_ref[...] = w / jnp.sum(w, axis=0, keepdims=True) * ROUTED_SCALE


def _route(x1_all, sh2, sc2, g2, whi, wlo, bias_col, grp_of_tile):
    t_all = x1_all.shape[0]
    tm = ROW_TILE
    fixed2 = lambda i: (0, 0)
    mod = lambda i: (grp_of_tile(i), 0, 0)
    return pl.pallas_call(
        _route_kernel,
        grid=(t_all // tm,),
        in_specs=[pl.BlockSpec((tm, D_MODEL), lambda i: (i, 0)),
                  pl.BlockSpec((None, tm, D_MODEL), mod),
                  pl.BlockSpec((None, tm, D_MODEL), mod),
                  pl.BlockSpec((1, D_MODEL), fixed2),
                  pl.BlockSpec(whi.shape, fixed2),
                  pl.BlockSpec(wlo.shape, fixed2),
                  pl.BlockSpec(bias_col.shape, fixed2)],
        out_specs=[pl.BlockSpec((TOP_K, tm), lambda i: (0, i)), pl.BlockSpec((TOP_K, tm), lambda i: (0, i))],
        out_shape=[jax.ShapeDtypeStruct((TOP_K, t_all), jnp.int32),
                   jax.ShapeDtypeStruct((TOP_K, t_all), F32)],
        compiler_params=_params(("parallel",)),
        name="route",
    )(x1_all, sh2, sc2, g2, whi, wlo, bias_col)


def _rank_kernel(e_ref, upper_ref, rank_ref, count_ref, run_ref):
    @pl.when(pl.program_id(0) == 0)
    def _():
        run_ref[...] = jnp.zeros(run_ref.shape, F32)

    e = e_ref[...]
    n = e.shape[1]
    rows = lax.broadcasted_iota(jnp.int32, (N_EXPERTS, n), 0)
    onehot = rows == e
    oh = jnp.where(onehot, 1.0, 0.0).astype(BF16)
    before = jnp.dot(oh, upper_ref[...], preferred_element_type=F32)
    run = run_ref[...]
    prior = jnp.concatenate([run] * (n // LANES), axis=1)
    rank = jnp.sum(jnp.where(onehot, before + prior, 0.0), axis=0, keepdims=True)
    rank_ref[...] = rank.astype(jnp.int32)
    total = run + jnp.dot(oh, jnp.ones((n, LANES), BF16), preferred_element_type=F32)
    run_ref[...] = total
    count_ref[...] = total


def _rank(e_flat):
    n_assign = e_flat.shape[1]
    n = RANK_TILE
    upper = (lax.broadcasted_iota(jnp.int32, (n, n), 0) < lax.broadcasted_iota(jnp.int32, (n, n), 1)).astype(BF16)
    return pl.pallas_call(
        _rank_kernel,
        grid=(n_assign // n,),
        in_specs=[pl.BlockSpec((1, n), lambda i: (0, i)), pl.BlockSpec((n, n), lambda i: (0, 0))],
        out_specs=[pl.BlockSpec((1, n), lambda i: (0, i)), pl.BlockSpec((N_EXPERTS, LANES), lambda i: (0, 0))],
        out_shape=[jax.ShapeDtypeStruct((1, n_assign), jnp.int32),
                   jax.ShapeDtypeStruct((N_EXPERTS, LANES), F32)],
        scratch_shapes=[pltpu.VMEM((N_EXPERTS, LANES), F32)],
        compiler_params=_params(("arbitrary",)),
        name="rank",
    )(e_flat, upper)


def _moe_kernel(be_ref, nv_ref, x_ref, sw_ref, wgu_ref, wdn_ref, y_ref, gu_bf, dn_bf):
    i = pl.program_id(0)
    prev = be_ref[jnp.maximum(i - 1, 0)]

    @pl.when(i < nv_ref[0])
    def _():
        @pl.when((i == 0) | (be_ref[i] != prev))
        def _():
            gu_bf[...] = wgu_ref[...].astype(BF16)
            dn_bf[...] = wdn_ref[...].astype(BF16)

        gu = jnp.dot(x_ref[...], gu_bf[...], preferred_element_type=F32)
        hid = _silu(gu[:, :EXPERT_DIM]) * gu[:, EXPERT_DIM:] * sw_ref[...]
        y_ref[...] = jnp.dot(hid.astype(BF16), dn_bf[...], preferred_element_type=F32).astype(BF16)


def _moe(block_e, n_valid, x_sorted, slot_w, w_gu, w_dn):
    n_slots = x_sorted.shape[0]
    blk = MOE_BLOCK
    n_blocks = n_slots // blk
    rows = lambda i, be, nv: (jnp.minimum(i, nv[0] - 1), 0)
    wmap = lambda i, be, nv: (be[jnp.minimum(i, nv[0] - 1)], 0, 0)
    return pl.pallas_call(
        _moe_kernel,
        grid_spec=pltpu.PrefetchScalarGridSpec(
            num_scalar_prefetch=2,
            grid=(n_blocks,),
            in_specs=[pl.BlockSpec((blk, D_MODEL), rows),
                      pl.BlockSpec((blk, 1), rows),
                      pl.BlockSpec((None, D_MODEL, 2 * EXPERT_DIM), wmap),
                      pl.BlockSpec((None, EXPERT_DIM, D_MODEL), wmap)],
            out_specs=pl.BlockSpec((blk, D_MODEL), rows),
            scratch_shapes=[pltpu.VMEM((D_MODEL, 2 * EXPERT_DIM), BF16),
                            pltpu.VMEM((EXPERT_DIM, D_MODEL), BF16)]),
        out_shape=jax.ShapeDtypeStruct((n_slots, D_MODEL), BF16),
        compiler_params=_params(("arbitrary",)),
        name="moe",
    )(block_e, n_valid, x_sorted, slot_w, w_gu, w_dn)


def _final_kernel(x1_ref, h2_ref, ym_ref, gt2_ref, wsg_ref, wsd_ref, gf_ref, o_ref):
    gu = jnp.dot(h2_ref[...], wsg_ref[...], preferred_element_type=F32)
    hid = _silu(gu[:, :SHARED_DIM]) * gu[:, SHARED_DIM:]
    y_sh = jnp.dot(hid.astype(BF16), wsd_ref[...], preferred_element_type=F32)
    x2 = x1_ref[...] + gt2_ref[...] * (ym_ref[...] + y_sh)
    o_ref[...] = _rms(x2, gf_ref[...])


def _final(x1_all, h2_all, y_moe, gt2, wsg, wsd, gf, grp_of_tile):
    t_all = x1_all.shape[0]
    tm = ROW_TILE
    row = lambda i: (i, 0)
    fixed2 = lambda i: (0, 0)
    return pl.pallas_call(
        _final_kernel,
        grid=(t_all // tm,),
        in_specs=[pl.BlockSpec((tm, D_MODEL), row),
                  pl.BlockSpec((tm, D_MODEL), row),
                  pl.BlockSpec((tm, D_MODEL), row),
                  pl.BlockSpec((None, tm, D_MODEL), lambda i: (grp_of_tile(i), 0, 0)),
                  pl.BlockSpec(wsg.shape, fixed2),
                  pl.BlockSpec(wsd.shape, fixed2),
                  pl.BlockSpec((1, D_MODEL), fixed2)],
        out_specs=pl.BlockSpec((tm, D_MODEL), row),
        out_shape=jax.ShapeDtypeStruct((t_all, D_MODEL), F32),
        compiler_params=_params(("parallel",)),
        name="final",
    )(x1_all, h2_all, y_moe, gt2, wsg, wsd, gf)


def _rope_tables(pos):
    freqs = ROPE_THETA ** (-jnp.arange(ROPE_HALF, dtype=F32) / ROPE_HALF)
    ang = pos.astype(F32)[:, None] * freqs
    return jnp.cos(ang), jnp.sin(ang)


def kernel(x_prompt, x_sample, cache_ckv, cache_krope, page_table, c_prompt, c_sample, w_ada, b_ada, norm1_g, norm2_g, w_in, a_vnorm_g, w_spatial, b_spatial, w_out_a, q_norm_g, w_uq, kv_norm_g, w_uk, w_uv, w_out_b, w_o, w_router, router_bias, w_exp_gu, w_exp_down, w_sh_gu, w_sh_down, final_norm_g):
    batch, seq, _ = x_prompt.shape
    dec_b, dec_s, _ = x_sample.shape
    n_pages = page_table.shape[1]
    n_past = n_pages * CHUNK
    t_p, t_s = batch * seq, dec_b * dec_s
    t_all = t_p + t_s
    tm = ROW_TILE
    assert seq % ATTN_Q_STEP == 0 and t_s % tm == 0 and CHUNK % dec_s == 0 and n_pages % PAGES_PER_STEP == 0
    assert (t_all * TOP_K) % RANK_TILE == 0 and w_ada.shape[0] == 1
    n_prompt_tiles = t_p // tm
    tiles_per_b = seq // tm

    def grp_of_tile(i):
        return jnp.minimum(i // tiles_per_b, batch) + jnp.maximum(i - n_prompt_tiles, 0)

    n_c = batch + dec_b
    c_rows = -(-n_c // 8) * 8
    c_all = jnp.concatenate([c_prompt, c_sample, jnp.zeros((c_rows - n_c, D_MODEL), F32)], axis=0)
    mod = _ada(c_all, w_ada[0].astype(BF16), b_ada[0][None, :])
    mod_p = jnp.broadcast_to(mod[:batch, None, :], (batch, tm, 6 * D_MODEL))
    mod_s = jnp.repeat(mod[batch:n_c], dec_s, axis=0).reshape(t_s // tm, tm, 6 * D_MODEL)
    mod_g = jnp.concatenate([mod_p, mod_s], axis=0)
    sh1, sc1, gt1, sh2, sc2, gt2 = [mod_g[:, :, k * D_MODEL:(k + 1) * D_MODEL] for k in range(6)]

    o_u, o_v, o_cq, o_ckv, o_kr = A_WIDTH, 2 * A_WIDTH, 2 * A_WIDTH + Q_LORA, 2 * A_WIDTH + Q_LORA + KV_LORA, \
        2 * A_WIDTH + Q_LORA + KV_LORA + QK_ROPE
    win = w_in[0]
    w5 = jnp.concatenate([win[:, :o_kr], jnp.zeros((D_MODEL, IN_END - IN_KR - QK_ROPE), F32)], axis=1).astype(BF16)
    wg = win[:, o_kr:].astype(BF16)
    row1 = lambda v: v.reshape(1, -1)
    tri = jnp.tril(jnp.ones((CHUNK, CHUNK), F32))
    wc_p = w_spatial[0] * tri
    per = CHUNK // dec_s
    small = (w_spatial[0] * tri)[:, :dec_s, :dec_s]
    wc_s = jnp.einsum('ab,gts->gatbs', jnp.eye(per, dtype=F32), small).reshape(A_GROUPS, CHUNK, CHUNK)
    wc = jnp.stack([wc_p, wc_s]).astype(BF16)
    bias_p = jnp.repeat(b_spatial[0].T, A_GROUP_DIM, axis=1)
    bias_s = jnp.tile(bias_p[:dec_s], (per, 1))
    bc = jnp.stack([bias_p, bias_s])

    pos_p = jnp.arange(seq)
    pos_s = n_past + jnp.arange(dec_s)
    cos_p, sin_p = _rope_tables(pos_p)
    cos_s, sin_s = _rope_tables(pos_s)
    cos_rows = jnp.concatenate([jnp.tile(cos_p, (batch, 1)), jnp.tile(cos_s, (dec_b, 1))], axis=0)
    sin_rows = jnp.concatenate([jnp.tile(sin_p, (batch, 1)), jnp.tile(sin_s, (dec_b, 1))], axis=0)
    lane_pad = jnp.zeros((t_all, LANES - QK_ROPE), F32)
    cos_kr = jnp.concatenate([cos_rows, cos_rows, lane_pad], axis=1)
    sin_kr = jnp.concatenate([-sin_rows, sin_rows, lane_pad], axis=1)

    x_all = jnp.concatenate([x_prompt.reshape(t_p, D_MODEL), x_sample.reshape(t_s, D_MODEL)], axis=0)
    g1, g2 = row1(norm1_g[0]), row1(norm2_g[0])
    a_all, v_all, cq_all, ckv_all, kr_all = _in_proj(
        x_all, sh1, sc1, g1, w5, row1(a_vnorm_g[0]), row1(q_norm_g[0]), row1(kv_norm_g[0]),
        cos_kr, sin_kr, wc, bc, grp_of_tile, n_prompt_tiles)

    wuq = w_uq[0]
    wuqt = jnp.pad(wuq, ((0, 0), (0, 0), (0, HEAD_PAD - QK_NOPE - QK_ROPE))).reshape(Q_LORA, -1).T.astype(BF16)
    wukp = jnp.pad(w_uk[0], ((0, 0), (0, 0), (0, HEAD_PAD - QK_NOPE))).reshape(KV_LORA, -1).astype(BF16)
    place_h = jnp.pad(jnp.eye(QK_ROPE, dtype=F32), ((0, 0), (QK_NOPE, HEAD_PAD - QK_NOPE - QK_ROPE)))
    place = jnp.tile(place_h, (1, N_HEADS)).astype(BF16)
    wuvt = w_uv[0].reshape(KV_LORA, -1).T.astype(BF16)
    qt, k, vt = _qkv(cq_all, ckv_all, kr_all, wuqt, wukp, place, wuvt, cos_p.T, sin_p.T, batch, seq)
    ot = _attn(qt, k, vt)

    wq_s = jnp.concatenate([wuq[:, :, :QK_NOPE].reshape(Q_LORA, -1),
                            wuq[:, :, QK_NOPE:QK_NOPE + ROPE_HALF].reshape(Q_LORA, -1),
                            wuq[:, :, QK_NOPE + ROPE_HALF:].reshape(Q_LORA, -1)], axis=1).astype(BF16)
    eye_h = jnp.eye(N_HEADS, dtype=F32)
    wuk_blk = jnp.einsum('rhd,hg->hdgr', w_uk[0], eye_h).reshape(N_HEADS * QK_NOPE, N_HEADS * KV_LORA).astype(BF16)
    wuv_blk = jnp.einsum('rhd,hg->hrgd', w_uv[0], eye_h).reshape(N_HEADS * KV_LORA, N_HEADS * V_HEAD).astype(BF16)
    cos_sq = jnp.tile(jnp.tile(cos_s, (1, N_HEADS)), (dec_b, 1))
    sin_sq = jnp.tile(jnp.tile(sin_s, (1, N_HEADS)), (dec_b, 1))
    ql, r1, r2 = _sample_q(cq_all[t_p:], wq_s, wuk_blk, cos_sq, sin_sq)
    rows = dec_s * N_HEADS
    ql3 = ql.reshape(dec_b, rows, KV_LORA)
    qr3 = jnp.concatenate([r1.reshape(dec_b, dec_s, N_HEADS, ROPE_HALF),
                           r2.reshape(dec_b, dec_s, N_HEADS, ROPE_HALF)], axis=-1).reshape(dec_b, rows, QK_ROPE)
    new_pad = ((0, 0), (0, 8 - dec_s), (0, 0))
    ckv_s = jnp.pad(ckv_all[t_p:].reshape(dec_b, dec_s, KV_LORA), new_pad)
    kr_s = jnp.pad(kr_all[t_p:].reshape(dec_b, dec_s, QK_ROPE), new_pad)
    o_lat = _sample_attn(page_table, cache_ckv[0], cache_krope[0], ql3, qr3, ckv_s, kr_s)
    o_s = _matmul(o_lat.reshape(t_s, N_HEADS * KV_LORA), wuv_blk, BF16)

    x1_all, h2_all = _merge(x_all, (sh1, sc1, gt1, sh2, sc2), g1, g2, a_all, ot, o_s, wg,
                            w_out_a[0].astype(BF16), w_out_b[0].astype(BF16), w_o[0].astype(BF16),
                            grp_of_tile, n_prompt_tiles)

    wr_t = w_router[0].T
    whi = wr_t.astype(BF16)
    wlo = (wr_t - whi.astype(F32)).astype(BF16)
    idx_t, wt_t = _route(x1_all, sh2, sc2, g2, whi, wlo, router_bias[0].reshape(-1, 1), grp_of_tile)
    n_assign = t_all * TOP_K
    e_flat = idx_t.reshape(1, n_assign)
    rank, counts = _rank(e_flat)
    counts = counts[:, 0].astype(jnp.int32)
    blk = MOE_BLOCK
    padded = (counts + blk - 1) // blk * blk
    pad_end = jnp.cumsum(padded)
    pad_start = pad_end - padded
    dest = pad_start[e_flat[0]] + rank[0]
    n_blocks = n_assign // blk + N_EXPERTS
    n_slots = n_blocks * blk
    tok = jnp.tile(jnp.arange(t_all, dtype=jnp.int32), TOP_K)
    slot_tok = jnp.zeros((n_slots,), jnp.int32).at[dest].set(tok)
    slot_w = jnp.zeros((n_slots,), F32).at[dest].set(wt_t.reshape(-1))
    block_e = jnp.minimum(jnp.searchsorted(pad_end, jnp.arange(n_blocks) * blk, side='right'),
                          N_EXPERTS - 1).astype(jnp.int32)
    n_valid = (pad_end[-1] // blk).astype(jnp.int32).reshape(1)
    x_sorted = h2_all[slot_tok]
    y_sorted = _moe(block_e, n_valid, x_sorted, slot_w.reshape(n_slots, 1), w_exp_gu[0], w_exp_down[0])
    y_moe = jnp.sum(y_sorted[dest.reshape(TOP_K, t_all)].astype(F32), axis=0)

    y_all = _final(x1_all, h2_all, y_moe, gt2, w_sh_gu[0].astype(BF16), w_sh_down[0].astype(BF16),
                   row1(final_norm_g), grp_of_tile)

    y_prompt = y_all[:t_p].reshape(batch, seq, D_MODEL)
    y_sample = y_all[t_p:].reshape(dec_b, dec_s, D_MODEL)
    new_ckv_prompt = ckv_all[:t_p].reshape(1, batch, seq, KV_LORA)
    new_krope_prompt = kr_all[:t_p].reshape(1, batch, seq, QK_ROPE)
    new_ckv_sample = ckv_all[t_p:].reshape(1, dec_b, dec_s, KV_LORA)
    new_krope_sample = kr_all[t_p:].reshape(1, dec_b, dec_s, QK_ROPE)
    new_chunk_v_sample = v_all[t_p:].reshape(1, dec_b, dec_s, A_WIDTH)
    return (y_prompt, y_sample, new_ckv_prompt, new_krope_prompt, new_ckv_sample, new_krope_sample,
            new_chunk_v_sample)
```

```python
import functools

import jax
import jax.numpy as jnp
from jax import lax
from jax.experimental import pallas as pl
from jax.experimental.pallas import tpu as pltpu
from jax.experimental.pallas import tpu_sc as plsc

F32 = jnp.float32
BF16 = jnp.bfloat16

D_MODEL = 1024
A_WIDTH = D_MODEL // 2
A_GROUPS = 8
A_GROUP_DIM = A_WIDTH // A_GROUPS
CHUNK = 128
N_HEADS = 8
QK_NOPE = 64
QK_ROPE = 32
ROPE_HALF = QK_ROPE // 2
V_HEAD = 64
Q_LORA = 384
KV_LORA = 256
ROPE_THETA = 10000.0
ATTN_SCALE = (QK_NOPE + QK_ROPE) ** -0.5
N_EXPERTS = 256
TOP_K = 8
N_GROUPS = 8
GROUP_SIZE = N_EXPERTS // N_GROUPS
TOPK_GROUPS = 4
EXPERT_DIM = 256
SHARED_DIM = 256
ROUTED_SCALE = 2.5
EPS = 1e-6

LANES = 128
HEAD_PAD = 128
ROW_TILE = 256
ATTN_KV_TILE = 256
ATTN_Q_STEP = 1024
RANK_TILE = 512
MOE_BLOCK = 256
MOE_SUB = 128
PAGES_PER_STEP = 16
PAGE_GROUP = 4
SC_WINDOW = 128
SC_SPLIT = 2
PIECE = D_MODEL // 2 // SC_SPLIT
VMEM_LIMIT = 48 * 1024 * 1024

IN_U, IN_V, IN_CQ, IN_CKV, IN_KR, IN_END = 0, 512, 1024, 1408, 1664, 1792

NT_DIMS = (((1,), (1,)), ((), ()))
TN_DIMS = (((0,), (0,)), ((), ()))


def _params(sem, vmem=VMEM_LIMIT):
    return pltpu.CompilerParams(dimension_semantics=sem, vmem_limit_bytes=vmem)


def _rms(x, g):
    return x * lax.rsqrt(jnp.mean(x * x, axis=-1, keepdims=True) + EPS) * g


def _gelu(x):
    return 0.5 * x * (1.0 + jnp.tanh(0.7978845608028654 * (x + 0.044715 * (x * x * x))))


def _sigmoid(x):
    return 1.0 / (1.0 + jnp.exp(-x))


def _silu(x):
    return x * _sigmoid(x)


def _pack_rows(x):
    pieces = []
    for c in range(SC_SPLIT):
        lo = x[:, 2 * c * PIECE:(2 * c + 1) * PIECE].astype(BF16).astype(F32)
        hi = x[:, (2 * c + 1) * PIECE:(2 * c + 2) * PIECE].astype(BF16).astype(F32)
        pieces.append(lax.bitcast_convert_type(hi, jnp.int32)
                      | lax.shift_right_logical(lax.bitcast_convert_type(lo, jnp.int32), 16))
    return pieces


def _unpack_rows(pieces):
    cols = []
    for p in pieces:
        cols.append(lax.bitcast_convert_type(lax.shift_left(p, 16), F32))
        cols.append(lax.bitcast_convert_type(p & jnp.int32(-65536), F32))
    return jnp.concatenate(cols, axis=1)


def _ada_kernel(c_ref, w_ref, b_ref, o_ref):
    s = _silu(c_ref[...]).astype(BF16)
    o_ref[...] = jnp.dot(s, w_ref[...], preferred_element_type=F32) + b_ref[...]


def _ada(c, w, b):
    rows, n = c.shape[0], w.shape[1]
    tn = 1536
    return pl.pallas_call(
        _ada_kernel,
        grid=(n // tn,),
        in_specs=[pl.BlockSpec((rows, D_MODEL), lambda j: (0, 0)),
                  pl.BlockSpec((D_MODEL, tn), lambda j: (0, j)),
                  pl.BlockSpec((1, tn), lambda j: (0, j))],
        out_specs=pl.BlockSpec((rows, tn), lambda j: (0, j)),
        out_shape=jax.ShapeDtypeStruct((rows, n), F32),
        compiler_params=_params(("parallel",)),
        name="ada",
    )(c, w, b)


def _in_kernel(x_ref, sh_ref, sc_ref, g1_ref, w_ref, vg_ref, qg_ref, kg_ref, cos_ref, sin_ref,
               wc_ref, bc_ref, a_ref, v_ref, cq_ref, ckv_ref, kr_ref):
    tm = x_ref.shape[0]
    h = _rms(x_ref[...], g1_ref[...]) * (1.0 + sc_ref[...]) + sh_ref[...]
    z = jnp.dot(h.astype(BF16), w_ref[...], preferred_element_type=F32)
    u = _gelu(z[:, IN_U:IN_V])
    v = _rms(_gelu(z[:, IN_V:IN_CQ]), vg_ref[...])
    v_ref[...] = v
    cq_ref[...] = _rms(z[:, IN_CQ:IN_CKV], qg_ref[...]).astype(BF16)
    ckv_ref[...] = _rms(z[:, IN_CKV:IN_KR], kg_ref[...])
    kr = z[:, IN_KR:IN_END]
    lane = lax.broadcasted_iota(jnp.int32, kr.shape, 1)
    swapped = jnp.where(lane < ROPE_HALF, pltpu.roll(kr, LANES - ROPE_HALF, 1), pltpu.roll(kr, ROPE_HALF, 1))
    kr_rot = kr * cos_ref[...] + swapped * sin_ref[...]
    kr_ref[...] = kr_rot[:, :QK_ROPE]
    vb = v.astype(BF16)
    half = A_WIDTH // 2
    grp_of_lane = lax.broadcasted_iota(jnp.int32, (CHUNK, half), 1) // A_GROUP_DIM
    for ci in range(tm // CHUNK):
        rows = slice(ci * CHUNK, (ci + 1) * CHUNK)
        parts = []
        for q in range(2):
            vq = vb[rows, q * half:(q + 1) * half]
            acc = None
            for gg in range(A_GROUPS // 2):
                vm = jnp.where(grp_of_lane == gg, vq, jnp.zeros_like(vq))
                part = jnp.dot(wc_ref[q * (A_GROUPS // 2) + gg], vm, preferred_element_type=F32)
                acc = part if acc is None else acc + part
            parts.append(acc)
        sp = jnp.concatenate(parts, axis=1) + bc_ref[...]
        a_ref[rows, :] = (u[rows, :] * sp).astype(BF16)


def _in_proj(x_all, sh1, sc1, g1, w5, vg, qg, kg, cos_kr, sin_kr, wc, bc, grp_of_tile, n_prompt_tiles):
    t_all = x_all.shape[0]
    tm = ROW_TILE
    n_tiles = t_all // tm
    row = lambda i: (i, 0)
    fixed2 = lambda i: (0, 0)
    kind = lambda i: (jnp.where(i < n_prompt_tiles, 0, 1), 0, 0)
    kind4 = lambda i: (jnp.where(i < n_prompt_tiles, 0, 1), 0, 0, 0)
    mod = lambda i: (grp_of_tile(i), 0, 0)
    return pl.pallas_call(
        _in_kernel,
        grid=(n_tiles,),
        in_specs=[pl.BlockSpec((tm, D_MODEL), row),
                  pl.BlockSpec((None, tm, D_MODEL), mod),
                  pl.BlockSpec((None, tm, D_MODEL), mod),
                  pl.BlockSpec((1, D_MODEL), fixed2),
                  pl.BlockSpec((D_MODEL, IN_END), fixed2),
                  pl.BlockSpec((1, A_WIDTH), fixed2),
                  pl.BlockSpec((1, Q_LORA), fixed2),
                  pl.BlockSpec((1, KV_LORA), fixed2),
                  pl.BlockSpec((tm, LANES), row),
                  pl.BlockSpec((tm, LANES), row),
                  pl.BlockSpec((None, A_GROUPS, CHUNK, CHUNK), kind4),
                  pl.BlockSpec((None, CHUNK, A_WIDTH), kind)],
        out_specs=[pl.BlockSpec((tm, A_WIDTH), row),
                   pl.BlockSpec((tm, A_WIDTH), row),
                   pl.BlockSpec((tm, Q_LORA), row),
                   pl.BlockSpec((tm, KV_LORA), row),
                   pl.BlockSpec((tm, QK_ROPE), row)],
        out_shape=[jax.ShapeDtypeStruct((t_all, A_WIDTH), BF16),
                   jax.ShapeDtypeStruct((t_all, A_WIDTH), F32),
                   jax.ShapeDtypeStruct((t_all, Q_LORA), BF16),
                   jax.ShapeDtypeStruct((t_all, KV_LORA), F32),
                   jax.ShapeDtypeStruct((t_all, QK_ROPE), F32)],
        compiler_params=_params(("parallel",)),
        name="in_proj",
    )(x_all, sh1, sc1, g1, w5, vg, qg, kg, cos_kr, sin_kr, wc, bc)


def _qkv_kernel(cq_ref, ckv_ref, kr_ref, wuqt_ref, wukp_ref, place_ref, wuvt_ref, cos_ref, sin_ref,
                qt_ref, k_ref, vt_ref):
    tm = cq_ref.shape[0]
    ckv = ckv_ref[...].astype(BF16)
    kr = kr_ref[...].astype(BF16)
    qt = lax.dot_general(wuqt_ref[...], cq_ref[...], NT_DIMS, preferred_element_type=F32)
    c = cos_ref[...]
    s = sin_ref[...]
    pad = jnp.zeros((HEAD_PAD - QK_NOPE - QK_ROPE, tm), F32)
    for h in range(N_HEADS):
        blk = qt[h * HEAD_PAD:(h + 1) * HEAD_PAD]
        x1 = blk[QK_NOPE:QK_NOPE + ROPE_HALF]
        x2 = blk[QK_NOPE + ROPE_HALF:QK_NOPE + QK_ROPE]
        full = jnp.concatenate([blk[:QK_NOPE], x1 * c - x2 * s, x1 * s + x2 * c, pad], axis=0)
        qt_ref[h] = (full * ATTN_SCALE).astype(BF16)
    k = (jnp.dot(ckv, wukp_ref[...], preferred_element_type=F32)
         + jnp.dot(kr, place_ref[...], preferred_element_type=F32))
    for h in range(N_HEADS):
        k_ref[h] = k[:, h * HEAD_PAD:(h + 1) * HEAD_PAD].astype(BF16)
    vt = lax.dot_general(wuvt_ref[...], ckv, NT_DIMS, preferred_element_type=F32)
    for h in range(N_HEADS):
        vt_ref[h] = vt[h * V_HEAD:(h + 1) * V_HEAD].astype(BF16)


def _qkv(cq_all, ckv_all, kr_all, wuqt, wukp, place, wuvt, cos_t, sin_t, batch, seq):
    tm = ATTN_KV_TILE
    nk = seq // tm
    row = lambda b, j: (b * nk + j, 0)
    fixed2 = lambda b, j: (0, 0)
    tab = lambda b, j: (0, j)
    return pl.pallas_call(
        _qkv_kernel,
        grid=(batch, nk),
        in_specs=[pl.BlockSpec((tm, Q_LORA), row),
                  pl.BlockSpec((tm, KV_LORA), row),
                  pl.BlockSpec((tm, QK_ROPE), row),
                  pl.BlockSpec(wuqt.shape, fixed2),
                  pl.BlockSpec(wukp.shape, fixed2),
                  pl.BlockSpec(place.shape, fixed2),
                  pl.BlockSpec(wuvt.shape, fixed2),
                  pl.BlockSpec((ROPE_HALF, tm), tab),
                  pl.BlockSpec((ROPE_HALF, tm), tab)],
        out_specs=[pl.BlockSpec((None, N_HEADS, HEAD_PAD, tm), lambda b, j: (b, 0, 0, j)),
                   pl.BlockSpec((None, N_HEADS, tm, HEAD_PAD), lambda b, j: (b, 0, j, 0)),
                   pl.BlockSpec((None, N_HEADS, None, V_HEAD, tm), lambda b, j: (b, 0, j, 0, 0))],
        out_shape=[jax.ShapeDtypeStruct((batch, N_HEADS, HEAD_PAD, seq), BF16),
                   jax.ShapeDtypeStruct((batch, N_HEADS, seq, HEAD_PAD), BF16),
                   jax.ShapeDtypeStruct((batch, N_HEADS, nk, V_HEAD, tm), BF16)],
        compiler_params=_params(("parallel", "parallel")),
        name="qkv",
    )(cq_all, ckv_all, kr_all, wuqt, wukp, place, wuvt, cos_t, sin_t)


def _attn_kernel(qt_ref, k_ref, vt_ref, o_ref):
    tk = ATTN_KV_TILE
    n_sub = o_ref.shape[1] // tk
    qi = pl.program_id(2)

    j0 = qi * n_sub

    def tiles(kb, vb, subs, carries, diagonal_sub):
        scores = [jnp.dot(kb, qt_ref[:, sub * tk:(sub + 1) * tk], preferred_element_type=F32) for sub in subs]
        stats = []
        for sub, s in zip(subs, scores):
            m, l, _ = carries[sub]
            if sub == diagonal_sub:
                key = lax.broadcasted_iota(jnp.int32, s.shape, 0)
                qry = lax.broadcasted_iota(jnp.int32, s.shape, 1)
                s = jnp.where(key <= qry, s, -jnp.inf)
            m_new = jnp.maximum(m, jnp.max(s, axis=0, keepdims=True))
            alpha = jnp.exp(m - m_new)
            p = jnp.exp(s - m_new)
            stats.append((m_new, alpha * l + jnp.sum(p, axis=0, keepdims=True), alpha, p.astype(BF16)))
        out = list(carries)
        for sub, (m_new, l_new, alpha, p) in zip(subs, stats):
            out[sub] = (m_new, l_new, alpha * carries[sub][2] + jnp.dot(vb, p, preferred_element_type=F32))
        return out

    def body(j, carries):
        return tuple(tiles(k_ref[j], vt_ref[j], range(n_sub), carries, None))

    init = tuple((jnp.full((1, tk), -jnp.inf, F32), jnp.zeros((1, tk), F32), jnp.zeros((V_HEAD, tk), F32))
                 for _ in range(n_sub))
    carries = list(lax.fori_loop(0, j0, body, init))
    for jj in range(n_sub):
        carries = tiles(k_ref[j0 + jj], vt_ref[j0 + jj], range(jj, n_sub), carries, jj)
    for sub in range(n_sub):
        _, l, acc = carries[sub]
        o_ref[:, sub * tk:(sub + 1) * tk] = (acc / l).astype(BF16)


def _attn(qt, k, vt):
    batch, _, _, seq = qt.shape
    tk = ATTN_KV_TILE
    nk = seq // tk
    k5 = k.reshape(batch, N_HEADS, nk, tk, HEAD_PAD)
    return pl.pallas_call(
        _attn_kernel,
        grid=(batch, N_HEADS, seq // ATTN_Q_STEP),
        in_specs=[pl.BlockSpec((None, None, HEAD_PAD, ATTN_Q_STEP), lambda b, h, i: (b, h, 0, i)),
                  pl.BlockSpec((None, None, nk, tk, HEAD_PAD), lambda b, h, i: (b, h, 0, 0, 0)),
                  pl.BlockSpec((None, None, nk, V_HEAD, tk), lambda b, h, i: (b, h, 0, 0, 0))],
        out_specs=pl.BlockSpec((None, V_HEAD, ATTN_Q_STEP), lambda b, h, i: (b, h, i)),
        out_shape=jax.ShapeDtypeStruct((batch, N_HEADS * V_HEAD, seq), BF16),
        compiler_params=_params(("parallel", "parallel", "parallel")),
        name="attn",
    )(qt, k5, vt)


def _sq_kernel(cq_ref, wq_ref, wukb_ref, cos_ref, sin_ref, ql_ref, r1_ref, r2_ref):
    n_nope = N_HEADS * QK_NOPE
    q = jnp.dot(cq_ref[...], wq_ref[...], preferred_element_type=F32)
    x1 = q[:, n_nope:n_nope + LANES]
    x2 = q[:, n_nope + LANES:]
    c = cos_ref[...]
    s = sin_ref[...]
    r1_ref[...] = ((x1 * c - x2 * s) * ATTN_SCALE).astype(BF16)
    r2_ref[...] = ((x1 * s + x2 * c) * ATTN_SCALE).astype(BF16)
    ql = jnp.dot(q[:, :n_nope].astype(BF16), wukb_ref[...], preferred_element_type=F32)
    ql_ref[...] = (ql * ATTN_SCALE).astype(BF16)


def _sample_q(cq_s, wq_s, wuk_blk, cos_s, sin_s):
    ts = cq_s.shape[0]
    full = lambda shape: pl.BlockSpec(shape, lambda i: (0,) * len(shape))
    return pl.pallas_call(
        _sq_kernel,
        grid=(1,),
        in_specs=[full(cq_s.shape), full(wq_s.shape), full(wuk_blk.shape), full(cos_s.shape), full(sin_s.shape)],
        out_specs=[full((ts, N_HEADS * KV_LORA)), full((ts, LANES)), full((ts, LANES))],
        out_shape=[jax.ShapeDtypeStruct((ts, N_HEADS * KV_LORA), BF16),
                   jax.ShapeDtypeStruct((ts, LANES), BF16),
                   jax.ShapeDtypeStruct((ts, LANES), BF16)],
        compiler_params=_params(("arbitrary",)),
        name="sample_q",
    )(cq_s, wq_s, wuk_blk, cos_s, sin_s)


def _sattn_kernel(pt_ref, *refs):
    npg = PAGES_PER_STEP
    ckv_pages = refs[:npg]
    kr_pages = refs[npg:2 * npg]
    ql_ref, qr_ref, cnew_ref, knew_ref, o_ref, m_ref, l_ref, acc_ref = refs[2 * npg:]
    j = pl.program_id(1)

    @pl.when(j == 0)
    def _():
        m_ref[...] = jnp.full(m_ref.shape, -jnp.inf, F32)
        l_ref[...] = jnp.zeros(l_ref.shape, F32)
        acc_ref[...] = jnp.zeros(acc_ref.shape, F32)

    ql = ql_ref[...]
    qr = qr_ref[...]

    def partial_softmax(blocks, mask=None):
        scores = [lax.dot_general(ql, keys, NT_DIMS, preferred_element_type=F32)
                  + jnp.dot(qr, rope_t, preferred_element_type=F32) for keys, rope_t in blocks]
        probs = []
        for s in scores:
            if mask is not None:
                s = jnp.where(mask(s.shape), s, -jnp.inf)
            m = jnp.max(s, axis=-1, keepdims=True)
            p = jnp.exp(s - m)
            probs.append((m, jnp.sum(p, axis=-1, keepdims=True), p.astype(BF16)))
        return [(m, l, jnp.dot(p, keys, preferred_element_type=F32)) for (m, l, p), (keys, _) in zip(probs, blocks)]

    def merge(parts):
        m_old = m_ref[...]
        m_new = m_old
        for m, _, _ in parts:
            m_new = jnp.maximum(m_new, m)
        alpha = jnp.exp(m_old - m_new)
        l = alpha * l_ref[...]
        acc = alpha * acc_ref[...]
        for m, lp, op in parts:
            w = jnp.exp(m - m_new)
            l = l + w * lp
            acc = acc + w * op
        m_ref[...] = m_new
        l_ref[...] = l
        acc_ref[...] = acc

    blocks = []
    for g in range(npg // PAGE_GROUP):
        pages = range(g * PAGE_GROUP, (g + 1) * PAGE_GROUP)
        blocks.append((jnp.concatenate([ckv_pages[p][...].astype(BF16) for p in pages], axis=0),
                       jnp.concatenate([kr_pages[p][...].astype(BF16) for p in pages], axis=1)))
    merge(partial_softmax(blocks))

    @pl.when(j == pl.num_programs(1) - 1)
    def _():
        def causal(shape):
            q_pos = lax.broadcasted_iota(jnp.int32, shape, 0) // N_HEADS
            return lax.broadcasted_iota(jnp.int32, shape, 1) <= q_pos

        merge(partial_softmax([(cnew_ref[...].astype(BF16), knew_ref[...].astype(BF16))], causal))
        o_ref[...] = acc_ref[...] / l_ref[...]


def _sample_attn(page_table, cache_ckv, cache_kr_t, ql, qr, cnew, knew_t):
    dec_b, n_pages = page_table.shape
    npg = PAGES_PER_STEP
    n_steps = n_pages // npg
    rows = ql.shape[1]
    n_new = cnew.shape[1]

    def page_map(p):
        return lambda b, j, pt: (pt[b * n_pages + j * npg + p], 0, 0)

    per_b = lambda b, j, pt: (b, 0, 0)
    in_specs = ([pl.BlockSpec((None, CHUNK, KV_LORA), page_map(p)) for p in range(npg)]
                + [pl.BlockSpec((None, QK_ROPE, CHUNK), page_map(p)) for p in range(npg)]
                + [pl.BlockSpec((None, rows, KV_LORA), per_b),
                   pl.BlockSpec((None, rows, QK_ROPE), per_b),
                   pl.BlockSpec((None, n_new, KV_LORA), per_b),
                   pl.BlockSpec((None, QK_ROPE, n_new), per_b)])
    return pl.pallas_call(
        _sattn_kernel,
        grid_spec=pltpu.PrefetchScalarGridSpec(
            num_scalar_prefetch=1,
            grid=(dec_b, n_steps),
            in_specs=in_specs,
            out_specs=pl.BlockSpec((None, rows, KV_LORA), per_b),
            scratch_shapes=[pltpu.VMEM((rows, 1), F32),
                            pltpu.VMEM((rows, 1), F32),
                            pltpu.VMEM((rows, KV_LORA), F32)]),
        out_shape=jax.ShapeDtypeStruct((dec_b, rows, KV_LORA), F32),
        compiler_params=_params(("parallel", "arbitrary")),
        name="sample_attn",
    )(page_table.reshape(-1), *([cache_ckv] * npg), *([cache_kr_t] * npg), ql, qr, cnew, knew_t)


def _mm_kernel(x_ref, w_ref, o_ref):
    o_ref[...] = jnp.dot(x_ref[...].astype(BF16), w_ref[...], preferred_element_type=F32).astype(o_ref.dtype)


def _matmul(x, w, out_dtype):
    m, n = x.shape[0], w.shape[1]
    full = lambda shape: pl.BlockSpec(shape, lambda i: (0,) * len(shape))
    return pl.pallas_call(
        _mm_kernel,
        grid=(1,),
        in_specs=[full(x.shape), full(w.shape)],
        out_specs=full((m, n)),
        out_shape=jax.ShapeDtypeStruct((m, n), out_dtype),
        compiler_params=_params(("arbitrary",)),
        name="matmul",
    )(x, w)


def _merge_kernel(n_prompt_tiles, x_ref, sh1_ref, sc1_ref, gt1_ref, sh2_ref, sc2_ref, g1_ref, g2_ref,
                  a_ref, ot_ref, os_ref, wg_ref, woa_ref, wob_ref, wo_ref, x1_ref, *rest):
    h2_refs, yb_ref = rest[:SC_SPLIT], rest[SC_SPLIT]
    i = pl.program_id(0)
    x = x_ref[...]
    h = _rms(x, g1_ref[...]) * (1.0 + sc1_ref[...]) + sh1_ref[...]
    gates = _sigmoid(jnp.dot(h.astype(BF16), wg_ref[...], preferred_element_type=F32))
    y_a = jnp.dot(a_ref[...], woa_ref[...], preferred_element_type=F32)

    @pl.when(i < n_prompt_tiles)
    def _():
        yb_ref[...] = lax.dot_general(ot_ref[...], wob_ref[...], TN_DIMS, preferred_element_type=F32)

    @pl.when(i >= n_prompt_tiles)
    def _():
        yb_ref[...] = jnp.dot(os_ref[...], wob_ref[...], preferred_element_type=F32)

    z = gates[:, :D_MODEL] * y_a + gates[:, D_MODEL:] * yb_ref[...]
    y = jnp.dot(z.astype(BF16), wo_ref[...], preferred_element_type=F32)
    x1 = x + gt1_ref[...] * y
    x1_ref[...] = x1
    for ref, piece in zip(h2_refs, _pack_rows(_rms(x1, g2_ref[...]) * (1.0 + sc2_ref[...]) + sh2_ref[...])):
        ref[...] = piece


def _merge(x_all, mods, g1, g2, a_all, ot, o_s, wg, woa, wob, wo, grp_of_tile, n_prompt_tiles):
    t_all = x_all.shape[0]
    tm = ROW_TILE
    seq = ot.shape[2]
    tpb = seq // tm
    row = lambda i: (i, 0)
    fixed2 = lambda i: (0, 0)
    mod = lambda i: (grp_of_tile(i), 0, 0)

    def ot_map(i):
        ic = jnp.minimum(i, n_prompt_tiles - 1)
        return (ic // tpb, 0, ic % tpb)

    os_map = lambda i: (jnp.maximum(i - n_prompt_tiles, 0), 0)
    mod_spec = pl.BlockSpec((None, tm, D_MODEL), mod)
    return pl.pallas_call(
        functools.partial(_merge_kernel, n_prompt_tiles),
        grid=(t_all // tm,),
        in_specs=[pl.BlockSpec((tm, D_MODEL), row),
                  mod_spec, mod_spec, mod_spec, mod_spec, mod_spec,
                  pl.BlockSpec((1, D_MODEL), fixed2),
                  pl.BlockSpec((1, D_MODEL), fixed2),
                  pl.BlockSpec((tm, A_WIDTH), row),
                  pl.BlockSpec((None, N_HEADS * V_HEAD, tm), ot_map),
                  pl.BlockSpec((tm, N_HEADS * V_HEAD), os_map),
                  pl.BlockSpec(wg.shape, fixed2),
                  pl.BlockSpec(woa.shape, fixed2),
                  pl.BlockSpec(wob.shape, fixed2),
                  pl.BlockSpec(wo.shape, fixed2)],
        out_specs=[pl.BlockSpec((tm, D_MODEL), row)] + [pl.BlockSpec((tm, PIECE), row)] * SC_SPLIT,
        out_shape=[jax.ShapeDtypeStruct((t_all, D_MODEL), F32)]
        + [jax.ShapeDtypeStruct((t_all, PIECE), jnp.int32)] * SC_SPLIT,
        scratch_shapes=[pltpu.VMEM((tm, D_MODEL), F32)],
        compiler_params=_params(("parallel",)),
        name="merge",
    )(x_all, *mods, g1, g2, a_all, ot, o_s, wg, woa, wob, wo)


def _first_argmax(v, rows):
    mx = jnp.max(v, axis=0, keepdims=True)
    idx = jnp.min(jnp.where(v == mx, rows, v.shape[0]), axis=0, keepdims=True)
    return mx, idx


def _route_kernel(x1_ref, sh2_ref, sc2_ref, g2_ref, whi_ref, wlo_ref, bias_ref, idx_ref, wt_ref):
    h2 = _rms(x1_ref[...], g2_ref[...]) * (1.0 + sc2_ref[...]) + sh2_ref[...]
    hi = h2.astype(BF16)
    lo = (h2 - hi.astype(F32)).astype(BF16)
    whi = whi_ref[...]
    logits = (lax.dot_general(whi, hi, NT_DIMS, preferred_element_type=F32)
              + lax.dot_general(whi, lo, NT_DIMS, preferred_element_type=F32)
              + lax.dot_general(wlo_ref[...], hi, NT_DIMS, preferred_element_type=F32))
    scores = _sigmoid(logits)
    sel = scores + bias_ref[...]
    tm = sel.shape[1]
    neg = -jnp.inf
    rows_g = lax.broadcasted_iota(jnp.int32, (GROUP_SIZE, tm), 0)
    gscore = []
    for g in range(N_GROUPS):
        blk = sel[g * GROUP_SIZE:(g + 1) * GROUP_SIZE]
        m1, i1 = _first_argmax(blk, rows_g)
        m2 = jnp.max(jnp.where(rows_g == i1, neg, blk), axis=0, keepdims=True)
        gscore.append(m1 + m2)
    gs = jnp.concatenate(gscore, axis=0)
    rows_8 = lax.broadcasted_iota(jnp.int32, gs.shape, 0)
    chosen = jnp.zeros(gs.shape, jnp.int32)
    for _ in range(TOPK_GROUPS):
        _, gi = _first_argmax(gs, rows_8)
        hit = rows_8 == gi
        chosen = jnp.where(hit, 1, chosen)
        gs = jnp.where(hit, neg, gs)
    cand = jnp.concatenate(
        [jnp.where(chosen[g:g + 1] > 0, sel[g * GROUP_SIZE:(g + 1) * GROUP_SIZE], neg) for g in range(N_GROUPS)],
        axis=0)
    rows_e = lax.broadcasted_iota(jnp.int32, cand.shape, 0)
    idxs, wts = [], []
    for _ in range(TOP_K):
        _, ei = _first_argmax(cand, rows_e)
        hit = rows_e == ei
        idxs.append(ei)
        wts.append(jnp.sum(jnp.where(hit, scores, 0.0), axis=0, keepdims=True))
        cand = jnp.where(hit, neg, cand)
    w = jnp.concatenate(wts, axis=0)
    idx_ref[...] = jnp.concatenate(idxs, axis=0)
    wt_ref[...] = w / jnp.sum(w, axis=0, keepdims=True) * ROUTED_SCALE


def _route(x1_all, sh2, sc2, g2, whi, wlo, bias_col, grp_of_tile):
    t_all = x1_all.shape[0]
    tm = ROW_TILE
    fixed2 = lambda i: (0, 0)
    mod = lambda i: (grp_of_tile(i), 0, 0)
    return pl.pallas_call(
        _route_kernel,
        grid=(t_all // tm,),
        in_specs=[pl.BlockSpec((tm, D_MODEL), lambda i: (i, 0)),
                  pl.BlockSpec((None, tm, D_MODEL), mod),
                  pl.BlockSpec((None, tm, D_MODEL), mod),
                  pl.BlockSpec((1, D_MODEL), fixed2),
                  pl.BlockSpec(whi.shape, fixed2),
                  pl.BlockSpec(wlo.shape, fixed2),
                  pl.BlockSpec(bias_col.shape, fixed2)],
        out_specs=[pl.BlockSpec((TOP_K, tm), lambda i: (0, i)), pl.BlockSpec((TOP_K, tm), lambda i: (0, i))],
        out_shape=[jax.ShapeDtypeStruct((TOP_K, t_all), jnp.int32),
                   jax.ShapeDtypeStruct((TOP_K, t_all), F32)],
        compiler_params=_params(("parallel",)),
        name="route",
    )(x1_all, sh2, sc2, g2, whi, wlo, bias_col)


def _rank_kernel(e_ref, upper_ref, rank_ref, count_ref, run_ref):
    @pl.when(pl.program_id(0) == 0)
    def _():
        run_ref[...] = jnp.zeros(run_ref.shape, F32)

    e = e_ref[...]
    n = e.shape[1]
    rows = lax.broadcasted_iota(jnp.int32, (N_EXPERTS, n), 0)
    onehot = rows == e
    oh = jnp.where(onehot, 1.0, 0.0).astype(BF16)
    before = jnp.dot(oh, upper_ref[...], preferred_element_type=F32)
    run = run_ref[...]
    prior = jnp.concatenate([run] * (n // LANES), axis=1)
    rank = jnp.sum(jnp.where(onehot, before + prior, 0.0), axis=0, keepdims=True)
    rank_ref[...] = rank.astype(jnp.int32)
    total = run + jnp.dot(oh, jnp.ones((n, LANES), BF16), preferred_element_type=F32)
    run_ref[...] = total
    count_ref[...] = total


def _rank(e_flat):
    n_assign = e_flat.shape[1]
    n = RANK_TILE
    upper = (lax.broadcasted_iota(jnp.int32, (n, n), 0) < lax.broadcasted_iota(jnp.int32, (n, n), 1)).astype(BF16)
    return pl.pallas_call(
        _rank_kernel,
        grid=(n_assign // n,),
        in_specs=[pl.BlockSpec((1, n), lambda i: (0, i)), pl.BlockSpec((n, n), lambda i: (0, 0))],
        out_specs=[pl.BlockSpec((1, n), lambda i: (0, i)), pl.BlockSpec((N_EXPERTS, LANES), lambda i: (0, 0))],
        out_shape=[jax.ShapeDtypeStruct((1, n_assign), jnp.int32),
                   jax.ShapeDtypeStruct((N_EXPERTS, LANES), F32)],
        scratch_shapes=[pltpu.VMEM((N_EXPERTS, LANES), F32)],
        compiler_params=_params(("arbitrary",)),
        name="rank",
    )(e_flat, upper)


def _moe_kernel(be_ref, nv_ref, *refs):
    x_refs = refs[:SC_SPLIT]
    wgu_ref, wdn_ref = refs[SC_SPLIT:SC_SPLIT + 2]
    y_refs = refs[SC_SPLIT + 2:2 * SC_SPLIT + 2]
    gu_bf, dn_bf = refs[2 * SC_SPLIT + 2:]
    i = pl.program_id(0)
    prev = be_ref[jnp.maximum(i - 1, 0)]

    @pl.when(i < nv_ref[0])
    def _():
        @pl.when((i == 0) | (be_ref[i] != prev))
        def _():
            gu_bf[...] = wgu_ref[...].astype(BF16)
            dn_bf[...] = wdn_ref[...].astype(BF16)

        subs = [slice(s * MOE_SUB, (s + 1) * MOE_SUB) for s in range(MOE_BLOCK // MOE_SUB)]
        gus = [jnp.dot(_unpack_rows([r[rows, :] for r in x_refs]).astype(BF16), gu_bf[...],
                       preferred_element_type=F32) for rows in subs]
        hids = [(_silu(gu[:, :EXPERT_DIM]) * gu[:, EXPERT_DIM:]).astype(BF16) for gu in gus]
        ys = [jnp.dot(hid, dn_bf[...], preferred_element_type=F32) for hid in hids]
        for rows, y in zip(subs, ys):
            for ref, piece in zip(y_refs, _pack_rows(y)):
                ref[rows, :] = piece


def _moe(block_e, n_valid, x_sorted, w_gu, w_dn):
    n_slots = x_sorted[0].shape[0]
    blk = MOE_BLOCK
    n_blocks = n_slots // blk
    rows = lambda i, be, nv: (jnp.minimum(i, nv[0] - 1), 0)
    wmap = lambda i, be, nv: (be[jnp.minimum(i, nv[0] - 1)], 0, 0)
    return pl.pallas_call(
        _moe_kernel,
        grid_spec=pltpu.PrefetchScalarGridSpec(
            num_scalar_prefetch=2,
            grid=(n_blocks,),
            in_specs=[pl.BlockSpec((blk, PIECE), rows)] * SC_SPLIT
            + [pl.BlockSpec((None, D_MODEL, 2 * EXPERT_DIM), wmap),
               pl.BlockSpec((None, EXPERT_DIM, D_MODEL), wmap)],
            out_specs=[pl.BlockSpec((blk, PIECE), rows)] * SC_SPLIT,
            scratch_shapes=[pltpu.VMEM((D_MODEL, 2 * EXPERT_DIM), BF16),
                            pltpu.VMEM((EXPERT_DIM, D_MODEL), BF16)]),
        out_shape=[jax.ShapeDtypeStruct((n_slots, PIECE), jnp.int32)] * SC_SPLIT,
        compiler_params=_params(("arbitrary",)),
        name="moe",
    )(block_e, n_valid, *x_sorted, w_gu, w_dn)


def _dest_kernel(e_ref, rank_ref, start_ref, dest_ref):
    e = e_ref[...]
    rows = lax.broadcasted_iota(jnp.int32, (N_EXPERTS, e.shape[1]), 0)
    start = jnp.sum(jnp.where(rows == e, start_ref[...], 0), axis=0, keepdims=True)
    dest_ref[...] = start + rank_ref[...]


def _dest(e_flat, rank, pad_start_col):
    n_assign = e_flat.shape[1]
    n = RANK_TILE
    tile = pl.BlockSpec((1, n), lambda i: (0, i))
    return pl.pallas_call(
        _dest_kernel,
        grid=(n_assign // n,),
        in_specs=[tile, tile, pl.BlockSpec((N_EXPERTS, 1), lambda i: (0, 0))],
        out_specs=tile,
        out_shape=jax.ShapeDtypeStruct((1, n_assign), jnp.int32),
        compiler_params=_params(("parallel",)),
        name="dest",
    )(e_flat, rank, pad_start_col)


def _sc_mesh():
    return plsc.VectorSubcoreMesh(core_axis_name="core", subcore_axis_name="subcore")


def _sc_scatter_rows(src, dest, n_slots):
    n_src, width = src.shape
    n_assign = dest.shape[1]
    src_blocks = n_src // SC_WINDOW

    @functools.partial(pl.kernel, out_type=jax.ShapeDtypeStruct((n_slots, width), src.dtype),
                       mesh=_sc_mesh(), scratch_types=[])
    def scatter(src_hbm, dest_hbm, out_hbm):
        def body(rows_vmem, dest_vmem):
            pltpu.sync_copy(rows_vmem, out_hbm.at[dest_vmem.at[0]])

        pltpu.emit_pipeline(
            body,
            grid=(n_assign // SC_WINDOW,),
            in_specs=[pl.BlockSpec((SC_WINDOW, width), lambda i: (i % src_blocks, 0)),
                      pl.BlockSpec((1, SC_WINDOW), lambda i: (0, i))],
            out_specs=[],
            core_axis_name=("core", "subcore"),
            dimension_semantics=(pltpu.PARALLEL,),
        )(src_hbm, dest_hbm)

    return scatter(src, dest)


def _sc_gather_rows(table, idx):
    width = table.shape[1]
    n_assign = idx.shape[1]

    @functools.partial(pl.kernel, out_type=jax.ShapeDtypeStruct((n_assign, width), table.dtype),
                       mesh=_sc_mesh(), scratch_types=[])
    def gather(table_hbm, idx_hbm, out_hbm):
        def body(idx_vmem, rows_vmem):
            pltpu.sync_copy(table_hbm.at[idx_vmem.at[0]], rows_vmem)

        pltpu.emit_pipeline(
            body,
            grid=(n_assign // SC_WINDOW,),
            in_specs=[pl.BlockSpec((1, SC_WINDOW), lambda i: (0, i))],
            out_specs=[pl.BlockSpec((SC_WINDOW, width), lambda i: (i, 0))],
            core_axis_name=("core", "subcore"),
            dimension_semantics=(pltpu.PARALLEL,),
        )(idx_hbm, out_hbm)

    return gather(table, idx)


def _final_kernel(x1_ref, *refs):
    h2_refs = refs[:SC_SPLIT]
    yg_refs = refs[SC_SPLIT:2 * SC_SPLIT]
    wt_ref, gt2_ref, wsg_ref, wsd_ref, gf_ref, o_ref = refs[2 * SC_SPLIT:]
    h2 = _unpack_rows([r[...] for r in h2_refs]).astype(BF16)
    gu = jnp.dot(h2, wsg_ref[...], preferred_element_type=F32)
    hid = _silu(gu[:, :SHARED_DIM]) * gu[:, SHARED_DIM:]
    y = jnp.dot(hid.astype(BF16), wsd_ref[...], preferred_element_type=F32)
    wt = wt_ref[...]
    for k in range(TOP_K):
        y = y + wt[:, k:k + 1] * _unpack_rows([r[k] for r in yg_refs])
    x2 = x1_ref[...] + gt2_ref[...] * y
    o_ref[...] = _rms(x2, gf_ref[...])


def _final(x1_all, h2_all, y_gath, wt_rows, gt2, wsg, wsd, gf, grp_of_tile):
    t_all = x1_all.shape[0]
    tm = ROW_TILE
    row = lambda i: (i, 0)
    fixed2 = lambda i: (0, 0)
    return pl.pallas_call(
        _final_kernel,
        grid=(t_all // tm,),
        in_specs=[pl.BlockSpec((tm, D_MODEL), row)]
        + [pl.BlockSpec((tm, PIECE), row)] * SC_SPLIT
        + [pl.BlockSpec((TOP_K, tm, PIECE), lambda i: (0, i, 0))] * SC_SPLIT
        + [pl.BlockSpec((tm, TOP_K), row),
           pl.BlockSpec((None, tm, D_MODEL), lambda i: (grp_of_tile(i), 0, 0)),
           pl.BlockSpec(wsg.shape, fixed2),
           pl.BlockSpec(wsd.shape, fixed2),
           pl.BlockSpec((1, D_MODEL), fixed2)],
        out_specs=pl.BlockSpec((tm, D_MODEL), row),
        out_shape=jax.ShapeDtypeStruct((t_all, D_MODEL), F32),
        compiler_params=_params(("parallel",)),
        name="final",
    )(x1_all, *h2_all, *y_gath, wt_rows, gt2, wsg, wsd, gf)


def _rope_tables(pos):
    freqs = ROPE_THETA ** (-jnp.arange(ROPE_HALF, dtype=F32) / ROPE_HALF)
    ang = pos.astype(F32)[:, None] * freqs
    return jnp.cos(ang), jnp.sin(ang)


def kernel(x_prompt, x_sample, cache_ckv, cache_krope, page_table, c_prompt, c_sample, w_ada, b_ada, norm1_g, norm2_g, w_in, a_vnorm_g, w_spatial, b_spatial, w_out_a, q_norm_g, w_uq, kv_norm_g, w_uk, w_uv, w_out_b, w_o, w_router, router_bias, w_exp_gu, w_exp_down, w_sh_gu, w_sh_down, final_norm_g):
    batch, seq, _ = x_prompt.shape
    dec_b, dec_s, _ = x_sample.shape
    n_pages = page_table.shape[1]
    n_past = n_pages * CHUNK
    t_p, t_s = batch * seq, dec_b * dec_s
    t_all = t_p + t_s
    tm = ROW_TILE
    assert seq % ATTN_Q_STEP == 0 and t_s % tm == 0 and CHUNK % dec_s == 0 and n_pages % PAGES_PER_STEP == 0
    assert (t_all * TOP_K) % RANK_TILE == 0 and w_ada.shape[0] == 1
    n_prompt_tiles = t_p // tm
    tiles_per_b = seq // tm

    def grp_of_tile(i):
        return jnp.minimum(i // tiles_per_b, batch) + jnp.maximum(i - n_prompt_tiles, 0)

    n_c = batch + dec_b
    c_rows = -(-n_c // 8) * 8
    c_all = jnp.concatenate([c_prompt, c_sample, jnp.zeros((c_rows - n_c, D_MODEL), F32)], axis=0)
    mod = _ada(c_all, w_ada[0].astype(BF16), b_ada[0][None, :])
    mod_p = jnp.broadcast_to(mod[:batch, None, :], (batch, tm, 6 * D_MODEL))
    mod_s = jnp.repeat(mod[batch:n_c], dec_s, axis=0).reshape(t_s // tm, tm, 6 * D_MODEL)
    mod_g = jnp.concatenate([mod_p, mod_s], axis=0)
    sh1, sc1, gt1, sh2, sc2, gt2 = [mod_g[:, :, k * D_MODEL:(k + 1) * D_MODEL] for k in range(6)]

    o_u, o_v, o_cq, o_ckv, o_kr = A_WIDTH, 2 * A_WIDTH, 2 * A_WIDTH + Q_LORA, 2 * A_WIDTH + Q_LORA + KV_LORA, \
        2 * A_WIDTH + Q_LORA + KV_LORA + QK_ROPE
    win = w_in[0]
    w5 = jnp.concatenate([win[:, :o_kr], jnp.zeros((D_MODEL, IN_END - IN_KR - QK_ROPE), F32)], axis=1).astype(BF16)
    wg = win[:, o_kr:].astype(BF16)
    row1 = lambda v: v.reshape(1, -1)
    tri = jnp.tril(jnp.ones((CHUNK, CHUNK), F32))
    wc_p = w_spatial[0] * tri
    per = CHUNK // dec_s
    small = (w_spatial[0] * tri)[:, :dec_s, :dec_s]
    wc_s = jnp.einsum('ab,gts->gatbs', jnp.eye(per, dtype=F32), small).reshape(A_GROUPS, CHUNK, CHUNK)
    wc = jnp.stack([wc_p, wc_s]).astype(BF16)
    bias_p = jnp.repeat(b_spatial[0].T, A_GROUP_DIM, axis=1)
    bias_s = jnp.tile(bias_p[:dec_s], (per, 1))
    bc = jnp.stack([bias_p, bias_s])

    pos_p = jnp.arange(seq)
    pos_s = n_past + jnp.arange(dec_s)
    cos_p, sin_p = _rope_tables(pos_p)
    cos_s, sin_s = _rope_tables(pos_s)
    cos_rows = jnp.concatenate([jnp.tile(cos_p, (batch, 1)), jnp.tile(cos_s, (dec_b, 1))], axis=0)
    sin_rows = jnp.concatenate([jnp.tile(sin_p, (batch, 1)), jnp.tile(sin_s, (dec_b, 1))], axis=0)
    lane_pad = jnp.zeros((t_all, LANES - QK_ROPE), F32)
    cos_kr = jnp.concatenate([cos_rows, cos_rows, lane_pad], axis=1)
    sin_kr = jnp.concatenate([-sin_rows, sin_rows, lane_pad], axis=1)

    x_all = jnp.concatenate([x_prompt.reshape(t_p, D_MODEL), x_sample.reshape(t_s, D_MODEL)], axis=0)
    g1, g2 = row1(norm1_g[0]), row1(norm2_g[0])
    a_all, v_all, cq_all, ckv_all, kr_all = _in_proj(
        x_all, sh1, sc1, g1, w5, row1(a_vnorm_g[0]), row1(q_norm_g[0]), row1(kv_norm_g[0]),
        cos_kr, sin_kr, wc, bc, grp_of_tile, n_prompt_tiles)

    wuq = w_uq[0]
    wuqt = jnp.pad(wuq, ((0, 0), (0, 0), (0, HEAD_PAD - QK_NOPE - QK_ROPE))).reshape(Q_LORA, -1).T.astype(BF16)
    wukp = jnp.pad(w_uk[0], ((0, 0), (0, 0), (0, HEAD_PAD - QK_NOPE))).reshape(KV_LORA, -1).astype(BF16)
    place_h = jnp.pad(jnp.eye(QK_ROPE, dtype=F32), ((0, 0), (QK_NOPE, HEAD_PAD - QK_NOPE - QK_ROPE)))
    place = jnp.tile(place_h, (1, N_HEADS)).astype(BF16)
    wuvt = w_uv[0].reshape(KV_LORA, -1).T.astype(BF16)
    qt, k, vt = _qkv(cq_all, ckv_all, kr_all, wuqt, wukp, place, wuvt, cos_p.T, sin_p.T, batch, seq)
    ot = _attn(qt, k, vt)

    wq_s = jnp.concatenate([wuq[:, :, :QK_NOPE].reshape(Q_LORA, -1),
                            wuq[:, :, QK_NOPE:QK_NOPE + ROPE_HALF].reshape(Q_LORA, -1),
                            wuq[:, :, QK_NOPE + ROPE_HALF:].reshape(Q_LORA, -1)], axis=1).astype(BF16)
    eye_h = jnp.eye(N_HEADS, dtype=F32)
    wuk_blk = jnp.einsum('rhd,hg->hdgr', w_uk[0], eye_h).reshape(N_HEADS * QK_NOPE, N_HEADS * KV_LORA).astype(BF16)
    wuv_blk = jnp.einsum('rhd,hg->hrgd', w_uv[0], eye_h).reshape(N_HEADS * KV_LORA, N_HEADS * V_HEAD).astype(BF16)
    cos_sq = jnp.tile(jnp.tile(cos_s, (1, N_HEADS)), (dec_b, 1))
    sin_sq = jnp.tile(jnp.tile(sin_s, (1, N_HEADS)), (dec_b, 1))
    ql, r1, r2 = _sample_q(cq_all[t_p:], wq_s, wuk_blk, cos_sq, sin_sq)
    rows = dec_s * N_HEADS
    ql3 = ql.reshape(dec_b, rows, KV_LORA)
    qr3 = jnp.concatenate([r1.reshape(dec_b, dec_s, N_HEADS, ROPE_HALF),
                           r2.reshape(dec_b, dec_s, N_HEADS, ROPE_HALF)], axis=-1).reshape(dec_b, rows, QK_ROPE)
    new_pad = ((0, 0), (0, 8 - dec_s), (0, 0))
    ckv_s = jnp.pad(ckv_all[t_p:].reshape(dec_b, dec_s, KV_LORA), new_pad)
    kr_s_t = jnp.swapaxes(jnp.pad(kr_all[t_p:].reshape(dec_b, dec_s, QK_ROPE), new_pad), 1, 2)
    cache_kr_t = jnp.swapaxes(cache_krope[0], 1, 2)
    o_lat = _sample_attn(page_table, cache_ckv[0], cache_kr_t, ql3, qr3, ckv_s, kr_s_t)
    o_s = _matmul(o_lat.reshape(t_s, N_HEADS * KV_LORA), wuv_blk, BF16)

    x1_all, *h2_all = _merge(x_all, (sh1, sc1, gt1, sh2, sc2), g1, g2, a_all, ot, o_s, wg,
                            w_out_a[0].astype(BF16), w_out_b[0].astype(BF16), w_o[0].astype(BF16),
                            grp_of_tile, n_prompt_tiles)

    wr_t = w_router[0].T
    whi = wr_t.astype(BF16)
    wlo = (wr_t - whi.astype(F32)).astype(BF16)
    idx_t, wt_t = _route(x1_all, sh2, sc2, g2, whi, wlo, router_bias[0].reshape(-1, 1), grp_of_tile)
    n_assign = t_all * TOP_K
    e_flat = idx_t.reshape(1, n_assign)
    rank, counts = _rank(e_flat)
    counts = counts[:, 0].astype(jnp.int32)
    blk = MOE_BLOCK
    padded = (counts + blk - 1) // blk * blk
    pad_end = jnp.cumsum(padded)
    pad_start = pad_end - padded
    dest = _dest(e_flat, rank, pad_start.reshape(-1, 1))
    n_blocks = -(-n_assign // blk) + N_EXPERTS
    n_slots = n_blocks * blk
    first_row = jnp.arange(n_blocks, dtype=jnp.int32) * blk
    block_e = jnp.minimum(jnp.sum(pad_end[None, :] <= first_row[:, None], axis=1), N_EXPERTS - 1).astype(jnp.int32)
    n_valid = (pad_end[-1] // blk).astype(jnp.int32).reshape(1)
    x_sorted = [_sc_scatter_rows(h, dest, n_slots) for h in h2_all]
    y_sorted = _moe(block_e, n_valid, x_sorted, w_exp_gu[0], w_exp_down[0])
    y_gath = [_sc_gather_rows(y, dest).reshape(TOP_K, t_all, PIECE) for y in y_sorted]

    y_all = _final(x1_all, h2_all, y_gath, wt_t.T, gt2, w_sh_gu[0].astype(BF16), w_sh_down[0].astype(BF16),
                   row1(final_norm_g), grp_of_tile)

    y_prompt = y_all[:t_p].reshape(batch, seq, D_MODEL)
    y_sample = y_all[t_p:].reshape(dec_b, dec_s, D_MODEL)
    new_ckv_prompt = ckv_all[:t_p].reshape(1, batch, seq, KV_LORA)
    new_krope_prompt = kr_all[:t_p].reshape(1, batch, seq, QK_ROPE)
    new_ckv_sample = ckv_all[t_p:].reshape(1, dec_b, dec_s, KV_LORA)
    new_krope_sample = kr_all[t_p:].reshape(1, dec_b, dec_s, QK_ROPE)
    new_chunk_v_sample = v_all[t_p:].reshape(1, dec_b, dec_s, A_WIDTH)
    return (y_prompt, y_sample, new_ckv_prompt, new_krope_prompt, new_ckv_sample, new_krope_sample,
            new_chunk_v_sample)
```

```python
import functools

import jax
import jax.numpy as jnp
from jax import lax
from jax.experimental import pallas as pl
from jax.experimental.pallas import tpu as pltpu
from jax.experimental.pallas import tpu_sc as plsc

F32 = jnp.float32
BF16 = jnp.bfloat16

D_MODEL = 1024
A_WIDTH = D_MODEL // 2
A_GROUPS = 8
A_GROUP_DIM = A_WIDTH // A_GROUPS
CHUNK = 128
N_HEADS = 8
QK_NOPE = 64
QK_ROPE = 32
ROPE_HALF = QK_ROPE // 2
V_HEAD = 64
Q_LORA = 384
KV_LORA = 256
ROPE_THETA = 10000.0
ATTN_SCALE = (QK_NOPE + QK_ROPE) ** -0.5
N_EXPERTS = 256
TOP_K = 8
N_GROUPS = 8
GROUP_SIZE = N_EXPERTS // N_GROUPS
TOPK_GROUPS = 4
EXPERT_DIM = 256
SHARED_DIM = 256
ROUTED_SCALE = 2.5
EPS = 1e-6

LANES = 128
HEAD_PAD = 128
ROW_TILE = 256
ATTN_KV_TILE = 256
ATTN_Q_STEP = 1024
ATTN_HEADS_PER_STEP = 2
LOG2_E = 1.4426950408889634
RANK_TILE = 512
MOE_BLOCK = 256
MOE_SUB = 128
PAGES_PER_STEP = 16
PAGE_GROUP = 4
SC_WINDOW = 128
SC_SPLIT = 2
PIECE = D_MODEL // 2 // SC_SPLIT
VMEM_LIMIT = 48 * 1024 * 1024

IN_U, IN_V, IN_CQ, IN_CKV, IN_KR, IN_END = 0, 512, 1024, 1408, 1664, 1792

NT_DIMS = (((1,), (1,)), ((), ()))
TN_DIMS = (((0,), (0,)), ((), ()))


def _params(sem, vmem=VMEM_LIMIT):
    return pltpu.CompilerParams(dimension_semantics=sem, vmem_limit_bytes=vmem)


def _rms(x, g):
    return x * lax.rsqrt(jnp.mean(x * x, axis=-1, keepdims=True) + EPS) * g


def _gelu(x):
    return 0.5 * x * (1.0 + jnp.tanh(0.7978845608028654 * (x + 0.044715 * (x * x * x))))


def _sigmoid(x):
    return 1.0 / (1.0 + jnp.exp(-x))


def _silu(x):
    return x * _sigmoid(x)


def _pack_rows(x):
    pieces = []
    for c in range(SC_SPLIT):
        lo = x[:, 2 * c * PIECE:(2 * c + 1) * PIECE].astype(BF16).astype(F32)
        hi = x[:, (2 * c + 1) * PIECE:(2 * c + 2) * PIECE].astype(BF16).astype(F32)
        pieces.append(lax.bitcast_convert_type(hi, jnp.int32)
                      | lax.shift_right_logical(lax.bitcast_convert_type(lo, jnp.int32), 16))
    return pieces


def _unpack_rows(pieces):
    cols = []
    for p in pieces:
        cols.append(lax.bitcast_convert_type(lax.shift_left(p, 16), F32))
        cols.append(lax.bitcast_convert_type(p & jnp.int32(-65536), F32))
    return jnp.concatenate(cols, axis=1)


def _ada_kernel(c_ref, w_ref, b_ref, o_ref):
    s = _silu(c_ref[...]).astype(BF16)
    o_ref[...] = jnp.dot(s, w_ref[...], preferred_element_type=F32) + b_ref[...]


def _ada(c, w, b):
    rows, n = c.shape[0], w.shape[1]
    tn = 1536
    return pl.pallas_call(
        _ada_kernel,
        grid=(n // tn,),
        in_specs=[pl.BlockSpec((rows, D_MODEL), lambda j: (0, 0)),
                  pl.BlockSpec((D_MODEL, tn), lambda j: (0, j)),
                  pl.BlockSpec((1, tn), lambda j: (0, j))],
        out_specs=pl.BlockSpec((rows, tn), lambda j: (0, j)),
        out_shape=jax.ShapeDtypeStruct((rows, n), F32),
        compiler_params=_params(("parallel",)),
        name="ada",
    )(c, w, b)


def _in_kernel(x_ref, sh_ref, sc_ref, g1_ref, w_ref, vg_ref, qg_ref, kg_ref, cos_ref, sin_ref,
               wc_ref, bc_ref, a_ref, v_ref, cq_ref, ckv_ref, kr_ref):
    tm = x_ref.shape[0]
    h = _rms(x_ref[...], g1_ref[...]) * (1.0 + sc_ref[...]) + sh_ref[...]
    z = jnp.dot(h.astype(BF16), w_ref[...], preferred_element_type=F32)
    u = _gelu(z[:, IN_U:IN_V])
    v = _rms(_gelu(z[:, IN_V:IN_CQ]), vg_ref[...])
    v_ref[...] = v
    cq_ref[...] = _rms(z[:, IN_CQ:IN_CKV], qg_ref[...]).astype(BF16)
    ckv_ref[...] = _rms(z[:, IN_CKV:IN_KR], kg_ref[...])
    kr = z[:, IN_KR:IN_END]
    lane = lax.broadcasted_iota(jnp.int32, kr.shape, 1)
    swapped = jnp.where(lane < ROPE_HALF, pltpu.roll(kr, LANES - ROPE_HALF, 1), pltpu.roll(kr, ROPE_HALF, 1))
    kr_rot = kr * cos_ref[...] + swapped * sin_ref[...]
    kr_ref[...] = kr_rot[:, :QK_ROPE]
    vb = v.astype(BF16)
    half = A_WIDTH // 2
    grp_of_lane = lax.broadcasted_iota(jnp.int32, (CHUNK, half), 1) // A_GROUP_DIM
    for ci in range(tm // CHUNK):
        rows = slice(ci * CHUNK, (ci + 1) * CHUNK)
        parts = []
        for q in range(2):
            vq = vb[rows, q * half:(q + 1) * half]
            acc = None
            for gg in range(A_GROUPS // 2):
                vm = jnp.where(grp_of_lane == gg, vq, jnp.zeros_like(vq))
                part = jnp.dot(wc_ref[q * (A_GROUPS // 2) + gg], vm, preferred_element_type=F32)
                acc = part if acc is None else acc + part
            parts.append(acc)
        sp = jnp.concatenate(parts, axis=1) + bc_ref[...]
        a_ref[rows, :] = (u[rows, :] * sp).astype(BF16)


def _in_proj(x_all, sh1, sc1, g1, w5, vg, qg, kg, cos_kr, sin_kr, wc, bc, grp_of_tile, n_prompt_tiles):
    t_all = x_all.shape[0]
    tm = ROW_TILE
    n_tiles = t_all // tm
    row = lambda i: (i, 0)
    fixed2 = lambda i: (0, 0)
    kind = lambda i: (jnp.where(i < n_prompt_tiles, 0, 1), 0, 0)
    kind4 = lambda i: (jnp.where(i < n_prompt_tiles, 0, 1), 0, 0, 0)
    mod = lambda i: (grp_of_tile(i), 0, 0)
    return pl.pallas_call(
        _in_kernel,
        grid=(n_tiles,),
        in_specs=[pl.BlockSpec((tm, D_MODEL), row),
                  pl.BlockSpec((None, tm, D_MODEL), mod),
                  pl.BlockSpec((None, tm, D_MODEL), mod),
                  pl.BlockSpec((1, D_MODEL), fixed2),
                  pl.BlockSpec((D_MODEL, IN_END), fixed2),
                  pl.BlockSpec((1, A_WIDTH), fixed2),
                  pl.BlockSpec((1, Q_LORA), fixed2),
                  pl.BlockSpec((1, KV_LORA), fixed2),
                  pl.BlockSpec((tm, LANES), row),
                  pl.BlockSpec((tm, LANES), row),
                  pl.BlockSpec((None, A_GROUPS, CHUNK, CHUNK), kind4),
                  pl.BlockSpec((None, CHUNK, A_WIDTH), kind)],
        out_specs=[pl.BlockSpec((tm, A_WIDTH), row),
                   pl.BlockSpec((tm, A_WIDTH), row),
                   pl.BlockSpec((tm, Q_LORA), row),
                   pl.BlockSpec((tm, KV_LORA), row),
                   pl.BlockSpec((tm, QK_ROPE), row)],
        out_shape=[jax.ShapeDtypeStruct((t_all, A_WIDTH), BF16),
                   jax.ShapeDtypeStruct((t_all, A_WIDTH), F32),
                   jax.ShapeDtypeStruct((t_all, Q_LORA), BF16),
                   jax.ShapeDtypeStruct((t_all, KV_LORA), F32),
                   jax.ShapeDtypeStruct((t_all, QK_ROPE), F32)],
        compiler_params=_params(("parallel",)),
        name="in_proj",
    )(x_all, sh1, sc1, g1, w5, vg, qg, kg, cos_kr, sin_kr, wc, bc)


def _qkv_kernel(cq_ref, ckv_ref, kr_ref, wuqt_ref, wukp_ref, place_ref, wuvt_ref, cos_ref, sin_ref,
                qt_ref, k_ref, vt_ref):
    tm = cq_ref.shape[0]
    ckv = ckv_ref[...].astype(BF16)
    kr = kr_ref[...].astype(BF16)
    qt = lax.dot_general(wuqt_ref[...], cq_ref[...], NT_DIMS, preferred_element_type=F32)
    c = cos_ref[...]
    s = sin_ref[...]
    pad = jnp.zeros((HEAD_PAD - QK_NOPE - QK_ROPE, tm), F32)
    for h in range(N_HEADS):
        blk = qt[h * HEAD_PAD:(h + 1) * HEAD_PAD]
        x1 = blk[QK_NOPE:QK_NOPE + ROPE_HALF]
        x2 = blk[QK_NOPE + ROPE_HALF:QK_NOPE + QK_ROPE]
        full = jnp.concatenate([blk[:QK_NOPE], x1 * c - x2 * s, x1 * s + x2 * c, pad], axis=0)
        qt_ref[h] = (full * (ATTN_SCALE * LOG2_E)).astype(BF16)
    k = (jnp.dot(ckv, wukp_ref[...], preferred_element_type=F32)
         + jnp.dot(kr, place_ref[...], preferred_element_type=F32))
    for h in range(N_HEADS):
        k_ref[h] = k[:, h * HEAD_PAD:(h + 1) * HEAD_PAD].astype(BF16)
    vt = lax.dot_general(wuvt_ref[...], ckv, NT_DIMS, preferred_element_type=F32)
    for h in range(N_HEADS):
        vt_ref[h] = vt[h * V_HEAD:(h + 1) * V_HEAD].astype(BF16)


def _qkv(cq_all, ckv_all, kr_all, wuqt, wukp, place, wuvt, cos_t, sin_t, batch, seq):
    tm = ATTN_KV_TILE
    nk = seq // tm
    row = lambda b, j: (b * nk + j, 0)
    fixed2 = lambda b, j: (0, 0)
    tab = lambda b, j: (0, j)
    return pl.pallas_call(
        _qkv_kernel,
        grid=(batch, nk),
        in_specs=[pl.BlockSpec((tm, Q_LORA), row),
                  pl.BlockSpec((tm, KV_LORA), row),
                  pl.BlockSpec((tm, QK_ROPE), row),
                  pl.BlockSpec(wuqt.shape, fixed2),
                  pl.BlockSpec(wukp.shape, fixed2),
                  pl.BlockSpec(place.shape, fixed2),
                  pl.BlockSpec(wuvt.shape, fixed2),
                  pl.BlockSpec((ROPE_HALF, tm), tab),
                  pl.BlockSpec((ROPE_HALF, tm), tab)],
        out_specs=[pl.BlockSpec((None, N_HEADS, HEAD_PAD, tm), lambda b, j: (b, 0, 0, j)),
                   pl.BlockSpec((None, N_HEADS, tm, HEAD_PAD), lambda b, j: (b, 0, j, 0)),
                   pl.BlockSpec((None, N_HEADS, None, V_HEAD, tm), lambda b, j: (b, 0, j, 0, 0))],
        out_shape=[jax.ShapeDtypeStruct((batch, N_HEADS, HEAD_PAD, seq), BF16),
                   jax.ShapeDtypeStruct((batch, N_HEADS, seq, HEAD_PAD), BF16),
                   jax.ShapeDtypeStruct((batch, N_HEADS, nk, V_HEAD, tm), BF16)],
        compiler_params=_params(("parallel", "parallel")),
        name="qkv",
    )(cq_all, ckv_all, kr_all, wuqt, wukp, place, wuvt, cos_t, sin_t)


def _attn_kernel(qt_ref, k_ref, vt_ref, o_ref):
    tk = ATTN_KV_TILE
    n_heads = qt_ref.shape[0]
    n_sub = qt_ref.shape[2] // tk
    qi = pl.program_id(2)
    j0 = qi * n_sub
    chains = [(h, sub) for h in range(n_heads) for sub in range(n_sub)]

    def tiles(j, active, carries, diagonal_sub):
        scores = [jnp.dot(k_ref[h, j], qt_ref[h, :, sub * tk:(sub + 1) * tk], preferred_element_type=F32)
                  for h, sub in active]
        stats = []
        for (h, sub), s in zip(active, scores):
            m, l, _ = carries[h * n_sub + sub]
            if sub == diagonal_sub:
                key = lax.broadcasted_iota(jnp.int32, s.shape, 0)
                qry = lax.broadcasted_iota(jnp.int32, s.shape, 1)
                s = jnp.where(key <= qry, s, -jnp.inf)
            m_new = jnp.maximum(m, jnp.max(s, axis=0, keepdims=True))
            alpha = jnp.exp2(m - m_new)
            p = jnp.exp2(s - m_new)
            stats.append((m_new, alpha * l + jnp.sum(p, axis=0, keepdims=True), alpha, p.astype(BF16)))
        out = list(carries)
        for (h, sub), (m_new, l_new, alpha, p) in zip(active, stats):
            c = h * n_sub + sub
            out[c] = (m_new, l_new, alpha * carries[c][2] + jnp.dot(vt_ref[h, j], p, preferred_element_type=F32))
        return out

    init = tuple((jnp.full((1, tk), -jnp.inf, F32), jnp.zeros((1, tk), F32), jnp.zeros((V_HEAD, tk), F32))
                 for _ in chains)
    carries = list(lax.fori_loop(0, j0, lambda j, c: tuple(tiles(j, chains, c, None)), init))
    for jj in range(n_sub):
        carries = tiles(j0 + jj, [(h, sub) for h, sub in chains if sub >= jj], carries, jj)
    for h, sub in chains:
        _, l, acc = carries[h * n_sub + sub]
        o_ref[h * V_HEAD:(h + 1) * V_HEAD, sub * tk:(sub + 1) * tk] = (acc / l).astype(BF16)


def _attn(qt, k, vt):
    batch, _, _, seq = qt.shape
    tk = ATTN_KV_TILE
    nk = seq // tk
    hp = ATTN_HEADS_PER_STEP
    k5 = k.reshape(batch, N_HEADS, nk, tk, HEAD_PAD)
    return pl.pallas_call(
        _attn_kernel,
        grid=(batch, N_HEADS // hp, seq // ATTN_Q_STEP),
        in_specs=[pl.BlockSpec((None, hp, HEAD_PAD, ATTN_Q_STEP), lambda b, h, i: (b, h, 0, i)),
                  pl.BlockSpec((None, hp, nk, tk, HEAD_PAD), lambda b, h, i: (b, h, 0, 0, 0)),
                  pl.BlockSpec((None, hp, nk, V_HEAD, tk), lambda b, h, i: (b, h, 0, 0, 0))],
        out_specs=pl.BlockSpec((None, hp * V_HEAD, ATTN_Q_STEP), lambda b, h, i: (b, h, i)),
        out_shape=jax.ShapeDtypeStruct((batch, N_HEADS * V_HEAD, seq), BF16),
        compiler_params=_params(("parallel", "parallel", "parallel")),
        name="attn",
    )(qt, k5, vt)


def _sq_kernel(cq_ref, wq_ref, wukb_ref, cos_ref, sin_ref, ql_ref, r1_ref, r2_ref):
    n_nope = N_HEADS * QK_NOPE
    q = jnp.dot(cq_ref[...], wq_ref[...], preferred_element_type=F32)
    x1 = q[:, n_nope:n_nope + LANES]
    x2 = q[:, n_nope + LANES:]
    c = cos_ref[...]
    s = sin_ref[...]
    r1_ref[...] = ((x1 * c - x2 * s) * ATTN_SCALE).astype(BF16)
    r2_ref[...] = ((x1 * s + x2 * c) * ATTN_SCALE).astype(BF16)
    ql = jnp.dot(q[:, :n_nope].astype(BF16), wukb_ref[...], preferred_element_type=F32)
    ql_ref[...] = (ql * ATTN_SCALE).astype(BF16)


def _sample_q(cq_s, wq_s, wuk_blk, cos_s, sin_s):
    ts = cq_s.shape[0]
    full = lambda shape: pl.BlockSpec(shape, lambda i: (0,) * len(shape))
    return pl.pallas_call(
        _sq_kernel,
        grid=(1,),
        in_specs=[full(cq_s.shape), full(wq_s.shape), full(wuk_blk.shape), full(cos_s.shape), full(sin_s.shape)],
        out_specs=[full((ts, N_HEADS * KV_LORA)), full((ts, LANES)), full((ts, LANES))],
        out_shape=[jax.ShapeDtypeStruct((ts, N_HEADS * KV_LORA), BF16),
                   jax.ShapeDtypeStruct((ts, LANES), BF16),
                   jax.ShapeDtypeStruct((ts, LANES), BF16)],
        compiler_params=_params(("arbitrary",)),
        name="sample_q",
    )(cq_s, wq_s, wuk_blk, cos_s, sin_s)


def _sattn_kernel(n_pages, pt_ref, ckv_hbm, kr_hbm, ql_ref, qr_ref, cnew_ref, knew_ref, o_ref,
                  cbuf, rbuf, sem_c, sem_r):
    npg = PAGES_PER_STEP
    n_chunks = n_pages // npg
    b = pl.program_id(0)

    def page_copies(bb, chunk, slot, p):
        page = pt_ref[bb * n_pages + chunk * npg + p]
        return (pltpu.make_async_copy(ckv_hbm.at[page], cbuf.at[slot, p], sem_c.at[slot]),
                pltpu.make_async_copy(kr_hbm.at[page], rbuf.at[slot, p], sem_r.at[slot]))

    def start_chunk(bb, chunk, slot):
        for p in range(npg):
            for cp in page_copies(bb, chunk, slot, p):
                cp.start()

    def wait_chunk(bb, chunk, slot):
        for p in range(npg):
            for cp in page_copies(bb, chunk, slot, p):
                cp.wait()

    @pl.when(b == 0)
    def _():
        start_chunk(0, 0, 0)

    ql = ql_ref[...]
    qr = qr_ref[...]

    def partial_softmax(blocks, mask=None):
        scores = [lax.dot_general(ql, keys, NT_DIMS, preferred_element_type=F32)
                  + jnp.dot(qr, rope_t, preferred_element_type=F32) for keys, rope_t in blocks]
        probs = []
        for s in scores:
            if mask is not None:
                s = jnp.where(mask(s.shape), s, -jnp.inf)
            m = jnp.max(s, axis=-1, keepdims=True)
            p = jnp.exp(s - m)
            probs.append((m, jnp.sum(p, axis=-1, keepdims=True), p.astype(BF16)))
        return [(m, l, jnp.dot(p, keys, preferred_element_type=F32)) for (m, l, p), (keys, _) in zip(probs, blocks)]

    def merge(state, parts):
        m_old, l_old, acc_old = state
        m_new = m_old
        for m, _, _ in parts:
            m_new = jnp.maximum(m_new, m)
        alpha = jnp.exp(m_old - m_new)
        l = alpha * l_old
        acc = alpha * acc_old
        for m, lp, op in parts:
            w = jnp.exp(m - m_new)
            l = l + w * lp
            acc = acc + w * op
        return m_new, l, acc

    def chunk_step(chunk, state):
        slot = chunk % 2
        @pl.when(chunk + 1 < n_chunks)
        def _():
            start_chunk(b, chunk + 1, 1 - slot)

        @pl.when((chunk + 1 == n_chunks) & (b + 1 < pl.num_programs(0)))
        def _():
            start_chunk(b + 1, 0, 1 - slot)

        wait_chunk(b, chunk, slot)
        blocks = []
        for g in range(npg // PAGE_GROUP):
            pages = range(g * PAGE_GROUP, (g + 1) * PAGE_GROUP)
            blocks.append((jnp.concatenate([cbuf[slot, p].astype(BF16) for p in pages], axis=0),
                           jnp.concatenate([rbuf[slot, p].astype(BF16) for p in pages], axis=1)))
        return merge(state, partial_softmax(blocks))

    rows = ql.shape[0]
    state = (jnp.full((rows, 1), -jnp.inf, F32), jnp.zeros((rows, 1), F32), jnp.zeros((rows, KV_LORA), F32))
    state = lax.fori_loop(0, n_chunks, chunk_step, state)

    def causal(shape):
        q_pos = lax.broadcasted_iota(jnp.int32, shape, 0) // N_HEADS
        return lax.broadcasted_iota(jnp.int32, shape, 1) <= q_pos

    _, l, acc = merge(state, partial_softmax([(cnew_ref[...].astype(BF16), knew_ref[...].astype(BF16))], causal))
    o_ref[...] = acc / l


def _sample_attn(page_table, cache_ckv, cache_kr_t, ql, qr, cnew, knew_t):
    dec_b, n_pages = page_table.shape
    npg = PAGES_PER_STEP
    assert (n_pages // npg) % 2 == 0
    rows = ql.shape[1]
    n_new = cnew.shape[1]
    per_b = lambda b, pt: (b, 0, 0)
    return pl.pallas_call(
        functools.partial(_sattn_kernel, n_pages),
        grid_spec=pltpu.PrefetchScalarGridSpec(
            num_scalar_prefetch=1,
            grid=(dec_b,),
            in_specs=[pl.BlockSpec(memory_space=pl.ANY),
                      pl.BlockSpec(memory_space=pl.ANY),
                      pl.BlockSpec((None, rows, KV_LORA), per_b),
                      pl.BlockSpec((None, rows, QK_ROPE), per_b),
                      pl.BlockSpec((None, n_new, KV_LORA), per_b),
                      pl.BlockSpec((None, QK_ROPE, n_new), per_b)],
            out_specs=pl.BlockSpec((None, rows, KV_LORA), per_b),
            scratch_shapes=[pltpu.VMEM((2, npg, CHUNK, KV_LORA), F32),
                            pltpu.VMEM((2, npg, QK_ROPE, CHUNK), F32),
                            pltpu.SemaphoreType.DMA((2,)),
                            pltpu.SemaphoreType.DMA((2,))]),
        out_shape=jax.ShapeDtypeStruct((dec_b, rows, KV_LORA), F32),
        compiler_params=_params(("arbitrary",)),
        name="sample_attn",
    )(page_table.reshape(-1), cache_ckv, cache_kr_t, ql, qr, cnew, knew_t)


def _mm_kernel(x_ref, w_ref, o_ref):
    o_ref[...] = jnp.dot(x_ref[...].astype(BF16), w_ref[...], preferred_element_type=F32).astype(o_ref.dtype)


def _matmul(x, w, out_dtype):
    m, n = x.shape[0], w.shape[1]
    full = lambda shape: pl.BlockSpec(shape, lambda i: (0,) * len(shape))
    return pl.pallas_call(
        _mm_kernel,
        grid=(1,),
        in_specs=[full(x.shape), full(w.shape)],
        out_specs=full((m, n)),
        out_shape=jax.ShapeDtypeStruct((m, n), out_dtype),
        compiler_params=_params(("arbitrary",)),
        name="matmul",
    )(x, w)


def _merge_kernel(n_prompt_tiles, x_ref, sh1_ref, sc1_ref, gt1_ref, sh2_ref, sc2_ref, g1_ref, g2_ref,
                  a_ref, ot_ref, os_ref, wg_ref, woa_ref, wob_ref, wo_ref, x1_ref, *rest):
    h2_refs, yb_ref = rest[:SC_SPLIT], rest[SC_SPLIT]
    i = pl.program_id(0)
    x = x_ref[...]
    h = _rms(x, g1_ref[...]) * (1.0 + sc1_ref[...]) + sh1_ref[...]
    gates = _sigmoid(jnp.dot(h.astype(BF16), wg_ref[...], preferred_element_type=F32))
    y_a = jnp.dot(a_ref[...], woa_ref[...], preferred_element_type=F32)

    @pl.when(i < n_prompt_tiles)
    def _():
        yb_ref[...] = lax.dot_general(ot_ref[...], wob_ref[...], TN_DIMS, preferred_element_type=F32)

    @pl.when(i >= n_prompt_tiles)
    def _():
        yb_ref[...] = jnp.dot(os_ref[...], wob_ref[...], preferred_element_type=F32)

    z = gates[:, :D_MODEL] * y_a + gates[:, D_MODEL:] * yb_ref[...]
    y = jnp.dot(z.astype(BF16), wo_ref[...], preferred_element_type=F32)
    x1 = x + gt1_ref[...] * y
    x1_ref[...] = x1
    for ref, piece in zip(h2_refs, _pack_rows(_rms(x1, g2_ref[...]) * (1.0 + sc2_ref[...]) + sh2_ref[...])):
        ref[...] = piece


def _merge(x_all, mods, g1, g2, a_all, ot, o_s, wg, woa, wob, wo, grp_of_tile, n_prompt_tiles):
    t_all = x_all.shape[0]
    tm = ROW_TILE
    seq = ot.shape[2]
    tpb = seq // tm
    row = lambda i: (i, 0)
    fixed2 = lambda i: (0, 0)
    mod = lambda i: (grp_of_tile(i), 0, 0)

    def ot_map(i):
        ic = jnp.minimum(i, n_prompt_tiles - 1)
        return (ic // tpb, 0, ic % tpb)

    os_map = lambda i: (jnp.maximum(i - n_prompt_tiles, 0), 0)
    mod_spec = pl.BlockSpec((None, tm, D_MODEL), mod)
    return pl.pallas_call(
        functools.partial(_merge_kernel, n_prompt_tiles),
        grid=(t_all // tm,),
        in_specs=[pl.BlockSpec((tm, D_MODEL), row),
                  mod_spec, mod_spec, mod_spec, mod_spec, mod_spec,
                  pl.BlockSpec((1, D_MODEL), fixed2),
                  pl.BlockSpec((1, D_MODEL), fixed2),
                  pl.BlockSpec((tm, A_WIDTH), row),
                  pl.BlockSpec((None, N_HEADS * V_HEAD, tm), ot_map),
                  pl.BlockSpec((tm, N_HEADS * V_HEAD), os_map),
                  pl.BlockSpec(wg.shape, fixed2),
                  pl.BlockSpec(woa.shape, fixed2),
                  pl.BlockSpec(wob.shape, fixed2),
                  pl.BlockSpec(wo.shape, fixed2)],
        out_specs=[pl.BlockSpec((tm, D_MODEL), row)] + [pl.BlockSpec((tm, PIECE), row)] * SC_SPLIT,
        out_shape=[jax.ShapeDtypeStruct((t_all, D_MODEL), F32)]
        + [jax.ShapeDtypeStruct((t_all, PIECE), jnp.int32)] * SC_SPLIT,
        scratch_shapes=[pltpu.VMEM((tm, D_MODEL), F32)],
        compiler_params=_params(("parallel",)),
        name="merge",
    )(x_all, *mods, g1, g2, a_all, ot, o_s, wg, woa, wob, wo)


def _first_argmax(v, rows):
    mx = jnp.max(v, axis=0, keepdims=True)
    idx = jnp.min(jnp.where(v == mx, rows, v.shape[0]), axis=0, keepdims=True)
    return mx, idx


def _route_kernel(x1_ref, sh2_ref, sc2_ref, g2_ref, whi_ref, wlo_ref, bias_ref, idx_ref, wt_ref):
    h2 = _rms(x1_ref[...], g2_ref[...]) * (1.0 + sc2_ref[...]) + sh2_ref[...]
    hi = h2.astype(BF16)
    lo = (h2 - hi.astype(F32)).astype(BF16)
    whi = whi_ref[...]
    logits = (lax.dot_general(whi, hi, NT_DIMS, preferred_element_type=F32)
              + lax.dot_general(whi, lo, NT_DIMS, preferred_element_type=F32)
              + lax.dot_general(wlo_ref[...], hi, NT_DIMS, preferred_element_type=F32))
    scores = _sigmoid(logits)
    sel = scores + bias_ref[...]
    tm = sel.shape[1]
    neg = -jnp.inf
    rows_g = lax.broadcasted_iota(jnp.int32, (GROUP_SIZE, tm), 0)
    gscore = []
    for g in range(N_GROUPS):
        blk = sel[g * GROUP_SIZE:(g + 1) * GROUP_SIZE]
        m1, i1 = _first_argmax(blk, rows_g)
        m2 = jnp.max(jnp.where(rows_g == i1, neg, blk), axis=0, keepdims=True)
        gscore.append(m1 + m2)
    gs = jnp.concatenate(gscore, axis=0)
    rows_8 = lax.broadcasted_iota(jnp.int32, gs.shape, 0)
    chosen = jnp.zeros(gs.shape, jnp.int32)
    for _ in range(TOPK_GROUPS):
        _, gi = _first_argmax(gs, rows_8)
        hit = rows_8 == gi
        chosen = jnp.where(hit, 1, chosen)
        gs = jnp.where(hit, neg, gs)
    cand = jnp.concatenate(
        [jnp.where(chosen[g:g + 1] > 0, sel[g * GROUP_SIZE:(g + 1) * GROUP_SIZE], neg) for g in range(N_GROUPS)],
        axis=0)
    rows_e = lax.broadcasted_iota(jnp.int32, cand.shape, 0)
    idxs, wts = [], []
    for _ in range(TOP_K):
        _, ei = _first_argmax(cand, rows_e)
        hit = rows_e == ei
        idxs.append(ei)
        wts.append(jnp.sum(jnp.where(hit, scores, 0.0), axis=0, keepdims=True))
        cand = jnp.where(hit, neg, cand)
    w = jnp.concatenate(wts, axis=0)
    idx_ref[...] = jnp.concatenate(idxs, axis=0)
    wt_ref[...] = w / jnp.sum(w, axis=0, keepdims=True) * ROUTED_SCALE


def _route(x1_all, sh2, sc2, g2, whi, wlo, bias_col, grp_of_tile):
    t_all = x1_all.shape[0]
    tm = ROW_TILE
    fixed2 = lambda i: (0, 0)
    mod = lambda i: (grp_of_tile(i), 0, 0)
    return pl.pallas_call(
        _route_kernel,
        grid=(t_all // tm,),
        in_specs=[pl.BlockSpec((tm, D_MODEL), lambda i: (i, 0)),
                  pl.BlockSpec((None, tm, D_MODEL), mod),
                  pl.BlockSpec((None, tm, D_MODEL), mod),
                  pl.BlockSpec((1, D_MODEL), fixed2),
                  pl.BlockSpec(whi.shape, fixed2),
                  pl.BlockSpec(wlo.shape, fixed2),
                  pl.BlockSpec(bias_col.shape, fixed2)],
        out_specs=[pl.BlockSpec((TOP_K, tm), lambda i: (0, i)), pl.BlockSpec((TOP_K, tm), lambda i: (0, i))],
        out_shape=[jax.ShapeDtypeStruct((TOP_K, t_all), jnp.int32),
                   jax.ShapeDtypeStruct((TOP_K, t_all), F32)],
        compiler_params=_params(("parallel",)),
        name="route",
    )(x1_all, sh2, sc2, g2, whi, wlo, bias_col)


def _rank_kernel(e_ref, upper_ref, rank_ref, count_ref, run_ref):
    @pl.when(pl.program_id(0) == 0)
    def _():
        run_ref[...] = jnp.zeros(run_ref.shape, F32)

    e = e_ref[...]
    n = e.shape[1]
    rows = lax.broadcasted_iota(jnp.int32, (N_EXPERTS, n), 0)
    onehot = rows == e
    oh = jnp.where(onehot, 1.0, 0.0).astype(BF16)
    before = jnp.dot(oh, upper_ref[...], preferred_element_type=F32)
    run = run_ref[...]
    prior = jnp.concatenate([run] * (n // LANES), axis=1)
    rank = jnp.sum(jnp.where(onehot, before + prior, 0.0), axis=0, keepdims=True)
    rank_ref[...] = rank.astype(jnp.int32)
    total = run + jnp.dot(oh, jnp.ones((n, LANES), BF16), preferred_element_type=F32)
    run_ref[...] = total
    count_ref[...] = total


def _rank(e_flat):
    n_assign = e_flat.shape[1]
    n = RANK_TILE
    upper = (lax.broadcasted_iota(jnp.int32, (n, n), 0) < lax.broadcasted_iota(jnp.int32, (n, n), 1)).astype(BF16)
    return pl.pallas_call(
        _rank_kernel,
        grid=(n_assign // n,),
        in_specs=[pl.BlockSpec((1, n), lambda i: (0, i)), pl.BlockSpec((n, n), lambda i: (0, 0))],
        out_specs=[pl.BlockSpec((1, n), lambda i: (0, i)), pl.BlockSpec((N_EXPERTS, LANES), lambda i: (0, 0))],
        out_shape=[jax.ShapeDtypeStruct((1, n_assign), jnp.int32),
                   jax.ShapeDtypeStruct((N_EXPERTS, LANES), F32)],
        scratch_shapes=[pltpu.VMEM((N_EXPERTS, LANES), F32)],
        compiler_params=_params(("arbitrary",)),
        name="rank",
    )(e_flat, upper)


def _moe_kernel(be_ref, nv_ref, *refs):
    x_refs = refs[:SC_SPLIT]
    wgu_ref, wdn_ref = refs[SC_SPLIT:SC_SPLIT + 2]
    y_refs = refs[SC_SPLIT + 2:2 * SC_SPLIT + 2]
    gu_bf, dn_bf = refs[2 * SC_SPLIT + 2:]
    i = pl.program_id(0)
    prev = be_ref[jnp.maximum(i - 1, 0)]

    @pl.when(i < nv_ref[0])
    def _():
        @pl.when((i == 0) | (be_ref[i] != prev))
        def _():
            gu_bf[...] = wgu_ref[...].astype(BF16)
            dn_bf[...] = wdn_ref[...].astype(BF16)

        subs = [slice(s * MOE_SUB, (s + 1) * MOE_SUB) for s in range(MOE_BLOCK // MOE_SUB)]
        gus = [jnp.dot(_unpack_rows([r[rows, :] for r in x_refs]).astype(BF16), gu_bf[...],
                       preferred_element_type=F32) for rows in subs]
        hids = [(_silu(gu[:, :EXPERT_DIM]) * gu[:, EXPERT_DIM:]).astype(BF16) for gu in gus]
        ys = [jnp.dot(hid, dn_bf[...], preferred_element_type=F32) for hid in hids]
        for rows, y in zip(subs, ys):
            for ref, piece in zip(y_refs, _pack_rows(y)):
                ref[rows, :] = piece


def _moe(block_e, n_valid, x_sorted, w_gu, w_dn):
    n_slots = x_sorted[0].shape[0]
    blk = MOE_BLOCK
    n_blocks = n_slots // blk
    rows = lambda i, be, nv: (jnp.minimum(i, nv[0] - 1), 0)
    wmap = lambda i, be, nv: (be[jnp.minimum(i, nv[0] - 1)], 0, 0)
    return pl.pallas_call(
        _moe_kernel,
        grid_spec=pltpu.PrefetchScalarGridSpec(
            num_scalar_prefetch=2,
            grid=(n_blocks,),
            in_specs=[pl.BlockSpec((blk, PIECE), rows)] * SC_SPLIT
            + [pl.BlockSpec((None, D_MODEL, 2 * EXPERT_DIM), wmap),
               pl.BlockSpec((None, EXPERT_DIM, D_MODEL), wmap)],
            out_specs=[pl.BlockSpec((blk, PIECE), rows)] * SC_SPLIT,
            scratch_shapes=[pltpu.VMEM((D_MODEL, 2 * EXPERT_DIM), BF16),
                            pltpu.VMEM((EXPERT_DIM, D_MODEL), BF16)]),
        out_shape=[jax.ShapeDtypeStruct((n_slots, PIECE), jnp.int32)] * SC_SPLIT,
        compiler_params=_params(("arbitrary",)),
        name="moe",
    )(block_e, n_valid, *x_sorted, w_gu, w_dn)


def _dest_kernel(e_ref, rank_ref, start_ref, dest_ref):
    e = e_ref[...]
    rows = lax.broadcasted_iota(jnp.int32, (N_EXPERTS, e.shape[1]), 0)
    start = jnp.sum(jnp.where(rows == e, start_ref[...], 0), axis=0, keepdims=True)
    dest_ref[...] = start + rank_ref[...]


def _dest(e_flat, rank, pad_start_col):
    n_assign = e_flat.shape[1]
    n = RANK_TILE
    tile = pl.BlockSpec((1, n), lambda i: (0, i))
    return pl.pallas_call(
        _dest_kernel,
        grid=(n_assign // n,),
        in_specs=[tile, tile, pl.BlockSpec((N_EXPERTS, 1), lambda i: (0, 0))],
        out_specs=tile,
        out_shape=jax.ShapeDtypeStruct((1, n_assign), jnp.int32),
        compiler_params=_params(("parallel",)),
        name="dest",
    )(e_flat, rank, pad_start_col)


def _sc_mesh():
    return plsc.VectorSubcoreMesh(core_axis_name="core", subcore_axis_name="subcore")


def _sc_scatter_rows(src, dest, n_slots):
    n_src, width = src.shape
    n_assign = dest.shape[1]
    src_blocks = n_src // SC_WINDOW

    @functools.partial(pl.kernel, out_type=jax.ShapeDtypeStruct((n_slots, width), src.dtype),
                       mesh=_sc_mesh(), scratch_types=[])
    def scatter(src_hbm, dest_hbm, out_hbm):
        def body(rows_vmem, dest_vmem):
            pltpu.sync_copy(rows_vmem, out_hbm.at[dest_vmem.at[0]])

        pltpu.emit_pipeline(
            body,
            grid=(n_assign // SC_WINDOW,),
            in_specs=[pl.BlockSpec((SC_WINDOW, width), lambda i: (i % src_blocks, 0)),
                      pl.BlockSpec((1, SC_WINDOW), lambda i: (0, i))],
            out_specs=[],
            core_axis_name=("core", "subcore"),
            dimension_semantics=(pltpu.PARALLEL,),
        )(src_hbm, dest_hbm)

    return scatter(src, dest)


def _sc_gather_rows(table, idx):
    width = table.shape[1]
    n_assign = idx.shape[1]

    @functools.partial(pl.kernel, out_type=jax.ShapeDtypeStruct((n_assign, width), table.dtype),
                       mesh=_sc_mesh(), scratch_types=[])
    def gather(table_hbm, idx_hbm, out_hbm):
        def body(idx_vmem, rows_vmem):
            pltpu.sync_copy(table_hbm.at[idx_vmem.at[0]], rows_vmem)

        pltpu.emit_pipeline(
            body,
            grid=(n_assign // SC_WINDOW,),
            in_specs=[pl.BlockSpec((1, SC_WINDOW), lambda i: (0, i))],
            out_specs=[pl.BlockSpec((SC_WINDOW, width), lambda i: (i, 0))],
            core_axis_name=("core", "subcore"),
            dimension_semantics=(pltpu.PARALLEL,),
        )(idx_hbm, out_hbm)

    return gather(table, idx)


def _final_kernel(x1_ref, *refs):
    h2_refs = refs[:SC_SPLIT]
    yg_refs = refs[SC_SPLIT:2 * SC_SPLIT]
    wt_ref, gt2_ref, wsg_ref, wsd_ref, gf_ref, o_ref = refs[2 * SC_SPLIT:]
    h2 = _unpack_rows([r[...] for r in h2_refs]).astype(BF16)
    gu = jnp.dot(h2, wsg_ref[...], preferred_element_type=F32)
    hid = _silu(gu[:, :SHARED_DIM]) * gu[:, SHARED_DIM:]
    y = jnp.dot(hid.astype(BF16), wsd_ref[...], preferred_element_type=F32)
    wt = wt_ref[...]
    for k in range(TOP_K):
        y = y + wt[:, k:k + 1] * _unpack_rows([r[k] for r in yg_refs])
    x2 = x1_ref[...] + gt2_ref[...] * y
    o_ref[...] = _rms(x2, gf_ref[...])


def _final(x1_all, h2_all, y_gath, wt_rows, gt2, wsg, wsd, gf, grp_of_tile):
    t_all = x1_all.shape[0]
    tm = ROW_TILE
    row = lambda i: (i, 0)
    fixed2 = lambda i: (0, 0)
    return pl.pallas_call(
        _final_kernel,
        grid=(t_all // tm,),
        in_specs=[pl.BlockSpec((tm, D_MODEL), row)]
        + [pl.BlockSpec((tm, PIECE), row)] * SC_SPLIT
        + [pl.BlockSpec((TOP_K, tm, PIECE), lambda i: (0, i, 0))] * SC_SPLIT
        + [pl.BlockSpec((tm, TOP_K), row),
           pl.BlockSpec((None, tm, D_MODEL), lambda i: (grp_of_tile(i), 0, 0)),
           pl.BlockSpec(wsg.shape, fixed2),
           pl.BlockSpec(wsd.shape, fixed2),
           pl.BlockSpec((1, D_MODEL), fixed2)],
        out_specs=pl.BlockSpec((tm, D_MODEL), row),
        out_shape=jax.ShapeDtypeStruct((t_all, D_MODEL), F32),
        compiler_params=_params(("parallel",)),
        name="final",
    )(x1_all, *h2_all, *y_gath, wt_rows, gt2, wsg, wsd, gf)


def _rope_tables(pos):
    freqs = ROPE_THETA ** (-jnp.arange(ROPE_HALF, dtype=F32) / ROPE_HALF)
    ang = pos.astype(F32)[:, None] * freqs
    return jnp.cos(ang), jnp.sin(ang)


def kernel(x_prompt, x_sample, cache_ckv, cache_krope, page_table, c_prompt, c_sample, w_ada, b_ada, norm1_g, norm2_g, w_in, a_vnorm_g, w_spatial, b_spatial, w_out_a, q_norm_g, w_uq, kv_norm_g, w_uk, w_uv, w_out_b, w_o, w_router, router_bias, w_exp_gu, w_exp_down, w_sh_gu, w_sh_down, final_norm_g):
    batch, seq, _ = x_prompt.shape
    dec_b, dec_s, _ = x_sample.shape
    n_pages = page_table.shape[1]
    n_past = n_pages * CHUNK
    t_p, t_s = batch * seq, dec_b * dec_s
    t_all = t_p + t_s
    tm = ROW_TILE
    assert seq % ATTN_Q_STEP == 0 and t_s % tm == 0 and CHUNK % dec_s == 0 and n_pages % PAGES_PER_STEP == 0
    assert (t_all * TOP_K) % RANK_TILE == 0 and w_ada.shape[0] == 1
    n_prompt_tiles = t_p // tm
    tiles_per_b = seq // tm

    def grp_of_tile(i):
        return jnp.minimum(i // tiles_per_b, batch) + jnp.maximum(i - n_prompt_tiles, 0)

    n_c = batch + dec_b
    c_rows = -(-n_c // 8) * 8
    c_all = jnp.concatenate([c_prompt, c_sample, jnp.zeros((c_rows - n_c, D_MODEL), F32)], axis=0)
    mod = _ada(c_all, w_ada[0].astype(BF16), b_ada[0][None, :])
    mod_p = jnp.broadcast_to(mod[:batch, None, :], (batch, tm, 6 * D_MODEL))
    mod_s = jnp.repeat(mod[batch:n_c], dec_s, axis=0).reshape(t_s // tm, tm, 6 * D_MODEL)
    mod_g = jnp.concatenate([mod_p, mod_s], axis=0)
    sh1, sc1, gt1, sh2, sc2, gt2 = [mod_g[:, :, k * D_MODEL:(k + 1) * D_MODEL] for k in range(6)]

    o_u, o_v, o_cq, o_ckv, o_kr = A_WIDTH, 2 * A_WIDTH, 2 * A_WIDTH + Q_LORA, 2 * A_WIDTH + Q_LORA + KV_LORA, \
        2 * A_WIDTH + Q_LORA + KV_LORA + QK_ROPE
    win = w_in[0]
    w5 = jnp.concatenate([win[:, :o_kr], jnp.zeros((D_MODEL, IN_END - IN_KR - QK_ROPE), F32)], axis=1).astype(BF16)
    wg = win[:, o_kr:].astype(BF16)
    row1 = lambda v: v.reshape(1, -1)
    tri = jnp.tril(jnp.ones((CHUNK, CHUNK), F32))
    wc_p = w_spatial[0] * tri
    per = CHUNK // dec_s
    small = (w_spatial[0] * tri)[:, :dec_s, :dec_s]
    wc_s = jnp.einsum('ab,gts->gatbs', jnp.eye(per, dtype=F32), small).reshape(A_GROUPS, CHUNK, CHUNK)
    wc = jnp.stack([wc_p, wc_s]).astype(BF16)
    bias_p = jnp.repeat(b_spatial[0].T, A_GROUP_DIM, axis=1)
    bias_s = jnp.tile(bias_p[:dec_s], (per, 1))
    bc = jnp.stack([bias_p, bias_s])

    pos_p = jnp.arange(seq)
    pos_s = n_past + jnp.arange(dec_s)
    cos_p, sin_p = _rope_tables(pos_p)
    cos_s, sin_s = _rope_tables(pos_s)
    cos_rows = jnp.concatenate([jnp.tile(cos_p, (batch, 1)), jnp.tile(cos_s, (dec_b, 1))], axis=0)
    sin_rows = jnp.concatenate([jnp.tile(sin_p, (batch, 1)), jnp.tile(sin_s, (dec_b, 1))], axis=0)
    lane_pad = jnp.zeros((t_all, LANES - QK_ROPE), F32)
    cos_kr = jnp.concatenate([cos_rows, cos_rows, lane_pad], axis=1)
    sin_kr = jnp.concatenate([-sin_rows, sin_rows, lane_pad], axis=1)

    x_all = jnp.concatenate([x_prompt.reshape(t_p, D_MODEL), x_sample.reshape(t_s, D_MODEL)], axis=0)
    g1, g2 = row1(norm1_g[0]), row1(norm2_g[0])
    a_all, v_all, cq_all, ckv_all, kr_all = _in_proj(
        x_all, sh1, sc1, g1, w5, row1(a_vnorm_g[0]), row1(q_norm_g[0]), row1(kv_norm_g[0]),
        cos_kr, sin_kr, wc, bc, grp_of_tile, n_prompt_tiles)

    wuq = w_uq[0]
    wuqt = jnp.pad(wuq, ((0, 0), (0, 0), (0, HEAD_PAD - QK_NOPE - QK_ROPE))).reshape(Q_LORA, -1).T.astype(BF16)
    wukp = jnp.pad(w_uk[0], ((0, 0), (0, 0), (0, HEAD_PAD - QK_NOPE))).reshape(KV_LORA, -1).astype(BF16)
    place_h = jnp.pad(jnp.eye(QK_ROPE, dtype=F32), ((0, 0), (QK_NOPE, HEAD_PAD - QK_NOPE - QK_ROPE)))
    place = jnp.tile(place_h, (1, N_HEADS)).astype(BF16)
    wuvt = w_uv[0].reshape(KV_LORA, -1).T.astype(BF16)
    qt, k, vt = _qkv(cq_all, ckv_all, kr_all, wuqt, wukp, place, wuvt, cos_p.T, sin_p.T, batch, seq)
    ot = _attn(qt, k, vt)

    wq_s = jnp.concatenate([wuq[:, :, :QK_NOPE].reshape(Q_LORA, -1),
                            wuq[:, :, QK_NOPE:QK_NOPE + ROPE_HALF].reshape(Q_LORA, -1),
                            wuq[:, :, QK_NOPE + ROPE_HALF:].reshape(Q_LORA, -1)], axis=1).astype(BF16)
    eye_h = jnp.eye(N_HEADS, dtype=F32)
    wuk_blk = jnp.einsum('rhd,hg->hdgr', w_uk[0], eye_h).reshape(N_HEADS * QK_NOPE, N_HEADS * KV_LORA).astype(BF16)
    wuv_blk = jnp.einsum('rhd,hg->hrgd', w_uv[0], eye_h).reshape(N_HEADS * KV_LORA, N_HEADS * V_HEAD).astype(BF16)
    cos_sq = jnp.tile(jnp.tile(cos_s, (1, N_HEADS)), (dec_b, 1))
    sin_sq = jnp.tile(jnp.tile(sin_s, (1, N_HEADS)), (dec_b, 1))
    ql, r1, r2 = _sample_q(cq_all[t_p:], wq_s, wuk_blk, cos_sq, sin_sq)
    rows = dec_s * N_HEADS
    ql3 = ql.reshape(dec_b, rows, KV_LORA)
    qr3 = jnp.concatenate([r1.reshape(dec_b, dec_s, N_HEADS, ROPE_HALF),
                           r2.reshape(dec_b, dec_s, N_HEADS, ROPE_HALF)], axis=-1).reshape(dec_b, rows, QK_ROPE)
    new_pad = ((0, 0), (0, 8 - dec_s), (0, 0))
    ckv_s = jnp.pad(ckv_all[t_p:].reshape(dec_b, dec_s, KV_LORA), new_pad)
    kr_s_t = jnp.swapaxes(jnp.pad(kr_all[t_p:].reshape(dec_b, dec_s, QK_ROPE), new_pad), 1, 2)
    cache_kr_t = jnp.swapaxes(cache_krope[0], 1, 2)
    o_lat = _sample_attn(page_table, cache_ckv[0], cache_kr_t, ql3, qr3, ckv_s, kr_s_t)
    o_s = _matmul(o_lat.reshape(t_s, N_HEADS * KV_LORA), wuv_blk, BF16)

    x1_all, *h2_all = _merge(x_all, (sh1, sc1, gt1, sh2, sc2), g1, g2, a_all, ot, o_s, wg,
                            w_out_a[0].astype(BF16), w_out_b[0].astype(BF16), w_o[0].astype(BF16),
                            grp_of_tile, n_prompt_tiles)

    wr_t = w_router[0].T
    whi = wr_t.astype(BF16)
    wlo = (wr_t - whi.astype(F32)).astype(BF16)
    idx_t, wt_t = _route(x1_all, sh2, sc2, g2, whi, wlo, router_bias[0].reshape(-1, 1), grp_of_tile)
    n_assign = t_all * TOP_K
    e_flat = idx_t.reshape(1, n_assign)
    rank, counts = _rank(e_flat)
    counts = counts[:, 0].astype(jnp.int32)
    blk = MOE_BLOCK
    padded = (counts + blk - 1) // blk * blk
    pad_end = jnp.cumsum(padded)
    pad_start = pad_end - padded
    dest = _dest(e_flat, rank, pad_start.reshape(-1, 1))
    n_blocks = -(-n_assign // blk) + N_EXPERTS
    n_slots = n_blocks * blk
    first_row = jnp.arange(n_blocks, dtype=jnp.int32) * blk
    block_e = jnp.minimum(jnp.sum(pad_end[None, :] <= first_row[:, None], axis=1), N_EXPERTS - 1).astype(jnp.int32)
    n_valid = (pad_end[-1] // blk).astype(jnp.int32).reshape(1)
    x_sorted = [_sc_scatter_rows(h, dest, n_slots) for h in h2_all]
    y_sorted = _moe(block_e, n_valid, x_sorted, w_exp_gu[0], w_exp_down[0])
    y_gath = [_sc_gather_rows(y, dest).reshape(TOP_K, t_all, PIECE) for y in y_sorted]

    y_all = _final(x1_all, h2_all, y_gath, wt_t.T, gt2, w_sh_gu[0].astype(BF16), w_sh_down[0].astype(BF16),
                   row1(final_norm_g), grp_of_tile)

    y_prompt = y_all[:t_p].reshape(batch, seq, D_MODEL)
    y_sample = y_all[t_p:].reshape(dec_b, dec_s, D_MODEL)
    new_ckv_prompt = ckv_all[:t_p].reshape(1, batch, seq, KV_LORA)
    new_krope_prompt = kr_all[:t_p].reshape(1, batch, seq, QK_ROPE)
    new_ckv_sample = ckv_all[t_p:].reshape(1, dec_b, dec_s, KV_LORA)
    new_krope_sample = kr_all[t_p:].reshape(1, dec_b, dec_s, QK_ROPE)
    new_chunk_v_sample = v_all[t_p:].reshape(1, dec_b, dec_s, A_WIDTH)
    return (y_prompt, y_sample, new_ckv_prompt, new_krope_prompt, new_ckv_sample, new_krope_sample,
            new_chunk_v_sample)
```

```python
import functools

import jax
import jax.numpy as jnp
from jax import lax
from jax.experimental import pallas as pl
from jax.experimental.pallas import tpu as pltpu
from jax.experimental.pallas import tpu_sc as plsc

F32 = jnp.float32
BF16 = jnp.bfloat16

D_MODEL = 1024
A_WIDTH = D_MODEL // 2
A_GROUPS = 8
A_GROUP_DIM = A_WIDTH // A_GROUPS
CHUNK = 128
N_HEADS = 8
QK_NOPE = 64
QK_ROPE = 32
ROPE_HALF = QK_ROPE // 2
V_HEAD = 64
Q_LORA = 384
KV_LORA = 256
ROPE_THETA = 10000.0
ATTN_SCALE = (QK_NOPE + QK_ROPE) ** -0.5
N_EXPERTS = 256
TOP_K = 8
N_GROUPS = 8
GROUP_SIZE = N_EXPERTS // N_GROUPS
TOPK_GROUPS = 4
EXPERT_DIM = 256
SHARED_DIM = 256
ROUTED_SCALE = 2.5
EPS = 1e-6

LANES = 128
HEAD_PAD = 128
ROW_TILE = 256
ATTN_KV_TILE = 256
ATTN_Q_STEP = 1024
ATTN_HEADS_PER_STEP = 2
LOG2_E = 1.4426950408889634
RANK_TILE = 512
MOE_BLOCK = 256
MOE_SUB = 128
PAGES_PER_STEP = 16
PAGE_GROUP = 4
SC_WINDOW = 128
SC_SPLIT = 2
PIECE = D_MODEL // 2 // SC_SPLIT
VMEM_LIMIT = 48 * 1024 * 1024

IN_U, IN_V, IN_CQ, IN_CKV, IN_KR, IN_END = 0, 512, 1024, 1408, 1664, 1792

NT_DIMS = (((1,), (1,)), ((), ()))
TN_DIMS = (((0,), (0,)), ((), ()))


def _params(sem, vmem=VMEM_LIMIT):
    return pltpu.CompilerParams(dimension_semantics=sem, vmem_limit_bytes=vmem)


def _rms(x, g):
    return x * lax.rsqrt(jnp.mean(x * x, axis=-1, keepdims=True) + EPS) * g


def _gelu(x):
    return 0.5 * x * (1.0 + jnp.tanh(0.7978845608028654 * (x + 0.044715 * (x * x * x))))


def _sigmoid(x):
    return 1.0 / (1.0 + jnp.exp(-x))


def _silu(x):
    return x * _sigmoid(x)


def _pack_rows(x):
    pieces = []
    for c in range(SC_SPLIT):
        lo = x[:, 2 * c * PIECE:(2 * c + 1) * PIECE].astype(BF16).astype(F32)
        hi = x[:, (2 * c + 1) * PIECE:(2 * c + 2) * PIECE].astype(BF16).astype(F32)
        pieces.append(lax.bitcast_convert_type(hi, jnp.int32)
                      | lax.shift_right_logical(lax.bitcast_convert_type(lo, jnp.int32), 16))
    return pieces


def _unpack_rows(pieces):
    cols = []
    for p in pieces:
        cols.append(lax.bitcast_convert_type(lax.shift_left(p, 16), F32))
        cols.append(lax.bitcast_convert_type(p & jnp.int32(-65536), F32))
    return jnp.concatenate(cols, axis=1)


def _ada_kernel(c_ref, w_ref, b_ref, o_ref):
    s = _silu(c_ref[...]).astype(BF16)
    o_ref[...] = jnp.dot(s, w_ref[...], preferred_element_type=F32) + b_ref[...]


def _ada(c, w, b):
    rows, n = c.shape[0], w.shape[1]
    tn = 1536
    return pl.pallas_call(
        _ada_kernel,
        grid=(n // tn,),
        in_specs=[pl.BlockSpec((rows, D_MODEL), lambda j: (0, 0)),
                  pl.BlockSpec((D_MODEL, tn), lambda j: (0, j)),
                  pl.BlockSpec((1, tn), lambda j: (0, j))],
        out_specs=pl.BlockSpec((rows, tn), lambda j: (0, j)),
        out_shape=jax.ShapeDtypeStruct((rows, n), F32),
        compiler_params=_params(("parallel",)),
        name="ada",
    )(c, w, b)


def _in_kernel(n_prompt_tiles, xp_ref, xs_ref, sh_ref, sc_ref, g1_ref, w_ref, vg_ref, qg_ref, kg_ref,
               cos_ref, sin_ref, wc_ref, bc_ref,
               a_ref, cq_ref, ckvp_ref, krp_ref, ckvs_ref, krs_ref, vs_ref):
    tm = xp_ref.shape[0]
    is_prompt = pl.program_id(0) < n_prompt_tiles
    x = jnp.where(is_prompt, xp_ref[...], xs_ref[...])
    h = _rms(x, g1_ref[...]) * (1.0 + sc_ref[...]) + sh_ref[...]
    z = jnp.dot(h.astype(BF16), w_ref[...], preferred_element_type=F32)
    u = _gelu(z[:, IN_U:IN_V])
    v = _rms(_gelu(z[:, IN_V:IN_CQ]), vg_ref[...])
    cq_ref[...] = _rms(z[:, IN_CQ:IN_CKV], qg_ref[...]).astype(BF16)
    ckv = _rms(z[:, IN_CKV:IN_KR], kg_ref[...])
    kr = z[:, IN_KR:IN_END]
    lane = lax.broadcasted_iota(jnp.int32, kr.shape, 1)
    swapped = jnp.where(lane < ROPE_HALF, pltpu.roll(kr, LANES - ROPE_HALF, 1), pltpu.roll(kr, ROPE_HALF, 1))
    kr_rot = (kr * cos_ref[...] + swapped * sin_ref[...])[:, :QK_ROPE]

    @pl.when(is_prompt)
    def _():
        ckvp_ref[...] = ckv
        krp_ref[...] = kr_rot

    @pl.when(jnp.logical_not(is_prompt))
    def _():
        ckvs_ref[...] = ckv
        krs_ref[...] = kr_rot
        vs_ref[...] = v

    vb = v.astype(BF16)
    half = A_WIDTH // 2
    grp_of_lane = lax.broadcasted_iota(jnp.int32, (CHUNK, half), 1) // A_GROUP_DIM
    for ci in range(tm // CHUNK):
        rows = slice(ci * CHUNK, (ci + 1) * CHUNK)
        parts = []
        for q in range(2):
            vq = vb[rows, q * half:(q + 1) * half]
            acc = None
            for gg in range(A_GROUPS // 2):
                vm = jnp.where(grp_of_lane == gg, vq, jnp.zeros_like(vq))
                part = jnp.dot(wc_ref[q * (A_GROUPS // 2) + gg], vm, preferred_element_type=F32)
                acc = part if acc is None else acc + part
            parts.append(acc)
        sp = jnp.concatenate(parts, axis=1) + bc_ref[...]
        a_ref[rows, :] = (u[rows, :] * sp).astype(BF16)


def _in_proj(x_p, x_s, sh1, sc1, g1, w5, vg, qg, kg, cos_kr, sin_kr, wc, bc, grp_of_tile, n_prompt_tiles):
    t_p, t_s = x_p.shape[0], x_s.shape[0]
    t_all = t_p + t_s
    tm = ROW_TILE
    n_tiles = t_all // tm
    row = lambda i: (i, 0)
    prow = lambda i: (jnp.minimum(i, n_prompt_tiles - 1), 0)
    srow = lambda i: (jnp.maximum(i - n_prompt_tiles, 0), 0)
    fixed2 = lambda i: (0, 0)
    kind = lambda i: (jnp.where(i < n_prompt_tiles, 0, 1), 0, 0)
    kind4 = lambda i: (jnp.where(i < n_prompt_tiles, 0, 1), 0, 0, 0)
    mod = lambda i: (grp_of_tile(i), 0, 0)
    return pl.pallas_call(
        functools.partial(_in_kernel, n_prompt_tiles),
        grid=(n_tiles,),
        in_specs=[pl.BlockSpec((tm, D_MODEL), prow),
                  pl.BlockSpec((tm, D_MODEL), srow),
                  pl.BlockSpec((None, tm, D_MODEL), mod),
                  pl.BlockSpec((None, tm, D_MODEL), mod),
                  pl.BlockSpec((1, D_MODEL), fixed2),
                  pl.BlockSpec((D_MODEL, IN_END), fixed2),
                  pl.BlockSpec((1, A_WIDTH), fixed2),
                  pl.BlockSpec((1, Q_LORA), fixed2),
                  pl.BlockSpec((1, KV_LORA), fixed2),
                  pl.BlockSpec((tm, LANES), row),
                  pl.BlockSpec((tm, LANES), row),
                  pl.BlockSpec((None, A_GROUPS, CHUNK, CHUNK), kind4),
                  pl.BlockSpec((None, CHUNK, A_WIDTH), kind)],
        out_specs=[pl.BlockSpec((tm, A_WIDTH), row),
                   pl.BlockSpec((tm, Q_LORA), row),
                   pl.BlockSpec((tm, KV_LORA), prow),
                   pl.BlockSpec((tm, QK_ROPE), prow),
                   pl.BlockSpec((tm, KV_LORA), srow),
                   pl.BlockSpec((tm, QK_ROPE), srow),
                   pl.BlockSpec((tm, A_WIDTH), srow)],
        out_shape=[jax.ShapeDtypeStruct((t_all, A_WIDTH), BF16),
                   jax.ShapeDtypeStruct((t_all, Q_LORA), BF16),
                   jax.ShapeDtypeStruct((t_p, KV_LORA), F32),
                   jax.ShapeDtypeStruct((t_p, QK_ROPE), F32),
                   jax.ShapeDtypeStruct((t_s, KV_LORA), F32),
                   jax.ShapeDtypeStruct((t_s, QK_ROPE), F32),
                   jax.ShapeDtypeStruct((t_s, A_WIDTH), F32)],
        compiler_params=_params(("arbitrary",)),
        name="in_proj",
    )(x_p, x_s, sh1, sc1, g1, w5, vg, qg, kg, cos_kr, sin_kr, wc, bc)


def _qkv_kernel(cq_ref, ckv_ref, kr_ref, wuqt_ref, wukp_ref, place_ref, wuvt_ref, cos_ref, sin_ref,
                qt_ref, k_ref, vt_ref):
    tm = cq_ref.shape[0]
    ckv = ckv_ref[...].astype(BF16)
    kr = kr_ref[...].astype(BF16)
    qt = lax.dot_general(wuqt_ref[...], cq_ref[...], NT_DIMS, preferred_element_type=F32)
    c = cos_ref[...]
    s = sin_ref[...]
    pad = jnp.zeros((HEAD_PAD - QK_NOPE - QK_ROPE, tm), F32)
    for h in range(N_HEADS):
        blk = qt[h * HEAD_PAD:(h + 1) * HEAD_PAD]
        x1 = blk[QK_NOPE:QK_NOPE + ROPE_HALF]
        x2 = blk[QK_NOPE + ROPE_HALF:QK_NOPE + QK_ROPE]
        full = jnp.concatenate([blk[:QK_NOPE], x1 * c - x2 * s, x1 * s + x2 * c, pad], axis=0)
        qt_ref[h] = (full * (ATTN_SCALE * LOG2_E)).astype(BF16)
    k = (jnp.dot(ckv, wukp_ref[...], preferred_element_type=F32)
         + jnp.dot(kr, place_ref[...], preferred_element_type=F32))
    for h in range(N_HEADS):
        k_ref[h] = k[:, h * HEAD_PAD:(h + 1) * HEAD_PAD].astype(BF16)
    vt = lax.dot_general(wuvt_ref[...], ckv, NT_DIMS, preferred_element_type=F32)
    for h in range(N_HEADS):
        vt_ref[h] = vt[h * V_HEAD:(h + 1) * V_HEAD].astype(BF16)


def _qkv(cq_all, ckv_all, kr_all, wuqt, wukp, place, wuvt, cos_t, sin_t, batch, seq):
    tm = ATTN_KV_TILE
    nk = seq // tm
    row = lambda b, j: (b * nk + j, 0)
    fixed2 = lambda b, j: (0, 0)
    tab = lambda b, j: (0, j)
    return pl.pallas_call(
        _qkv_kernel,
        grid=(batch, nk),
        in_specs=[pl.BlockSpec((tm, Q_LORA), row),
                  pl.BlockSpec((tm, KV_LORA), row),
                  pl.BlockSpec((tm, QK_ROPE), row),
                  pl.BlockSpec(wuqt.shape, fixed2),
                  pl.BlockSpec(wukp.shape, fixed2),
                  pl.BlockSpec(place.shape, fixed2),
                  pl.BlockSpec(wuvt.shape, fixed2),
                  pl.BlockSpec((ROPE_HALF, tm), tab),
                  pl.BlockSpec((ROPE_HALF, tm), tab)],
        out_specs=[pl.BlockSpec((None, N_HEADS, HEAD_PAD, tm), lambda b, j: (b, 0, 0, j)),
                   pl.BlockSpec((None, N_HEADS, tm, HEAD_PAD), lambda b, j: (b, 0, j, 0)),
                   pl.BlockSpec((None, N_HEADS, None, V_HEAD, tm), lambda b, j: (b, 0, j, 0, 0))],
        out_shape=[jax.ShapeDtypeStruct((batch, N_HEADS, HEAD_PAD, seq), BF16),
                   jax.ShapeDtypeStruct((batch, N_HEADS, seq, HEAD_PAD), BF16),
                   jax.ShapeDtypeStruct((batch, N_HEADS, nk, V_HEAD, tm), BF16)],
        compiler_params=_params(("parallel", "parallel")),
        name="qkv",
    )(cq_all, ckv_all, kr_all, wuqt, wukp, place, wuvt, cos_t, sin_t)


def _attn_kernel(qt_ref, k_ref, vt_ref, o_ref):
    tk = ATTN_KV_TILE
    n_heads = qt_ref.shape[0]
    n_sub = qt_ref.shape[2] // tk
    qi = pl.program_id(2)
    j0 = qi * n_sub
    chains = [(h, sub) for h in range(n_heads) for sub in range(n_sub)]

    def tiles(j, active, carries, diagonal_sub):
        scores = [jnp.dot(k_ref[h, j], qt_ref[h, :, sub * tk:(sub + 1) * tk], preferred_element_type=F32)
                  for h, sub in active]
        stats = []
        for (h, sub), s in zip(active, scores):
            m, l, _ = carries[h * n_sub + sub]
            if sub == diagonal_sub:
                key = lax.broadcasted_iota(jnp.int32, s.shape, 0)
                qry = lax.broadcasted_iota(jnp.int32, s.shape, 1)
                s = jnp.where(key <= qry, s, -jnp.inf)
            m_new = jnp.maximum(m, jnp.max(s, axis=0, keepdims=True))
            alpha = jnp.exp2(m - m_new)
            p = jnp.exp2(s - m_new)
            stats.append((m_new, alpha * l + jnp.sum(p, axis=0, keepdims=True), alpha, p.astype(BF16)))
        out = list(carries)
        for (h, sub), (m_new, l_new, alpha, p) in zip(active, stats):
            c = h * n_sub + sub
            out[c] = (m_new, l_new, alpha * carries[c][2] + jnp.dot(vt_ref[h, j], p, preferred_element_type=F32))
        return out

    init = tuple((jnp.full((1, tk), -jnp.inf, F32), jnp.zeros((1, tk), F32), jnp.zeros((V_HEAD, tk), F32))
                 for _ in chains)
    carries = list(lax.fori_loop(0, j0, lambda j, c: tuple(tiles(j, chains, c, None)), init))
    for jj in range(n_sub):
        carries = tiles(j0 + jj, [(h, sub) for h, sub in chains if sub >= jj], carries, jj)
    for h, sub in chains:
        _, l, acc = carries[h * n_sub + sub]
        o_ref[h * V_HEAD:(h + 1) * V_HEAD, sub * tk:(sub + 1) * tk] = (acc / l).astype(BF16)


def _attn(qt, k, vt):
    batch, _, _, seq = qt.shape
    tk = ATTN_KV_TILE
    nk = seq // tk
    hp = ATTN_HEADS_PER_STEP
    k5 = k.reshape(batch, N_HEADS, nk, tk, HEAD_PAD)
    return pl.pallas_call(
        _attn_kernel,
        grid=(batch, N_HEADS // hp, seq // ATTN_Q_STEP),
        in_specs=[pl.BlockSpec((None, hp, HEAD_PAD, ATTN_Q_STEP), lambda b, h, i: (b, h, 0, i)),
                  pl.BlockSpec((None, hp, nk, tk, HEAD_PAD), lambda b, h, i: (b, h, 0, 0, 0)),
                  pl.BlockSpec((None, hp, nk, V_HEAD, tk), lambda b, h, i: (b, h, 0, 0, 0))],
        out_specs=pl.BlockSpec((None, hp * V_HEAD, ATTN_Q_STEP), lambda b, h, i: (b, h, i)),
        out_shape=jax.ShapeDtypeStruct((batch, N_HEADS * V_HEAD, seq), BF16),
        compiler_params=_params(("parallel", "parallel", "parallel")),
        name="attn",
    )(qt, k5, vt)


def _sq_kernel(cq_ref, wq_ref, wukb_ref, cos_ref, sin_ref, ql_ref, r1_ref, r2_ref):
    n_nope = N_HEADS * QK_NOPE
    q = jnp.dot(cq_ref[...], wq_ref[...], preferred_element_type=F32)
    x1 = q[:, n_nope:n_nope + LANES]
    x2 = q[:, n_nope + LANES:]
    c = cos_ref[...]
    s = sin_ref[...]
    r1_ref[...] = ((x1 * c - x2 * s) * ATTN_SCALE).astype(BF16)
    r2_ref[...] = ((x1 * s + x2 * c) * ATTN_SCALE).astype(BF16)
    ql = jnp.dot(q[:, :n_nope].astype(BF16), wukb_ref[...], preferred_element_type=F32)
    ql_ref[...] = (ql * ATTN_SCALE).astype(BF16)


def _sample_q(cq_s, wq_s, wuk_blk, cos_s, sin_s):
    ts = cq_s.shape[0]
    full = lambda shape: pl.BlockSpec(shape, lambda i: (0,) * len(shape))
    return pl.pallas_call(
        _sq_kernel,
        grid=(1,),
        in_specs=[full(cq_s.shape), full(wq_s.shape), full(wuk_blk.shape), full(cos_s.shape), full(sin_s.shape)],
        out_specs=[full((ts, N_HEADS * KV_LORA)), full((ts, LANES)), full((ts, LANES))],
        out_shape=[jax.ShapeDtypeStruct((ts, N_HEADS * KV_LORA), BF16),
                   jax.ShapeDtypeStruct((ts, LANES), BF16),
                   jax.ShapeDtypeStruct((ts, LANES), BF16)],
        compiler_params=_params(("arbitrary",)),
        name="sample_q",
    )(cq_s, wq_s, wuk_blk, cos_s, sin_s)


def _sattn_kernel(n_pages, pt_ref, ckv_hbm, kr_hbm, ql_ref, qr_ref, cnew_ref, knew_ref, o_ref,
                  cbuf, rbuf, sem_c, sem_r):
    npg = PAGES_PER_STEP
    n_chunks = n_pages // npg
    PAGE_SLOTS = cbuf.shape[0]
    PAGE_LOOKAHEAD = PAGE_SLOTS // 2
    b = pl.program_id(0)

    def page_copies(bb, chunk, slot, p):
        page = pt_ref[bb * n_pages + chunk * npg + p]
        return (pltpu.make_async_copy(ckv_hbm.at[page], cbuf.at[slot, p], sem_c.at[slot]),
                pltpu.make_async_copy(kr_hbm.at[page], rbuf.at[slot, p], sem_r.at[slot]))

    def start_chunk(bb, chunk, slot):
        for p in range(npg):
            for cp in page_copies(bb, chunk, slot, p):
                cp.start()

    def wait_chunk(bb, chunk, slot):
        for p in range(npg):
            for cp in page_copies(bb, chunk, slot, p):
                cp.wait()

    @pl.when(b == 0)
    def _():
        for chunk in range(PAGE_LOOKAHEAD):
            start_chunk(0, chunk, chunk)

    ql = ql_ref[...]
    qr = qr_ref[...]

    def partial_softmax(blocks, mask=None):
        scores = [lax.dot_general(ql, keys, NT_DIMS, preferred_element_type=F32)
                  + jnp.dot(qr, rope_t, preferred_element_type=F32) for keys, rope_t in blocks]
        probs = []
        for s in scores:
            if mask is not None:
                s = jnp.where(mask(s.shape), s, -jnp.inf)
            m = jnp.max(s, axis=-1, keepdims=True)
            p = jnp.exp(s - m)
            probs.append((m, jnp.sum(p, axis=-1, keepdims=True), p.astype(BF16)))
        return [(m, l, jnp.dot(p, keys, preferred_element_type=F32)) for (m, l, p), (keys, _) in zip(probs, blocks)]

    def merge(state, parts):
        m_old, l_old, acc_old = state
        m_new = m_old
        for m, _, _ in parts:
            m_new = jnp.maximum(m_new, m)
        alpha = jnp.exp(m_old - m_new)
        l = alpha * l_old
        acc = alpha * acc_old
        for m, lp, op in parts:
            w = jnp.exp(m - m_new)
            l = l + w * lp
            acc = acc + w * op
        return m_new, l, acc

    def chunk_step(chunk, state):
        slot = chunk % PAGE_SLOTS
        ahead = chunk + PAGE_LOOKAHEAD
        ahead_slot = ahead % PAGE_SLOTS

        @pl.when(ahead < n_chunks)
        def _():
            start_chunk(b, ahead, ahead_slot)

        @pl.when((ahead >= n_chunks) & (b + 1 < pl.num_programs(0)))
        def _():
            start_chunk(b + 1, ahead - n_chunks, ahead_slot)

        wait_chunk(b, chunk, slot)
        blocks = []
        for g in range(npg // PAGE_GROUP):
            pages = range(g * PAGE_GROUP, (g + 1) * PAGE_GROUP)
            blocks.append((jnp.concatenate([cbuf[slot, p].astype(BF16) for p in pages], axis=0),
                           jnp.concatenate([rbuf[slot, p].astype(BF16) for p in pages], axis=1)))
        return merge(state, partial_softmax(blocks))

    rows = ql.shape[0]
    state = (jnp.full((rows, 1), -jnp.inf, F32), jnp.zeros((rows, 1), F32), jnp.zeros((rows, KV_LORA), F32))
    state = lax.fori_loop(0, n_chunks, chunk_step, state)

    def causal(shape):
        q_pos = lax.broadcasted_iota(jnp.int32, shape, 0) // N_HEADS
        return lax.broadcasted_iota(jnp.int32, shape, 1) <= q_pos

    _, l, acc = merge(state, partial_softmax([(cnew_ref[...].astype(BF16), knew_ref[...].astype(BF16))], causal))
    o_ref[...] = acc / l


def _sample_attn(page_table, cache_ckv, cache_kr_t, ql, qr, cnew, knew_t):
    dec_b, n_pages = page_table.shape
    npg = PAGES_PER_STEP
    n_chunks = n_pages // npg
    slots = 4 if n_chunks % 4 == 0 else 2
    assert n_chunks % slots == 0
    rows = ql.shape[1]
    n_new = cnew.shape[1]
    per_b = lambda b, pt: (b, 0, 0)
    return pl.pallas_call(
        functools.partial(_sattn_kernel, n_pages),
        grid_spec=pltpu.PrefetchScalarGridSpec(
            num_scalar_prefetch=1,
            grid=(dec_b,),
            in_specs=[pl.BlockSpec(memory_space=pl.ANY),
                      pl.BlockSpec(memory_space=pl.ANY),
                      pl.BlockSpec((None, rows, KV_LORA), per_b),
                      pl.BlockSpec((None, rows, QK_ROPE), per_b),
                      pl.BlockSpec((None, n_new, KV_LORA), per_b),
                      pl.BlockSpec((None, QK_ROPE, n_new), per_b)],
            out_specs=pl.BlockSpec((None, rows, KV_LORA), per_b),
            scratch_shapes=[pltpu.VMEM((slots, npg, CHUNK, KV_LORA), F32),
                            pltpu.VMEM((slots, npg, QK_ROPE, CHUNK), F32),
                            pltpu.SemaphoreType.DMA((slots,)),
                            pltpu.SemaphoreType.DMA((slots,))]),
        out_shape=jax.ShapeDtypeStruct((dec_b, rows, KV_LORA), F32),
        compiler_params=_params(("arbitrary",)),
        name="sample_attn",
    )(page_table.reshape(-1), cache_ckv, cache_kr_t, ql, qr, cnew, knew_t)


def _mm_kernel(x_ref, w_ref, o_ref):
    o_ref[...] = jnp.dot(x_ref[...].astype(BF16), w_ref[...], preferred_element_type=F32).astype(o_ref.dtype)


def _matmul(x, w, out_dtype):
    m, n = x.shape[0], w.shape[1]
    full = lambda shape: pl.BlockSpec(shape, lambda i: (0,) * len(shape))
    return pl.pallas_call(
        _mm_kernel,
        grid=(1,),
        in_specs=[full(x.shape), full(w.shape)],
        out_specs=full((m, n)),
        out_shape=jax.ShapeDtypeStruct((m, n), out_dtype),
        compiler_params=_params(("arbitrary",)),
        name="matmul",
    )(x, w)


def _merge_kernel(n_prompt_tiles, xp_ref, xs_ref, sh1_ref, sc1_ref, gt1_ref, sh2_ref, sc2_ref, g1_ref, g2_ref,
                  a_ref, ot_ref, os_ref, wg_ref, woa_ref, wob_ref, wo_ref, x1_ref, *rest):
    h2_refs, yb_ref = rest[:SC_SPLIT], rest[SC_SPLIT]
    i = pl.program_id(0)
    x = jnp.where(i < n_prompt_tiles, xp_ref[...], xs_ref[...])
    h = _rms(x, g1_ref[...]) * (1.0 + sc1_ref[...]) + sh1_ref[...]
    gates = _sigmoid(jnp.dot(h.astype(BF16), wg_ref[...], preferred_element_type=F32))
    y_a = jnp.dot(a_ref[...], woa_ref[...], preferred_element_type=F32)

    @pl.when(i < n_prompt_tiles)
    def _():
        yb_ref[...] = lax.dot_general(ot_ref[...], wob_ref[...], TN_DIMS, preferred_element_type=F32)

    @pl.when(i >= n_prompt_tiles)
    def _():
        yb_ref[...] = jnp.dot(os_ref[...], wob_ref[...], preferred_element_type=F32)

    z = gates[:, :D_MODEL] * y_a + gates[:, D_MODEL:] * yb_ref[...]
    y = jnp.dot(z.astype(BF16), wo_ref[...], preferred_element_type=F32)
    x1 = x + gt1_ref[...] * y
    x1_ref[...] = x1
    for ref, piece in zip(h2_refs, _pack_rows(_rms(x1, g2_ref[...]) * (1.0 + sc2_ref[...]) + sh2_ref[...])):
        ref[...] = piece


def _merge(x_p, x_s, mods, g1, g2, a_all, ot, o_s, wg, woa, wob, wo, grp_of_tile, n_prompt_tiles):
    t_all = x_p.shape[0] + x_s.shape[0]
    tm = ROW_TILE
    seq = ot.shape[2]
    tpb = seq // tm
    row = lambda i: (i, 0)
    fixed2 = lambda i: (0, 0)
    mod = lambda i: (grp_of_tile(i), 0, 0)

    def ot_map(i):
        ic = jnp.minimum(i, n_prompt_tiles - 1)
        return (ic // tpb, 0, ic % tpb)

    os_map = lambda i: (jnp.maximum(i - n_prompt_tiles, 0), 0)
    mod_spec = pl.BlockSpec((None, tm, D_MODEL), mod)
    return pl.pallas_call(
        functools.partial(_merge_kernel, n_prompt_tiles),
        grid=(t_all // tm,),
        in_specs=[pl.BlockSpec((tm, D_MODEL), lambda i: (jnp.minimum(i, n_prompt_tiles - 1), 0)),
                  pl.BlockSpec((tm, D_MODEL), os_map),
                  mod_spec, mod_spec, mod_spec, mod_spec, mod_spec,
                  pl.BlockSpec((1, D_MODEL), fixed2),
                  pl.BlockSpec((1, D_MODEL), fixed2),
                  pl.BlockSpec((tm, A_WIDTH), row),
                  pl.BlockSpec((None, N_HEADS * V_HEAD, tm), ot_map),
                  pl.BlockSpec((tm, N_HEADS * V_HEAD), os_map),
                  pl.BlockSpec(wg.shape, fixed2),
                  pl.BlockSpec(woa.shape, fixed2),
                  pl.BlockSpec(wob.shape, fixed2),
                  pl.BlockSpec(wo.shape, fixed2)],
        out_specs=[pl.BlockSpec((tm, D_MODEL), row)] + [pl.BlockSpec((tm, PIECE), row)] * SC_SPLIT,
        out_shape=[jax.ShapeDtypeStruct((t_all, D_MODEL), F32)]
        + [jax.ShapeDtypeStruct((t_all, PIECE), jnp.int32)] * SC_SPLIT,
        scratch_shapes=[pltpu.VMEM((tm, D_MODEL), F32)],
        compiler_params=_params(("parallel",)),
        name="merge",
    )(x_p, x_s, *mods, g1, g2, a_all, ot, o_s, wg, woa, wob, wo)


def _first_argmax(v, rows):
    mx = jnp.max(v, axis=0, keepdims=True)
    idx = jnp.min(jnp.where(v == mx, rows, v.shape[0]), axis=0, keepdims=True)
    return mx, idx


def _route_kernel(x1_ref, sh2_ref, sc2_ref, g2_ref, whi_ref, wlo_ref, bias_ref, idx_ref, wt_ref):
    h2 = _rms(x1_ref[...], g2_ref[...]) * (1.0 + sc2_ref[...]) + sh2_ref[...]
    hi = h2.astype(BF16)
    lo = (h2 - hi.astype(F32)).astype(BF16)
    whi = whi_ref[...]
    logits = (lax.dot_general(whi, hi, NT_DIMS, preferred_element_type=F32)
              + lax.dot_general(whi, lo, NT_DIMS, preferred_element_type=F32)
              + lax.dot_general(wlo_ref[...], hi, NT_DIMS, preferred_element_type=F32))
    scores = _sigmoid(logits)
    sel = scores + bias_ref[...]
    tm = sel.shape[1]
    neg = -jnp.inf
    rows_g = lax.broadcasted_iota(jnp.int32, (GROUP_SIZE, tm), 0)
    gscore = []
    for g in range(N_GROUPS):
        blk = sel[g * GROUP_SIZE:(g + 1) * GROUP_SIZE]
        m1, i1 = _first_argmax(blk, rows_g)
        m2 = jnp.max(jnp.where(rows_g == i1, neg, blk), axis=0, keepdims=True)
        gscore.append(m1 + m2)
    gs = jnp.concatenate(gscore, axis=0)
    rows_8 = lax.broadcasted_iota(jnp.int32, gs.shape, 0)
    chosen = jnp.zeros(gs.shape, jnp.int32)
    for _ in range(TOPK_GROUPS):
        _, gi = _first_argmax(gs, rows_8)
        hit = rows_8 == gi
        chosen = jnp.where(hit, 1, chosen)
        gs = jnp.where(hit, neg, gs)
    cand = jnp.concatenate(
        [jnp.where(chosen[g:g + 1] > 0, sel[g * GROUP_SIZE:(g + 1) * GROUP_SIZE], neg) for g in range(N_GROUPS)],
        axis=0)
    rows_e = lax.broadcasted_iota(jnp.int32, cand.shape, 0)
    idxs, wts = [], []
    for _ in range(TOP_K):
        _, ei = _first_argmax(cand, rows_e)
        hit = rows_e == ei
        idxs.append(ei)
        wts.append(jnp.sum(jnp.where(hit, scores, 0.0), axis=0, keepdims=True))
        cand = jnp.where(hit, neg, cand)
    w = jnp.concatenate(wts, axis=0)
    idx_ref[...] = jnp.concatenate(idxs, axis=0)
    wt_ref[...] = w / jnp.sum(w, axis=0, keepdims=True) * ROUTED_SCALE


def _route(x1_all, sh2, sc2, g2, whi, wlo, bias_col, grp_of_tile):
    t_all = x1_all.shape[0]
    tm = ROW_TILE
    fixed2 = lambda i: (0, 0)
    mod = lambda i: (grp_of_tile(i), 0, 0)
    return pl.pallas_call(
        _route_kernel,
        grid=(t_all // tm,),
        in_specs=[pl.BlockSpec((tm, D_MODEL), lambda i: (i, 0)),
                  pl.BlockSpec((None, tm, D_MODEL), mod),
                  pl.BlockSpec((None, tm, D_MODEL), mod),
                  pl.BlockSpec((1, D_MODEL), fixed2),
                  pl.BlockSpec(whi.shape, fixed2),
                  pl.BlockSpec(wlo.shape, fixed2),
                  pl.BlockSpec(bias_col.shape, fixed2)],
        out_specs=[pl.BlockSpec((TOP_K, tm), lambda i: (0, i)), pl.BlockSpec((TOP_K, tm), lambda i: (0, i))],
        out_shape=[jax.ShapeDtypeStruct((TOP_K, t_all), jnp.int32),
                   jax.ShapeDtypeStruct((TOP_K, t_all), F32)],
        compiler_params=_params(("parallel",)),
        name="route",
    )(x1_all, sh2, sc2, g2, whi, wlo, bias_col)


def _rank_kernel(e_ref, upper_ref, rank_ref, count_ref, run_ref):
    @pl.when(pl.program_id(0) == 0)
    def _():
        run_ref[...] = jnp.zeros(run_ref.shape, F32)

    e = e_ref[...]
    n = e.shape[1]
    rows = lax.broadcasted_iota(jnp.int32, (N_EXPERTS, n), 0)
    onehot = rows == e
    oh = jnp.where(onehot, 1.0, 0.0).astype(BF16)
    before = jnp.dot(oh, upper_ref[...], preferred_element_type=F32)
    run = run_ref[...]
    prior = jnp.concatenate([run] * (n // LANES), axis=1)
    rank = jnp.sum(jnp.where(onehot, before + prior, 0.0), axis=0, keepdims=True)
    rank_ref[...] = rank.astype(jnp.int32)
    total = run + jnp.dot(oh, jnp.ones((n, LANES), BF16), preferred_element_type=F32)
    run_ref[...] = total
    count_ref[...] = total


def _rank(e_flat):
    n_assign = e_flat.shape[1]
    n = RANK_TILE
    upper = (lax.broadcasted_iota(jnp.int32, (n, n), 0) < lax.broadcasted_iota(jnp.int32, (n, n), 1)).astype(BF16)
    return pl.pallas_call(
        _rank_kernel,
        grid=(n_assign // n,),
        in_specs=[pl.BlockSpec((1, n), lambda i: (0, i)), pl.BlockSpec((n, n), lambda i: (0, 0))],
        out_specs=[pl.BlockSpec((1, n), lambda i: (0, i)), pl.BlockSpec((N_EXPERTS, LANES), lambda i: (0, 0))],
        out_shape=[jax.ShapeDtypeStruct((1, n_assign), jnp.int32),
                   jax.ShapeDtypeStruct((N_EXPERTS, LANES), F32)],
        scratch_shapes=[pltpu.VMEM((N_EXPERTS, LANES), F32)],
        compiler_params=_params(("arbitrary",)),
        name="rank",
    )(e_flat, upper)


def _moe_kernel(be_ref, nv_ref, next_ref, *refs):
    x_refs = refs[:SC_SPLIT]
    wgu_hbm, wdn_hbm = refs[SC_SPLIT:SC_SPLIT + 2]
    y_refs = refs[SC_SPLIT + 2:2 * SC_SPLIT + 2]
    gu_f32, dn_f32, gu_bf, dn_bf, sem, run_ref = refs[2 * SC_SPLIT + 2:]
    i = pl.program_id(0)
    expert = be_ref[i]
    prev = be_ref[jnp.maximum(i - 1, 0)]

    def weight_copies(e, slot):
        return (pltpu.make_async_copy(wgu_hbm.at[e], gu_f32.at[slot], sem.at[0, slot]),
                pltpu.make_async_copy(wdn_hbm.at[e], dn_f32.at[slot], sem.at[1, slot]))

    @pl.when(i == 0)
    def _():
        run_ref[0] = 0
        for cp in weight_copies(expert, 0):
            cp.start()

    @pl.when(i < nv_ref[0])
    def _():
        @pl.when((i == 0) | (expert != prev))
        def _():
            run = run_ref[0] + jnp.where(i == 0, 0, 1)
            run_ref[0] = run
            slot = run % 2
            for cp in weight_copies(expert, slot):
                cp.wait()
            following = next_ref[expert]

            @pl.when(following < N_EXPERTS)
            def _():
                for cp in weight_copies(following, 1 - slot):
                    cp.start()

            gu_bf[...] = gu_f32[slot].astype(BF16)
            dn_bf[...] = dn_f32[slot].astype(BF16)

        subs = [slice(s * MOE_SUB, (s + 1) * MOE_SUB) for s in range(MOE_BLOCK // MOE_SUB)]
        gus = [jnp.dot(_unpack_rows([r[rows, :] for r in x_refs]).astype(BF16), gu_bf[...],
                       preferred_element_type=F32) for rows in subs]
        hids = [(_silu(gu[:, :EXPERT_DIM]) * gu[:, EXPERT_DIM:]).astype(BF16) for gu in gus]
        ys = [jnp.dot(hid, dn_bf[...], preferred_element_type=F32) for hid in hids]
        for rows, y in zip(subs, ys):
            for ref, piece in zip(y_refs, _pack_rows(y)):
                ref[rows, :] = piece


def _moe(block_e, n_valid, next_expert, x_sorted, w_gu, w_dn):
    n_slots = x_sorted[0].shape[0]
    blk = MOE_BLOCK
    n_blocks = n_slots // blk
    rows = lambda i, be, nv, nx: (jnp.minimum(i, nv[0] - 1), 0)
    return pl.pallas_call(
        _moe_kernel,
        grid_spec=pltpu.PrefetchScalarGridSpec(
            num_scalar_prefetch=3,
            grid=(n_blocks,),
            in_specs=[pl.BlockSpec((blk, PIECE), rows)] * SC_SPLIT
            + [pl.BlockSpec(memory_space=pl.ANY), pl.BlockSpec(memory_space=pl.ANY)],
            out_specs=[pl.BlockSpec((blk, PIECE), rows)] * SC_SPLIT,
            scratch_shapes=[pltpu.VMEM((2, D_MODEL, 2 * EXPERT_DIM), F32),
                            pltpu.VMEM((2, EXPERT_DIM, D_MODEL), F32),
                            pltpu.VMEM((D_MODEL, 2 * EXPERT_DIM), BF16),
                            pltpu.VMEM((EXPERT_DIM, D_MODEL), BF16),
                            pltpu.SemaphoreType.DMA((2, 2)),
                            pltpu.SMEM((1,), jnp.int32)]),
        out_shape=[jax.ShapeDtypeStruct((n_slots, PIECE), jnp.int32)] * SC_SPLIT,
        compiler_params=_params(("arbitrary",)),
        name="moe",
    )(block_e, n_valid, next_expert, *x_sorted, w_gu, w_dn)


def _dest_kernel(e_ref, rank_ref, start_ref, dest_ref):
    e = e_ref[...]
    rows = lax.broadcasted_iota(jnp.int32, (N_EXPERTS, e.shape[1]), 0)
    start = jnp.sum(jnp.where(rows == e, start_ref[...], 0), axis=0, keepdims=True)
    dest_ref[...] = start + rank_ref[...]


def _dest(e_flat, rank, pad_start_col):
    n_assign = e_flat.shape[1]
    n = RANK_TILE
    tile = pl.BlockSpec((1, n), lambda i: (0, i))
    return pl.pallas_call(
        _dest_kernel,
        grid=(n_assign // n,),
        in_specs=[tile, tile, pl.BlockSpec((N_EXPERTS, 1), lambda i: (0, 0))],
        out_specs=tile,
        out_shape=jax.ShapeDtypeStruct((1, n_assign), jnp.int32),
        compiler_params=_params(("parallel",)),
        name="dest",
    )(e_flat, rank, pad_start_col)


def _sc_mesh():
    return plsc.VectorSubcoreMesh(core_axis_name="core", subcore_axis_name="subcore")


def _sc_scatter_rows(src, dest, n_slots):
    n_src, width = src.shape
    n_assign = dest.shape[1]
    src_blocks = n_src // SC_WINDOW

    @functools.partial(pl.kernel, out_type=jax.ShapeDtypeStruct((n_slots, width), src.dtype),
                       mesh=_sc_mesh(), scratch_types=[])
    def scatter(src_hbm, dest_hbm, out_hbm):
        def body(rows_vmem, dest_vmem):
            pltpu.sync_copy(rows_vmem, out_hbm.at[dest_vmem.at[0]])

        pltpu.emit_pipeline(
            body,
            grid=(n_assign // SC_WINDOW,),
            in_specs=[pl.BlockSpec((SC_WINDOW, width), lambda i: (i % src_blocks, 0)),
                      pl.BlockSpec((1, SC_WINDOW), lambda i: (0, i))],
            out_specs=[],
            core_axis_name=("core", "subcore"),
            dimension_semantics=(pltpu.PARALLEL,),
        )(src_hbm, dest_hbm)

    return scatter(src, dest)


def _sc_gather_rows(table, idx):
    width = table.shape[1]
    n_assign = idx.shape[1]

    @functools.partial(pl.kernel, out_type=jax.ShapeDtypeStruct((n_assign, width), table.dtype),
                       mesh=_sc_mesh(), scratch_types=[])
    def gather(table_hbm, idx_hbm, out_hbm):
        def body(idx_vmem, rows_vmem):
            pltpu.sync_copy(table_hbm.at[idx_vmem.at[0]], rows_vmem)

        pltpu.emit_pipeline(
            body,
            grid=(n_assign // SC_WINDOW,),
            in_specs=[pl.BlockSpec((1, SC_WINDOW), lambda i: (0, i))],
            out_specs=[pl.BlockSpec((SC_WINDOW, width), lambda i: (i, 0))],
            core_axis_name=("core", "subcore"),
            dimension_semantics=(pltpu.PARALLEL,),
        )(idx_hbm, out_hbm)

    return gather(table, idx)


def _final_kernel(n_prompt_tiles, x1_ref, *refs):
    h2_refs = refs[:SC_SPLIT]
    yg_refs = refs[SC_SPLIT:2 * SC_SPLIT]
    wt_ref, gt2_ref, wsg_ref, wsd_ref, gf_ref, op_ref, os_ref = refs[2 * SC_SPLIT:]
    is_prompt = pl.program_id(0) < n_prompt_tiles
    h2 = _unpack_rows([r[...] for r in h2_refs]).astype(BF16)
    gu = jnp.dot(h2, wsg_ref[...], preferred_element_type=F32)
    hid = _silu(gu[:, :SHARED_DIM]) * gu[:, SHARED_DIM:]
    y = jnp.dot(hid.astype(BF16), wsd_ref[...], preferred_element_type=F32)
    wt = wt_ref[...]
    for k in range(TOP_K):
        y = y + wt[:, k:k + 1] * _unpack_rows([r[k] for r in yg_refs])
    x2 = x1_ref[...] + gt2_ref[...] * y
    out = _rms(x2, gf_ref[...])

    @pl.when(is_prompt)
    def _():
        op_ref[...] = out

    @pl.when(jnp.logical_not(is_prompt))
    def _():
        os_ref[...] = out


def _final(x1_all, h2_all, y_gath, wt_rows, gt2, wsg, wsd, gf, grp_of_tile, n_prompt_tiles):
    t_all = x1_all.shape[0]
    tm = ROW_TILE
    t_p = n_prompt_tiles * tm
    row = lambda i: (i, 0)
    fixed2 = lambda i: (0, 0)
    return pl.pallas_call(
        functools.partial(_final_kernel, n_prompt_tiles),
        grid=(t_all // tm,),
        in_specs=[pl.BlockSpec((tm, D_MODEL), row)]
        + [pl.BlockSpec((tm, PIECE), row)] * SC_SPLIT
        + [pl.BlockSpec((TOP_K, tm, PIECE), lambda i: (0, i, 0))] * SC_SPLIT
        + [pl.BlockSpec((tm, TOP_K), row),
           pl.BlockSpec((None, tm, D_MODEL), lambda i: (grp_of_tile(i), 0, 0)),
           pl.BlockSpec(wsg.shape, fixed2),
           pl.BlockSpec(wsd.shape, fixed2),
           pl.BlockSpec((1, D_MODEL), fixed2)],
        out_specs=[pl.BlockSpec((tm, D_MODEL), lambda i: (jnp.minimum(i, n_prompt_tiles - 1), 0)),
                   pl.BlockSpec((tm, D_MODEL), lambda i: (jnp.maximum(i - n_prompt_tiles, 0), 0))],
        out_shape=[jax.ShapeDtypeStruct((t_p, D_MODEL), F32),
                   jax.ShapeDtypeStruct((t_all - t_p, D_MODEL), F32)],
        compiler_params=_params(("arbitrary",)),
        name="final",
    )(x1_all, *h2_all, *y_gath, wt_rows, gt2, wsg, wsd, gf)


def _rope_tables(pos):
    freqs = ROPE_THETA ** (-jnp.arange(ROPE_HALF, dtype=F32) / ROPE_HALF)
    ang = pos.astype(F32)[:, None] * freqs
    return jnp.cos(ang), jnp.sin(ang)


def kernel(x_prompt, x_sample, cache_ckv, cache_krope, page_table, c_prompt, c_sample, w_ada, b_ada, norm1_g, norm2_g, w_in, a_vnorm_g, w_spatial, b_spatial, w_out_a, q_norm_g, w_uq, kv_norm_g, w_uk, w_uv, w_out_b, w_o, w_router, router_bias, w_exp_gu, w_exp_down, w_sh_gu, w_sh_down, final_norm_g):
    batch, seq, _ = x_prompt.shape
    dec_b, dec_s, _ = x_sample.shape
    n_pages = page_table.shape[1]
    n_past = n_pages * CHUNK
    t_p, t_s = batch * seq, dec_b * dec_s
    t_all = t_p + t_s
    tm = ROW_TILE
    assert seq % ATTN_Q_STEP == 0 and t_s % tm == 0 and CHUNK % dec_s == 0 and n_pages % PAGES_PER_STEP == 0
    assert (t_all * TOP_K) % RANK_TILE == 0 and w_ada.shape[0] == 1
    n_prompt_tiles = t_p // tm
    tiles_per_b = seq // tm

    def grp_of_tile(i):
        return jnp.minimum(i // tiles_per_b, batch) + jnp.maximum(i - n_prompt_tiles, 0)

    n_c = batch + dec_b
    c_rows = -(-n_c // 8) * 8
    c_all = jnp.concatenate([c_prompt, c_sample, jnp.zeros((c_rows - n_c, D_MODEL), F32)], axis=0)
    mod = _ada(c_all, w_ada[0].astype(BF16), b_ada[0][None, :])
    mod_p = jnp.broadcast_to(mod[:batch, None, :], (batch, tm, 6 * D_MODEL))
    mod_s = jnp.repeat(mod[batch:n_c], dec_s, axis=0).reshape(t_s // tm, tm, 6 * D_MODEL)
    mod_g = jnp.concatenate([mod_p, mod_s], axis=0)
    sh1, sc1, gt1, sh2, sc2, gt2 = [mod_g[:, :, k * D_MODEL:(k + 1) * D_MODEL] for k in range(6)]

    o_u, o_v, o_cq, o_ckv, o_kr = A_WIDTH, 2 * A_WIDTH, 2 * A_WIDTH + Q_LORA, 2 * A_WIDTH + Q_LORA + KV_LORA, \
        2 * A_WIDTH + Q_LORA + KV_LORA + QK_ROPE
    win = w_in[0]
    w5 = jnp.concatenate([win[:, :o_kr], jnp.zeros((D_MODEL, IN_END - IN_KR - QK_ROPE), F32)], axis=1).astype(BF16)
    wg = win[:, o_kr:].astype(BF16)
    row1 = lambda v: v.reshape(1, -1)
    tri = jnp.tril(jnp.ones((CHUNK, CHUNK), F32))
    wc_p = w_spatial[0] * tri
    per = CHUNK // dec_s
    small = (w_spatial[0] * tri)[:, :dec_s, :dec_s]
    wc_s = jnp.einsum('ab,gts->gatbs', jnp.eye(per, dtype=F32), small).reshape(A_GROUPS, CHUNK, CHUNK)
    wc = jnp.stack([wc_p, wc_s]).astype(BF16)
    bias_p = jnp.repeat(b_spatial[0].T, A_GROUP_DIM, axis=1)
    bias_s = jnp.tile(bias_p[:dec_s], (per, 1))
    bc = jnp.stack([bias_p, bias_s])

    pos_p = jnp.arange(seq)
    pos_s = n_past + jnp.arange(dec_s)
    cos_p, sin_p = _rope_tables(pos_p)
    cos_s, sin_s = _rope_tables(pos_s)
    cos_rows = jnp.concatenate([jnp.tile(cos_p, (batch, 1)), jnp.tile(cos_s, (dec_b, 1))], axis=0)
    sin_rows = jnp.concatenate([jnp.tile(sin_p, (batch, 1)), jnp.tile(sin_s, (dec_b, 1))], axis=0)
    lane_pad = jnp.zeros((t_all, LANES - QK_ROPE), F32)
    cos_kr = jnp.concatenate([cos_rows, cos_rows, lane_pad], axis=1)
    sin_kr = jnp.concatenate([-sin_rows, sin_rows, lane_pad], axis=1)

    x_p, x_s = x_prompt.reshape(t_p, D_MODEL), x_sample.reshape(t_s, D_MODEL)
    g1, g2 = row1(norm1_g[0]), row1(norm2_g[0])
    a_all, cq_all, ckv_p, kr_p, ckv_s, kr_s, v_s = _in_proj(
        x_p, x_s, sh1, sc1, g1, w5, row1(a_vnorm_g[0]), row1(q_norm_g[0]), row1(kv_norm_g[0]),
        cos_kr, sin_kr, wc, bc, grp_of_tile, n_prompt_tiles)

    wuq = w_uq[0]
    wuqt = jnp.pad(wuq, ((0, 0), (0, 0), (0, HEAD_PAD - QK_NOPE - QK_ROPE))).reshape(Q_LORA, -1).T.astype(BF16)
    wukp = jnp.pad(w_uk[0], ((0, 0), (0, 0), (0, HEAD_PAD - QK_NOPE))).reshape(KV_LORA, -1).astype(BF16)
    place_h = jnp.pad(jnp.eye(QK_ROPE, dtype=F32), ((0, 0), (QK_NOPE, HEAD_PAD - QK_NOPE - QK_ROPE)))
    place = jnp.tile(place_h, (1, N_HEADS)).astype(BF16)
    wuvt = w_uv[0].reshape(KV_LORA, -1).T.astype(BF16)
    qt, k, vt = _qkv(cq_all, ckv_p, kr_p, wuqt, wukp, place, wuvt, cos_p.T, sin_p.T, batch, seq)
    ot = _attn(qt, k, vt)

    wq_s = jnp.concatenate([wuq[:, :, :QK_NOPE].reshape(Q_LORA, -1),
                            wuq[:, :, QK_NOPE:QK_NOPE + ROPE_HALF].reshape(Q_LORA, -1),
                            wuq[:, :, QK_NOPE + ROPE_HALF:].reshape(Q_LORA, -1)], axis=1).astype(BF16)
    eye_h = jnp.eye(N_HEADS, dtype=F32)
    wuk_blk = jnp.einsum('rhd,hg->hdgr', w_uk[0], eye_h).reshape(N_HEADS * QK_NOPE, N_HEADS * KV_LORA).astype(BF16)
    wuv_blk = jnp.einsum('rhd,hg->hrgd', w_uv[0], eye_h).reshape(N_HEADS * KV_LORA, N_HEADS * V_HEAD).astype(BF16)
    cos_sq = jnp.tile(jnp.tile(cos_s, (1, N_HEADS)), (dec_b, 1))
    sin_sq = jnp.tile(jnp.tile(sin_s, (1, N_HEADS)), (dec_b, 1))
    ql, r1, r2 = _sample_q(cq_all[t_p:], wq_s, wuk_blk, cos_sq, sin_sq)
    rows = dec_s * N_HEADS
    ql3 = ql.reshape(dec_b, rows, KV_LORA)
    qr3 = jnp.concatenate([r1.reshape(dec_b, dec_s, N_HEADS, ROPE_HALF),
                           r2.reshape(dec_b, dec_s, N_HEADS, ROPE_HALF)], axis=-1).reshape(dec_b, rows, QK_ROPE)
    new_pad = ((0, 0), (0, 8 - dec_s), (0, 0))
    ckv_new = jnp.pad(ckv_s.reshape(dec_b, dec_s, KV_LORA), new_pad)
    kr_new_t = jnp.swapaxes(jnp.pad(kr_s.reshape(dec_b, dec_s, QK_ROPE), new_pad), 1, 2)
    cache_kr_t = jnp.swapaxes(cache_krope[0], 1, 2)
    o_lat = _sample_attn(page_table, cache_ckv[0], cache_kr_t, ql3, qr3, ckv_new, kr_new_t)
    o_s = _matmul(o_lat.reshape(t_s, N_HEADS * KV_LORA), wuv_blk, BF16)

    x1_all, *h2_all = _merge(x_p, x_s, (sh1, sc1, gt1, sh2, sc2), g1, g2, a_all, ot, o_s, wg,
                            w_out_a[0].astype(BF16), w_out_b[0].astype(BF16), w_o[0].astype(BF16),
                            grp_of_tile, n_prompt_tiles)

    wr_t = w_router[0].T
    whi = wr_t.astype(BF16)
    wlo = (wr_t - whi.astype(F32)).astype(BF16)
    idx_t, wt_t = _route(x1_all, sh2, sc2, g2, whi, wlo, router_bias[0].reshape(-1, 1), grp_of_tile)
    n_assign = t_all * TOP_K
    e_flat = idx_t.reshape(1, n_assign)
    rank, counts = _rank(e_flat)
    counts = counts[:, 0].astype(jnp.int32)
    blk = MOE_BLOCK
    padded = (counts + blk - 1) // blk * blk
    pad_end = jnp.cumsum(padded)
    pad_start = pad_end - padded
    dest = _dest(e_flat, rank, pad_start.reshape(-1, 1))
    n_blocks = -(-n_assign // blk) + N_EXPERTS
    n_slots = n_blocks * blk
    first_row = jnp.arange(n_blocks, dtype=jnp.int32) * blk
    block_e = jnp.minimum(jnp.sum(pad_end[None, :] <= first_row[:, None], axis=1), N_EXPERTS - 1).astype(jnp.int32)
    n_valid = (pad_end[-1] // blk).astype(jnp.int32).reshape(1)
    experts = jnp.arange(N_EXPERTS, dtype=jnp.int32)
    owners = jnp.where(padded > 0, experts, N_EXPERTS)
    later_owner = lax.cummin(owners, reverse=True)
    next_expert = jnp.concatenate([later_owner[1:], jnp.full((1,), N_EXPERTS, jnp.int32)])
    x_sorted = [_sc_scatter_rows(h, dest, n_slots) for h in h2_all]
    y_sorted = _moe(block_e, n_valid, next_expert, x_sorted, w_exp_gu[0], w_exp_down[0])
    y_gath = [_sc_gather_rows(y, dest).reshape(TOP_K, t_all, PIECE) for y in y_sorted]

    y_p, y_s = _final(x1_all, h2_all, y_gath, wt_t.T, gt2, w_sh_gu[0].astype(BF16), w_sh_down[0].astype(BF16),
                      row1(final_norm_g), grp_of_tile, n_prompt_tiles)

    y_prompt = y_p.reshape(batch, seq, D_MODEL)
    y_sample = y_s.reshape(dec_b, dec_s, D_MODEL)
    new_ckv_prompt = ckv_p.reshape(1, batch, seq, KV_LORA)
    new_krope_prompt = kr_p.reshape(1, batch, seq, QK_ROPE)
    new_ckv_sample = ckv_s.reshape(1, dec_b, dec_s, KV_LORA)
    new_krope_sample = kr_s.reshape(1, dec_b, dec_s, QK_ROPE)
    new_chunk_v_sample = v_s.reshape(1, dec_b, dec_s, A_WIDTH)
    return (y_prompt, y_sample, new_ckv_prompt, new_krope_prompt, new_ckv_sample, new_krope_sample,
            new_chunk_v_sample)
```

```python
import functools

import jax
import jax.numpy as jnp
from jax import lax
from jax.experimental import pallas as pl
from jax.experimental.pallas import tpu as pltpu
from jax.experimental.pallas import tpu_sc as plsc

F32 = jnp.float32
BF16 = jnp.bfloat16

D_MODEL = 1024
A_WIDTH = D_MODEL // 2
A_GROUPS = 8
A_GROUP_DIM = A_WIDTH // A_GROUPS
CHUNK = 128
N_HEADS = 8
QK_NOPE = 64
QK_ROPE = 32
ROPE_HALF = QK_ROPE // 2
V_HEAD = 64
Q_LORA = 384
KV_LORA = 256
ROPE_THETA = 10000.0
ATTN_SCALE = (QK_NOPE + QK_ROPE) ** -0.5
N_EXPERTS = 256
TOP_K = 8
N_GROUPS = 8
GROUP_SIZE = N_EXPERTS // N_GROUPS
TOPK_GROUPS = 4
EXPERT_DIM = 256
SHARED_DIM = 256
ROUTED_SCALE = 2.5
EPS = 1e-6

LANES = 128
HEAD_PAD = 128
ROW_TILE = 256
ATTN_KV_TILE = 256
ATTN_Q_STEP = 1024
ATTN_HEADS_PER_STEP = 2
LOG2_E = 1.4426950408889634
RANK_TILE = 512
MOE_BLOCK = 256
MOE_SUB = 128
PAGES_PER_STEP = 32
PAGE_GROUP = 8
V_ROWS = V_HEAD + 16
SC_WINDOW = 128
SC_SPLIT = 2
PIECE = D_MODEL // 2 // SC_SPLIT
VMEM_LIMIT = 48 * 1024 * 1024

IN_U, IN_V, IN_CQ, IN_CKV, IN_KR, IN_END = 0, 512, 1024, 1408, 1664, 1792

NT_DIMS = (((1,), (1,)), ((), ()))
TN_DIMS = (((0,), (0,)), ((), ()))


def _params(sem, vmem=VMEM_LIMIT):
    return pltpu.CompilerParams(dimension_semantics=sem, vmem_limit_bytes=vmem)


def _rms(x, g):
    return x * lax.rsqrt(jnp.mean(x * x, axis=-1, keepdims=True) + EPS) * g


def _gelu(x):
    return 0.5 * x * (1.0 + jnp.tanh(0.7978845608028654 * (x + 0.044715 * (x * x * x))))


def _sigmoid(x):
    return 1.0 / (1.0 + jnp.exp(-x))


def _silu(x):
    return x * _sigmoid(x)


def _pack_rows(x):
    pieces = []
    for c in range(SC_SPLIT):
        lo = x[:, 2 * c * PIECE:(2 * c + 1) * PIECE].astype(BF16).astype(F32)
        hi = x[:, (2 * c + 1) * PIECE:(2 * c + 2) * PIECE].astype(BF16).astype(F32)
        pieces.append(lax.bitcast_convert_type(hi, jnp.int32)
                      | lax.shift_right_logical(lax.bitcast_convert_type(lo, jnp.int32), 16))
    return pieces


def _unpack_rows(pieces):
    cols = []
    for p in pieces:
        cols.append(lax.bitcast_convert_type(lax.shift_left(p, 16), F32))
        cols.append(lax.bitcast_convert_type(p & jnp.int32(-65536), F32))
    return jnp.concatenate(cols, axis=1)


def _ada_kernel(c_ref, w_ref, b_ref, o_ref):
    s = _silu(c_ref[...]).astype(BF16)
    o_ref[...] = jnp.dot(s, w_ref[...], preferred_element_type=F32) + b_ref[...]


def _ada(c, w, b):
    rows, n = c.shape[0], w.shape[1]
    tn = 1536
    return pl.pallas_call(
        _ada_kernel,
        grid=(n // tn,),
        in_specs=[pl.BlockSpec((rows, D_MODEL), lambda j: (0, 0)),
                  pl.BlockSpec((D_MODEL, tn), lambda j: (0, j)),
                  pl.BlockSpec((1, tn), lambda j: (0, j))],
        out_specs=pl.BlockSpec((rows, tn), lambda j: (0, j)),
        out_shape=jax.ShapeDtypeStruct((rows, n), F32),
        compiler_params=_params(("parallel",)),
        name="ada",
    )(c, w, b)


def _in_kernel(n_prompt_tiles, xp_ref, xs_ref, sh_ref, sc_ref, g1_ref, w_ref, vg_ref, qg_ref, kg_ref,
               cos_ref, sin_ref, wc_ref, bc_ref,
               a_ref, cq_ref, ckvp_ref, krp_ref, ckvs_ref, krs_ref, vs_ref):
    tm = xp_ref.shape[0]
    is_prompt = pl.program_id(0) < n_prompt_tiles
    x = jnp.where(is_prompt, xp_ref[...], xs_ref[...])
    h = _rms(x, g1_ref[...]) * (1.0 + sc_ref[...]) + sh_ref[...]
    z = jnp.dot(h.astype(BF16), w_ref[...], preferred_element_type=F32)
    u = _gelu(z[:, IN_U:IN_V])
    v = _rms(_gelu(z[:, IN_V:IN_CQ]), vg_ref[...])
    cq_ref[...] = _rms(z[:, IN_CQ:IN_CKV], qg_ref[...]).astype(BF16)
    ckv = _rms(z[:, IN_CKV:IN_KR], kg_ref[...])
    kr = z[:, IN_KR:IN_END]
    lane = lax.broadcasted_iota(jnp.int32, kr.shape, 1)
    swapped = jnp.where(lane < ROPE_HALF, pltpu.roll(kr, LANES - ROPE_HALF, 1), pltpu.roll(kr, ROPE_HALF, 1))
    kr_rot = (kr * cos_ref[...] + swapped * sin_ref[...])[:, :QK_ROPE]

    @pl.when(is_prompt)
    def _():
        ckvp_ref[...] = ckv
        krp_ref[...] = kr_rot

    @pl.when(jnp.logical_not(is_prompt))
    def _():
        ckvs_ref[...] = ckv
        krs_ref[...] = kr_rot
        vs_ref[...] = v

    vb = v.astype(BF16)
    half = A_WIDTH // 2
    grp_of_lane = lax.broadcasted_iota(jnp.int32, (CHUNK, half), 1) // A_GROUP_DIM
    for ci in range(tm // CHUNK):
        rows = slice(ci * CHUNK, (ci + 1) * CHUNK)
        parts = []
        for q in range(2):
            vq = vb[rows, q * half:(q + 1) * half]
            acc = None
            for gg in range(A_GROUPS // 2):
                vm = jnp.where(grp_of_lane == gg, vq, jnp.zeros_like(vq))
                part = jnp.dot(wc_ref[q * (A_GROUPS // 2) + gg], vm, preferred_element_type=F32)
                acc = part if acc is None else acc + part
            parts.append(acc)
        sp = jnp.concatenate(parts, axis=1) + bc_ref[...]
        a_ref[rows, :] = (u[rows, :] * sp).astype(BF16)


def _in_proj(x_p, x_s, sh1, sc1, g1, w5, vg, qg, kg, cos_kr, sin_kr, wc, bc, grp_of_tile, n_prompt_tiles):
    t_p, t_s = x_p.shape[0], x_s.shape[0]
    t_all = t_p + t_s
    tm = ROW_TILE
    n_tiles = t_all // tm
    row = lambda i: (i, 0)
    prow = lambda i: (jnp.minimum(i, n_prompt_tiles - 1), 0)
    srow = lambda i: (jnp.maximum(i - n_prompt_tiles, 0), 0)
    fixed2 = lambda i: (0, 0)
    kind = lambda i: (jnp.where(i < n_prompt_tiles, 0, 1), 0, 0)
    kind4 = lambda i: (jnp.where(i < n_prompt_tiles, 0, 1), 0, 0, 0)
    mod = lambda i: (grp_of_tile(i), 0, 0)
    return pl.pallas_call(
        functools.partial(_in_kernel, n_prompt_tiles),
        grid=(n_tiles,),
        in_specs=[pl.BlockSpec((tm, D_MODEL), prow),
                  pl.BlockSpec((tm, D_MODEL), srow),
                  pl.BlockSpec((None, tm, D_MODEL), mod),
                  pl.BlockSpec((None, tm, D_MODEL), mod),
                  pl.BlockSpec((1, D_MODEL), fixed2),
                  pl.BlockSpec((D_MODEL, IN_END), fixed2),
                  pl.BlockSpec((1, A_WIDTH), fixed2),
                  pl.BlockSpec((1, Q_LORA), fixed2),
                  pl.BlockSpec((1, KV_LORA), fixed2),
                  pl.BlockSpec((tm, LANES), row),
                  pl.BlockSpec((tm, LANES), row),
                  pl.BlockSpec((None, A_GROUPS, CHUNK, CHUNK), kind4),
                  pl.BlockSpec((None, CHUNK, A_WIDTH), kind)],
        out_specs=[pl.BlockSpec((tm, A_WIDTH), row),
                   pl.BlockSpec((tm, Q_LORA), row),
                   pl.BlockSpec((tm, KV_LORA), prow),
                   pl.BlockSpec((tm, QK_ROPE), prow),
                   pl.BlockSpec((tm, KV_LORA), srow),
                   pl.BlockSpec((tm, QK_ROPE), srow),
                   pl.BlockSpec((tm, A_WIDTH), srow)],
        out_shape=[jax.ShapeDtypeStruct((t_all, A_WIDTH), BF16),
                   jax.ShapeDtypeStruct((t_all, Q_LORA), BF16),
                   jax.ShapeDtypeStruct((t_p, KV_LORA), F32),
                   jax.ShapeDtypeStruct((t_p, QK_ROPE), F32),
                   jax.ShapeDtypeStruct((t_s, KV_LORA), F32),
                   jax.ShapeDtypeStruct((t_s, QK_ROPE), F32),
                   jax.ShapeDtypeStruct((t_s, A_WIDTH), F32)],
        compiler_params=_params(("arbitrary",)),
        name="in_proj",
    )(x_p, x_s, sh1, sc1, g1, w5, vg, qg, kg, cos_kr, sin_kr, wc, bc)


def _qkv_kernel(cq_ref, ckv_ref, kr_ref, wuqt_ref, wukp_ref, place_ref, wuvt_ref, cos_ref, sin_ref,
                qt_ref, k_ref, vt_ref):
    tm = cq_ref.shape[0]
    ckv = ckv_ref[...].astype(BF16)
    kr = kr_ref[...].astype(BF16)
    qt = lax.dot_general(wuqt_ref[...], cq_ref[...], NT_DIMS, preferred_element_type=F32)
    c = cos_ref[...]
    s = sin_ref[...]
    pad = jnp.zeros((HEAD_PAD - QK_NOPE - QK_ROPE, tm), F32)
    for h in range(N_HEADS):
        blk = qt[h * HEAD_PAD:(h + 1) * HEAD_PAD]
        x1 = blk[QK_NOPE:QK_NOPE + ROPE_HALF]
        x2 = blk[QK_NOPE + ROPE_HALF:QK_NOPE + QK_ROPE]
        full = jnp.concatenate([blk[:QK_NOPE], x1 * c - x2 * s, x1 * s + x2 * c, pad], axis=0)
        qt_ref[h] = (full * (ATTN_SCALE * LOG2_E)).astype(BF16)
    k = (jnp.dot(ckv, wukp_ref[...], preferred_element_type=F32)
         + jnp.dot(kr, place_ref[...], preferred_element_type=F32))
    for h in range(N_HEADS):
        k_ref[h] = k[:, h * HEAD_PAD:(h + 1) * HEAD_PAD].astype(BF16)
    vt = lax.dot_general(wuvt_ref[...], ckv, NT_DIMS, preferred_element_type=F32)
    extra = V_ROWS - V_HEAD
    ones_row = jnp.where(lax.broadcasted_iota(jnp.int32, (extra, tm), 0) == 0, 1.0, 0.0)
    for h in range(N_HEADS):
        vt_ref[h] = jnp.concatenate([vt[h * V_HEAD:(h + 1) * V_HEAD], ones_row], axis=0).astype(BF16)


def _qkv(cq_all, ckv_all, kr_all, wuqt, wukp, place, wuvt, cos_t, sin_t, batch, seq):
    tm = ATTN_KV_TILE
    nk = seq // tm
    row = lambda b, j: (b * nk + j, 0)
    fixed2 = lambda b, j: (0, 0)
    tab = lambda b, j: (0, j)
    return pl.pallas_call(
        _qkv_kernel,
        grid=(batch, nk),
        in_specs=[pl.BlockSpec((tm, Q_LORA), row),
                  pl.BlockSpec((tm, KV_LORA), row),
                  pl.BlockSpec((tm, QK_ROPE), row),
                  pl.BlockSpec(wuqt.shape, fixed2),
                  pl.BlockSpec(wukp.shape, fixed2),
                  pl.BlockSpec(place.shape, fixed2),
                  pl.BlockSpec(wuvt.shape, fixed2),
                  pl.BlockSpec((ROPE_HALF, tm), tab),
                  pl.BlockSpec((ROPE_HALF, tm), tab)],
        out_specs=[pl.BlockSpec((None, N_HEADS, HEAD_PAD, tm), lambda b, j: (b, 0, 0, j)),
                   pl.BlockSpec((None, N_HEADS, tm, HEAD_PAD), lambda b, j: (b, 0, j, 0)),
                   pl.BlockSpec((None, N_HEADS, None, V_ROWS, tm), lambda b, j: (b, 0, j, 0, 0))],
        out_shape=[jax.ShapeDtypeStruct((batch, N_HEADS, HEAD_PAD, seq), BF16),
                   jax.ShapeDtypeStruct((batch, N_HEADS, seq, HEAD_PAD), BF16),
                   jax.ShapeDtypeStruct((batch, N_HEADS, nk, V_ROWS, tm), BF16)],
        compiler_params=_params(("parallel", "parallel")),
        name="qkv",
    )(cq_all, ckv_all, kr_all, wuqt, wukp, place, wuvt, cos_t, sin_t)


def _attn_kernel(qt_ref, k_ref, vt_ref, o_ref):
    tk = ATTN_KV_TILE
    n_heads = qt_ref.shape[0]
    n_sub = qt_ref.shape[2] // tk
    qi = pl.program_id(2)
    j0 = qi * n_sub
    chains = [(h, sub) for h in range(n_heads) for sub in range(n_sub)]

    def tiles(j, active, carries, diagonal_sub):
        scores = [jnp.dot(k_ref[h, j], qt_ref[h, :, sub * tk:(sub + 1) * tk], preferred_element_type=F32)
                  for h, sub in active]
        stats = []
        for (h, sub), s in zip(active, scores):
            m, _ = carries[h * n_sub + sub]
            if sub == diagonal_sub:
                key = lax.broadcasted_iota(jnp.int32, s.shape, 0)
                qry = lax.broadcasted_iota(jnp.int32, s.shape, 1)
                s = jnp.where(key <= qry, s, -jnp.inf)
            m_new = jnp.maximum(m, jnp.max(s, axis=0, keepdims=True))
            stats.append((m_new, jnp.exp2(m - m_new), jnp.exp2(s - m_new).astype(BF16)))
        out = list(carries)
        for (h, sub), (m_new, alpha, p) in zip(active, stats):
            c = h * n_sub + sub
            out[c] = (m_new, alpha * carries[c][1] + jnp.dot(vt_ref[h, j], p, preferred_element_type=F32))
        return out

    init = tuple((jnp.full((1, tk), -jnp.inf, F32), jnp.zeros((V_ROWS, tk), F32)) for _ in chains)
    carries = list(lax.fori_loop(0, j0, lambda j, c: tuple(tiles(j, chains, c, None)), init))
    for jj in range(n_sub):
        carries = tiles(j0 + jj, [(h, sub) for h, sub in chains if sub >= jj], carries, jj)
    for h, sub in chains:
        _, acc = carries[h * n_sub + sub]
        o_ref[h * V_HEAD:(h + 1) * V_HEAD, sub * tk:(sub + 1) * tk] = (
            acc[:V_HEAD] / acc[V_HEAD:V_HEAD + 1]).astype(BF16)


def _attn(qt, k, vt):
    batch, _, _, seq = qt.shape
    tk = ATTN_KV_TILE
    nk = seq // tk
    hp = ATTN_HEADS_PER_STEP
    k5 = k.reshape(batch, N_HEADS, nk, tk, HEAD_PAD)
    return pl.pallas_call(
        _attn_kernel,
        grid=(batch, N_HEADS // hp, seq // ATTN_Q_STEP),
        in_specs=[pl.BlockSpec((None, hp, HEAD_PAD, ATTN_Q_STEP), lambda b, h, i: (b, h, 0, i)),
                  pl.BlockSpec((None, hp, nk, tk, HEAD_PAD), lambda b, h, i: (b, h, 0, 0, 0)),
                  pl.BlockSpec((None, hp, nk, V_ROWS, tk), lambda b, h, i: (b, h, 0, 0, 0))],
        out_specs=pl.BlockSpec((None, hp * V_HEAD, ATTN_Q_STEP), lambda b, h, i: (b, h, i)),
        out_shape=jax.ShapeDtypeStruct((batch, N_HEADS * V_HEAD, seq), BF16),
        compiler_params=_params(("parallel", "parallel", "parallel")),
        name="attn",
    )(qt, k5, vt)


def _sq_kernel(cq_ref, wq_ref, wukb_ref, cos_ref, sin_ref, ql_ref, r1_ref, r2_ref):
    n_nope = N_HEADS * QK_NOPE
    q = jnp.dot(cq_ref[...], wq_ref[...], preferred_element_type=F32)
    x1 = q[:, n_nope:n_nope + LANES]
    x2 = q[:, n_nope + LANES:]
    c = cos_ref[...]
    s = sin_ref[...]
    r1_ref[...] = ((x1 * c - x2 * s) * ATTN_SCALE).astype(BF16)
    r2_ref[...] = ((x1 * s + x2 * c) * ATTN_SCALE).astype(BF16)
    ql = jnp.dot(q[:, :n_nope].astype(BF16), wukb_ref[...], preferred_element_type=F32)
    ql_ref[...] = (ql * ATTN_SCALE).astype(BF16)


def _sample_q(cq_s, wq_s, wuk_blk, cos_s, sin_s):
    ts = cq_s.shape[0]
    full = lambda shape: pl.BlockSpec(shape, lambda i: (0,) * len(shape))
    return pl.pallas_call(
        _sq_kernel,
        grid=(1,),
        in_specs=[full(cq_s.shape), full(wq_s.shape), full(wuk_blk.shape), full(cos_s.shape), full(sin_s.shape)],
        out_specs=[full((ts, N_HEADS * KV_LORA)), full((ts, LANES)), full((ts, LANES))],
        out_shape=[jax.ShapeDtypeStruct((ts, N_HEADS * KV_LORA), BF16),
                   jax.ShapeDtypeStruct((ts, LANES), BF16),
                   jax.ShapeDtypeStruct((ts, LANES), BF16)],
        compiler_params=_params(("arbitrary",)),
        name="sample_q",
    )(cq_s, wq_s, wuk_blk, cos_s, sin_s)


def _sattn_kernel(n_pages, pt_ref, ckv_hbm, kr_hbm, ql_ref, qr_ref, cnew_ref, knew_ref, o_ref,
                  cbuf, rbuf, sem_c, sem_r):
    npg = PAGES_PER_STEP
    n_chunks = n_pages // npg
    PAGE_SLOTS = cbuf.shape[0]
    PAGE_LOOKAHEAD = PAGE_SLOTS // 2
    b = pl.program_id(0)

    def page_copies(bb, chunk, slot, p):
        page = pt_ref[bb * n_pages + chunk * npg + p]
        return (pltpu.make_async_copy(ckv_hbm.at[page], cbuf.at[slot, p], sem_c.at[slot]),
                pltpu.make_async_copy(kr_hbm.at[page], rbuf.at[slot, p], sem_r.at[slot]))

    def start_chunk(bb, chunk, slot):
        for p in range(npg):
            for cp in page_copies(bb, chunk, slot, p):
                cp.start()

    def wait_chunk(bb, chunk, slot):
        for p in range(npg):
            for cp in page_copies(bb, chunk, slot, p):
                cp.wait()

    @pl.when(b == 0)
    def _():
        for chunk in range(PAGE_LOOKAHEAD):
            start_chunk(0, chunk, chunk)

    ql = ql_ref[...]
    qr = qr_ref[...]

    def partial_softmax(blocks, mask=None):
        scores = [lax.dot_general(ql, keys, NT_DIMS, preferred_element_type=F32)
                  + jnp.dot(qr, rope_t, preferred_element_type=F32) for keys, rope_t in blocks]
        probs = []
        for s in scores:
            if mask is not None:
                s = jnp.where(mask(s.shape), s, -jnp.inf)
            m = jnp.max(s, axis=-1, keepdims=True)
            p = jnp.exp(s - m)
            probs.append((m, jnp.sum(p, axis=-1, keepdims=True), p.astype(BF16)))
        return [(m, l, jnp.dot(p, keys, preferred_element_type=F32)) for (m, l, p), (keys, _) in zip(probs, blocks)]

    def merge(state, parts):
        m_old, l_old, acc_old = state
        m_new = m_old
        for m, _, _ in parts:
            m_new = jnp.maximum(m_new, m)
        alpha = jnp.exp(m_old - m_new)
        l = alpha * l_old
        acc = alpha * acc_old
        for m, lp, op in parts:
            w = jnp.exp(m - m_new)
            l = l + w * lp
            acc = acc + w * op
        return m_new, l, acc

    def chunk_step(chunk, state):
        slot = chunk % PAGE_SLOTS
        wait_chunk(b, chunk, slot)
        ahead = chunk + PAGE_LOOKAHEAD
        ahead_b = jnp.where(ahead >= n_chunks, b + 1, b)
        ahead_b = jnp.where(ahead_b >= pl.num_programs(0), 0, ahead_b)
        start_chunk(ahead_b, ahead % n_chunks, ahead % PAGE_SLOTS)

        blocks = []
        for g in range(npg // PAGE_GROUP):
            pages = range(g * PAGE_GROUP, (g + 1) * PAGE_GROUP)
            blocks.append((jnp.concatenate([cbuf[slot, p].astype(BF16) for p in pages], axis=0),
                           jnp.concatenate([rbuf[slot, p].astype(BF16) for p in pages], axis=1)))
        return merge(state, partial_softmax(blocks))

    rows = ql.shape[0]
    state = (jnp.full((rows, 1), -jnp.inf, F32), jnp.zeros((rows, 1), F32), jnp.zeros((rows, KV_LORA), F32))
    state = lax.fori_loop(0, n_chunks, chunk_step, state)

    @pl.when(b == pl.num_programs(0) - 1)
    def _():
        for chunk in range(PAGE_LOOKAHEAD):
            wait_chunk(0, chunk, chunk % PAGE_SLOTS)

    def causal(shape):
        q_pos = lax.broadcasted_iota(jnp.int32, shape, 0) // N_HEADS
        return lax.broadcasted_iota(jnp.int32, shape, 1) <= q_pos

    _, l, acc = merge(state, partial_softmax([(cnew_ref[...].astype(BF16), knew_ref[...].astype(BF16))], causal))
    o_ref[...] = acc / l


def _sample_attn(page_table, cache_ckv, cache_kr_t, ql, qr, cnew, knew_t):
    dec_b, n_pages = page_table.shape
    npg = PAGES_PER_STEP
    n_chunks = n_pages // npg
    slots = 4 if n_chunks % 4 == 0 else 2
    assert n_chunks % slots == 0
    rows = ql.shape[1]
    n_new = cnew.shape[1]
    per_b = lambda b, pt: (b, 0, 0)
    return pl.pallas_call(
        functools.partial(_sattn_kernel, n_pages),
        grid_spec=pltpu.PrefetchScalarGridSpec(
            num_scalar_prefetch=1,
            grid=(dec_b,),
            in_specs=[pl.BlockSpec(memory_space=pl.ANY),
                      pl.BlockSpec(memory_space=pl.ANY),
                      pl.BlockSpec((None, rows, KV_LORA), per_b),
                      pl.BlockSpec((None, rows, QK_ROPE), per_b),
                      pl.BlockSpec((None, n_new, KV_LORA), per_b),
                      pl.BlockSpec((None, QK_ROPE, n_new), per_b)],
            out_specs=pl.BlockSpec((None, rows, KV_LORA), per_b),
            scratch_shapes=[pltpu.VMEM((slots, npg, CHUNK, KV_LORA), F32),
                            pltpu.VMEM((slots, npg, QK_ROPE, CHUNK), F32),
                            pltpu.SemaphoreType.DMA((slots,)),
                            pltpu.SemaphoreType.DMA((slots,))]),
        out_shape=jax.ShapeDtypeStruct((dec_b, rows, KV_LORA), F32),
        compiler_params=_params(("arbitrary",)),
        name="sample_attn",
    )(page_table.reshape(-1), cache_ckv, cache_kr_t, ql, qr, cnew, knew_t)


def _mm_kernel(x_ref, w_ref, o_ref):
    o_ref[...] = jnp.dot(x_ref[...].astype(BF16), w_ref[...], preferred_element_type=F32).astype(o_ref.dtype)


def _matmul(x, w, out_dtype):
    m, n = x.shape[0], w.shape[1]
    full = lambda shape: pl.BlockSpec(shape, lambda i: (0,) * len(shape))
    return pl.pallas_call(
        _mm_kernel,
        grid=(1,),
        in_specs=[full(x.shape), full(w.shape)],
        out_specs=full((m, n)),
        out_shape=jax.ShapeDtypeStruct((m, n), out_dtype),
        compiler_params=_params(("arbitrary",)),
        name="matmul",
    )(x, w)


def _merge_kernel(n_prompt_tiles, xp_ref, xs_ref, sh1_ref, sc1_ref, gt1_ref, sh2_ref, sc2_ref, g1_ref, g2_ref,
                  a_ref, ot_ref, os_ref, wg_ref, woa_ref, wob_ref, wo_ref, x1_ref, *rest):
    h2_refs, yb_ref = rest[:SC_SPLIT], rest[SC_SPLIT]
    i = pl.program_id(0)
    x = jnp.where(i < n_prompt_tiles, xp_ref[...], xs_ref[...])
    h = _rms(x, g1_ref[...]) * (1.0 + sc1_ref[...]) + sh1_ref[...]
    gates = _sigmoid(jnp.dot(h.astype(BF16), wg_ref[...], preferred_element_type=F32))
    y_a = jnp.dot(a_ref[...], woa_ref[...], preferred_element_type=F32)

    @pl.when(i < n_prompt_tiles)
    def _():
        yb_ref[...] = lax.dot_general(ot_ref[...], wob_ref[...], TN_DIMS, preferred_element_type=F32)

    @pl.when(i >= n_prompt_tiles)
    def _():
        yb_ref[...] = jnp.dot(os_ref[...], wob_ref[...], preferred_element_type=F32)

    z = gates[:, :D_MODEL] * y_a + gates[:, D_MODEL:] * yb_ref[...]
    y = jnp.dot(z.astype(BF16), wo_ref[...], preferred_element_type=F32)
    x1 = x + gt1_ref[...] * y
    x1_ref[...] = x1
    for ref, piece in zip(h2_refs, _pack_rows(_rms(x1, g2_ref[...]) * (1.0 + sc2_ref[...]) + sh2_ref[...])):
        ref[...] = piece


def _merge(x_p, x_s, mods, g1, g2, a_all, ot, o_s, wg, woa, wob, wo, grp_of_tile, n_prompt_tiles):
    t_all = x_p.shape[0] + x_s.shape[0]
    tm = ROW_TILE
    seq = ot.shape[2]
    tpb = seq // tm
    row = lambda i: (i, 0)
    fixed2 = lambda i: (0, 0)
    mod = lambda i: (grp_of_tile(i), 0, 0)

    def ot_map(i):
        ic = jnp.minimum(i, n_prompt_tiles - 1)
        return (ic // tpb, 0, ic % tpb)

    os_map = lambda i: (jnp.maximum(i - n_prompt_tiles, 0), 0)
    mod_spec = pl.BlockSpec((None, tm, D_MODEL), mod)
    return pl.pallas_call(
        functools.partial(_merge_kernel, n_prompt_tiles),
        grid=(t_all // tm,),
        in_specs=[pl.BlockSpec((tm, D_MODEL), lambda i: (jnp.minimum(i, n_prompt_tiles - 1), 0)),
                  pl.BlockSpec((tm, D_MODEL), os_map),
                  mod_spec, mod_spec, mod_spec, mod_spec, mod_spec,
                  pl.BlockSpec((1, D_MODEL), fixed2),
                  pl.BlockSpec((1, D_MODEL), fixed2),
                  pl.BlockSpec((tm, A_WIDTH), row),
                  pl.BlockSpec((None, N_HEADS * V_HEAD, tm), ot_map),
                  pl.BlockSpec((tm, N_HEADS * V_HEAD), os_map),
                  pl.BlockSpec(wg.shape, fixed2),
                  pl.BlockSpec(woa.shape, fixed2),
                  pl.BlockSpec(wob.shape, fixed2),
                  pl.BlockSpec(wo.shape, fixed2)],
        out_specs=[pl.BlockSpec((tm, D_MODEL), row)] + [pl.BlockSpec((tm, PIECE), row)] * SC_SPLIT,
        out_shape=[jax.ShapeDtypeStruct((t_all, D_MODEL), F32)]
        + [jax.ShapeDtypeStruct((t_all, PIECE), jnp.int32)] * SC_SPLIT,
        scratch_shapes=[pltpu.VMEM((tm, D_MODEL), F32)],
        compiler_params=_params(("parallel",)),
        name="merge",
    )(x_p, x_s, *mods, g1, g2, a_all, ot, o_s, wg, woa, wob, wo)


def _first_argmax(v, rows):
    mx = jnp.max(v, axis=0, keepdims=True)
    idx = jnp.min(jnp.where(v == mx, rows, v.shape[0]), axis=0, keepdims=True)
    return mx, idx


def _route_kernel(x1_ref, sh2_ref, sc2_ref, g2_ref, whi_ref, wlo_ref, bias_ref, idx_ref, wt_ref):
    h2 = _rms(x1_ref[...], g2_ref[...]) * (1.0 + sc2_ref[...]) + sh2_ref[...]
    hi = h2.astype(BF16)
    lo = (h2 - hi.astype(F32)).astype(BF16)
    whi = whi_ref[...]
    logits = (lax.dot_general(whi, hi, NT_DIMS, preferred_element_type=F32)
              + lax.dot_general(whi, lo, NT_DIMS, preferred_element_type=F32)
              + lax.dot_general(wlo_ref[...], hi, NT_DIMS, preferred_element_type=F32))
    scores = _sigmoid(logits)
    sel = scores + bias_ref[...]
    tm = sel.shape[1]
    neg = -jnp.inf
    rows_g = lax.broadcasted_iota(jnp.int32, (GROUP_SIZE, tm), 0)
    gscore = []
    for g in range(N_GROUPS):
        blk = sel[g * GROUP_SIZE:(g + 1) * GROUP_SIZE]
        m1, i1 = _first_argmax(blk, rows_g)
        m2 = jnp.max(jnp.where(rows_g == i1, neg, blk), axis=0, keepdims=True)
        gscore.append(m1 + m2)
    gs = jnp.concatenate(gscore, axis=0)
    rows_8 = lax.broadcasted_iota(jnp.int32, gs.shape, 0)
    chosen = jnp.zeros(gs.shape, jnp.int32)
    for _ in range(TOPK_GROUPS):
        _, gi = _first_argmax(gs, rows_8)
        hit = rows_8 == gi
        chosen = jnp.where(hit, 1, chosen)
        gs = jnp.where(hit, neg, gs)
    cand = jnp.concatenate(
        [jnp.where(chosen[g:g + 1] > 0, sel[g * GROUP_SIZE:(g + 1) * GROUP_SIZE], neg) for g in range(N_GROUPS)],
        axis=0)
    rows_e = lax.broadcasted_iota(jnp.int32, cand.shape, 0)
    idxs, wts = [], []
    for _ in range(TOP_K):
        _, ei = _first_argmax(cand, rows_e)
        hit = rows_e == ei
        idxs.append(ei)
        wts.append(jnp.sum(jnp.where(hit, scores, 0.0), axis=0, keepdims=True))
        cand = jnp.where(hit, neg, cand)
    w = jnp.concatenate(wts, axis=0)
    idx_ref[...] = jnp.concatenate(idxs, axis=0)
    wt_ref[...] = w / jnp.sum(w, axis=0, keepdims=True) * ROUTED_SCALE


def _route(x1_all, sh2, sc2, g2, whi, wlo, bias_col, grp_of_tile):
    t_all = x1_all.shape[0]
    tm = ROW_TILE
    fixed2 = lambda i: (0, 0)
    mod = lambda i: (grp_of_tile(i), 0, 0)
    return pl.pallas_call(
        _route_kernel,
        grid=(t_all // tm,),
        in_specs=[pl.BlockSpec((tm, D_MODEL), lambda i: (i, 0)),
                  pl.BlockSpec((None, tm, D_MODEL), mod),
                  pl.BlockSpec((None, tm, D_MODEL), mod),
                  pl.BlockSpec((1, D_MODEL), fixed2),
                  pl.BlockSpec(whi.shape, fixed2),
                  pl.BlockSpec(wlo.shape, fixed2),
                  pl.BlockSpec(bias_col.shape, fixed2)],
        out_specs=[pl.BlockSpec((TOP_K, tm), lambda i: (0, i)), pl.BlockSpec((TOP_K, tm), lambda i: (0, i))],
        out_shape=[jax.ShapeDtypeStruct((TOP_K, t_all), jnp.int32),
                   jax.ShapeDtypeStruct((TOP_K, t_all), F32)],
        compiler_params=_params(("parallel",)),
        name="route",
    )(x1_all, sh2, sc2, g2, whi, wlo, bias_col)


def _rank_kernel(e_ref, upper_ref, rank_ref, count_ref, run_ref):
    @pl.when(pl.program_id(0) == 0)
    def _():
        run_ref[...] = jnp.zeros(run_ref.shape, F32)

    e = e_ref[...]
    n = e.shape[1]
    rows = lax.broadcasted_iota(jnp.int32, (N_EXPERTS, n), 0)
    onehot = rows == e
    oh = jnp.where(onehot, 1.0, 0.0).astype(BF16)
    before = jnp.dot(oh, upper_ref[...], preferred_element_type=F32)
    run = run_ref[...]
    prior = jnp.concatenate([run] * (n // LANES), axis=1)
    rank = jnp.sum(jnp.where(onehot, before + prior, 0.0), axis=0, keepdims=True)
    rank_ref[...] = rank.astype(jnp.int32)
    total = run + jnp.dot(oh, jnp.ones((n, LANES), BF16), preferred_element_type=F32)
    run_ref[...] = total
    count_ref[...] = total


def _rank(e_flat):
    n_assign = e_flat.shape[1]
    n = RANK_TILE
    upper = (lax.broadcasted_iota(jnp.int32, (n, n), 0) < lax.broadcasted_iota(jnp.int32, (n, n), 1)).astype(BF16)
    return pl.pallas_call(
        _rank_kernel,
        grid=(n_assign // n,),
        in_specs=[pl.BlockSpec((1, n), lambda i: (0, i)), pl.BlockSpec((n, n), lambda i: (0, 0))],
        out_specs=[pl.BlockSpec((1, n), lambda i: (0, i)), pl.BlockSpec((N_EXPERTS, LANES), lambda i: (0, 0))],
        out_shape=[jax.ShapeDtypeStruct((1, n_assign), jnp.int32),
                   jax.ShapeDtypeStruct((N_EXPERTS, LANES), F32)],
        scratch_shapes=[pltpu.VMEM((N_EXPERTS, LANES), F32)],
        compiler_params=_params(("arbitrary",)),
        name="rank",
    )(e_flat, upper)


def _moe_kernel(be_ref, nv_ref, next_ref, *refs):
    x_refs = refs[:SC_SPLIT]
    wgu_hbm, wdn_hbm = refs[SC_SPLIT:SC_SPLIT + 2]
    y_refs = refs[SC_SPLIT + 2:2 * SC_SPLIT + 2]
    gu_f32, dn_f32, gu_bf, dn_bf, sem, run_ref = refs[2 * SC_SPLIT + 2:]
    i = pl.program_id(0)
    expert = be_ref[i]
    prev = be_ref[jnp.maximum(i - 1, 0)]

    def weight_copies(e, slot):
        return (pltpu.make_async_copy(wgu_hbm.at[e], gu_f32.at[slot], sem.at[0, slot]),
                pltpu.make_async_copy(wdn_hbm.at[e], dn_f32.at[slot], sem.at[1, slot]))

    @pl.when(i == 0)
    def _():
        run_ref[0] = 0
        for cp in weight_copies(expert, 0):
            cp.start()

    @pl.when(i < nv_ref[0])
    def _():
        @pl.when((i == 0) | (expert != prev))
        def _():
            run = run_ref[0] + jnp.where(i == 0, 0, 1)
            run_ref[0] = run
            slot = run % 2
            for cp in weight_copies(expert, slot):
                cp.wait()
            following = next_ref[expert]

            @pl.when(following < N_EXPERTS)
            def _():
                for cp in weight_copies(following, 1 - slot):
                    cp.start()

            gu_bf[...] = gu_f32[slot].astype(BF16)
            dn_bf[...] = dn_f32[slot].astype(BF16)

        subs = [slice(s * MOE_SUB, (s + 1) * MOE_SUB) for s in range(MOE_BLOCK // MOE_SUB)]
        gus = [jnp.dot(_unpack_rows([r[rows, :] for r in x_refs]).astype(BF16), gu_bf[...],
                       preferred_element_type=F32) for rows in subs]
        hids = [(_silu(gu[:, :EXPERT_DIM]) * gu[:, EXPERT_DIM:]).astype(BF16) for gu in gus]
        ys = [jnp.dot(hid, dn_bf[...], preferred_element_type=F32) for hid in hids]
        for rows, y in zip(subs, ys):
            for ref, piece in zip(y_refs, _pack_rows(y)):
                ref[rows, :] = piece


def _moe(block_e, n_valid, next_expert, x_sorted, w_gu, w_dn):
    n_slots = x_sorted[0].shape[0]
    blk = MOE_BLOCK
    n_blocks = n_slots // blk
    rows = lambda i, be, nv, nx: (jnp.minimum(i, nv[0] - 1), 0)
    return pl.pallas_call(
        _moe_kernel,
        grid_spec=pltpu.PrefetchScalarGridSpec(
            num_scalar_prefetch=3,
            grid=(n_blocks,),
            in_specs=[pl.BlockSpec((blk, PIECE), rows)] * SC_SPLIT
            + [pl.BlockSpec(memory_space=pl.ANY), pl.BlockSpec(memory_space=pl.ANY)],
            out_specs=[pl.BlockSpec((blk, PIECE), rows)] * SC_SPLIT,
            scratch_shapes=[pltpu.VMEM((2, D_MODEL, 2 * EXPERT_DIM), F32),
                            pltpu.VMEM((2, EXPERT_DIM, D_MODEL), F32),
                            pltpu.VMEM((D_MODEL, 2 * EXPERT_DIM), BF16),
                            pltpu.VMEM((EXPERT_DIM, D_MODEL), BF16),
                            pltpu.SemaphoreType.DMA((2, 2)),
                            pltpu.SMEM((1,), jnp.int32)]),
        out_shape=[jax.ShapeDtypeStruct((n_slots, PIECE), jnp.int32)] * SC_SPLIT,
        compiler_params=_params(("arbitrary",)),
        name="moe",
    )(block_e, n_valid, next_expert, *x_sorted, w_gu, w_dn)


def _dest_kernel(e_ref, rank_ref, start_ref, dest_ref):
    e = e_ref[...]
    rows = lax.broadcasted_iota(jnp.int32, (N_EXPERTS, e.shape[1]), 0)
    start = jnp.sum(jnp.where(rows == e, start_ref[...], 0), axis=0, keepdims=True)
    dest_ref[...] = start + rank_ref[...]


def _dest(e_flat, rank, pad_start_col):
    n_assign = e_flat.shape[1]
    n = RANK_TILE
    tile = pl.BlockSpec((1, n), lambda i: (0, i))
    return pl.pallas_call(
        _dest_kernel,
        grid=(n_assign // n,),
        in_specs=[tile, tile, pl.BlockSpec((N_EXPERTS, 1), lambda i: (0, 0))],
        out_specs=tile,
        out_shape=jax.ShapeDtypeStruct((1, n_assign), jnp.int32),
        compiler_params=_params(("parallel",)),
        name="dest",
    )(e_flat, rank, pad_start_col)


def _sc_mesh():
    return plsc.VectorSubcoreMesh(core_axis_name="core", subcore_axis_name="subcore")


def _sc_scatter_rows(src, dest, n_slots):
    n_src, width = src.shape
    n_assign = dest.shape[1]
    src_blocks = n_src // SC_WINDOW

    @functools.partial(pl.kernel, out_type=jax.ShapeDtypeStruct((n_slots, width), src.dtype),
                       mesh=_sc_mesh(), scratch_types=[])
    def scatter(src_hbm, dest_hbm, out_hbm):
        def body(rows_vmem, dest_vmem):
            pltpu.sync_copy(rows_vmem, out_hbm.at[dest_vmem.at[0]])

        pltpu.emit_pipeline(
            body,
            grid=(n_assign // SC_WINDOW,),
            in_specs=[pl.BlockSpec((SC_WINDOW, width), lambda i: (i % src_blocks, 0)),
                      pl.BlockSpec((1, SC_WINDOW), lambda i: (0, i))],
            out_specs=[],
            core_axis_name=("core", "subcore"),
            dimension_semantics=(pltpu.PARALLEL,),
        )(src_hbm, dest_hbm)

    return scatter(src, dest)


def _sc_gather_rows(table, idx):
    width = table.shape[1]
    n_assign = idx.shape[1]

    @functools.partial(pl.kernel, out_type=jax.ShapeDtypeStruct((n_assign, width), table.dtype),
                       mesh=_sc_mesh(), scratch_types=[])
    def gather(table_hbm, idx_hbm, out_hbm):
        def body(idx_vmem, rows_vmem):
            pltpu.sync_copy(table_hbm.at[idx_vmem.at[0]], rows_vmem)

        pltpu.emit_pipeline(
            body,
            grid=(n_assign // SC_WINDOW,),
            in_specs=[pl.BlockSpec((1, SC_WINDOW), lambda i: (0, i))],
            out_specs=[pl.BlockSpec((SC_WINDOW, width), lambda i: (i, 0))],
            core_axis_name=("core", "subcore"),
            dimension_semantics=(pltpu.PARALLEL,),
        )(idx_hbm, out_hbm)

    return gather(table, idx)


def _final_kernel(n_prompt_tiles, x1_ref, *refs):
    h2_refs = refs[:SC_SPLIT]
    yg_refs = refs[SC_SPLIT:2 * SC_SPLIT]
    wt_ref, gt2_ref, wsg_ref, wsd_ref, gf_ref, op_ref, os_ref = refs[2 * SC_SPLIT:]
    is_prompt = pl.program_id(0) < n_prompt_tiles
    h2 = _unpack_rows([r[...] for r in h2_refs]).astype(BF16)
    gu = jnp.dot(h2, wsg_ref[...], preferred_element_type=F32)
    hid = _silu(gu[:, :SHARED_DIM]) * gu[:, SHARED_DIM:]
    y = jnp.dot(hid.astype(BF16), wsd_ref[...], preferred_element_type=F32)
    wt = wt_ref[...]
    for k in range(TOP_K):
        y = y + wt[:, k:k + 1] * _unpack_rows([r[k] for r in yg_refs])
    x2 = x1_ref[...] + gt2_ref[...] * y
    out = _rms(x2, gf_ref[...])

    @pl.when(is_prompt)
    def _():
        op_ref[...] = out

    @pl.when(jnp.logical_not(is_prompt))
    def _():
        os_ref[...] = out


def _final(x1_all, h2_all, y_gath, wt_rows, gt2, wsg, wsd, gf, grp_of_tile, n_prompt_tiles):
    t_all = x1_all.shape[0]
    tm = ROW_TILE
    t_p = n_prompt_tiles * tm
    row = lambda i: (i, 0)
    fixed2 = lambda i: (0, 0)
    return pl.pallas_call(
        functools.partial(_final_kernel, n_prompt_tiles),
        grid=(t_all // tm,),
        in_specs=[pl.BlockSpec((tm, D_MODEL), row)]
        + [pl.BlockSpec((tm, PIECE), row)] * SC_SPLIT
        + [pl.BlockSpec((TOP_K, tm, PIECE), lambda i: (0, i, 0))] * SC_SPLIT
        + [pl.BlockSpec((tm, TOP_K), row),
           pl.BlockSpec((None, tm, D_MODEL), lambda i: (grp_of_tile(i), 0, 0)),
           pl.BlockSpec(wsg.shape, fixed2),
           pl.BlockSpec(wsd.shape, fixed2),
           pl.BlockSpec((1, D_MODEL), fixed2)],
        out_specs=[pl.BlockSpec((tm, D_MODEL), lambda i: (jnp.minimum(i, n_prompt_tiles - 1), 0)),
                   pl.BlockSpec((tm, D_MODEL), lambda i: (jnp.maximum(i - n_prompt_tiles, 0), 0))],
        out_shape=[jax.ShapeDtypeStruct((t_p, D_MODEL), F32),
                   jax.ShapeDtypeStruct((t_all - t_p, D_MODEL), F32)],
        compiler_params=_params(("arbitrary",)),
        name="final",
    )(x1_all, *h2_all, *y_gath, wt_rows, gt2, wsg, wsd, gf)


def _rope_tables(pos):
    freqs = ROPE_THETA ** (-jnp.arange(ROPE_HALF, dtype=F32) / ROPE_HALF)
    ang = pos.astype(F32)[:, None] * freqs
    return jnp.cos(ang), jnp.sin(ang)


def kernel(x_prompt, x_sample, cache_ckv, cache_krope, page_table, c_prompt, c_sample, w_ada, b_ada, norm1_g, norm2_g, w_in, a_vnorm_g, w_spatial, b_spatial, w_out_a, q_norm_g, w_uq, kv_norm_g, w_uk, w_uv, w_out_b, w_o, w_router, router_bias, w_exp_gu, w_exp_down, w_sh_gu, w_sh_down, final_norm_g):
    batch, seq, _ = x_prompt.shape
    dec_b, dec_s, _ = x_sample.shape
    n_pages = page_table.shape[1]
    n_past = n_pages * CHUNK
    t_p, t_s = batch * seq, dec_b * dec_s
    t_all = t_p + t_s
    tm = ROW_TILE
    assert seq % ATTN_Q_STEP == 0 and t_s % tm == 0 and CHUNK % dec_s == 0 and n_pages % PAGES_PER_STEP == 0
    assert (t_all * TOP_K) % RANK_TILE == 0 and w_ada.shape[0] == 1
    n_prompt_tiles = t_p // tm
    tiles_per_b = seq // tm

    def grp_of_tile(i):
        return jnp.minimum(i // tiles_per_b, batch) + jnp.maximum(i - n_prompt_tiles, 0)

    n_c = batch + dec_b
    c_rows = -(-n_c // 8) * 8
    c_all = jnp.concatenate([c_prompt, c_sample, jnp.zeros((c_rows - n_c, D_MODEL), F32)], axis=0)
    mod = _ada(c_all, w_ada[0].astype(BF16), b_ada[0][None, :])
    mod_p = jnp.broadcast_to(mod[:batch, None, :], (batch, tm, 6 * D_MODEL))
    mod_s = jnp.repeat(mod[batch:n_c], dec_s, axis=0).reshape(t_s // tm, tm, 6 * D_MODEL)
    mod_g = jnp.concatenate([mod_p, mod_s], axis=0)
    sh1, sc1, gt1, sh2, sc2, gt2 = [mod_g[:, :, k * D_MODEL:(k + 1) * D_MODEL] for k in range(6)]

    o_u, o_v, o_cq, o_ckv, o_kr = A_WIDTH, 2 * A_WIDTH, 2 * A_WIDTH + Q_LORA, 2 * A_WIDTH + Q_LORA + KV_LORA, \
        2 * A_WIDTH + Q_LORA + KV_LORA + QK_ROPE
    win = w_in[0]
    w5 = jnp.concatenate([win[:, :o_kr], jnp.zeros((D_MODEL, IN_END - IN_KR - QK_ROPE), F32)], axis=1).astype(BF16)
    wg = win[:, o_kr:].astype(BF16)
    row1 = lambda v: v.reshape(1, -1)
    tri = jnp.tril(jnp.ones((CHUNK, CHUNK), F32))
    wc_p = w_spatial[0] * tri
    per = CHUNK // dec_s
    small = (w_spatial[0] * tri)[:, :dec_s, :dec_s]
    wc_s = jnp.einsum('ab,gts->gatbs', jnp.eye(per, dtype=F32), small).reshape(A_GROUPS, CHUNK, CHUNK)
    wc = jnp.stack([wc_p, wc_s]).astype(BF16)
    bias_p = jnp.repeat(b_spatial[0].T, A_GROUP_DIM, axis=1)
    bias_s = jnp.tile(bias_p[:dec_s], (per, 1))
    bc = jnp.stack([bias_p, bias_s])

    pos_p = jnp.arange(seq)
    pos_s = n_past + jnp.arange(dec_s)
    cos_p, sin_p = _rope_tables(pos_p)
    cos_s, sin_s = _rope_tables(pos_s)
    cos_rows = jnp.concatenate([jnp.tile(cos_p, (batch, 1)), jnp.tile(cos_s, (dec_b, 1))], axis=0)
    sin_rows = jnp.concatenate([jnp.tile(sin_p, (batch, 1)), jnp.tile(sin_s, (dec_b, 1))], axis=0)
    lane_pad = jnp.zeros((t_all, LANES - QK_ROPE), F32)
    cos_kr = jnp.concatenate([cos_rows, cos_rows, lane_pad], axis=1)
    sin_kr = jnp.concatenate([-sin_rows, sin_rows, lane_pad], axis=1)

    x_p, x_s = x_prompt.reshape(t_p, D_MODEL), x_sample.reshape(t_s, D_MODEL)
    g1, g2 = row1(norm1_g[0]), row1(norm2_g[0])
    a_all, cq_all, ckv_p, kr_p, ckv_s, kr_s, v_s = _in_proj(
        x_p, x_s, sh1, sc1, g1, w5, row1(a_vnorm_g[0]), row1(q_norm_g[0]), row1(kv_norm_g[0]),
        cos_kr, sin_kr, wc, bc, grp_of_tile, n_prompt_tiles)

    wuq = w_uq[0]
    wuqt = jnp.pad(wuq, ((0, 0), (0, 0), (0, HEAD_PAD - QK_NOPE - QK_ROPE))).reshape(Q_LORA, -1).T.astype(BF16)
    wukp = jnp.pad(w_uk[0], ((0, 0), (0, 0), (0, HEAD_PAD - QK_NOPE))).reshape(KV_LORA, -1).astype(BF16)
    place_h = jnp.pad(jnp.eye(QK_ROPE, dtype=F32), ((0, 0), (QK_NOPE, HEAD_PAD - QK_NOPE - QK_ROPE)))
    place = jnp.tile(place_h, (1, N_HEADS)).astype(BF16)
    wuvt = w_uv[0].reshape(KV_LORA, -1).T.astype(BF16)
    qt, k, vt = _qkv(cq_all, ckv_p, kr_p, wuqt, wukp, place, wuvt, cos_p.T, sin_p.T, batch, seq)
    ot = _attn(qt, k, vt)

    wq_s = jnp.concatenate([wuq[:, :, :QK_NOPE].reshape(Q_LORA, -1),
                            wuq[:, :, QK_NOPE:QK_NOPE + ROPE_HALF].reshape(Q_LORA, -1),
                            wuq[:, :, QK_NOPE + ROPE_HALF:].reshape(Q_LORA, -1)], axis=1).astype(BF16)
    eye_h = jnp.eye(N_HEADS, dtype=F32)
    wuk_blk = jnp.einsum('rhd,hg->hdgr', w_uk[0], eye_h).reshape(N_HEADS * QK_NOPE, N_HEADS * KV_LORA).astype(BF16)
    wuv_blk = jnp.einsum('rhd,hg->hrgd', w_uv[0], eye_h).reshape(N_HEADS * KV_LORA, N_HEADS * V_HEAD).astype(BF16)
    cos_sq = jnp.tile(jnp.tile(cos_s, (1, N_HEADS)), (dec_b, 1))
    sin_sq = jnp.tile(jnp.tile(sin_s, (1, N_HEADS)), (dec_b, 1))
    ql, r1, r2 = _sample_q(cq_all[t_p:], wq_s, wuk_blk, cos_sq, sin_sq)
    rows = dec_s * N_HEADS
    ql3 = ql.reshape(dec_b, rows, KV_LORA)
    qr3 = jnp.concatenate([r1.reshape(dec_b, dec_s, N_HEADS, ROPE_HALF),
                           r2.reshape(dec_b, dec_s, N_HEADS, ROPE_HALF)], axis=-1).reshape(dec_b, rows, QK_ROPE)
    new_pad = ((0, 0), (0, 8 - dec_s), (0, 0))
    ckv_new = jnp.pad(ckv_s.reshape(dec_b, dec_s, KV_LORA), new_pad)
    kr_new_t = jnp.swapaxes(jnp.pad(kr_s.reshape(dec_b, dec_s, QK_ROPE), new_pad), 1, 2)
    cache_kr_t = jnp.swapaxes(cache_krope[0], 1, 2)
    o_lat = _sample_attn(page_table, cache_ckv[0], cache_kr_t, ql3, qr3, ckv_new, kr_new_t)
    o_s = _matmul(o_lat.reshape(t_s, N_HEADS * KV_LORA), wuv_blk, BF16)

    x1_all, *h2_all = _merge(x_p, x_s, (sh1, sc1, gt1, sh2, sc2), g1, g2, a_all, ot, o_s, wg,
                            w_out_a[0].astype(BF16), w_out_b[0].astype(BF16), w_o[0].astype(BF16),
                            grp_of_tile, n_prompt_tiles)

    wr_t = w_router[0].T
    whi = wr_t.astype(BF16)
    wlo = (wr_t - whi.astype(F32)).astype(BF16)
    idx_t, wt_t = _route(x1_all, sh2, sc2, g2, whi, wlo, router_bias[0].reshape(-1, 1), grp_of_tile)
    n_assign = t_all * TOP_K
    e_flat = idx_t.reshape(1, n_assign)
    rank, counts = _rank(e_flat)
    counts = counts[:, 0].astype(jnp.int32)
    blk = MOE_BLOCK
    padded = (counts + blk - 1) // blk * blk
    pad_end = jnp.cumsum(padded)
    pad_start = pad_end - padded
    dest = _dest(e_flat, rank, pad_start.reshape(-1, 1))
    n_blocks = -(-n_assign // blk) + N_EXPERTS
    n_slots = n_blocks * blk
    first_row = jnp.arange(n_blocks, dtype=jnp.int32) * blk
    block_e = jnp.minimum(jnp.sum(pad_end[None, :] <= first_row[:, None], axis=1), N_EXPERTS - 1).astype(jnp.int32)
    n_valid = (pad_end[-1] // blk).astype(jnp.int32).reshape(1)
    experts = jnp.arange(N_EXPERTS, dtype=jnp.int32)
    owners = jnp.where(padded > 0, experts, N_EXPERTS)
    later_owner = lax.cummin(owners, reverse=True)
    next_expert = jnp.concatenate([later_owner[1:], jnp.full((1,), N_EXPERTS, jnp.int32)])
    x_sorted = [_sc_scatter_rows(h, dest, n_slots) for h in h2_all]
    y_sorted = _moe(block_e, n_valid, next_expert, x_sorted, w_exp_gu[0], w_exp_down[0])
    y_gath = [_sc_gather_rows(y, dest).reshape(TOP_K, t_all, PIECE) for y in y_sorted]

    y_p, y_s = _final(x1_all, h2_all, y_gath, wt_t.T, gt2, w_sh_gu[0].astype(BF16), w_sh_down[0].astype(BF16),
                      row1(final_norm_g), grp_of_tile, n_prompt_tiles)

    y_prompt = y_p.reshape(batch, seq, D_MODEL)
    y_sample = y_s.reshape(dec_b, dec_s, D_MODEL)
    new_ckv_prompt = ckv_p.reshape(1, batch, seq, KV_LORA)
    new_krope_prompt = kr_p.reshape(1, batch, seq, QK_ROPE)
    new_ckv_sample = ckv_s.reshape(1, dec_b, dec_s, KV_LORA)
    new_krope_sample = kr_s.reshape(1, dec_b, dec_s, QK_ROPE)
    new_chunk_v_sample = v_s.reshape(1, dec_b, dec_s, A_WIDTH)
    return (y_prompt, y_sample, new_ckv_prompt, new_krope_prompt, new_ckv_sample, new_krope_sample,
            new_chunk_v_sample)
```

```python
import functools

import jax
import jax.numpy as jnp
from jax import lax
from jax.experimental import pallas as pl
from jax.experimental.pallas import tpu as pltpu
from jax.experimental.pallas import tpu_sc as plsc

F32 = jnp.float32
BF16 = jnp.bfloat16

D_MODEL = 1024
A_WIDTH = D_MODEL // 2
A_GROUPS = 8
A_GROUP_DIM = A_WIDTH // A_GROUPS
CHUNK = 128
N_HEADS = 8
QK_NOPE = 64
QK_ROPE = 32
ROPE_HALF = QK_ROPE // 2
V_HEAD = 64
Q_LORA = 384
KV_LORA = 256
ROPE_THETA = 10000.0
ATTN_SCALE = (QK_NOPE + QK_ROPE) ** -0.5
N_EXPERTS = 256
TOP_K = 8
N_GROUPS = 8
GROUP_SIZE = N_EXPERTS // N_GROUPS
TOPK_GROUPS = 4
EXPERT_DIM = 256
SHARED_DIM = 256
ROUTED_SCALE = 2.5
EPS = 1e-6

LANES = 128
HEAD_PAD = 128
ROW_TILE = 256
ATTN_KV_TILE = 256
ATTN_Q_STEP = 1024
ATTN_HEADS_PER_STEP = 2
LOG2_E = 1.4426950408889634
RANK_TILE = 512
MOE_BLOCK = 256
MOE_SUB = 128
PAGES_PER_STEP = 32
PAGE_GROUP = 8
V_ROWS = V_HEAD + 16
SC_WINDOW = 128
SC_SPLIT = 2
PIECE = D_MODEL // 2 // SC_SPLIT
VMEM_LIMIT = 48 * 1024 * 1024

IN_U, IN_V, IN_CQ, IN_CKV, IN_KR, IN_END = 0, 512, 1024, 1408, 1664, 1792

NT_DIMS = (((1,), (1,)), ((), ()))
TN_DIMS = (((0,), (0,)), ((), ()))


def _params(sem, vmem=VMEM_LIMIT):
    return pltpu.CompilerParams(dimension_semantics=sem, vmem_limit_bytes=vmem)


def _rms(x, g):
    return x * lax.rsqrt(jnp.mean(x * x, axis=-1, keepdims=True) + EPS) * g


def _gelu(x):
    return 0.5 * x * (1.0 + jnp.tanh(0.7978845608028654 * (x + 0.044715 * (x * x * x))))


def _sigmoid(x):
    return 1.0 / (1.0 + jnp.exp(-x))


def _silu(x):
    return x * _sigmoid(x)


def _pack_rows(x):
    pieces = []
    for c in range(SC_SPLIT):
        lo = x[:, 2 * c * PIECE:(2 * c + 1) * PIECE].astype(BF16).astype(F32)
        hi = x[:, (2 * c + 1) * PIECE:(2 * c + 2) * PIECE].astype(BF16).astype(F32)
        pieces.append(lax.bitcast_convert_type(hi, jnp.int32)
                      | lax.shift_right_logical(lax.bitcast_convert_type(lo, jnp.int32), 16))
    return pieces


def _unpack_rows(pieces):
    cols = []
    for p in pieces:
        cols.append(lax.bitcast_convert_type(lax.shift_left(p, 16), F32))
        cols.append(lax.bitcast_convert_type(p & jnp.int32(-65536), F32))
    return jnp.concatenate(cols, axis=1)


def _ada_kernel(c_ref, w_ref, b_ref, o_ref):
    s = _silu(c_ref[...]).astype(BF16)
    o_ref[...] = jnp.dot(s, w_ref[...], preferred_element_type=F32) + b_ref[...]


def _ada(c, w, b):
    rows, n = c.shape[0], w.shape[1]
    tn = 1536
    return pl.pallas_call(
        _ada_kernel,
        grid=(n // tn,),
        in_specs=[pl.BlockSpec((rows, D_MODEL), lambda j: (0, 0)),
                  pl.BlockSpec((D_MODEL, tn), lambda j: (0, j)),
                  pl.BlockSpec((1, tn), lambda j: (0, j))],
        out_specs=pl.BlockSpec((rows, tn), lambda j: (0, j)),
        out_shape=jax.ShapeDtypeStruct((rows, n), F32),
        compiler_params=_params(("parallel",)),
        name="ada",
    )(c, w, b)


def _in_kernel(n_prompt_tiles, xp_ref, xs_ref, sh_ref, sc_ref, g1_ref, w_ref, vg_ref, qg_ref, kg_ref,
               cos_ref, sin_ref, wc_ref, bc_ref,
               a_ref, cq_ref, ckvp_ref, krp_ref, ckvs_ref, krs_ref, vs_ref):
    tm = xp_ref.shape[0]
    is_prompt = pl.program_id(0) < n_prompt_tiles
    x = jnp.where(is_prompt, xp_ref[...], xs_ref[...])
    h = _rms(x, g1_ref[...]) * (1.0 + sc_ref[...]) + sh_ref[...]
    z = jnp.dot(h.astype(BF16), w_ref[...], preferred_element_type=F32)
    u = _gelu(z[:, IN_U:IN_V])
    v = _rms(_gelu(z[:, IN_V:IN_CQ]), vg_ref[...])
    cq_ref[...] = _rms(z[:, IN_CQ:IN_CKV], qg_ref[...]).astype(BF16)
    ckv = _rms(z[:, IN_CKV:IN_KR], kg_ref[...])
    kr = z[:, IN_KR:IN_END]
    lane = lax.broadcasted_iota(jnp.int32, kr.shape, 1)
    swapped = jnp.where(lane < ROPE_HALF, pltpu.roll(kr, LANES - ROPE_HALF, 1), pltpu.roll(kr, ROPE_HALF, 1))
    kr_rot = (kr * cos_ref[...] + swapped * sin_ref[...])[:, :QK_ROPE]

    @pl.when(is_prompt)
    def _():
        ckvp_ref[...] = ckv
        krp_ref[...] = kr_rot

    @pl.when(jnp.logical_not(is_prompt))
    def _():
        ckvs_ref[...] = ckv
        krs_ref[...] = kr_rot
        vs_ref[...] = v

    vb = v.astype(BF16)
    half = A_WIDTH // 2
    grp_of_lane = lax.broadcasted_iota(jnp.int32, (CHUNK, half), 1) // A_GROUP_DIM
    for ci in range(tm // CHUNK):
        rows = slice(ci * CHUNK, (ci + 1) * CHUNK)
        parts = []
        for q in range(2):
            vq = vb[rows, q * half:(q + 1) * half]
            acc = None
            for gg in range(A_GROUPS // 2):
                vm = jnp.where(grp_of_lane == gg, vq, jnp.zeros_like(vq))
                part = jnp.dot(wc_ref[q * (A_GROUPS // 2) + gg], vm, preferred_element_type=F32)
                acc = part if acc is None else acc + part
            parts.append(acc)
        sp = jnp.concatenate(parts, axis=1) + bc_ref[...]
        a_ref[rows, :] = (u[rows, :] * sp).astype(BF16)


def _in_proj(x_p, x_s, sh1, sc1, g1, w5, vg, qg, kg, cos_kr, sin_kr, wc, bc, grp_of_tile, n_prompt_tiles):
    t_p, t_s = x_p.shape[0], x_s.shape[0]
    t_all = t_p + t_s
    tm = ROW_TILE
    n_tiles = t_all // tm
    row = lambda i: (i, 0)
    prow = lambda i: (jnp.minimum(i, n_prompt_tiles - 1), 0)
    srow = lambda i: (jnp.maximum(i - n_prompt_tiles, 0), 0)
    fixed2 = lambda i: (0, 0)
    kind = lambda i: (jnp.where(i < n_prompt_tiles, 0, 1), 0, 0)
    kind4 = lambda i: (jnp.where(i < n_prompt_tiles, 0, 1), 0, 0, 0)
    mod = lambda i: (grp_of_tile(i), 0, 0)
    return pl.pallas_call(
        functools.partial(_in_kernel, n_prompt_tiles),
        grid=(n_tiles,),
        in_specs=[pl.BlockSpec((tm, D_MODEL), prow),
                  pl.BlockSpec((tm, D_MODEL), srow),
                  pl.BlockSpec((None, tm, D_MODEL), mod),
                  pl.BlockSpec((None, tm, D_MODEL), mod),
                  pl.BlockSpec((1, D_MODEL), fixed2),
                  pl.BlockSpec((D_MODEL, IN_END), fixed2),
                  pl.BlockSpec((1, A_WIDTH), fixed2),
                  pl.BlockSpec((1, Q_LORA), fixed2),
                  pl.BlockSpec((1, KV_LORA), fixed2),
                  pl.BlockSpec((tm, LANES), row),
                  pl.BlockSpec((tm, LANES), row),
                  pl.BlockSpec((None, A_GROUPS, CHUNK, CHUNK), kind4),
                  pl.BlockSpec((None, CHUNK, A_WIDTH), kind)],
        out_specs=[pl.BlockSpec((tm, A_WIDTH), row),
                   pl.BlockSpec((tm, Q_LORA), row),
                   pl.BlockSpec((tm, KV_LORA), prow),
                   pl.BlockSpec((tm, QK_ROPE), prow),
                   pl.BlockSpec((tm, KV_LORA), srow),
                   pl.BlockSpec((tm, QK_ROPE), srow),
                   pl.BlockSpec((tm, A_WIDTH), srow)],
        out_shape=[jax.ShapeDtypeStruct((t_all, A_WIDTH), BF16),
                   jax.ShapeDtypeStruct((t_all, Q_LORA), BF16),
                   jax.ShapeDtypeStruct((t_p, KV_LORA), F32),
                   jax.ShapeDtypeStruct((t_p, QK_ROPE), F32),
                   jax.ShapeDtypeStruct((t_s, KV_LORA), F32),
                   jax.ShapeDtypeStruct((t_s, QK_ROPE), F32),
                   jax.ShapeDtypeStruct((t_s, A_WIDTH), F32)],
        compiler_params=_params(("arbitrary",)),
        name="in_proj",
    )(x_p, x_s, sh1, sc1, g1, w5, vg, qg, kg, cos_kr, sin_kr, wc, bc)


def _qkv_kernel(cq_ref, ckv_ref, kr_ref, wuqt_ref, wukp_ref, place_ref, wuvt_ref, cos_ref, sin_ref,
                qt_ref, k_ref, vt_ref):
    tm = cq_ref.shape[0]
    ckv = ckv_ref[...].astype(BF16)
    kr = kr_ref[...].astype(BF16)
    qt = lax.dot_general(wuqt_ref[...], cq_ref[...], NT_DIMS, preferred_element_type=F32)
    c = cos_ref[...]
    s = sin_ref[...]
    pad = jnp.zeros((HEAD_PAD - QK_NOPE - QK_ROPE, tm), F32)
    for h in range(N_HEADS):
        blk = qt[h * HEAD_PAD:(h + 1) * HEAD_PAD]
        x1 = blk[QK_NOPE:QK_NOPE + ROPE_HALF]
        x2 = blk[QK_NOPE + ROPE_HALF:QK_NOPE + QK_ROPE]
        full = jnp.concatenate([blk[:QK_NOPE], x1 * c - x2 * s, x1 * s + x2 * c, pad], axis=0)
        qt_ref[h] = (full * (ATTN_SCALE * LOG2_E)).astype(BF16)
    k = (jnp.dot(ckv, wukp_ref[...], preferred_element_type=F32)
         + jnp.dot(kr, place_ref[...], preferred_element_type=F32))
    for h in range(N_HEADS):
        k_ref[h] = k[:, h * HEAD_PAD:(h + 1) * HEAD_PAD].astype(BF16)
    vt = lax.dot_general(wuvt_ref[...], ckv, NT_DIMS, preferred_element_type=F32)
    extra = V_ROWS - V_HEAD
    ones_row = jnp.where(lax.broadcasted_iota(jnp.int32, (extra, tm), 0) == 0, 1.0, 0.0)
    for h in range(N_HEADS):
        vt_ref[h] = jnp.concatenate([vt[h * V_HEAD:(h + 1) * V_HEAD], ones_row], axis=0).astype(BF16)


def _qkv(cq_all, ckv_all, kr_all, wuqt, wukp, place, wuvt, cos_t, sin_t, batch, seq):
    tm = ATTN_KV_TILE
    nk = seq // tm
    row = lambda b, j: (b * nk + j, 0)
    fixed2 = lambda b, j: (0, 0)
    tab = lambda b, j: (0, j)
    return pl.pallas_call(
        _qkv_kernel,
        grid=(batch, nk),
        in_specs=[pl.BlockSpec((tm, Q_LORA), row),
                  pl.BlockSpec((tm, KV_LORA), row),
                  pl.BlockSpec((tm, QK_ROPE), row),
                  pl.BlockSpec(wuqt.shape, fixed2),
                  pl.BlockSpec(wukp.shape, fixed2),
                  pl.BlockSpec(place.shape, fixed2),
                  pl.BlockSpec(wuvt.shape, fixed2),
                  pl.BlockSpec((ROPE_HALF, tm), tab),
                  pl.BlockSpec((ROPE_HALF, tm), tab)],
        out_specs=[pl.BlockSpec((None, N_HEADS, HEAD_PAD, tm), lambda b, j: (b, 0, 0, j)),
                   pl.BlockSpec((None, N_HEADS, tm, HEAD_PAD), lambda b, j: (b, 0, j, 0)),
                   pl.BlockSpec((None, N_HEADS, None, V_ROWS, tm), lambda b, j: (b, 0, j, 0, 0))],
        out_shape=[jax.ShapeDtypeStruct((batch, N_HEADS, HEAD_PAD, seq), BF16),
                   jax.ShapeDtypeStruct((batch, N_HEADS, seq, HEAD_PAD), BF16),
                   jax.ShapeDtypeStruct((batch, N_HEADS, nk, V_ROWS, tm), BF16)],
        compiler_params=_params(("parallel", "parallel")),
        name="qkv",
    )(cq_all, ckv_all, kr_all, wuqt, wukp, place, wuvt, cos_t, sin_t)


def _attn_kernel(qt_ref, k_ref, vt_ref, o_ref):
    tk = ATTN_KV_TILE
    n_heads = qt_ref.shape[0]
    n_sub = qt_ref.shape[2] // tk
    qi = pl.program_id(2)
    j0 = qi * n_sub
    chains = [(h, sub) for h in range(n_heads) for sub in range(n_sub)]

    def tiles(j, active, carries, diagonal_sub):
        scores = [jnp.dot(k_ref[h, j], qt_ref[h, :, sub * tk:(sub + 1) * tk], preferred_element_type=F32)
                  for h, sub in active]
        stats = []
        for (h, sub), s in zip(active, scores):
            m, _ = carries[h * n_sub + sub]
            if sub == diagonal_sub:
                key = lax.broadcasted_iota(jnp.int32, s.shape, 0)
                qry = lax.broadcasted_iota(jnp.int32, s.shape, 1)
                s = jnp.where(key <= qry, s, -jnp.inf)
            m_new = jnp.maximum(m, jnp.max(s, axis=0, keepdims=True))
            stats.append((m_new, jnp.exp2(m - m_new), jnp.exp2(s - m_new).astype(BF16)))
        out = list(carries)
        for (h, sub), (m_new, alpha, p) in zip(active, stats):
            c = h * n_sub + sub
            out[c] = (m_new, alpha * carries[c][1] + jnp.dot(vt_ref[h, j], p, preferred_element_type=F32))
        return out

    init = tuple((jnp.full((1, tk), -jnp.inf, F32), jnp.zeros((V_ROWS, tk), F32)) for _ in chains)
    carries = list(lax.fori_loop(0, j0, lambda j, c: tuple(tiles(j, chains, c, None)), init))
    for jj in range(n_sub):
        carries = tiles(j0 + jj, [(h, sub) for h, sub in chains if sub >= jj], carries, jj)
    for h, sub in chains:
        _, acc = carries[h * n_sub + sub]
        o_ref[h * V_HEAD:(h + 1) * V_HEAD, sub * tk:(sub + 1) * tk] = (
            acc[:V_HEAD] / acc[V_HEAD:V_HEAD + 1]).astype(BF16)


def _attn(qt, k, vt):
    batch, _, _, seq = qt.shape
    tk = ATTN_KV_TILE
    nk = seq // tk
    hp = ATTN_HEADS_PER_STEP
    k5 = k.reshape(batch, N_HEADS, nk, tk, HEAD_PAD)
    return pl.pallas_call(
        _attn_kernel,
        grid=(batch, N_HEADS // hp, seq // ATTN_Q_STEP),
        in_specs=[pl.BlockSpec((None, hp, HEAD_PAD, ATTN_Q_STEP), lambda b, h, i: (b, h, 0, i)),
                  pl.BlockSpec((None, hp, nk, tk, HEAD_PAD), lambda b, h, i: (b, h, 0, 0, 0)),
                  pl.BlockSpec((None, hp, nk, V_ROWS, tk), lambda b, h, i: (b, h, 0, 0, 0))],
        out_specs=pl.BlockSpec((None, hp * V_HEAD, ATTN_Q_STEP), lambda b, h, i: (b, h, i)),
        out_shape=jax.ShapeDtypeStruct((batch, N_HEADS * V_HEAD, seq), BF16),
        compiler_params=_params(("parallel", "parallel", "parallel")),
        name="attn",
    )(qt, k5, vt)


def _sq_kernel(cq_ref, wq_ref, wukb_ref, cos_ref, sin_ref, ql_ref, r1_ref, r2_ref):
    n_nope = N_HEADS * QK_NOPE
    q = jnp.dot(cq_ref[...], wq_ref[...], preferred_element_type=F32)
    x1 = q[:, n_nope:n_nope + LANES]
    x2 = q[:, n_nope + LANES:]
    c = cos_ref[...]
    s = sin_ref[...]
    r1_ref[...] = ((x1 * c - x2 * s) * ATTN_SCALE).astype(BF16)
    r2_ref[...] = ((x1 * s + x2 * c) * ATTN_SCALE).astype(BF16)
    ql = jnp.dot(q[:, :n_nope].astype(BF16), wukb_ref[...], preferred_element_type=F32)
    ql_ref[...] = (ql * ATTN_SCALE).astype(BF16)


def _sample_q(cq_s, wq_s, wuk_blk, cos_s, sin_s):
    ts = cq_s.shape[0]
    full = lambda shape: pl.BlockSpec(shape, lambda i: (0,) * len(shape))
    return pl.pallas_call(
        _sq_kernel,
        grid=(1,),
        in_specs=[full(cq_s.shape), full(wq_s.shape), full(wuk_blk.shape), full(cos_s.shape), full(sin_s.shape)],
        out_specs=[full((ts, N_HEADS * KV_LORA)), full((ts, LANES)), full((ts, LANES))],
        out_shape=[jax.ShapeDtypeStruct((ts, N_HEADS * KV_LORA), BF16),
                   jax.ShapeDtypeStruct((ts, LANES), BF16),
                   jax.ShapeDtypeStruct((ts, LANES), BF16)],
        compiler_params=_params(("arbitrary",)),
        name="sample_q",
    )(cq_s, wq_s, wuk_blk, cos_s, sin_s)


def _sattn_kernel(n_pages, pt_ref, ckv_hbm, kr_hbm, ql_ref, qr_ref, cnew_ref, knew_ref, o_ref,
                  cbuf, rbuf, sem_c, sem_r):
    npg = PAGES_PER_STEP
    n_chunks = n_pages // npg
    PAGE_SLOTS = cbuf.shape[0]
    PAGE_LOOKAHEAD = PAGE_SLOTS // 2
    b = pl.program_id(0)

    def page_copies(bb, chunk, slot, p):
        page = pt_ref[bb * n_pages + chunk * npg + p]
        return (pltpu.make_async_copy(ckv_hbm.at[page], cbuf.at[slot, p], sem_c.at[slot]),
                pltpu.make_async_copy(kr_hbm.at[page], rbuf.at[slot, p], sem_r.at[slot]))

    def start_chunk(bb, chunk, slot):
        for p in range(npg):
            for cp in page_copies(bb, chunk, slot, p):
                cp.start()

    def wait_chunk(bb, chunk, slot):
        for p in range(npg):
            for cp in page_copies(bb, chunk, slot, p):
                cp.wait()

    @pl.when(b == 0)
    def _():
        for chunk in range(PAGE_LOOKAHEAD):
            start_chunk(0, chunk, chunk)

    ql = ql_ref[...]
    qr = qr_ref[...]

    def partial_softmax(blocks, mask=None):
        scores = [lax.dot_general(ql, keys, NT_DIMS, preferred_element_type=F32)
                  + jnp.dot(qr, rope_t, preferred_element_type=F32) for keys, rope_t in blocks]
        probs = []
        for s in scores:
            if mask is not None:
                s = jnp.where(mask(s.shape), s, -jnp.inf)
            m = jnp.max(s, axis=-1, keepdims=True)
            p = jnp.exp(s - m)
            probs.append((m, jnp.sum(p, axis=-1, keepdims=True), p.astype(BF16)))
        return [(m, l, jnp.dot(p, keys, preferred_element_type=F32)) for (m, l, p), (keys, _) in zip(probs, blocks)]

    def merge(state, parts):
        m_old, l_old, acc_old = state
        m_new = m_old
        for m, _, _ in parts:
            m_new = jnp.maximum(m_new, m)
        alpha = jnp.exp(m_old - m_new)
        l = alpha * l_old
        acc = alpha * acc_old
        for m, lp, op in parts:
            w = jnp.exp(m - m_new)
            l = l + w * lp
            acc = acc + w * op
        return m_new, l, acc

    def chunk_step(chunk, state):
        slot = chunk % PAGE_SLOTS
        wait_chunk(b, chunk, slot)
        ahead = chunk + PAGE_LOOKAHEAD
        ahead_b = jnp.where(ahead >= n_chunks, b + 1, b)
        ahead_b = jnp.where(ahead_b >= pl.num_programs(0), 0, ahead_b)
        start_chunk(ahead_b, ahead % n_chunks, ahead % PAGE_SLOTS)

        blocks = []
        for g in range(npg // PAGE_GROUP):
            pages = range(g * PAGE_GROUP, (g + 1) * PAGE_GROUP)
            blocks.append((jnp.concatenate([cbuf[slot, p].astype(BF16) for p in pages], axis=0),
                           jnp.concatenate([rbuf[slot, p].astype(BF16) for p in pages], axis=1)))
        return merge(state, partial_softmax(blocks))

    rows = ql.shape[0]
    state = (jnp.full((rows, 1), -jnp.inf, F32), jnp.zeros((rows, 1), F32), jnp.zeros((rows, KV_LORA), F32))
    state = lax.fori_loop(0, n_chunks, chunk_step, state)

    @pl.when(b == pl.num_programs(0) - 1)
    def _():
        for chunk in range(PAGE_LOOKAHEAD):
            wait_chunk(0, chunk, chunk % PAGE_SLOTS)

    def causal(shape):
        q_pos = lax.broadcasted_iota(jnp.int32, shape, 0) // N_HEADS
        return lax.broadcasted_iota(jnp.int32, shape, 1) <= q_pos

    _, l, acc = merge(state, partial_softmax([(cnew_ref[...].astype(BF16), knew_ref[...].astype(BF16))], causal))
    o_ref[...] = acc / l


def _sample_attn(page_table, cache_ckv, cache_kr_t, ql, qr, cnew, knew_t):
    dec_b, n_pages = page_table.shape
    npg = PAGES_PER_STEP
    n_chunks = n_pages // npg
    slots = 4 if n_chunks % 4 == 0 else 2
    assert n_chunks % slots == 0
    rows = ql.shape[1]
    n_new = cnew.shape[1]
    per_b = lambda b, pt: (b, 0, 0)
    return pl.pallas_call(
        functools.partial(_sattn_kernel, n_pages),
        grid_spec=pltpu.PrefetchScalarGridSpec(
            num_scalar_prefetch=1,
            grid=(dec_b,),
            in_specs=[pl.BlockSpec(memory_space=pl.ANY),
                      pl.BlockSpec(memory_space=pl.ANY),
                      pl.BlockSpec((None, rows, KV_LORA), per_b),
                      pl.BlockSpec((None, rows, QK_ROPE), per_b),
                      pl.BlockSpec((None, n_new, KV_LORA), per_b),
                      pl.BlockSpec((None, QK_ROPE, n_new), per_b)],
            out_specs=pl.BlockSpec((None, rows, KV_LORA), per_b),
            scratch_shapes=[pltpu.VMEM((slots, npg, CHUNK, KV_LORA), F32),
                            pltpu.VMEM((slots, npg, QK_ROPE, CHUNK), F32),
                            pltpu.SemaphoreType.DMA((slots,)),
                            pltpu.SemaphoreType.DMA((slots,))]),
        out_shape=jax.ShapeDtypeStruct((dec_b, rows, KV_LORA), F32),
        compiler_params=_params(("arbitrary",)),
        name="sample_attn",
    )(page_table.reshape(-1), cache_ckv, cache_kr_t, ql, qr, cnew, knew_t)


def _mm_kernel(x_ref, w_ref, o_ref):
    o_ref[...] = jnp.dot(x_ref[...].astype(BF16), w_ref[...], preferred_element_type=F32).astype(o_ref.dtype)


def _matmul(x, w, out_dtype):
    m, n = x.shape[0], w.shape[1]
    full = lambda shape: pl.BlockSpec(shape, lambda i: (0,) * len(shape))
    return pl.pallas_call(
        _mm_kernel,
        grid=(1,),
        in_specs=[full(x.shape), full(w.shape)],
        out_specs=full((m, n)),
        out_shape=jax.ShapeDtypeStruct((m, n), out_dtype),
        compiler_params=_params(("arbitrary",)),
        name="matmul",
    )(x, w)


def _merge_kernel(n_prompt_tiles, xp_ref, xs_ref, sh1_ref, sc1_ref, gt1_ref, sh2_ref, sc2_ref, g1_ref, g2_ref,
                  a_ref, ot_ref, os_ref, wg_ref, woa_ref, wob_ref, wo_ref, x1_ref, *rest):
    h2_refs, yb_ref = rest[:SC_SPLIT], rest[SC_SPLIT]
    i = pl.program_id(0)
    x = jnp.where(i < n_prompt_tiles, xp_ref[...], xs_ref[...])
    h = _rms(x, g1_ref[...]) * (1.0 + sc1_ref[...]) + sh1_ref[...]
    gates = _sigmoid(jnp.dot(h.astype(BF16), wg_ref[...], preferred_element_type=F32))
    y_a = jnp.dot(a_ref[...], woa_ref[...], preferred_element_type=F32)

    @pl.when(i < n_prompt_tiles)
    def _():
        yb_ref[...] = lax.dot_general(ot_ref[...], wob_ref[...], TN_DIMS, preferred_element_type=F32)

    @pl.when(i >= n_prompt_tiles)
    def _():
        yb_ref[...] = jnp.dot(os_ref[...], wob_ref[...], preferred_element_type=F32)

    z = gates[:, :D_MODEL] * y_a + gates[:, D_MODEL:] * yb_ref[...]
    y = jnp.dot(z.astype(BF16), wo_ref[...], preferred_element_type=F32)
    x1 = x + gt1_ref[...] * y
    x1_ref[...] = x1
    for ref, piece in zip(h2_refs, _pack_rows(_rms(x1, g2_ref[...]) * (1.0 + sc2_ref[...]) + sh2_ref[...])):
        ref[...] = piece


def _merge(x_p, x_s, mods, g1, g2, a_all, ot, o_s, wg, woa, wob, wo, grp_of_tile, n_prompt_tiles):
    t_all = x_p.shape[0] + x_s.shape[0]
    tm = ROW_TILE
    seq = ot.shape[2]
    tpb = seq // tm
    row = lambda i: (i, 0)
    fixed2 = lambda i: (0, 0)
    mod = lambda i: (grp_of_tile(i), 0, 0)

    def ot_map(i):
        ic = jnp.minimum(i, n_prompt_tiles - 1)
        return (ic // tpb, 0, ic % tpb)

    os_map = lambda i: (jnp.maximum(i - n_prompt_tiles, 0), 0)
    mod_spec = pl.BlockSpec((None, tm, D_MODEL), mod)
    return pl.pallas_call(
        functools.partial(_merge_kernel, n_prompt_tiles),
        grid=(t_all // tm,),
        in_specs=[pl.BlockSpec((tm, D_MODEL), lambda i: (jnp.minimum(i, n_prompt_tiles - 1), 0)),
                  pl.BlockSpec((tm, D_MODEL), os_map),
                  mod_spec, mod_spec, mod_spec, mod_spec, mod_spec,
                  pl.BlockSpec((1, D_MODEL), fixed2),
                  pl.BlockSpec((1, D_MODEL), fixed2),
                  pl.BlockSpec((tm, A_WIDTH), row),
                  pl.BlockSpec((None, N_HEADS * V_HEAD, tm), ot_map),
                  pl.BlockSpec((tm, N_HEADS * V_HEAD), os_map),
                  pl.BlockSpec(wg.shape, fixed2),
                  pl.BlockSpec(woa.shape, fixed2),
                  pl.BlockSpec(wob.shape, fixed2),
                  pl.BlockSpec(wo.shape, fixed2)],
        out_specs=[pl.BlockSpec((tm, D_MODEL), row)] + [pl.BlockSpec((tm, PIECE), row)] * SC_SPLIT,
        out_shape=[jax.ShapeDtypeStruct((t_all, D_MODEL), F32)]
        + [jax.ShapeDtypeStruct((t_all, PIECE), jnp.int32)] * SC_SPLIT,
        scratch_shapes=[pltpu.VMEM((tm, D_MODEL), F32)],
        compiler_params=_params(("parallel",)),
        name="merge",
    )(x_p, x_s, *mods, g1, g2, a_all, ot, o_s, wg, woa, wob, wo)


def _first_argmax(v, rows):
    mx = jnp.max(v, axis=0, keepdims=True)
    idx = jnp.min(jnp.where(v == mx, rows, v.shape[0]), axis=0, keepdims=True)
    return mx, idx


def _route_kernel(x1_ref, sh2_ref, sc2_ref, g2_ref, whi_ref, wlo_ref, bias_ref, upper_ref,
                  idx_ref, wt_ref, rank_ref, count_ref, run_ref):
    h2 = _rms(x1_ref[...], g2_ref[...]) * (1.0 + sc2_ref[...]) + sh2_ref[...]
    hi = h2.astype(BF16)
    lo = (h2 - hi.astype(F32)).astype(BF16)
    whi = whi_ref[...]
    logits = (lax.dot_general(whi, hi, NT_DIMS, preferred_element_type=F32)
              + lax.dot_general(whi, lo, NT_DIMS, preferred_element_type=F32)
              + lax.dot_general(wlo_ref[...], hi, NT_DIMS, preferred_element_type=F32))
    scores = _sigmoid(logits)
    sel = scores + bias_ref[...]
    tm = sel.shape[1]
    neg = -jnp.inf
    rows_g = lax.broadcasted_iota(jnp.int32, (GROUP_SIZE, tm), 0)
    gscore = []
    for g in range(N_GROUPS):
        blk = sel[g * GROUP_SIZE:(g + 1) * GROUP_SIZE]
        m1, i1 = _first_argmax(blk, rows_g)
        m2 = jnp.max(jnp.where(rows_g == i1, neg, blk), axis=0, keepdims=True)
        gscore.append(m1 + m2)
    gs = jnp.concatenate(gscore, axis=0)
    rows_8 = lax.broadcasted_iota(jnp.int32, gs.shape, 0)
    chosen = jnp.zeros(gs.shape, jnp.int32)
    for _ in range(TOPK_GROUPS):
        _, gi = _first_argmax(gs, rows_8)
        hit = rows_8 == gi
        chosen = jnp.where(hit, 1, chosen)
        gs = jnp.where(hit, neg, gs)
    cand = jnp.concatenate(
        [jnp.where(chosen[g:g + 1] > 0, sel[g * GROUP_SIZE:(g + 1) * GROUP_SIZE], neg) for g in range(N_GROUPS)],
        axis=0)
    rows_e = lax.broadcasted_iota(jnp.int32, cand.shape, 0)
    idxs, wts, hits = [], [], []
    for _ in range(TOP_K):
        _, ei = _first_argmax(cand, rows_e)
        hit = rows_e == ei
        idxs.append(ei)
        hits.append(hit)
        wts.append(jnp.sum(jnp.where(hit, scores, 0.0), axis=0, keepdims=True))
        cand = jnp.where(hit, neg, cand)
    w = jnp.concatenate(wts, axis=0)
    idx_ref[...] = jnp.concatenate(idxs, axis=0)
    wt_ref[...] = w / jnp.sum(w, axis=0, keepdims=True) * ROUTED_SCALE

    @pl.when(pl.program_id(0) == 0)
    def _():
        run_ref[...] = jnp.zeros(run_ref.shape, F32)

    run = run_ref[...]
    ones = jnp.ones((tm, LANES), BF16)
    ranks = []
    for hit in hits:
        oh = jnp.where(hit, 1.0, 0.0).astype(BF16)
        before = jnp.dot(oh, upper_ref[...], preferred_element_type=F32)
        prior = jnp.concatenate([run] * (tm // LANES), axis=1)
        ranks.append(jnp.sum(jnp.where(hit, before + prior, 0.0), axis=0, keepdims=True))
        run = run + jnp.dot(oh, ones, preferred_element_type=F32)
    rank_ref[...] = jnp.concatenate(ranks, axis=0).astype(jnp.int32)
    run_ref[...] = run
    count_ref[...] = run


def _route(x1_all, sh2, sc2, g2, whi, wlo, bias_col, grp_of_tile):
    t_all = x1_all.shape[0]
    tm = ROW_TILE
    fixed2 = lambda i: (0, 0)
    mod = lambda i: (grp_of_tile(i), 0, 0)
    col = lambda i: (0, i)
    upper = (lax.broadcasted_iota(jnp.int32, (tm, tm), 0) < lax.broadcasted_iota(jnp.int32, (tm, tm), 1)).astype(BF16)
    return pl.pallas_call(
        _route_kernel,
        grid=(t_all // tm,),
        in_specs=[pl.BlockSpec((tm, D_MODEL), lambda i: (i, 0)),
                  pl.BlockSpec((None, tm, D_MODEL), mod),
                  pl.BlockSpec((None, tm, D_MODEL), mod),
                  pl.BlockSpec((1, D_MODEL), fixed2),
                  pl.BlockSpec(whi.shape, fixed2),
                  pl.BlockSpec(wlo.shape, fixed2),
                  pl.BlockSpec(bias_col.shape, fixed2),
                  pl.BlockSpec((tm, tm), fixed2)],
        out_specs=[pl.BlockSpec((TOP_K, tm), col), pl.BlockSpec((TOP_K, tm), col), pl.BlockSpec((TOP_K, tm), col),
                   pl.BlockSpec((N_EXPERTS, LANES), fixed2)],
        out_shape=[jax.ShapeDtypeStruct((TOP_K, t_all), jnp.int32),
                   jax.ShapeDtypeStruct((TOP_K, t_all), F32),
                   jax.ShapeDtypeStruct((TOP_K, t_all), jnp.int32),
                   jax.ShapeDtypeStruct((N_EXPERTS, LANES), F32)],
        scratch_shapes=[pltpu.VMEM((N_EXPERTS, LANES), F32)],
        compiler_params=_params(("arbitrary",)),
        name="route",
    )(x1_all, sh2, sc2, g2, whi, wlo, bias_col, upper)


def _moe_kernel(be_ref, nv_ref, next_ref, *refs):
    x_refs = refs[:SC_SPLIT]
    wgu_hbm, wdn_hbm = refs[SC_SPLIT:SC_SPLIT + 2]
    y_refs = refs[SC_SPLIT + 2:2 * SC_SPLIT + 2]
    gu_f32, dn_f32, gu_bf, dn_bf, sem, run_ref = refs[2 * SC_SPLIT + 2:]
    i = pl.program_id(0)
    expert = be_ref[i]
    prev = be_ref[jnp.maximum(i - 1, 0)]

    def weight_copies(e, slot):
        return (pltpu.make_async_copy(wgu_hbm.at[e], gu_f32.at[slot], sem.at[0, slot]),
                pltpu.make_async_copy(wdn_hbm.at[e], dn_f32.at[slot], sem.at[1, slot]))

    @pl.when(i == 0)
    def _():
        run_ref[0] = 0
        for cp in weight_copies(expert, 0):
            cp.start(priority=1)

    @pl.when(i < nv_ref[0])
    def _():
        @pl.when((i == 0) | (expert != prev))
        def _():
            run = run_ref[0] + jnp.where(i == 0, 0, 1)
            run_ref[0] = run
            slot = run % 2
            for cp in weight_copies(expert, slot):
                cp.wait()
            following = next_ref[expert]

            @pl.when(following < N_EXPERTS)
            def _():
                for cp in weight_copies(following, 1 - slot):
                    cp.start(priority=1)

            gu_bf[...] = gu_f32[slot].astype(BF16)
            dn_bf[...] = dn_f32[slot].astype(BF16)

        subs = [slice(s * MOE_SUB, (s + 1) * MOE_SUB) for s in range(MOE_BLOCK // MOE_SUB)]
        gus = [jnp.dot(_unpack_rows([r[rows, :] for r in x_refs]).astype(BF16), gu_bf[...],
                       preferred_element_type=F32) for rows in subs]
        hids = [(_silu(gu[:, :EXPERT_DIM]) * gu[:, EXPERT_DIM:]).astype(BF16) for gu in gus]
        ys = [jnp.dot(hid, dn_bf[...], preferred_element_type=F32) for hid in hids]
        for rows, y in zip(subs, ys):
            for ref, piece in zip(y_refs, _pack_rows(y)):
                ref[rows, :] = piece


def _moe(block_e, n_valid, next_expert, x_sorted, w_gu, w_dn):
    n_slots = x_sorted[0].shape[0]
    blk = MOE_BLOCK
    n_blocks = n_slots // blk
    rows = lambda i, be, nv, nx: (jnp.minimum(i, nv[0] - 1), 0)
    return pl.pallas_call(
        _moe_kernel,
        grid_spec=pltpu.PrefetchScalarGridSpec(
            num_scalar_prefetch=3,
            grid=(n_blocks,),
            in_specs=[pl.BlockSpec((blk, PIECE), rows)] * SC_SPLIT
            + [pl.BlockSpec(memory_space=pl.ANY), pl.BlockSpec(memory_space=pl.ANY)],
            out_specs=[pl.BlockSpec((blk, PIECE), rows)] * SC_SPLIT,
            scratch_shapes=[pltpu.VMEM((2, D_MODEL, 2 * EXPERT_DIM), F32),
                            pltpu.VMEM((2, EXPERT_DIM, D_MODEL), F32),
                            pltpu.VMEM((D_MODEL, 2 * EXPERT_DIM), BF16),
                            pltpu.VMEM((EXPERT_DIM, D_MODEL), BF16),
                            pltpu.SemaphoreType.DMA((2, 2)),
                            pltpu.SMEM((1,), jnp.int32)]),
        out_shape=[jax.ShapeDtypeStruct((n_slots, PIECE), jnp.int32)] * SC_SPLIT,
        compiler_params=_params(("arbitrary",)),
        name="moe",
    )(block_e, n_valid, next_expert, *x_sorted, w_gu, w_dn)


def _dest_kernel(e_ref, rank_ref, start_ref, dest_ref):
    e = e_ref[...]
    rows = lax.broadcasted_iota(jnp.int32, (N_EXPERTS, e.shape[1]), 0)
    start = jnp.sum(jnp.where(rows == e, start_ref[...], 0), axis=0, keepdims=True)
    dest_ref[...] = start + rank_ref[...]


def _dest(e_flat, rank, pad_start_col):
    n_assign = e_flat.shape[1]
    n = RANK_TILE
    tile = pl.BlockSpec((1, n), lambda i: (0, i))
    return pl.pallas_call(
        _dest_kernel,
        grid=(n_assign // n,),
        in_specs=[tile, tile, pl.BlockSpec((N_EXPERTS, 1), lambda i: (0, 0))],
        out_specs=tile,
        out_shape=jax.ShapeDtypeStruct((1, n_assign), jnp.int32),
        compiler_params=_params(("parallel",)),
        name="dest",
    )(e_flat, rank, pad_start_col)


def _sc_mesh():
    return plsc.VectorSubcoreMesh(core_axis_name="core", subcore_axis_name="subcore")


def _sc_scatter_rows(src, dest, n_slots):
    n_src, width = src.shape
    n_assign = dest.shape[1]
    src_blocks = n_src // SC_WINDOW

    @functools.partial(pl.kernel, out_type=jax.ShapeDtypeStruct((n_slots, width), src.dtype),
                       mesh=_sc_mesh(), scratch_types=[])
    def scatter(src_hbm, dest_hbm, out_hbm):
        def body(rows_vmem, dest_vmem):
            pltpu.sync_copy(rows_vmem, out_hbm.at[dest_vmem.at[0]])

        pltpu.emit_pipeline(
            body,
            grid=(n_assign // SC_WINDOW,),
            in_specs=[pl.BlockSpec((SC_WINDOW, width), lambda i: (i % src_blocks, 0)),
                      pl.BlockSpec((1, SC_WINDOW), lambda i: (0, i))],
            out_specs=[],
            core_axis_name=("core", "subcore"),
            dimension_semantics=(pltpu.PARALLEL,),
        )(src_hbm, dest_hbm)

    return scatter(src, dest)


def _sc_gather_rows(table, idx):
    width = table.shape[1]
    n_assign = idx.shape[1]

    @functools.partial(pl.kernel, out_type=jax.ShapeDtypeStruct((n_assign, width), table.dtype),
                       mesh=_sc_mesh(), scratch_types=[])
    def gather(table_hbm, idx_hbm, out_hbm):
        def body(idx_vmem, rows_vmem):
            pltpu.sync_copy(table_hbm.at[idx_vmem.at[0]], rows_vmem)

        pltpu.emit_pipeline(
            body,
            grid=(n_assign // SC_WINDOW,),
            in_specs=[pl.BlockSpec((1, SC_WINDOW), lambda i: (0, i))],
            out_specs=[pl.BlockSpec((SC_WINDOW, width), lambda i: (i, 0))],
            core_axis_name=("core", "subcore"),
            dimension_semantics=(pltpu.PARALLEL,),
        )(idx_hbm, out_hbm)

    return gather(table, idx)


def _final_kernel(n_prompt_tiles, x1_ref, *refs):
    h2_refs = refs[:SC_SPLIT]
    yg_refs = refs[SC_SPLIT:2 * SC_SPLIT]
    wt_ref, gt2_ref, wsg_ref, wsd_ref, gf_ref, op_ref, os_ref = refs[2 * SC_SPLIT:]
    is_prompt = pl.program_id(0) < n_prompt_tiles
    h2 = _unpack_rows([r[...] for r in h2_refs]).astype(BF16)
    gu = jnp.dot(h2, wsg_ref[...], preferred_element_type=F32)
    hid = _silu(gu[:, :SHARED_DIM]) * gu[:, SHARED_DIM:]
    y = jnp.dot(hid.astype(BF16), wsd_ref[...], preferred_element_type=F32)
    wt = wt_ref[...]
    for k in range(TOP_K):
        y = y + wt[:, k:k + 1] * _unpack_rows([r[k] for r in yg_refs])
    x2 = x1_ref[...] + gt2_ref[...] * y
    out = _rms(x2, gf_ref[...])

    @pl.when(is_prompt)
    def _():
        op_ref[...] = out

    @pl.when(jnp.logical_not(is_prompt))
    def _():
        os_ref[...] = out


def _final(x1_all, h2_all, y_gath, wt_rows, gt2, wsg, wsd, gf, grp_of_tile, n_prompt_tiles):
    t_all = x1_all.shape[0]
    tm = ROW_TILE
    t_p = n_prompt_tiles * tm
    row = lambda i: (i, 0)
    fixed2 = lambda i: (0, 0)
    return pl.pallas_call(
        functools.partial(_final_kernel, n_prompt_tiles),
        grid=(t_all // tm,),
        in_specs=[pl.BlockSpec((tm, D_MODEL), row)]
        + [pl.BlockSpec((tm, PIECE), row)] * SC_SPLIT
        + [pl.BlockSpec((TOP_K, tm, PIECE), lambda i: (0, i, 0))] * SC_SPLIT
        + [pl.BlockSpec((tm, TOP_K), row),
           pl.BlockSpec((None, tm, D_MODEL), lambda i: (grp_of_tile(i), 0, 0)),
           pl.BlockSpec(wsg.shape, fixed2),
           pl.BlockSpec(wsd.shape, fixed2),
           pl.BlockSpec((1, D_MODEL), fixed2)],
        out_specs=[pl.BlockSpec((tm, D_MODEL), lambda i: (jnp.minimum(i, n_prompt_tiles - 1), 0)),
                   pl.BlockSpec((tm, D_MODEL), lambda i: (jnp.maximum(i - n_prompt_tiles, 0), 0))],
        out_shape=[jax.ShapeDtypeStruct((t_p, D_MODEL), F32),
                   jax.ShapeDtypeStruct((t_all - t_p, D_MODEL), F32)],
        compiler_params=_params(("arbitrary",)),
        name="final",
    )(x1_all, *h2_all, *y_gath, wt_rows, gt2, wsg, wsd, gf)


def _rope_tables(pos):
    freqs = ROPE_THETA ** (-jnp.arange(ROPE_HALF, dtype=F32) / ROPE_HALF)
    ang = pos.astype(F32)[:, None] * freqs
    return jnp.cos(ang), jnp.sin(ang)


def kernel(x_prompt, x_sample, cache_ckv, cache_krope, page_table, c_prompt, c_sample, w_ada, b_ada, norm1_g, norm2_g, w_in, a_vnorm_g, w_spatial, b_spatial, w_out_a, q_norm_g, w_uq, kv_norm_g, w_uk, w_uv, w_out_b, w_o, w_router, router_bias, w_exp_gu, w_exp_down, w_sh_gu, w_sh_down, final_norm_g):
    batch, seq, _ = x_prompt.shape
    dec_b, dec_s, _ = x_sample.shape
    n_pages = page_table.shape[1]
    n_past = n_pages * CHUNK
    t_p, t_s = batch * seq, dec_b * dec_s
    t_all = t_p + t_s
    tm = ROW_TILE
    assert seq % ATTN_Q_STEP == 0 and t_s % tm == 0 and CHUNK % dec_s == 0 and n_pages % PAGES_PER_STEP == 0
    assert (t_all * TOP_K) % RANK_TILE == 0 and w_ada.shape[0] == 1
    n_prompt_tiles = t_p // tm
    tiles_per_b = seq // tm

    def grp_of_tile(i):
        return jnp.minimum(i // tiles_per_b, batch) + jnp.maximum(i - n_prompt_tiles, 0)

    n_c = batch + dec_b
    c_rows = -(-n_c // 8) * 8
    c_all = jnp.concatenate([c_prompt, c_sample, jnp.zeros((c_rows - n_c, D_MODEL), F32)], axis=0)
    mod = _ada(c_all, w_ada[0].astype(BF16), b_ada[0][None, :])
    mod_p = jnp.broadcast_to(mod[:batch, None, :], (batch, tm, 6 * D_MODEL))
    mod_s = jnp.repeat(mod[batch:n_c], dec_s, axis=0).reshape(t_s // tm, tm, 6 * D_MODEL)
    mod_g = jnp.concatenate([mod_p, mod_s], axis=0)
    sh1, sc1, gt1, sh2, sc2, gt2 = [mod_g[:, :, k * D_MODEL:(k + 1) * D_MODEL] for k in range(6)]

    o_u, o_v, o_cq, o_ckv, o_kr = A_WIDTH, 2 * A_WIDTH, 2 * A_WIDTH + Q_LORA, 2 * A_WIDTH + Q_LORA + KV_LORA, \
        2 * A_WIDTH + Q_LORA + KV_LORA + QK_ROPE
    win = w_in[0]
    w5 = jnp.concatenate([win[:, :o_kr], jnp.zeros((D_MODEL, IN_END - IN_KR - QK_ROPE), F32)], axis=1).astype(BF16)
    wg = win[:, o_kr:].astype(BF16)
    row1 = lambda v: v.reshape(1, -1)
    tri = jnp.tril(jnp.ones((CHUNK, CHUNK), F32))
    wc_p = w_spatial[0] * tri
    per = CHUNK // dec_s
    small = (w_spatial[0] * tri)[:, :dec_s, :dec_s]
    wc_s = jnp.einsum('ab,gts->gatbs', jnp.eye(per, dtype=F32), small).reshape(A_GROUPS, CHUNK, CHUNK)
    wc = jnp.stack([wc_p, wc_s]).astype(BF16)
    bias_p = jnp.repeat(b_spatial[0].T, A_GROUP_DIM, axis=1)
    bias_s = jnp.tile(bias_p[:dec_s], (per, 1))
    bc = jnp.stack([bias_p, bias_s])

    pos_p = jnp.arange(seq)
    pos_s = n_past + jnp.arange(dec_s)
    cos_p, sin_p = _rope_tables(pos_p)
    cos_s, sin_s = _rope_tables(pos_s)
    cos_rows = jnp.concatenate([jnp.tile(cos_p, (batch, 1)), jnp.tile(cos_s, (dec_b, 1))], axis=0)
    sin_rows = jnp.concatenate([jnp.tile(sin_p, (batch, 1)), jnp.tile(sin_s, (dec_b, 1))], axis=0)
    lane_pad = jnp.zeros((t_all, LANES - QK_ROPE), F32)
    cos_kr = jnp.concatenate([cos_rows, cos_rows, lane_pad], axis=1)
    sin_kr = jnp.concatenate([-sin_rows, sin_rows, lane_pad], axis=1)

    x_p, x_s = x_prompt.reshape(t_p, D_MODEL), x_sample.reshape(t_s, D_MODEL)
    g1, g2 = row1(norm1_g[0]), row1(norm2_g[0])
    a_all, cq_all, ckv_p, kr_p, ckv_s, kr_s, v_s = _in_proj(
        x_p, x_s, sh1, sc1, g1, w5, row1(a_vnorm_g[0]), row1(q_norm_g[0]), row1(kv_norm_g[0]),
        cos_kr, sin_kr, wc, bc, grp_of_tile, n_prompt_tiles)

    wuq = w_uq[0]
    wuqt = jnp.pad(wuq, ((0, 0), (0, 0), (0, HEAD_PAD - QK_NOPE - QK_ROPE))).reshape(Q_LORA, -1).T.astype(BF16)
    wukp = jnp.pad(w_uk[0], ((0, 0), (0, 0), (0, HEAD_PAD - QK_NOPE))).reshape(KV_LORA, -1).astype(BF16)
    place_h = jnp.pad(jnp.eye(QK_ROPE, dtype=F32), ((0, 0), (QK_NOPE, HEAD_PAD - QK_NOPE - QK_ROPE)))
    place = jnp.tile(place_h, (1, N_HEADS)).astype(BF16)
    wuvt = w_uv[0].reshape(KV_LORA, -1).T.astype(BF16)
    qt, k, vt = _qkv(cq_all, ckv_p, kr_p, wuqt, wukp, place, wuvt, cos_p.T, sin_p.T, batch, seq)
    ot = _attn(qt, k, vt)

    wq_s = jnp.concatenate([wuq[:, :, :QK_NOPE].reshape(Q_LORA, -1),
                            wuq[:, :, QK_NOPE:QK_NOPE + ROPE_HALF].reshape(Q_LORA, -1),
                            wuq[:, :, QK_NOPE + ROPE_HALF:].reshape(Q_LORA, -1)], axis=1).astype(BF16)
    eye_h = jnp.eye(N_HEADS, dtype=F32)
    wuk_blk = jnp.einsum('rhd,hg->hdgr', w_uk[0], eye_h).reshape(N_HEADS * QK_NOPE, N_HEADS * KV_LORA).astype(BF16)
    wuv_blk = jnp.einsum('rhd,hg->hrgd', w_uv[0], eye_h).reshape(N_HEADS * KV_LORA, N_HEADS * V_HEAD).astype(BF16)
    cos_sq = jnp.tile(jnp.tile(cos_s, (1, N_HEADS)), (dec_b, 1))
    sin_sq = jnp.tile(jnp.tile(sin_s, (1, N_HEADS)), (dec_b, 1))
    ql, r1, r2 = _sample_q(cq_all[t_p:], wq_s, wuk_blk, cos_sq, sin_sq)
    rows = dec_s * N_HEADS
    ql3 = ql.reshape(dec_b, rows, KV_LORA)
    qr3 = jnp.concatenate([r1.reshape(dec_b, dec_s, N_HEADS, ROPE_HALF),
                           r2.reshape(dec_b, dec_s, N_HEADS, ROPE_HALF)], axis=-1).reshape(dec_b, rows, QK_ROPE)
    new_pad = ((0, 0), (0, 8 - dec_s), (0, 0))
    ckv_new = jnp.pad(ckv_s.reshape(dec_b, dec_s, KV_LORA), new_pad)
    kr_new_t = jnp.swapaxes(jnp.pad(kr_s.reshape(dec_b, dec_s, QK_ROPE), new_pad), 1, 2)
    cache_kr_t = jnp.swapaxes(cache_krope[0], 1, 2)
    o_lat = _sample_attn(page_table, cache_ckv[0], cache_kr_t, ql3, qr3, ckv_new, kr_new_t)
    o_s = _matmul(o_lat.reshape(t_s, N_HEADS * KV_LORA), wuv_blk, BF16)

    x1_all, *h2_all = _merge(x_p, x_s, (sh1, sc1, gt1, sh2, sc2), g1, g2, a_all, ot, o_s, wg,
                            w_out_a[0].astype(BF16), w_out_b[0].astype(BF16), w_o[0].astype(BF16),
                            grp_of_tile, n_prompt_tiles)

    wr_t = w_router[0].T
    whi = wr_t.astype(BF16)
    wlo = (wr_t - whi.astype(F32)).astype(BF16)
    idx_t, wt_t, rank_t, counts = _route(x1_all, sh2, sc2, g2, whi, wlo, router_bias[0].reshape(-1, 1),
                                         grp_of_tile)
    n_assign = t_all * TOP_K
    e_flat = idx_t.reshape(1, n_assign)
    counts = counts[:, 0].astype(jnp.int32)
    blk = MOE_BLOCK
    padded = (counts + blk - 1) // blk * blk
    pad_end = jnp.cumsum(padded)
    pad_start = pad_end - padded
    dest = _dest(e_flat, rank_t.reshape(1, n_assign), pad_start.reshape(-1, 1))
    n_blocks = -(-n_assign // blk) + N_EXPERTS
    n_slots = n_blocks * blk
    first_row = jnp.arange(n_blocks, dtype=jnp.int32) * blk
    block_e = jnp.minimum(jnp.sum(pad_end[None, :] <= first_row[:, None], axis=1), N_EXPERTS - 1).astype(jnp.int32)
    n_valid = (pad_end[-1] // blk).astype(jnp.int32).reshape(1)
    experts = jnp.arange(N_EXPERTS, dtype=jnp.int32)
    owners = jnp.where(padded > 0, experts, N_EXPERTS)
    later_owner = lax.cummin(owners, reverse=True)
    next_expert = jnp.concatenate([later_owner[1:], jnp.full((1,), N_EXPERTS, jnp.int32)])
    x_sorted = [_sc_scatter_rows(h, dest, n_slots) for h in h2_all]
    y_sorted = _moe(block_e, n_valid, next_expert, x_sorted, w_exp_gu[0], w_exp_down[0])
    y_gath = [_sc_gather_rows(y, dest).reshape(TOP_K, t_all, PIECE) for y in y_sorted]

    y_p, y_s = _final(x1_all, h2_all, y_gath, wt_t.T, gt2, w_sh_gu[0].astype(BF16), w_sh_down[0].astype(BF16),
                      row1(final_norm_g), grp_of_tile, n_prompt_tiles)

    y_prompt = y_p.reshape(batch, seq, D_MODEL)
    y_sample = y_s.reshape(dec_b, dec_s, D_MODEL)
    new_ckv_prompt = ckv_p.reshape(1, batch, seq, KV_LORA)
    new_krope_prompt = kr_p.reshape(1, batch, seq, QK_ROPE)
    new_ckv_sample = ckv_s.reshape(1, dec_b, dec_s, KV_LORA)
    new_krope_sample = kr_s.reshape(1, dec_b, dec_s, QK_ROPE)
    new_chunk_v_sample = v_s.reshape(1, dec_b, dec_s, A_WIDTH)
    return (y_prompt, y_sample, new_ckv_prompt, new_krope_prompt, new_ckv_sample, new_krope_sample,
            new_chunk_v_sample)
```

```python
import functools

import jax
import jax.numpy as jnp
from jax import lax
from jax.experimental import pallas as pl
from jax.experimental.pallas import tpu as pltpu
from jax.experimental.pallas import tpu_sc as plsc

F32 = jnp.float32
BF16 = jnp.bfloat16

D_MODEL = 1024
A_WIDTH = D_MODEL // 2
A_GROUPS = 8
A_GROUP_DIM = A_WIDTH // A_GROUPS
CHUNK = 128
N_HEADS = 8
QK_NOPE = 64
QK_ROPE = 32
ROPE_HALF = QK_ROPE // 2
V_HEAD = 64
Q_LORA = 384
KV_LORA = 256
ROPE_THETA = 10000.0
ATTN_SCALE = (QK_NOPE + QK_ROPE) ** -0.5
N_EXPERTS = 256
TOP_K = 8
N_GROUPS = 8
GROUP_SIZE = N_EXPERTS // N_GROUPS
TOPK_GROUPS = 4
EXPERT_DIM = 256
SHARED_DIM = 256
ROUTED_SCALE = 2.5
EPS = 1e-6

LANES = 128
HEAD_PAD = 128
ROW_TILE = 256
ATTN_KV_TILE = 256
ATTN_Q_STEP = 1024
ATTN_HEADS_PER_STEP = 2
LOG2_E = 1.4426950408889634
RANK_TILE = 512
MOE_BLOCK = 256
MOE_SUB = 128
PAGES_PER_STEP = 64
PAGE_GROUP = 8
V_ROWS = V_HEAD + 16
SC_WINDOW = 128
SC_SPLIT = 2
PIECE = D_MODEL // 2 // SC_SPLIT
VMEM_LIMIT = 48 * 1024 * 1024

IN_U, IN_V, IN_CQ, IN_CKV, IN_KR, IN_END = 0, 512, 1024, 1408, 1664, 1792

NT_DIMS = (((1,), (1,)), ((), ()))
TN_DIMS = (((0,), (0,)), ((), ()))


def _params(sem, vmem=VMEM_LIMIT):
    return pltpu.CompilerParams(dimension_semantics=sem, vmem_limit_bytes=vmem)


def _rms(x, g):
    return x * lax.rsqrt(jnp.mean(x * x, axis=-1, keepdims=True) + EPS) * g


def _gelu(x):
    return 0.5 * x * (1.0 + jnp.tanh(0.7978845608028654 * (x + 0.044715 * (x * x * x))))


def _sigmoid(x):
    return 1.0 / (1.0 + jnp.exp(-x))


def _silu(x):
    return x * _sigmoid(x)


def _pack_rows(x):
    pieces = []
    for c in range(SC_SPLIT):
        lo = x[:, 2 * c * PIECE:(2 * c + 1) * PIECE].astype(BF16).astype(F32)
        hi = x[:, (2 * c + 1) * PIECE:(2 * c + 2) * PIECE].astype(BF16).astype(F32)
        pieces.append(lax.bitcast_convert_type(hi, jnp.int32)
                      | lax.shift_right_logical(lax.bitcast_convert_type(lo, jnp.int32), 16))
    return pieces


def _unpack_rows(pieces):
    cols = []
    for p in pieces:
        cols.append(lax.bitcast_convert_type(lax.shift_left(p, 16), F32))
        cols.append(lax.bitcast_convert_type(p & jnp.int32(-65536), F32))
    return jnp.concatenate(cols, axis=1)


def _ada_kernel(c_ref, w_ref, b_ref, o_ref):
    s = _silu(c_ref[...]).astype(BF16)
    o_ref[...] = jnp.dot(s, w_ref[...], preferred_element_type=F32) + b_ref[...]


def _ada(c, w, b):
    rows, n = c.shape[0], w.shape[1]
    tn = 1536
    return pl.pallas_call(
        _ada_kernel,
        grid=(n // tn,),
        in_specs=[pl.BlockSpec((rows, D_MODEL), lambda j: (0, 0)),
                  pl.BlockSpec((D_MODEL, tn), lambda j: (0, j)),
                  pl.BlockSpec((1, tn), lambda j: (0, j))],
        out_specs=pl.BlockSpec((rows, tn), lambda j: (0, j)),
        out_shape=jax.ShapeDtypeStruct((rows, n), F32),
        compiler_params=_params(("parallel",)),
        name="ada",
    )(c, w, b)


def _in_kernel(n_prompt_tiles, xp_ref, xs_ref, sh_ref, sc_ref, g1_ref, w_ref, vg_ref, qg_ref, kg_ref,
               cos_ref, sin_ref, wc_ref, bc_ref,
               a_ref, cq_ref, ckvp_ref, krp_ref, ckvs_ref, krs_ref, vs_ref):
    tm = xp_ref.shape[0]
    is_prompt = pl.program_id(0) < n_prompt_tiles
    x = jnp.where(is_prompt, xp_ref[...], xs_ref[...])
    h = _rms(x, g1_ref[...]) * (1.0 + sc_ref[...]) + sh_ref[...]
    z = jnp.dot(h.astype(BF16), w_ref[...], preferred_element_type=F32)
    u = _gelu(z[:, IN_U:IN_V])
    v = _rms(_gelu(z[:, IN_V:IN_CQ]), vg_ref[...])
    cq_ref[...] = _rms(z[:, IN_CQ:IN_CKV], qg_ref[...]).astype(BF16)
    ckv = _rms(z[:, IN_CKV:IN_KR], kg_ref[...])
    kr = z[:, IN_KR:IN_END]
    lane = lax.broadcasted_iota(jnp.int32, kr.shape, 1)
    swapped = jnp.where(lane < ROPE_HALF, pltpu.roll(kr, LANES - ROPE_HALF, 1), pltpu.roll(kr, ROPE_HALF, 1))
    kr_rot = (kr * cos_ref[...] + swapped * sin_ref[...])[:, :QK_ROPE]

    @pl.when(is_prompt)
    def _():
        ckvp_ref[...] = ckv
        krp_ref[...] = kr_rot

    @pl.when(jnp.logical_not(is_prompt))
    def _():
        ckvs_ref[...] = ckv
        krs_ref[...] = kr_rot
        vs_ref[...] = v

    vb = v.astype(BF16)
    half = A_WIDTH // 2
    grp_of_lane = lax.broadcasted_iota(jnp.int32, (CHUNK, half), 1) // A_GROUP_DIM
    for ci in range(tm // CHUNK):
        rows = slice(ci * CHUNK, (ci + 1) * CHUNK)
        parts = []
        for q in range(2):
            vq = vb[rows, q * half:(q + 1) * half]
            acc = None
            for gg in range(A_GROUPS // 2):
                vm = jnp.where(grp_of_lane == gg, vq, jnp.zeros_like(vq))
                part = jnp.dot(wc_ref[q * (A_GROUPS // 2) + gg], vm, preferred_element_type=F32)
                acc = part if acc is None else acc + part
            parts.append(acc)
        sp = jnp.concatenate(parts, axis=1) + bc_ref[...]
        a_ref[rows, :] = (u[rows, :] * sp).astype(BF16)


MOD_SHIFT1, MOD_SCALE1, MOD_GATE1, MOD_SHIFT2, MOD_SCALE2, MOD_GATE2 = range(6)


def _mod_spec(grp_of_tile, component):
    return pl.BlockSpec((None, ROW_TILE, D_MODEL), lambda i: (grp_of_tile(i), 0, component))


def _in_proj(x_p, x_s, mod_g, g1, w5, vg, qg, kg, cos_kr, sin_kr, wc, bc, grp_of_tile, n_prompt_tiles):
    t_p, t_s = x_p.shape[0], x_s.shape[0]
    t_all = t_p + t_s
    tm = ROW_TILE
    n_tiles = t_all // tm
    row = lambda i: (i, 0)
    prow = lambda i: (jnp.minimum(i, n_prompt_tiles - 1), 0)
    srow = lambda i: (jnp.maximum(i - n_prompt_tiles, 0), 0)
    fixed2 = lambda i: (0, 0)
    kind = lambda i: (jnp.where(i < n_prompt_tiles, 0, 1), 0, 0)
    kind4 = lambda i: (jnp.where(i < n_prompt_tiles, 0, 1), 0, 0, 0)
    return pl.pallas_call(
        functools.partial(_in_kernel, n_prompt_tiles),
        grid=(n_tiles,),
        in_specs=[pl.BlockSpec((tm, D_MODEL), prow),
                  pl.BlockSpec((tm, D_MODEL), srow),
                  _mod_spec(grp_of_tile, MOD_SHIFT1),
                  _mod_spec(grp_of_tile, MOD_SCALE1),
                  pl.BlockSpec((1, D_MODEL), fixed2),
                  pl.BlockSpec((D_MODEL, IN_END), fixed2),
                  pl.BlockSpec((1, A_WIDTH), fixed2),
                  pl.BlockSpec((1, Q_LORA), fixed2),
                  pl.BlockSpec((1, KV_LORA), fixed2),
                  pl.BlockSpec((tm, LANES), row),
                  pl.BlockSpec((tm, LANES), row),
                  pl.BlockSpec((None, A_GROUPS, CHUNK, CHUNK), kind4),
                  pl.BlockSpec((None, CHUNK, A_WIDTH), kind)],
        out_specs=[pl.BlockSpec((tm, A_WIDTH), row),
                   pl.BlockSpec((tm, Q_LORA), row),
                   pl.BlockSpec((tm, KV_LORA), prow),
                   pl.BlockSpec((tm, QK_ROPE), prow),
                   pl.BlockSpec((tm, KV_LORA), srow),
                   pl.BlockSpec((tm, QK_ROPE), srow),
                   pl.BlockSpec((tm, A_WIDTH), srow)],
        out_shape=[jax.ShapeDtypeStruct((t_all, A_WIDTH), BF16),
                   jax.ShapeDtypeStruct((t_all, Q_LORA), BF16),
                   jax.ShapeDtypeStruct((t_p, KV_LORA), F32),
                   jax.ShapeDtypeStruct((t_p, QK_ROPE), F32),
                   jax.ShapeDtypeStruct((t_s, KV_LORA), F32),
                   jax.ShapeDtypeStruct((t_s, QK_ROPE), F32),
                   jax.ShapeDtypeStruct((t_s, A_WIDTH), F32)],
        compiler_params=_params(("arbitrary",)),
        name="in_proj",
    )(x_p, x_s, mod_g, mod_g, g1, w5, vg, qg, kg, cos_kr, sin_kr, wc, bc)


def _qkv_kernel(cq_ref, ckv_ref, kr_ref, wuqt_ref, wukp_ref, place_ref, wuvt_ref, cos_ref, sin_ref,
                qt_ref, k_ref, vt_ref):
    tm = cq_ref.shape[0]
    ckv = ckv_ref[...].astype(BF16)
    kr = kr_ref[...].astype(BF16)
    qt = lax.dot_general(wuqt_ref[...], cq_ref[...], NT_DIMS, preferred_element_type=F32)
    c = cos_ref[...]
    s = sin_ref[...]
    pad = jnp.zeros((HEAD_PAD - QK_NOPE - QK_ROPE, tm), F32)
    for h in range(N_HEADS):
        blk = qt[h * HEAD_PAD:(h + 1) * HEAD_PAD]
        x1 = blk[QK_NOPE:QK_NOPE + ROPE_HALF]
        x2 = blk[QK_NOPE + ROPE_HALF:QK_NOPE + QK_ROPE]
        full = jnp.concatenate([blk[:QK_NOPE], x1 * c - x2 * s, x1 * s + x2 * c, pad], axis=0)
        qt_ref[h] = (full * (ATTN_SCALE * LOG2_E)).astype(BF16)
    k = (jnp.dot(ckv, wukp_ref[...], preferred_element_type=F32)
         + jnp.dot(kr, place_ref[...], preferred_element_type=F32))
    for h in range(N_HEADS):
        k_ref[h] = k[:, h * HEAD_PAD:(h + 1) * HEAD_PAD].astype(BF16)
    vt = lax.dot_general(wuvt_ref[...], ckv, NT_DIMS, preferred_element_type=F32)
    extra = V_ROWS - V_HEAD
    ones_row = jnp.where(lax.broadcasted_iota(jnp.int32, (extra, tm), 0) == 0, 1.0, 0.0)
    for h in range(N_HEADS):
        vt_ref[h] = jnp.concatenate([vt[h * V_HEAD:(h + 1) * V_HEAD], ones_row], axis=0).astype(BF16)


def _qkv(cq_all, ckv_all, kr_all, wuqt, wukp, place, wuvt, cos_t, sin_t, batch, seq):
    tm = ATTN_KV_TILE
    nk = seq // tm
    row = lambda b, j: (b * nk + j, 0)
    fixed2 = lambda b, j: (0, 0)
    tab = lambda b, j: (0, j)
    return pl.pallas_call(
        _qkv_kernel,
        grid=(batch, nk),
        in_specs=[pl.BlockSpec((tm, Q_LORA), row),
                  pl.BlockSpec((tm, KV_LORA), row),
                  pl.BlockSpec((tm, QK_ROPE), row),
                  pl.BlockSpec(wuqt.shape, fixed2),
                  pl.BlockSpec(wukp.shape, fixed2),
                  pl.BlockSpec(place.shape, fixed2),
                  pl.BlockSpec(wuvt.shape, fixed2),
                  pl.BlockSpec((ROPE_HALF, tm), tab),
                  pl.BlockSpec((ROPE_HALF, tm), tab)],
        out_specs=[pl.BlockSpec((None, N_HEADS, HEAD_PAD, tm), lambda b, j: (b, 0, 0, j)),
                   pl.BlockSpec((None, N_HEADS, tm, HEAD_PAD), lambda b, j: (b, 0, j, 0)),
                   pl.BlockSpec((None, N_HEADS, None, V_ROWS, tm), lambda b, j: (b, 0, j, 0, 0))],
        out_shape=[jax.ShapeDtypeStruct((batch, N_HEADS, HEAD_PAD, seq), BF16),
                   jax.ShapeDtypeStruct((batch, N_HEADS, seq, HEAD_PAD), BF16),
                   jax.ShapeDtypeStruct((batch, N_HEADS, nk, V_ROWS, tm), BF16)],
        compiler_params=_params(("parallel", "parallel")),
        name="qkv",
    )(cq_all, ckv_all, kr_all, wuqt, wukp, place, wuvt, cos_t, sin_t)


def _attn_kernel(qt_ref, k_ref, vt_ref, o_ref):
    tk = ATTN_KV_TILE
    n_heads = qt_ref.shape[0]
    n_sub = qt_ref.shape[2] // tk
    qi = pl.program_id(2)
    j0 = qi * n_sub
    chains = [(h, sub) for h in range(n_heads) for sub in range(n_sub)]

    def tiles(j, active, carries, diagonal_sub):
        scores = [jnp.dot(k_ref[h, j], qt_ref[h, :, sub * tk:(sub + 1) * tk], preferred_element_type=F32)
                  for h, sub in active]
        stats = []
        for (h, sub), s in zip(active, scores):
            m, _ = carries[h * n_sub + sub]
            if sub == diagonal_sub:
                key = lax.broadcasted_iota(jnp.int32, s.shape, 0)
                qry = lax.broadcasted_iota(jnp.int32, s.shape, 1)
                s = jnp.where(key <= qry, s, -jnp.inf)
            m_new = jnp.maximum(m, jnp.max(s, axis=0, keepdims=True))
            stats.append((m_new, jnp.exp2(m - m_new), jnp.exp2(s - m_new).astype(BF16)))
        out = list(carries)
        for (h, sub), (m_new, alpha, p) in zip(active, stats):
            c = h * n_sub + sub
            out[c] = (m_new, alpha * carries[c][1] + jnp.dot(vt_ref[h, j], p, preferred_element_type=F32))
        return out

    init = tuple((jnp.full((1, tk), -jnp.inf, F32), jnp.zeros((V_ROWS, tk), F32)) for _ in chains)
    carries = list(lax.fori_loop(0, j0, lambda j, c: tuple(tiles(j, chains, c, None)), init))
    for jj in range(n_sub):
        carries = tiles(j0 + jj, [(h, sub) for h, sub in chains if sub >= jj], carries, jj)
    for h, sub in chains:
        _, acc = carries[h * n_sub + sub]
        o_ref[h * V_HEAD:(h + 1) * V_HEAD, sub * tk:(sub + 1) * tk] = (
            acc[:V_HEAD] / acc[V_HEAD:V_HEAD + 1]).astype(BF16)


def _attn(qt, k, vt):
    batch, _, _, seq = qt.shape
    tk = ATTN_KV_TILE
    nk = seq // tk
    hp = ATTN_HEADS_PER_STEP
    k5 = k.reshape(batch, N_HEADS, nk, tk, HEAD_PAD)
    return pl.pallas_call(
        _attn_kernel,
        grid=(batch, N_HEADS // hp, seq // ATTN_Q_STEP),
        in_specs=[pl.BlockSpec((None, hp, HEAD_PAD, ATTN_Q_STEP), lambda b, h, i: (b, h, 0, i)),
                  pl.BlockSpec((None, hp, nk, tk, HEAD_PAD), lambda b, h, i: (b, h, 0, 0, 0)),
                  pl.BlockSpec((None, hp, nk, V_ROWS, tk), lambda b, h, i: (b, h, 0, 0, 0))],
        out_specs=pl.BlockSpec((None, hp * V_HEAD, ATTN_Q_STEP), lambda b, h, i: (b, h, i)),
        out_shape=jax.ShapeDtypeStruct((batch, N_HEADS * V_HEAD, seq), BF16),
        compiler_params=_params(("parallel", "parallel", "parallel")),
        name="attn",
    )(qt, k5, vt)


def _sq_kernel(cq_ref, wq_ref, wukb_ref, cos_ref, sin_ref, ql_ref, r1_ref, r2_ref):
    n_nope = N_HEADS * QK_NOPE
    q = jnp.dot(cq_ref[...], wq_ref[...], preferred_element_type=F32)
    x1 = q[:, n_nope:n_nope + LANES]
    x2 = q[:, n_nope + LANES:]
    c = cos_ref[...]
    s = sin_ref[...]
    r1_ref[...] = ((x1 * c - x2 * s) * ATTN_SCALE).astype(BF16)
    r2_ref[...] = ((x1 * s + x2 * c) * ATTN_SCALE).astype(BF16)
    ql = jnp.dot(q[:, :n_nope].astype(BF16), wukb_ref[...], preferred_element_type=F32)
    ql_ref[...] = (ql * ATTN_SCALE).astype(BF16)


def _sample_q(cq_s, wq_s, wuk_blk, cos_s, sin_s):
    ts = cq_s.shape[0]
    full = lambda shape: pl.BlockSpec(shape, lambda i: (0,) * len(shape))
    return pl.pallas_call(
        _sq_kernel,
        grid=(1,),
        in_specs=[full(cq_s.shape), full(wq_s.shape), full(wuk_blk.shape), full(cos_s.shape), full(sin_s.shape)],
        out_specs=[full((ts, N_HEADS * KV_LORA)), full((ts, LANES)), full((ts, LANES))],
        out_shape=[jax.ShapeDtypeStruct((ts, N_HEADS * KV_LORA), BF16),
                   jax.ShapeDtypeStruct((ts, LANES), BF16),
                   jax.ShapeDtypeStruct((ts, LANES), BF16)],
        compiler_params=_params(("arbitrary",)),
        name="sample_q",
    )(cq_s, wq_s, wuk_blk, cos_s, sin_s)


def _sattn_kernel(n_pages, pt_ref, ckv_hbm, kr_hbm, ql_ref, qr_ref, cnew_ref, knew_ref, o_ref,
                  cbuf, rbuf, sem_c, sem_r):
    npg = PAGES_PER_STEP
    n_chunks = n_pages // npg
    PAGE_SLOTS = cbuf.shape[0]
    PAGE_LOOKAHEAD = PAGE_SLOTS // 2
    b = pl.program_id(0)

    def page_copies(bb, chunk, slot, p):
        page = pt_ref[bb * n_pages + chunk * npg + p]
        return (pltpu.make_async_copy(ckv_hbm.at[page], cbuf.at[slot, p], sem_c.at[slot]),
                pltpu.make_async_copy(kr_hbm.at[page], rbuf.at[slot, p], sem_r.at[slot]))

    def start_chunk(bb, chunk, slot):
        for p in range(npg):
            for cp in page_copies(bb, chunk, slot, p):
                cp.start()

    def wait_chunk(bb, chunk, slot):
        for p in range(npg):
            for cp in page_copies(bb, chunk, slot, p):
                cp.wait()

    @pl.when(b == 0)
    def _():
        for chunk in range(PAGE_LOOKAHEAD):
            start_chunk(0, chunk, chunk)

    ql = ql_ref[...]
    qr = qr_ref[...]

    def partial_softmax(blocks):
        scores = [lax.dot_general(ql, keys, NT_DIMS, preferred_element_type=F32)
                  + jnp.dot(qr, rope_t, preferred_element_type=F32) for keys, rope_t, _ in blocks]
        probs = []
        for s, (_, _, mask) in zip(scores, blocks):
            if mask is not None:
                s = jnp.where(mask(s.shape), s, -jnp.inf)
            m = jnp.max(s, axis=-1, keepdims=True)
            p = jnp.exp(s - m)
            probs.append((m, jnp.sum(p, axis=-1, keepdims=True), p.astype(BF16)))
        return [(m, l, jnp.dot(p, blk[0], preferred_element_type=F32)) for (m, l, p), blk in zip(probs, blocks)]

    def merge(state, parts):
        m_old, l_old, acc_old = state
        m_new = m_old
        for m, _, _ in parts:
            m_new = jnp.maximum(m_new, m)
        alpha = jnp.exp(m_old - m_new)
        l = alpha * l_old
        acc = alpha * acc_old
        for m, lp, op in parts:
            w = jnp.exp(m - m_new)
            l = l + w * lp
            acc = acc + w * op
        return m_new, l, acc

    def chunk_step(chunk, state, extra_blocks=()):
        slot = chunk % PAGE_SLOTS
        wait_chunk(b, chunk, slot)
        ahead = chunk + PAGE_LOOKAHEAD
        ahead_b = jnp.where(ahead >= n_chunks, b + 1, b)
        ahead_b = jnp.where(ahead_b >= pl.num_programs(0), 0, ahead_b)
        start_chunk(ahead_b, ahead % n_chunks, ahead % PAGE_SLOTS)

        blocks = []
        for g in range(npg // PAGE_GROUP):
            pages = range(g * PAGE_GROUP, (g + 1) * PAGE_GROUP)
            blocks.append((jnp.concatenate([cbuf[slot, p].astype(BF16) for p in pages], axis=0),
                           jnp.concatenate([rbuf[slot, p].astype(BF16) for p in pages], axis=1), None))
        return merge(state, partial_softmax(blocks + list(extra_blocks)))

    def causal(shape):
        q_pos = lax.broadcasted_iota(jnp.int32, shape, 0) // N_HEADS
        return lax.broadcasted_iota(jnp.int32, shape, 1) <= q_pos

    rows = ql.shape[0]
    state = (jnp.full((rows, 1), -jnp.inf, F32), jnp.zeros((rows, 1), F32), jnp.zeros((rows, KV_LORA), F32))
    state = lax.fori_loop(0, n_chunks - 1, chunk_step, state)
    new_rows = (cnew_ref[...].astype(BF16), knew_ref[...].astype(BF16), causal)
    _, l, acc = chunk_step(n_chunks - 1, state, [new_rows])
    o_ref[...] = acc / l

    @pl.when(b == pl.num_programs(0) - 1)
    def _():
        for chunk in range(PAGE_LOOKAHEAD):
            wait_chunk(0, chunk, chunk % PAGE_SLOTS)


def _sample_attn(page_table, cache_ckv, cache_kr_t, ql, qr, cnew, knew_t):
    dec_b, n_pages = page_table.shape
    npg = PAGES_PER_STEP
    n_chunks = n_pages // npg
    slots = 4 if n_chunks % 4 == 0 else 2
    assert n_chunks % slots == 0
    rows = ql.shape[1]
    n_new = cnew.shape[1]
    per_b = lambda b, pt: (b, 0, 0)
    return pl.pallas_call(
        functools.partial(_sattn_kernel, n_pages),
        grid_spec=pltpu.PrefetchScalarGridSpec(
            num_scalar_prefetch=1,
            grid=(dec_b,),
            in_specs=[pl.BlockSpec(memory_space=pl.ANY),
                      pl.BlockSpec(memory_space=pl.ANY),
                      pl.BlockSpec((None, rows, KV_LORA), per_b),
                      pl.BlockSpec((None, rows, QK_ROPE), per_b),
                      pl.BlockSpec((None, n_new, KV_LORA), per_b),
                      pl.BlockSpec((None, QK_ROPE, n_new), per_b)],
            out_specs=pl.BlockSpec((None, rows, KV_LORA), per_b),
            scratch_shapes=[pltpu.VMEM((slots, npg, CHUNK, KV_LORA), F32),
                            pltpu.VMEM((slots, npg, QK_ROPE, CHUNK), F32),
                            pltpu.SemaphoreType.DMA((slots,)),
                            pltpu.SemaphoreType.DMA((slots,))]),
        out_shape=jax.ShapeDtypeStruct((dec_b, rows, KV_LORA), F32),
        compiler_params=_params(("arbitrary",)),
        name="sample_attn",
    )(page_table.reshape(-1), cache_ckv, cache_kr_t, ql, qr, cnew, knew_t)


def _mm_kernel(x_ref, w_ref, o_ref):
    o_ref[...] = jnp.dot(x_ref[...].astype(BF16), w_ref[...], preferred_element_type=F32).astype(o_ref.dtype)


def _matmul(x, w, out_dtype):
    m, n = x.shape[0], w.shape[1]
    full = lambda shape: pl.BlockSpec(shape, lambda i: (0,) * len(shape))
    return pl.pallas_call(
        _mm_kernel,
        grid=(1,),
        in_specs=[full(x.shape), full(w.shape)],
        out_specs=full((m, n)),
        out_shape=jax.ShapeDtypeStruct((m, n), out_dtype),
        compiler_params=_params(("arbitrary",)),
        name="matmul",
    )(x, w)


def _merge_kernel(n_prompt_tiles, xp_ref, xs_ref, sh1_ref, sc1_ref, gt1_ref, sh2_ref, sc2_ref, g1_ref, g2_ref,
                  a_ref, ot_ref, os_ref, wg_ref, woa_ref, wob_ref, wo_ref, x1_ref, *rest):
    h2_refs, yb_ref = rest[:SC_SPLIT], rest[SC_SPLIT]
    i = pl.program_id(0)
    x = jnp.where(i < n_prompt_tiles, xp_ref[...], xs_ref[...])
    h = _rms(x, g1_ref[...]) * (1.0 + sc1_ref[...]) + sh1_ref[...]
    gates = _sigmoid(jnp.dot(h.astype(BF16), wg_ref[...], preferred_element_type=F32))
    y_a = jnp.dot(a_ref[...], woa_ref[...], preferred_element_type=F32)

    @pl.when(i < n_prompt_tiles)
    def _():
        yb_ref[...] = lax.dot_general(ot_ref[...], wob_ref[...], TN_DIMS, preferred_element_type=F32)

    @pl.when(i >= n_prompt_tiles)
    def _():
        yb_ref[...] = jnp.dot(os_ref[...], wob_ref[...], preferred_element_type=F32)

    z = gates[:, :D_MODEL] * y_a + gates[:, D_MODEL:] * yb_ref[...]
    y = jnp.dot(z.astype(BF16), wo_ref[...], preferred_element_type=F32)
    x1 = x + gt1_ref[...] * y
    x1_ref[...] = x1
    for ref, piece in zip(h2_refs, _pack_rows(_rms(x1, g2_ref[...]) * (1.0 + sc2_ref[...]) + sh2_ref[...])):
        ref[...] = piece


def _merge(x_p, x_s, mod_g, g1, g2, a_all, ot, o_s, wg, woa, wob, wo, grp_of_tile, n_prompt_tiles):
    t_all = x_p.shape[0] + x_s.shape[0]
    tm = ROW_TILE
    seq = ot.shape[2]
    tpb = seq // tm
    row = lambda i: (i, 0)
    fixed2 = lambda i: (0, 0)
    mod_comps = (MOD_SHIFT1, MOD_SCALE1, MOD_GATE1, MOD_SHIFT2, MOD_SCALE2)

    def ot_map(i):
        ic = jnp.minimum(i, n_prompt_tiles - 1)
        return (ic // tpb, 0, ic % tpb)

    os_map = lambda i: (jnp.maximum(i - n_prompt_tiles, 0), 0)
    return pl.pallas_call(
        functools.partial(_merge_kernel, n_prompt_tiles),
        grid=(t_all // tm,),
        in_specs=[pl.BlockSpec((tm, D_MODEL), lambda i: (jnp.minimum(i, n_prompt_tiles - 1), 0)),
                  pl.BlockSpec((tm, D_MODEL), os_map)]
        + [_mod_spec(grp_of_tile, c) for c in mod_comps]
        + [pl.BlockSpec((1, D_MODEL), fixed2),
                  pl.BlockSpec((1, D_MODEL), fixed2),
                  pl.BlockSpec((tm, A_WIDTH), row),
                  pl.BlockSpec((None, N_HEADS * V_HEAD, tm), ot_map),
                  pl.BlockSpec((tm, N_HEADS * V_HEAD), os_map),
                  pl.BlockSpec(wg.shape, fixed2),
                  pl.BlockSpec(woa.shape, fixed2),
                  pl.BlockSpec(wob.shape, fixed2),
                  pl.BlockSpec(wo.shape, fixed2)],
        out_specs=[pl.BlockSpec((tm, D_MODEL), row)] + [pl.BlockSpec((tm, PIECE), row)] * SC_SPLIT,
        out_shape=[jax.ShapeDtypeStruct((t_all, D_MODEL), F32)]
        + [jax.ShapeDtypeStruct((t_all, PIECE), jnp.int32)] * SC_SPLIT,
        scratch_shapes=[pltpu.VMEM((tm, D_MODEL), F32)],
        compiler_params=_params(("parallel",)),
        name="merge",
    )(x_p, x_s, *([mod_g] * len(mod_comps)), g1, g2, a_all, ot, o_s, wg, woa, wob, wo)


def _first_argmax(v, rows):
    mx = jnp.max(v, axis=0, keepdims=True)
    idx = jnp.min(jnp.where(v == mx, rows, v.shape[0]), axis=0, keepdims=True)
    return mx, idx


def _route_kernel(x1_ref, sh2_ref, sc2_ref, g2_ref, whi_ref, wlo_ref, bias_ref, upper_ref,
                  idx_ref, wt_ref, rank_ref, count_ref, run_ref):
    h2 = _rms(x1_ref[...], g2_ref[...]) * (1.0 + sc2_ref[...]) + sh2_ref[...]
    hi = h2.astype(BF16)
    lo = (h2 - hi.astype(F32)).astype(BF16)
    whi = whi_ref[...]
    logits = (lax.dot_general(whi, hi, NT_DIMS, preferred_element_type=F32)
              + lax.dot_general(whi, lo, NT_DIMS, preferred_element_type=F32)
              + lax.dot_general(wlo_ref[...], hi, NT_DIMS, preferred_element_type=F32))
    scores = _sigmoid(logits)
    sel = scores + bias_ref[...]
    tm = sel.shape[1]
    neg = -jnp.inf
    rows_g = lax.broadcasted_iota(jnp.int32, (GROUP_SIZE, tm), 0)
    gscore = []
    for g in range(N_GROUPS):
        blk = sel[g * GROUP_SIZE:(g + 1) * GROUP_SIZE]
        m1, i1 = _first_argmax(blk, rows_g)
        m2 = jnp.max(jnp.where(rows_g == i1, neg, blk), axis=0, keepdims=True)
        gscore.append(m1 + m2)
    gs = jnp.concatenate(gscore, axis=0)
    rows_8 = lax.broadcasted_iota(jnp.int32, gs.shape, 0)
    chosen = jnp.zeros(gs.shape, jnp.int32)
    for _ in range(TOPK_GROUPS):
        _, gi = _first_argmax(gs, rows_8)
        hit = rows_8 == gi
        chosen = jnp.where(hit, 1, chosen)
        gs = jnp.where(hit, neg, gs)
    cand = jnp.concatenate(
        [jnp.where(chosen[g:g + 1] > 0, sel[g * GROUP_SIZE:(g + 1) * GROUP_SIZE], neg) for g in range(N_GROUPS)],
        axis=0)
    rows_e = lax.broadcasted_iota(jnp.int32, cand.shape, 0)
    idxs, wts, hits = [], [], []
    for _ in range(TOP_K):
        _, ei = _first_argmax(cand, rows_e)
        hit = rows_e == ei
        idxs.append(ei)
        hits.append(hit)
        wts.append(jnp.sum(jnp.where(hit, scores, 0.0), axis=0, keepdims=True))
        cand = jnp.where(hit, neg, cand)
    w = jnp.concatenate(wts, axis=0)
    idx_ref[...] = jnp.concatenate(idxs, axis=0)
    wt_ref[...] = w / jnp.sum(w, axis=0, keepdims=True) * ROUTED_SCALE

    @pl.when(pl.program_id(0) == 0)
    def _():
        run_ref[...] = jnp.zeros(run_ref.shape, F32)

    run = run_ref[...]
    ones = jnp.ones((tm, LANES), BF16)
    onehots = [jnp.where(hit, 1.0, 0.0).astype(BF16) for hit in hits]
    befores = [jnp.dot(oh, upper_ref[...], preferred_element_type=F32) for oh in onehots]
    totals = [jnp.dot(oh, ones, preferred_element_type=F32) for oh in onehots]
    ranks = []
    for hit, before, total in zip(hits, befores, totals):
        prior = jnp.concatenate([run] * (tm // LANES), axis=1)
        ranks.append(jnp.sum(jnp.where(hit, before + prior, 0.0), axis=0, keepdims=True))
        run = run + total
    rank_ref[...] = jnp.concatenate(ranks, axis=0).astype(jnp.int32)
    run_ref[...] = run
    count_ref[...] = run


def _route(x1_all, mod_g, g2, whi, wlo, bias_col, grp_of_tile):
    t_all = x1_all.shape[0]
    tm = ROW_TILE
    fixed2 = lambda i: (0, 0)
    col = lambda i: (0, i)
    upper = (lax.broadcasted_iota(jnp.int32, (tm, tm), 0) < lax.broadcasted_iota(jnp.int32, (tm, tm), 1)).astype(BF16)
    return pl.pallas_call(
        _route_kernel,
        grid=(t_all // tm,),
        in_specs=[pl.BlockSpec((tm, D_MODEL), lambda i: (i, 0)),
                  _mod_spec(grp_of_tile, MOD_SHIFT2),
                  _mod_spec(grp_of_tile, MOD_SCALE2),
                  pl.BlockSpec((1, D_MODEL), fixed2),
                  pl.BlockSpec(whi.shape, fixed2),
                  pl.BlockSpec(wlo.shape, fixed2),
                  pl.BlockSpec(bias_col.shape, fixed2),
                  pl.BlockSpec((tm, tm), fixed2)],
        out_specs=[pl.BlockSpec((TOP_K, tm), col), pl.BlockSpec((TOP_K, tm), col), pl.BlockSpec((TOP_K, tm), col),
                   pl.BlockSpec((N_EXPERTS, LANES), fixed2)],
        out_shape=[jax.ShapeDtypeStruct((TOP_K, t_all), jnp.int32),
                   jax.ShapeDtypeStruct((TOP_K, t_all), F32),
                   jax.ShapeDtypeStruct((TOP_K, t_all), jnp.int32),
                   jax.ShapeDtypeStruct((N_EXPERTS, LANES), F32)],
        scratch_shapes=[pltpu.VMEM((N_EXPERTS, LANES), F32)],
        compiler_params=_params(("arbitrary",)),
        name="route",
    )(x1_all, mod_g, mod_g, g2, whi, wlo, bias_col, upper)


def _moe_kernel(be_ref, nv_ref, next_ref, *refs):
    x_refs = refs[:SC_SPLIT]
    wgu_hbm, wdn_hbm = refs[SC_SPLIT:SC_SPLIT + 2]
    y_refs = refs[SC_SPLIT + 2:2 * SC_SPLIT + 2]
    gu_f32, dn_f32, gu_bf, dn_bf, sem, run_ref = refs[2 * SC_SPLIT + 2:]
    i = pl.program_id(0)
    expert = be_ref[i]
    prev = be_ref[jnp.maximum(i - 1, 0)]

    def weight_copies(e, slot):
        return (pltpu.make_async_copy(wgu_hbm.at[e], gu_f32.at[slot], sem.at[0, slot]),
                pltpu.make_async_copy(wdn_hbm.at[e], dn_f32.at[slot], sem.at[1, slot]))

    @pl.when(i == 0)
    def _():
        run_ref[0] = 0
        for cp in weight_copies(expert, 0):
            cp.start(priority=1)

    @pl.when(i < nv_ref[0])
    def _():
        @pl.when((i == 0) | (expert != prev))
        def _():
            run = run_ref[0] + jnp.where(i == 0, 0, 1)
            run_ref[0] = run
            slot = run % 2
            for cp in weight_copies(expert, slot):
                cp.wait()
            following = next_ref[expert]

            @pl.when(following < N_EXPERTS)
            def _():
                for cp in weight_copies(following, 1 - slot):
                    cp.start(priority=1)

            gu_bf[...] = gu_f32[slot].astype(BF16)
            dn_bf[...] = dn_f32[slot].astype(BF16)

        subs = [slice(s * MOE_SUB, (s + 1) * MOE_SUB) for s in range(MOE_BLOCK // MOE_SUB)]
        gus = [jnp.dot(_unpack_rows([r[rows, :] for r in x_refs]).astype(BF16), gu_bf[...],
                       preferred_element_type=F32) for rows in subs]
        hids = [(_silu(gu[:, :EXPERT_DIM]) * gu[:, EXPERT_DIM:]).astype(BF16) for gu in gus]
        ys = [jnp.dot(hid, dn_bf[...], preferred_element_type=F32) for hid in hids]
        for rows, y in zip(subs, ys):
            for ref, piece in zip(y_refs, _pack_rows(y)):
                ref[rows, :] = piece


def _moe(block_e, n_valid, next_expert, x_sorted, w_gu, w_dn):
    n_slots = x_sorted[0].shape[0]
    blk = MOE_BLOCK
    n_blocks = n_slots // blk
    rows = lambda i, be, nv, nx: (jnp.minimum(i, nv[0] - 1), 0)
    return pl.pallas_call(
        _moe_kernel,
        grid_spec=pltpu.PrefetchScalarGridSpec(
            num_scalar_prefetch=3,
            grid=(n_blocks,),
            in_specs=[pl.BlockSpec((blk, PIECE), rows)] * SC_SPLIT
            + [pl.BlockSpec(memory_space=pl.ANY), pl.BlockSpec(memory_space=pl.ANY)],
            out_specs=[pl.BlockSpec((blk, PIECE), rows)] * SC_SPLIT,
            scratch_shapes=[pltpu.VMEM((2, D_MODEL, 2 * EXPERT_DIM), F32),
                            pltpu.VMEM((2, EXPERT_DIM, D_MODEL), F32),
                            pltpu.VMEM((D_MODEL, 2 * EXPERT_DIM), BF16),
                            pltpu.VMEM((EXPERT_DIM, D_MODEL), BF16),
                            pltpu.SemaphoreType.DMA((2, 2)),
                            pltpu.SMEM((1,), jnp.int32)]),
        out_shape=[jax.ShapeDtypeStruct((n_slots, PIECE), jnp.int32)] * SC_SPLIT,
        compiler_params=_params(("arbitrary",)),
        name="moe",
    )(block_e, n_valid, next_expert, *x_sorted, w_gu, w_dn)


def _dest_kernel(e_ref, rank_ref, start_ref, dest_ref):
    e = e_ref[...]
    rows = lax.broadcasted_iota(jnp.int32, (N_EXPERTS, e.shape[1]), 0)
    start = jnp.sum(jnp.where(rows == e, start_ref[...], 0), axis=0, keepdims=True)
    dest_ref[...] = start + rank_ref[...]


def _dest(e_flat, rank, pad_start_col):
    n_assign = e_flat.shape[1]
    n = RANK_TILE
    tile = pl.BlockSpec((1, n), lambda i: (0, i))
    return pl.pallas_call(
        _dest_kernel,
        grid=(n_assign // n,),
        in_specs=[tile, tile, pl.BlockSpec((N_EXPERTS, 1), lambda i: (0, 0))],
        out_specs=tile,
        out_shape=jax.ShapeDtypeStruct((1, n_assign), jnp.int32),
        compiler_params=_params(("parallel",)),
        name="dest",
    )(e_flat, rank, pad_start_col)


def _sc_mesh():
    return plsc.VectorSubcoreMesh(core_axis_name="core", subcore_axis_name="subcore")


def _sc_scatter_rows(src, dest, n_slots):
    n_src, width = src.shape
    n_assign = dest.shape[1]
    src_blocks = n_src // SC_WINDOW

    @functools.partial(pl.kernel, out_type=jax.ShapeDtypeStruct((n_slots, width), src.dtype),
                       mesh=_sc_mesh(), scratch_types=[])
    def scatter(src_hbm, dest_hbm, out_hbm):
        def body(rows_vmem, dest_vmem):
            pltpu.sync_copy(rows_vmem, out_hbm.at[dest_vmem.at[0]])

        pltpu.emit_pipeline(
            body,
            grid=(n_assign // SC_WINDOW,),
            in_specs=[pl.BlockSpec((SC_WINDOW, width), lambda i: (i % src_blocks, 0)),
                      pl.BlockSpec((1, SC_WINDOW), lambda i: (0, i))],
            out_specs=[],
            core_axis_name=("core", "subcore"),
            dimension_semantics=(pltpu.PARALLEL,),
        )(src_hbm, dest_hbm)

    return scatter(src, dest)


def _sc_gather_rows(table, idx):
    width = table.shape[1]
    n_assign = idx.shape[1]

    @functools.partial(pl.kernel, out_type=jax.ShapeDtypeStruct((n_assign, width), table.dtype),
                       mesh=_sc_mesh(), scratch_types=[])
    def gather(table_hbm, idx_hbm, out_hbm):
        def body(idx_vmem, rows_vmem):
            pltpu.sync_copy(table_hbm.at[idx_vmem.at[0]], rows_vmem)

        pltpu.emit_pipeline(
            body,
            grid=(n_assign // SC_WINDOW,),
            in_specs=[pl.BlockSpec((1, SC_WINDOW), lambda i: (0, i))],
            out_specs=[pl.BlockSpec((SC_WINDOW, width), lambda i: (i, 0))],
            core_axis_name=("core", "subcore"),
            dimension_semantics=(pltpu.PARALLEL,),
        )(idx_hbm, out_hbm)

    return gather(table, idx)


def _final_kernel(n_prompt_tiles, x1_ref, *refs):
    h2_refs = refs[:SC_SPLIT]
    yg_refs = refs[SC_SPLIT:2 * SC_SPLIT]
    wt_ref, gt2_ref, wsg_ref, wsd_ref, gf_ref, op_ref, os_ref = refs[2 * SC_SPLIT:]
    is_prompt = pl.program_id(0) < n_prompt_tiles
    h2 = _unpack_rows([r[...] for r in h2_refs]).astype(BF16)
    gu = jnp.dot(h2, wsg_ref[...], preferred_element_type=F32)
    hid = _silu(gu[:, :SHARED_DIM]) * gu[:, SHARED_DIM:]
    y = jnp.dot(hid.astype(BF16), wsd_ref[...], preferred_element_type=F32)
    wt = wt_ref[...]
    for k in range(TOP_K):
        y = y + wt[:, k:k + 1] * _unpack_rows([r[k] for r in yg_refs])
    x2 = x1_ref[...] + gt2_ref[...] * y
    out = _rms(x2, gf_ref[...])

    @pl.when(is_prompt)
    def _():
        op_ref[...] = out

    @pl.when(jnp.logical_not(is_prompt))
    def _():
        os_ref[...] = out


def _final(x1_all, h2_all, y_gath, wt_rows, mod_g, wsg, wsd, gf, grp_of_tile, n_prompt_tiles):
    t_all = x1_all.shape[0]
    tm = ROW_TILE
    t_p = n_prompt_tiles * tm
    row = lambda i: (i, 0)
    fixed2 = lambda i: (0, 0)
    return pl.pallas_call(
        functools.partial(_final_kernel, n_prompt_tiles),
        grid=(t_all // tm,),
        in_specs=[pl.BlockSpec((tm, D_MODEL), row)]
        + [pl.BlockSpec((tm, PIECE), row)] * SC_SPLIT
        + [pl.BlockSpec((TOP_K, tm, PIECE), lambda i: (0, i, 0))] * SC_SPLIT
        + [pl.BlockSpec((tm, TOP_K), row),
           _mod_spec(grp_of_tile, MOD_GATE2),
           pl.BlockSpec(wsg.shape, fixed2),
           pl.BlockSpec(wsd.shape, fixed2),
           pl.BlockSpec((1, D_MODEL), fixed2)],
        out_specs=[pl.BlockSpec((tm, D_MODEL), lambda i: (jnp.minimum(i, n_prompt_tiles - 1), 0)),
                   pl.BlockSpec((tm, D_MODEL), lambda i: (jnp.maximum(i - n_prompt_tiles, 0), 0))],
        out_shape=[jax.ShapeDtypeStruct((t_p, D_MODEL), F32),
                   jax.ShapeDtypeStruct((t_all - t_p, D_MODEL), F32)],
        compiler_params=_params(("arbitrary",)),
        name="final",
    )(x1_all, *h2_all, *y_gath, wt_rows, mod_g, wsg, wsd, gf)


def _rope_tables(pos):
    freqs = ROPE_THETA ** (-jnp.arange(ROPE_HALF, dtype=F32) / ROPE_HALF)
    ang = pos.astype(F32)[:, None] * freqs
    return jnp.cos(ang), jnp.sin(ang)


def kernel(x_prompt, x_sample, cache_ckv, cache_krope, page_table, c_prompt, c_sample, w_ada, b_ada, norm1_g, norm2_g, w_in, a_vnorm_g, w_spatial, b_spatial, w_out_a, q_norm_g, w_uq, kv_norm_g, w_uk, w_uv, w_out_b, w_o, w_router, router_bias, w_exp_gu, w_exp_down, w_sh_gu, w_sh_down, final_norm_g):
    batch, seq, _ = x_prompt.shape
    dec_b, dec_s, _ = x_sample.shape
    n_pages = page_table.shape[1]
    n_past = n_pages * CHUNK
    t_p, t_s = batch * seq, dec_b * dec_s
    t_all = t_p + t_s
    tm = ROW_TILE
    assert seq % ATTN_Q_STEP == 0 and t_s % tm == 0 and CHUNK % dec_s == 0 and n_pages % PAGES_PER_STEP == 0
    assert (t_all * TOP_K) % RANK_TILE == 0 and w_ada.shape[0] == 1
    n_prompt_tiles = t_p // tm
    tiles_per_b = seq // tm

    def grp_of_tile(i):
        return jnp.minimum(i // tiles_per_b, batch) + jnp.maximum(i - n_prompt_tiles, 0)

    n_c = batch + dec_b
    c_rows = -(-n_c // 8) * 8
    c_all = jnp.concatenate([c_prompt, c_sample, jnp.zeros((c_rows - n_c, D_MODEL), F32)], axis=0)
    mod = _ada(c_all, w_ada[0].astype(BF16), b_ada[0][None, :])
    mod_p = jnp.broadcast_to(mod[:batch, None, :], (batch, tm, 6 * D_MODEL))
    mod_s = jnp.repeat(mod[batch:n_c], dec_s, axis=0).reshape(t_s // tm, tm, 6 * D_MODEL)
    mod_g = jnp.concatenate([mod_p, mod_s], axis=0)

    o_u, o_v, o_cq, o_ckv, o_kr = A_WIDTH, 2 * A_WIDTH, 2 * A_WIDTH + Q_LORA, 2 * A_WIDTH + Q_LORA + KV_LORA, \
        2 * A_WIDTH + Q_LORA + KV_LORA + QK_ROPE
    win = w_in[0]
    w5 = jnp.concatenate([win[:, :o_kr], jnp.zeros((D_MODEL, IN_END - IN_KR - QK_ROPE), F32)], axis=1).astype(BF16)
    wg = win[:, o_kr:].astype(BF16)
    row1 = lambda v: v.reshape(1, -1)
    tri = jnp.tril(jnp.ones((CHUNK, CHUNK), F32))
    wc_p = w_spatial[0] * tri
    per = CHUNK // dec_s
    small = (w_spatial[0] * tri)[:, :dec_s, :dec_s]
    wc_s = jnp.einsum('ab,gts->gatbs', jnp.eye(per, dtype=F32), small).reshape(A_GROUPS, CHUNK, CHUNK)
    wc = jnp.stack([wc_p, wc_s]).astype(BF16)
    bias_p = jnp.repeat(b_spatial[0].T, A_GROUP_DIM, axis=1)
    bias_s = jnp.tile(bias_p[:dec_s], (per, 1))
    bc = jnp.stack([bias_p, bias_s])

    pos_p = jnp.arange(seq)
    pos_s = n_past + jnp.arange(dec_s)
    cos_p, sin_p = _rope_tables(pos_p)
    cos_s, sin_s = _rope_tables(pos_s)
    cos_rows = jnp.concatenate([jnp.tile(cos_p, (batch, 1)), jnp.tile(cos_s, (dec_b, 1))], axis=0)
    sin_rows = jnp.concatenate([jnp.tile(sin_p, (batch, 1)), jnp.tile(sin_s, (dec_b, 1))], axis=0)
    lane_pad = jnp.zeros((t_all, LANES - QK_ROPE), F32)
    cos_kr = jnp.concatenate([cos_rows, cos_rows, lane_pad], axis=1)
    sin_kr = jnp.concatenate([-sin_rows, sin_rows, lane_pad], axis=1)

    x_p, x_s = x_prompt.reshape(t_p, D_MODEL), x_sample.reshape(t_s, D_MODEL)
    g1, g2 = row1(norm1_g[0]), row1(norm2_g[0])
    a_all, cq_all, ckv_p, kr_p, ckv_s, kr_s, v_s = _in_proj(
        x_p, x_s, mod_g, g1, w5, row1(a_vnorm_g[0]), row1(q_norm_g[0]), row1(kv_norm_g[0]),
        cos_kr, sin_kr, wc, bc, grp_of_tile, n_prompt_tiles)

    wuq = w_uq[0]
    wuqt = jnp.pad(wuq, ((0, 0), (0, 0), (0, HEAD_PAD - QK_NOPE - QK_ROPE))).reshape(Q_LORA, -1).T.astype(BF16)
    wukp = jnp.pad(w_uk[0], ((0, 0), (0, 0), (0, HEAD_PAD - QK_NOPE))).reshape(KV_LORA, -1).astype(BF16)
    place_h = jnp.pad(jnp.eye(QK_ROPE, dtype=F32), ((0, 0), (QK_NOPE, HEAD_PAD - QK_NOPE - QK_ROPE)))
    place = jnp.tile(place_h, (1, N_HEADS)).astype(BF16)
    wuvt = w_uv[0].reshape(KV_LORA, -1).T.astype(BF16)
    qt, k, vt = _qkv(cq_all, ckv_p, kr_p, wuqt, wukp, place, wuvt, cos_p.T, sin_p.T, batch, seq)
    ot = _attn(qt, k, vt)

    wq_s = jnp.concatenate([wuq[:, :, :QK_NOPE].reshape(Q_LORA, -1),
                            wuq[:, :, QK_NOPE:QK_NOPE + ROPE_HALF].reshape(Q_LORA, -1),
                            wuq[:, :, QK_NOPE + ROPE_HALF:].reshape(Q_LORA, -1)], axis=1).astype(BF16)
    eye_h = jnp.eye(N_HEADS, dtype=F32)
    wuk_blk = jnp.einsum('rhd,hg->hdgr', w_uk[0], eye_h).reshape(N_HEADS * QK_NOPE, N_HEADS * KV_LORA).astype(BF16)
    wuv_blk = jnp.einsum('rhd,hg->hrgd', w_uv[0], eye_h).reshape(N_HEADS * KV_LORA, N_HEADS * V_HEAD).astype(BF16)
    cos_sq = jnp.tile(jnp.tile(cos_s, (1, N_HEADS)), (dec_b, 1))
    sin_sq = jnp.tile(jnp.tile(sin_s, (1, N_HEADS)), (dec_b, 1))
    ql, r1, r2 = _sample_q(cq_all[t_p:], wq_s, wuk_blk, cos_sq, sin_sq)
    rows = dec_s * N_HEADS
    ql3 = ql.reshape(dec_b, rows, KV_LORA)
    qr3 = jnp.concatenate([r1.reshape(dec_b, dec_s, N_HEADS, ROPE_HALF),
                           r2.reshape(dec_b, dec_s, N_HEADS, ROPE_HALF)], axis=-1).reshape(dec_b, rows, QK_ROPE)
    new_pad = ((0, 0), (0, 8 - dec_s), (0, 0))
    ckv_new = jnp.pad(ckv_s.reshape(dec_b, dec_s, KV_LORA), new_pad)
    kr_new_t = jnp.swapaxes(jnp.pad(kr_s.reshape(dec_b, dec_s, QK_ROPE), new_pad), 1, 2)
    cache_kr_t = jnp.swapaxes(cache_krope[0], 1, 2)
    o_lat = _sample_attn(page_table, cache_ckv[0], cache_kr_t, ql3, qr3, ckv_new, kr_new_t)
    o_s = _matmul(o_lat.reshape(t_s, N_HEADS * KV_LORA), wuv_blk, BF16)

    x1_all, *h2_all = _merge(x_p, x_s, mod_g, g1, g2, a_all, ot, o_s, wg,
                            w_out_a[0].astype(BF16), w_out_b[0].astype(BF16), w_o[0].astype(BF16),
                            grp_of_tile, n_prompt_tiles)

    wr_t = w_router[0].T
    whi = wr_t.astype(BF16)
    wlo = (wr_t - whi.astype(F32)).astype(BF16)
    idx_t, wt_t, rank_t, counts = _route(x1_all, mod_g, g2, whi, wlo, router_bias[0].reshape(-1, 1),
                                         grp_of_tile)
    n_assign = t_all * TOP_K
    e_flat = idx_t.reshape(1, n_assign)
    counts = counts[:, 0].astype(jnp.int32)
    blk = MOE_BLOCK
    padded = (counts + blk - 1) // blk * blk
    pad_end = jnp.cumsum(padded)
    pad_start = pad_end - padded
    dest = _dest(e_flat, rank_t.reshape(1, n_assign), pad_start.reshape(-1, 1))
    n_blocks = -(-n_assign // blk) + N_EXPERTS
    n_slots = n_blocks * blk
    first_row = jnp.arange(n_blocks, dtype=jnp.int32) * blk
    block_e = jnp.minimum(jnp.sum(pad_end[None, :] <= first_row[:, None], axis=1), N_EXPERTS - 1).astype(jnp.int32)
    n_valid = (pad_end[-1] // blk).astype(jnp.int32).reshape(1)
    experts = jnp.arange(N_EXPERTS, dtype=jnp.int32)
    owners = jnp.where(padded > 0, experts, N_EXPERTS)
    later_owner = lax.cummin(owners, reverse=True)
    next_expert = jnp.concatenate([later_owner[1:], jnp.full((1,), N_EXPERTS, jnp.int32)])
    x_sorted = [_sc_scatter_rows(h, dest, n_slots) for h in h2_all]
    y_sorted = _moe(block_e, n_valid, next_expert, x_sorted, w_exp_gu[0], w_exp_down[0])
    y_gath = [_sc_gather_rows(y, dest).reshape(TOP_K, t_all, PIECE) for y in y_sorted]

    y_p, y_s = _final(x1_all, h2_all, y_gath, wt_t.T, mod_g, w_sh_gu[0].astype(BF16), w_sh_down[0].astype(BF16),
                      row1(final_norm_g), grp_of_tile, n_prompt_tiles)

    y_prompt = y_p.reshape(batch, seq, D_MODEL)
    y_sample = y_s.reshape(dec_b, dec_s, D_MODEL)
    new_ckv_prompt = ckv_p.reshape(1, batch, seq, KV_LORA)
    new_krope_prompt = kr_p.reshape(1, batch, seq, QK_ROPE)
    new_ckv_sample = ckv_s.reshape(1, dec_b, dec_s, KV_LORA)
    new_krope_sample = kr_s.reshape(1, dec_b, dec_s, QK_ROPE)
    new_chunk_v_sample = v_s.reshape(1, dec_b, dec_s, A_WIDTH)
    return (y_prompt, y_sample, new_ckv_prompt, new_krope_prompt, new_ckv_sample, new_krope_sample,
            new_chunk_v_sample)
```

```python
import functools

import jax
import jax.numpy as jnp
from jax import lax
from jax.experimental import pallas as pl
from jax.experimental.pallas import tpu as pltpu
from jax.experimental.pallas import tpu_sc as plsc

F32 = jnp.float32
BF16 = jnp.bfloat16

D_MODEL = 1024
A_WIDTH = D_MODEL // 2
A_GROUPS = 8
A_GROUP_DIM = A_WIDTH // A_GROUPS
CHUNK = 128
N_HEADS = 8
QK_NOPE = 64
QK_ROPE = 32
ROPE_HALF = QK_ROPE // 2
V_HEAD = 64
Q_LORA = 384
KV_LORA = 256
ROPE_THETA = 10000.0
ATTN_SCALE = (QK_NOPE + QK_ROPE) ** -0.5
N_EXPERTS = 256
TOP_K = 8
N_GROUPS = 8
GROUP_SIZE = N_EXPERTS // N_GROUPS
TOPK_GROUPS = 4
EXPERT_DIM = 256
SHARED_DIM = 256
ROUTED_SCALE = 2.5
EPS = 1e-6

LANES = 128
HEAD_PAD = 128
ROW_TILE = 256
ATTN_KV_TILE = 256
ATTN_Q_STEP = 512
ATTN_HEADS_PER_STEP = 4
LOG2_E = 1.4426950408889634
RANK_TILE = 512
MOE_BLOCK = 256
MOE_SUB = 128
PAGES_PER_STEP = 32
PAGE_GROUP = 8
V_ROWS = V_HEAD + 16
SC_WINDOW = 128
SC_SPLIT = 2
PIECE = D_MODEL // 2 // SC_SPLIT
VMEM_LIMIT = 48 * 1024 * 1024

IN_U, IN_V, IN_CQ, IN_CKV, IN_KR, IN_END = 0, 512, 1024, 1408, 1664, 1792

NT_DIMS = (((1,), (1,)), ((), ()))
TN_DIMS = (((0,), (0,)), ((), ()))


def _params(sem, vmem=VMEM_LIMIT):
    return pltpu.CompilerParams(dimension_semantics=sem, vmem_limit_bytes=vmem)


def _rms(x, g):
    return x * lax.rsqrt(jnp.mean(x * x, axis=-1, keepdims=True) + EPS) * g


def _gelu(x):
    return 0.5 * x * (1.0 + jnp.tanh(0.7978845608028654 * (x + 0.044715 * (x * x * x))))


def _sigmoid(x):
    return 1.0 / (1.0 + jnp.exp(-x))


def _silu(x):
    return x * _sigmoid(x)


def _pack_rows(x):
    pieces = []
    for c in range(SC_SPLIT):
        lo = x[:, 2 * c * PIECE:(2 * c + 1) * PIECE].astype(BF16).astype(F32)
        hi = x[:, (2 * c + 1) * PIECE:(2 * c + 2) * PIECE].astype(BF16).astype(F32)
        pieces.append(lax.bitcast_convert_type(hi, jnp.int32)
                      | lax.shift_right_logical(lax.bitcast_convert_type(lo, jnp.int32), 16))
    return pieces


def _unpack_rows(pieces):
    cols = []
    for p in pieces:
        cols.append(lax.bitcast_convert_type(lax.shift_left(p, 16), F32))
        cols.append(lax.bitcast_convert_type(p & jnp.int32(-65536), F32))
    return jnp.concatenate(cols, axis=1)


def _ada_kernel(c_ref, w_ref, b_ref, o_ref):
    s = _silu(c_ref[...]).astype(BF16)
    o_ref[...] = jnp.dot(s, w_ref[...], preferred_element_type=F32) + b_ref[...]


def _ada(c, w, b):
    rows, n = c.shape[0], w.shape[1]
    tn = 1536
    return pl.pallas_call(
        _ada_kernel,
        grid=(n // tn,),
        in_specs=[pl.BlockSpec((rows, D_MODEL), lambda j: (0, 0)),
                  pl.BlockSpec((D_MODEL, tn), lambda j: (0, j)),
                  pl.BlockSpec((1, tn), lambda j: (0, j))],
        out_specs=pl.BlockSpec((rows, tn), lambda j: (0, j)),
        out_shape=jax.ShapeDtypeStruct((rows, n), F32),
        compiler_params=_params(("parallel",)),
        name="ada",
    )(c, w, b)


def _in_kernel(n_prompt_tiles, xp_ref, xs_ref, sh_ref, sc_ref, g1_ref, w_ref, vg_ref, qg_ref, kg_ref,
               cos_ref, sin_ref, wc_ref, bc_ref,
               a_ref, cq_ref, ckvp_ref, krp_ref, ckvs_ref, krs_ref, vs_ref):
    tm = xp_ref.shape[0]
    is_prompt = pl.program_id(0) < n_prompt_tiles
    x = jnp.where(is_prompt, xp_ref[...], xs_ref[...])
    h = _rms(x, g1_ref[...]) * (1.0 + sc_ref[...]) + sh_ref[...]
    z = jnp.dot(h.astype(BF16), w_ref[...], preferred_element_type=F32)
    u = _gelu(z[:, IN_U:IN_V])
    v = _rms(_gelu(z[:, IN_V:IN_CQ]), vg_ref[...])
    cq_ref[...] = _rms(z[:, IN_CQ:IN_CKV], qg_ref[...]).astype(BF16)
    ckv = _rms(z[:, IN_CKV:IN_KR], kg_ref[...])
    kr = z[:, IN_KR:IN_END]
    lane = lax.broadcasted_iota(jnp.int32, kr.shape, 1)
    swapped = jnp.where(lane < ROPE_HALF, pltpu.roll(kr, LANES - ROPE_HALF, 1), pltpu.roll(kr, ROPE_HALF, 1))
    kr_rot = (kr * cos_ref[...] + swapped * sin_ref[...])[:, :QK_ROPE]

    @pl.when(is_prompt)
    def _():
        ckvp_ref[...] = ckv
        krp_ref[...] = kr_rot

    @pl.when(jnp.logical_not(is_prompt))
    def _():
        ckvs_ref[...] = ckv
        krs_ref[...] = kr_rot
        vs_ref[...] = v

    vb = v.astype(BF16)
    half = A_WIDTH // 2
    grp_of_lane = lax.broadcasted_iota(jnp.int32, (CHUNK, half), 1) // A_GROUP_DIM
    for ci in range(tm // CHUNK):
        rows = slice(ci * CHUNK, (ci + 1) * CHUNK)
        parts = []
        for q in range(2):
            vq = vb[rows, q * half:(q + 1) * half]
            acc = None
            for gg in range(A_GROUPS // 2):
                vm = jnp.where(grp_of_lane == gg, vq, jnp.zeros_like(vq))
                part = jnp.dot(wc_ref[q * (A_GROUPS // 2) + gg], vm, preferred_element_type=F32)
                acc = part if acc is None else acc + part
            parts.append(acc)
        sp = jnp.concatenate(parts, axis=1) + bc_ref[...]
        a_ref[rows, :] = (u[rows, :] * sp).astype(BF16)


MOD_SHIFT1, MOD_SCALE1, MOD_GATE1, MOD_SHIFT2, MOD_SCALE2, MOD_GATE2 = range(6)


def _mod_spec(grp_of_tile, component):
    return pl.BlockSpec((None, ROW_TILE, D_MODEL), lambda i: (grp_of_tile(i), 0, component))


def _in_proj(x_p, x_s, mod_g, g1, w5, vg, qg, kg, cos_kr, sin_kr, wc, bc, grp_of_tile, n_prompt_tiles):
    t_p, t_s = x_p.shape[0], x_s.shape[0]
    t_all = t_p + t_s
    tm = ROW_TILE
    n_tiles = t_all // tm
    row = lambda i: (i, 0)
    prow = lambda i: (jnp.minimum(i, n_prompt_tiles - 1), 0)
    srow = lambda i: (jnp.maximum(i - n_prompt_tiles, 0), 0)
    fixed2 = lambda i: (0, 0)
    kind = lambda i: (jnp.where(i < n_prompt_tiles, 0, 1), 0, 0)
    kind4 = lambda i: (jnp.where(i < n_prompt_tiles, 0, 1), 0, 0, 0)
    return pl.pallas_call(
        functools.partial(_in_kernel, n_prompt_tiles),
        grid=(n_tiles,),
        in_specs=[pl.BlockSpec((tm, D_MODEL), prow),
                  pl.BlockSpec((tm, D_MODEL), srow),
                  _mod_spec(grp_of_tile, MOD_SHIFT1),
                  _mod_spec(grp_of_tile, MOD_SCALE1),
                  pl.BlockSpec((1, D_MODEL), fixed2),
                  pl.BlockSpec((D_MODEL, IN_END), fixed2),
                  pl.BlockSpec((1, A_WIDTH), fixed2),
                  pl.BlockSpec((1, Q_LORA), fixed2),
                  pl.BlockSpec((1, KV_LORA), fixed2),
                  pl.BlockSpec((tm, LANES), row),
                  pl.BlockSpec((tm, LANES), row),
                  pl.BlockSpec((None, A_GROUPS, CHUNK, CHUNK), kind4),
                  pl.BlockSpec((None, CHUNK, A_WIDTH), kind)],
        out_specs=[pl.BlockSpec((tm, A_WIDTH), row),
                   pl.BlockSpec((tm, Q_LORA), row),
                   pl.BlockSpec((tm, KV_LORA), prow),
                   pl.BlockSpec((tm, QK_ROPE), prow),
                   pl.BlockSpec((tm, KV_LORA), srow),
                   pl.BlockSpec((tm, QK_ROPE), srow),
                   pl.BlockSpec((tm, A_WIDTH), srow)],
        out_shape=[jax.ShapeDtypeStruct((t_all, A_WIDTH), BF16),
                   jax.ShapeDtypeStruct((t_all, Q_LORA), BF16),
                   jax.ShapeDtypeStruct((t_p, KV_LORA), F32),
                   jax.ShapeDtypeStruct((t_p, QK_ROPE), F32),
                   jax.ShapeDtypeStruct((t_s, KV_LORA), F32),
                   jax.ShapeDtypeStruct((t_s, QK_ROPE), F32),
                   jax.ShapeDtypeStruct((t_s, A_WIDTH), F32)],
        compiler_params=_params(("arbitrary",)),
        name="in_proj",
    )(x_p, x_s, mod_g, mod_g, g1, w5, vg, qg, kg, cos_kr, sin_kr, wc, bc)


def _qkv_kernel(cq_ref, ckv_ref, kr_ref, wuqt_ref, wukp_ref, place_ref, wuvt_ref, cos_ref, sin_ref,
                qt_ref, k_ref, vt_ref):
    tm = cq_ref.shape[0]
    ckv = ckv_ref[...].astype(BF16)
    kr = kr_ref[...].astype(BF16)
    qt = lax.dot_general(wuqt_ref[...], cq_ref[...], NT_DIMS, preferred_element_type=F32)
    c = cos_ref[...]
    s = sin_ref[...]
    pad = jnp.zeros((HEAD_PAD - QK_NOPE - QK_ROPE, tm), F32)
    for h in range(N_HEADS):
        blk = qt[h * HEAD_PAD:(h + 1) * HEAD_PAD]
        x1 = blk[QK_NOPE:QK_NOPE + ROPE_HALF]
        x2 = blk[QK_NOPE + ROPE_HALF:QK_NOPE + QK_ROPE]
        full = jnp.concatenate([blk[:QK_NOPE], x1 * c - x2 * s, x1 * s + x2 * c, pad], axis=0)
        qt_ref[h] = (full * (ATTN_SCALE * LOG2_E)).astype(BF16)
    k = (jnp.dot(ckv, wukp_ref[...], preferred_element_type=F32)
         + jnp.dot(kr, place_ref[...], preferred_element_type=F32))
    for h in range(N_HEADS):
        k_ref[h] = k[:, h * HEAD_PAD:(h + 1) * HEAD_PAD].astype(BF16)
    vt = lax.dot_general(wuvt_ref[...], ckv, NT_DIMS, preferred_element_type=F32)
    extra = V_ROWS - V_HEAD
    ones_row = jnp.where(lax.broadcasted_iota(jnp.int32, (extra, tm), 0) == 0, 1.0, 0.0)
    for h in range(N_HEADS):
        vt_ref[h] = jnp.concatenate([vt[h * V_HEAD:(h + 1) * V_HEAD], ones_row], axis=0).astype(BF16)


def _qkv(cq_all, ckv_all, kr_all, wuqt, wukp, place, wuvt, cos_t, sin_t, batch, seq):
    tm = ATTN_KV_TILE
    nk = seq // tm
    row = lambda b, j: (b * nk + j, 0)
    fixed2 = lambda b, j: (0, 0)
    tab = lambda b, j: (0, j)
    return pl.pallas_call(
        _qkv_kernel,
        grid=(batch, nk),
        in_specs=[pl.BlockSpec((tm, Q_LORA), row),
                  pl.BlockSpec((tm, KV_LORA), row),
                  pl.BlockSpec((tm, QK_ROPE), row),
                  pl.BlockSpec(wuqt.shape, fixed2),
                  pl.BlockSpec(wukp.shape, fixed2),
                  pl.BlockSpec(place.shape, fixed2),
                  pl.BlockSpec(wuvt.shape, fixed2),
                  pl.BlockSpec((ROPE_HALF, tm), tab),
                  pl.BlockSpec((ROPE_HALF, tm), tab)],
        out_specs=[pl.BlockSpec((None, N_HEADS, HEAD_PAD, tm), lambda b, j: (b, 0, 0, j)),
                   pl.BlockSpec((None, N_HEADS, tm, HEAD_PAD), lambda b, j: (b, 0, j, 0)),
                   pl.BlockSpec((None, N_HEADS, None, V_ROWS, tm), lambda b, j: (b, 0, j, 0, 0))],
        out_shape=[jax.ShapeDtypeStruct((batch, N_HEADS, HEAD_PAD, seq), BF16),
                   jax.ShapeDtypeStruct((batch, N_HEADS, seq, HEAD_PAD), BF16),
                   jax.ShapeDtypeStruct((batch, N_HEADS, nk, V_ROWS, tm), BF16)],
        compiler_params=_params(("parallel", "parallel")),
        name="qkv",
    )(cq_all, ckv_all, kr_all, wuqt, wukp, place, wuvt, cos_t, sin_t)


def _attn_kernel(qt_ref, k_ref, vt_ref, o_ref):
    tk = ATTN_KV_TILE
    n_heads = qt_ref.shape[0]
    n_sub = qt_ref.shape[2] // tk
    qi = pl.program_id(2)
    j0 = qi * n_sub
    chains = [(h, sub) for h in range(n_heads) for sub in range(n_sub)]

    def tiles(j, active, carries, diagonal_sub):
        scores = [jnp.dot(k_ref[h, j], qt_ref[h, :, sub * tk:(sub + 1) * tk], preferred_element_type=F32)
                  for h, sub in active]
        stats = []
        for (h, sub), s in zip(active, scores):
            m, _ = carries[h * n_sub + sub]
            if sub == diagonal_sub:
                key = lax.broadcasted_iota(jnp.int32, s.shape, 0)
                qry = lax.broadcasted_iota(jnp.int32, s.shape, 1)
                s = jnp.where(key <= qry, s, -jnp.inf)
            m_new = jnp.maximum(m, jnp.max(s, axis=0, keepdims=True))
            stats.append((m_new, jnp.exp2(m - m_new), jnp.exp2(s - m_new).astype(BF16)))
        out = list(carries)
        for (h, sub), (m_new, alpha, p) in zip(active, stats):
            c = h * n_sub + sub
            out[c] = (m_new, alpha * carries[c][1] + jnp.dot(vt_ref[h, j], p, preferred_element_type=F32))
        return out

    init = tuple((jnp.full((1, tk), -jnp.inf, F32), jnp.zeros((V_ROWS, tk), F32)) for _ in chains)
    carries = list(lax.fori_loop(0, j0, lambda j, c: tuple(tiles(j, chains, c, None)), init))
    for jj in range(n_sub):
        carries = tiles(j0 + jj, [(h, sub) for h, sub in chains if sub >= jj], carries, jj)
    for h, sub in chains:
        _, acc = carries[h * n_sub + sub]
        o_ref[h * V_HEAD:(h + 1) * V_HEAD, sub * tk:(sub + 1) * tk] = (
            acc[:V_HEAD] / acc[V_HEAD:V_HEAD + 1]).astype(BF16)


def _attn(qt, k, vt):
    batch, _, _, seq = qt.shape
    tk = ATTN_KV_TILE
    nk = seq // tk
    hp = ATTN_HEADS_PER_STEP
    k5 = k.reshape(batch, N_HEADS, nk, tk, HEAD_PAD)
    return pl.pallas_call(
        _attn_kernel,
        grid=(batch, N_HEADS // hp, seq // ATTN_Q_STEP),
        in_specs=[pl.BlockSpec((None, hp, HEAD_PAD, ATTN_Q_STEP), lambda b, h, i: (b, h, 0, i)),
                  pl.BlockSpec((None, hp, nk, tk, HEAD_PAD), lambda b, h, i: (b, h, 0, 0, 0)),
                  pl.BlockSpec((None, hp, nk, V_ROWS, tk), lambda b, h, i: (b, h, 0, 0, 0))],
        out_specs=pl.BlockSpec((None, hp * V_HEAD, ATTN_Q_STEP), lambda b, h, i: (b, h, i)),
        out_shape=jax.ShapeDtypeStruct((batch, N_HEADS * V_HEAD, seq), BF16),
        compiler_params=_params(("parallel", "parallel", "parallel")),
        name="attn",
    )(qt, k5, vt)


def _sq_kernel(cq_ref, wq_ref, wukb_ref, cos_ref, sin_ref, ql_ref, r1_ref, r2_ref):
    n_nope = N_HEADS * QK_NOPE
    q = jnp.dot(cq_ref[...], wq_ref[...], preferred_element_type=F32)
    x1 = q[:, n_nope:n_nope + LANES]
    x2 = q[:, n_nope + LANES:]
    c = cos_ref[...]
    s = sin_ref[...]
    r1_ref[...] = ((x1 * c - x2 * s) * ATTN_SCALE).astype(BF16)
    r2_ref[...] = ((x1 * s + x2 * c) * ATTN_SCALE).astype(BF16)
    ql = jnp.dot(q[:, :n_nope].astype(BF16), wukb_ref[...], preferred_element_type=F32)
    ql_ref[...] = (ql * ATTN_SCALE).astype(BF16)


def _sample_q(cq_s, wq_s, wuk_blk, cos_s, sin_s):
    ts = cq_s.shape[0]
    full = lambda shape: pl.BlockSpec(shape, lambda i: (0,) * len(shape))
    return pl.pallas_call(
        _sq_kernel,
        grid=(1,),
        in_specs=[full(cq_s.shape), full(wq_s.shape), full(wuk_blk.shape), full(cos_s.shape), full(sin_s.shape)],
        out_specs=[full((ts, N_HEADS * KV_LORA)), full((ts, LANES)), full((ts, LANES))],
        out_shape=[jax.ShapeDtypeStruct((ts, N_HEADS * KV_LORA), BF16),
                   jax.ShapeDtypeStruct((ts, LANES), BF16),
                   jax.ShapeDtypeStruct((ts, LANES), BF16)],
        compiler_params=_params(("arbitrary",)),
        name="sample_q",
    )(cq_s, wq_s, wuk_blk, cos_s, sin_s)


def _sattn_kernel(n_pages, pt_ref, ckv_hbm, kr_hbm, ql_ref, qr_ref, cnew_ref, knew_ref, o_ref,
                  cbuf, rbuf, sem_c, sem_r):
    npg = PAGES_PER_STEP
    n_chunks = n_pages // npg
    PAGE_SLOTS = cbuf.shape[0]
    PAGE_LOOKAHEAD = PAGE_SLOTS - 1
    b = pl.program_id(0)

    def page_copies(bb, chunk, slot, p):
        page = pt_ref[bb * n_pages + chunk * npg + p]
        return (pltpu.make_async_copy(ckv_hbm.at[page], cbuf.at[slot, p], sem_c.at[slot]),
                pltpu.make_async_copy(kr_hbm.at[page], rbuf.at[slot, p], sem_r.at[slot]))

    def start_chunk(bb, chunk, slot):
        for p in range(npg):
            for cp in page_copies(bb, chunk, slot, p):
                cp.start()

    def wait_chunk(bb, chunk, slot):
        for p in range(npg):
            for cp in page_copies(bb, chunk, slot, p):
                cp.wait()

    @pl.when(b == 0)
    def _():
        for chunk in range(PAGE_LOOKAHEAD):
            start_chunk(0, chunk, chunk)

    ql = ql_ref[...]
    qr = qr_ref[...]

    def partial_softmax(blocks):
        scores = [lax.dot_general(ql, keys, NT_DIMS, preferred_element_type=F32)
                  + jnp.dot(qr, rope_t, preferred_element_type=F32) for keys, rope_t, _ in blocks]
        probs = []
        for s, (_, _, mask) in zip(scores, blocks):
            if mask is not None:
                s = jnp.where(mask(s.shape), s, -jnp.inf)
            m = jnp.max(s, axis=-1, keepdims=True)
            p = jnp.exp(s - m)
            probs.append((m, jnp.sum(p, axis=-1, keepdims=True), p.astype(BF16)))
        return [(m, l, jnp.dot(p, blk[0], preferred_element_type=F32)) for (m, l, p), blk in zip(probs, blocks)]

    def merge(state, parts):
        m_old, l_old, acc_old = state
        m_new = m_old
        for m, _, _ in parts:
            m_new = jnp.maximum(m_new, m)
        alpha = jnp.exp(m_old - m_new)
        l = alpha * l_old
        acc = alpha * acc_old
        for m, lp, op in parts:
            w = jnp.exp(m - m_new)
            l = l + w * lp
            acc = acc + w * op
        return m_new, l, acc

    def chunk_step(chunk, state, extra_blocks=()):
        slot = chunk % PAGE_SLOTS
        wait_chunk(b, chunk, slot)
        ahead = chunk + PAGE_LOOKAHEAD
        ahead_b = jnp.where(ahead >= n_chunks, b + 1, b)
        ahead_b = jnp.where(ahead_b >= pl.num_programs(0), 0, ahead_b)
        start_chunk(ahead_b, ahead % n_chunks, ahead % PAGE_SLOTS)

        blocks = []
        for g in range(npg // PAGE_GROUP):
            pages = range(g * PAGE_GROUP, (g + 1) * PAGE_GROUP)
            blocks.append((jnp.concatenate([cbuf[slot, p].astype(BF16) for p in pages], axis=0),
                           jnp.concatenate([rbuf[slot, p].astype(BF16) for p in pages], axis=1), None))
        return merge(state, partial_softmax(blocks + list(extra_blocks)))

    def causal(shape):
        q_pos = lax.broadcasted_iota(jnp.int32, shape, 0) // N_HEADS
        return lax.broadcasted_iota(jnp.int32, shape, 1) <= q_pos

    rows = ql.shape[0]
    state = (jnp.full((rows, 1), -jnp.inf, F32), jnp.zeros((rows, 1), F32), jnp.zeros((rows, KV_LORA), F32))
    state = lax.fori_loop(0, n_chunks - 1, chunk_step, state)
    new_rows = (cnew_ref[...].astype(BF16), knew_ref[...].astype(BF16), causal)
    _, l, acc = chunk_step(n_chunks - 1, state, [new_rows])
    o_ref[...] = acc / l

    @pl.when(b == pl.num_programs(0) - 1)
    def _():
        for chunk in range(PAGE_LOOKAHEAD):
            wait_chunk(0, chunk, chunk % PAGE_SLOTS)


def _sample_attn(page_table, cache_ckv, cache_kr_t, ql, qr, cnew, knew_t):
    dec_b, n_pages = page_table.shape
    npg = PAGES_PER_STEP
    n_chunks = n_pages // npg
    slots = 4 if n_chunks % 4 == 0 else 2
    assert n_chunks % slots == 0
    rows = ql.shape[1]
    n_new = cnew.shape[1]
    per_b = lambda b, pt: (b, 0, 0)
    return pl.pallas_call(
        functools.partial(_sattn_kernel, n_pages),
        grid_spec=pltpu.PrefetchScalarGridSpec(
            num_scalar_prefetch=1,
            grid=(dec_b,),
            in_specs=[pl.BlockSpec(memory_space=pl.ANY),
                      pl.BlockSpec(memory_space=pl.ANY),
                      pl.BlockSpec((None, rows, KV_LORA), per_b),
                      pl.BlockSpec((None, rows, QK_ROPE), per_b),
                      pl.BlockSpec((None, n_new, KV_LORA), per_b),
                      pl.BlockSpec((None, QK_ROPE, n_new), per_b)],
            out_specs=pl.BlockSpec((None, rows, KV_LORA), per_b),
            scratch_shapes=[pltpu.VMEM((slots, npg, CHUNK, KV_LORA), F32),
                            pltpu.VMEM((slots, npg, QK_ROPE, CHUNK), F32),
                            pltpu.SemaphoreType.DMA((slots,)),
                            pltpu.SemaphoreType.DMA((slots,))]),
        out_shape=jax.ShapeDtypeStruct((dec_b, rows, KV_LORA), F32),
        compiler_params=_params(("arbitrary",)),
        name="sample_attn",
    )(page_table.reshape(-1), cache_ckv, cache_kr_t, ql, qr, cnew, knew_t)


def _mm_kernel(x_ref, w_ref, o_ref):
    o_ref[...] = jnp.dot(x_ref[...].astype(BF16), w_ref[...], preferred_element_type=F32).astype(o_ref.dtype)


def _matmul(x, w, out_dtype):
    m, n = x.shape[0], w.shape[1]
    full = lambda shape: pl.BlockSpec(shape, lambda i: (0,) * len(shape))
    return pl.pallas_call(
        _mm_kernel,
        grid=(1,),
        in_specs=[full(x.shape), full(w.shape)],
        out_specs=full((m, n)),
        out_shape=jax.ShapeDtypeStruct((m, n), out_dtype),
        compiler_params=_params(("arbitrary",)),
        name="matmul",
    )(x, w)


def _merge_kernel(n_prompt_tiles, xp_ref, xs_ref, sh1_ref, sc1_ref, gt1_ref, sh2_ref, sc2_ref, g1_ref, g2_ref,
                  a_ref, ot_ref, os_ref, wg_ref, woa_ref, wob_ref, wo_ref, x1_ref, *rest):
    h2_refs, yb_ref = rest[:SC_SPLIT], rest[SC_SPLIT]
    i = pl.program_id(0)
    x = jnp.where(i < n_prompt_tiles, xp_ref[...], xs_ref[...])
    h = _rms(x, g1_ref[...]) * (1.0 + sc1_ref[...]) + sh1_ref[...]
    gates = _sigmoid(jnp.dot(h.astype(BF16), wg_ref[...], preferred_element_type=F32))
    y_a = jnp.dot(a_ref[...], woa_ref[...], preferred_element_type=F32)

    @pl.when(i < n_prompt_tiles)
    def _():
        yb_ref[...] = lax.dot_general(ot_ref[...], wob_ref[...], TN_DIMS, preferred_element_type=F32)

    @pl.when(i >= n_prompt_tiles)
    def _():
        yb_ref[...] = jnp.dot(os_ref[...], wob_ref[...], preferred_element_type=F32)

    z = gates[:, :D_MODEL] * y_a + gates[:, D_MODEL:] * yb_ref[...]
    y = jnp.dot(z.astype(BF16), wo_ref[...], preferred_element_type=F32)
    x1 = x + gt1_ref[...] * y
    x1_ref[...] = x1
    for ref, piece in zip(h2_refs, _pack_rows(_rms(x1, g2_ref[...]) * (1.0 + sc2_ref[...]) + sh2_ref[...])):
        ref[...] = piece


def _merge(x_p, x_s, mod_g, g1, g2, a_all, ot, o_s, wg, woa, wob, wo, grp_of_tile, n_prompt_tiles):
    t_all = x_p.shape[0] + x_s.shape[0]
    tm = ROW_TILE
    seq = ot.shape[2]
    tpb = seq // tm
    row = lambda i: (i, 0)
    fixed2 = lambda i: (0, 0)
    mod_comps = (MOD_SHIFT1, MOD_SCALE1, MOD_GATE1, MOD_SHIFT2, MOD_SCALE2)

    def ot_map(i):
        ic = jnp.minimum(i, n_prompt_tiles - 1)
        return (ic // tpb, 0, ic % tpb)

    os_map = lambda i: (jnp.maximum(i - n_prompt_tiles, 0), 0)
    return pl.pallas_call(
        functools.partial(_merge_kernel, n_prompt_tiles),
        grid=(t_all // tm,),
        in_specs=[pl.BlockSpec((tm, D_MODEL), lambda i: (jnp.minimum(i, n_prompt_tiles - 1), 0)),
                  pl.BlockSpec((tm, D_MODEL), os_map)]
        + [_mod_spec(grp_of_tile, c) for c in mod_comps]
        + [pl.BlockSpec((1, D_MODEL), fixed2),
                  pl.BlockSpec((1, D_MODEL), fixed2),
                  pl.BlockSpec((tm, A_WIDTH), row),
                  pl.BlockSpec((None, N_HEADS * V_HEAD, tm), ot_map),
                  pl.BlockSpec((tm, N_HEADS * V_HEAD), os_map),
                  pl.BlockSpec(wg.shape, fixed2),
                  pl.BlockSpec(woa.shape, fixed2),
                  pl.BlockSpec(wob.shape, fixed2),
                  pl.BlockSpec(wo.shape, fixed2)],
        out_specs=[pl.BlockSpec((tm, D_MODEL), row)] + [pl.BlockSpec((tm, PIECE), row)] * SC_SPLIT,
        out_shape=[jax.ShapeDtypeStruct((t_all, D_MODEL), F32)]
        + [jax.ShapeDtypeStruct((t_all, PIECE), jnp.int32)] * SC_SPLIT,
        scratch_shapes=[pltpu.VMEM((tm, D_MODEL), F32)],
        compiler_params=_params(("parallel",)),
        name="merge",
    )(x_p, x_s, *([mod_g] * len(mod_comps)), g1, g2, a_all, ot, o_s, wg, woa, wob, wo)


def _first_argmax(v, rows):
    mx = jnp.max(v, axis=0, keepdims=True)
    idx = jnp.min(jnp.where(v == mx, rows, v.shape[0]), axis=0, keepdims=True)
    return mx, idx


def _route_kernel(x1_ref, sh2_ref, sc2_ref, g2_ref, whi_ref, wlo_ref, bias_ref, upper_ref,
                  idx_ref, wt_ref, rank_ref, count_ref, run_ref):
    h2 = _rms(x1_ref[...], g2_ref[...]) * (1.0 + sc2_ref[...]) + sh2_ref[...]
    hi = h2.astype(BF16)
    lo = (h2 - hi.astype(F32)).astype(BF16)
    whi = whi_ref[...]
    logits = (lax.dot_general(whi, hi, NT_DIMS, preferred_element_type=F32)
              + lax.dot_general(whi, lo, NT_DIMS, preferred_element_type=F32)
              + lax.dot_general(wlo_ref[...], hi, NT_DIMS, preferred_element_type=F32))
    scores = _sigmoid(logits)
    sel = scores + bias_ref[...]
    tm = sel.shape[1]
    neg = -jnp.inf
    rows_g = lax.broadcasted_iota(jnp.int32, (GROUP_SIZE, tm), 0)
    gscore = []
    for g in range(N_GROUPS):
        blk = sel[g * GROUP_SIZE:(g + 1) * GROUP_SIZE]
        m1, i1 = _first_argmax(blk, rows_g)
        m2 = jnp.max(jnp.where(rows_g == i1, neg, blk), axis=0, keepdims=True)
        gscore.append(m1 + m2)
    gs = jnp.concatenate(gscore, axis=0)
    rows_8 = lax.broadcasted_iota(jnp.int32, gs.shape, 0)
    chosen = jnp.zeros(gs.shape, jnp.int32)
    for _ in range(TOPK_GROUPS):
        _, gi = _first_argmax(gs, rows_8)
        hit = rows_8 == gi
        chosen = jnp.where(hit, 1, chosen)
        gs = jnp.where(hit, neg, gs)
    cand = jnp.concatenate(
        [jnp.where(chosen[g:g + 1] > 0, sel[g * GROUP_SIZE:(g + 1) * GROUP_SIZE], neg) for g in range(N_GROUPS)],
        axis=0)
    rows_e = lax.broadcasted_iota(jnp.int32, cand.shape, 0)
    idxs, wts, hits = [], [], []
    for _ in range(TOP_K):
        _, ei = _first_argmax(cand, rows_e)
        hit = rows_e == ei
        idxs.append(ei)
        hits.append(hit)
        wts.append(jnp.sum(jnp.where(hit, scores, 0.0), axis=0, keepdims=True))
        cand = jnp.where(hit, neg, cand)
    w = jnp.concatenate(wts, axis=0)
    idx_ref[...] = jnp.concatenate(idxs, axis=0)
    wt_ref[...] = w / jnp.sum(w, axis=0, keepdims=True) * ROUTED_SCALE

    @pl.when(pl.program_id(0) == 0)
    def _():
        run_ref[...] = jnp.zeros(run_ref.shape, F32)

    run = run_ref[...]
    ones = jnp.ones((tm, LANES), BF16)
    onehots = [jnp.where(hit, 1.0, 0.0).astype(BF16) for hit in hits]
    befores = [jnp.dot(oh, upper_ref[...], preferred_element_type=F32) for oh in onehots]
    totals = [jnp.dot(oh, ones, preferred_element_type=F32) for oh in onehots]
    ranks = []
    for hit, before, total in zip(hits, befores, totals):
        prior = jnp.concatenate([run] * (tm // LANES), axis=1)
        ranks.append(jnp.sum(jnp.where(hit, before + prior, 0.0), axis=0, keepdims=True))
        run = run + total
    rank_ref[...] = jnp.concatenate(ranks, axis=0).astype(jnp.int32)
    run_ref[...] = run
    count_ref[...] = run


def _route(x1_all, mod_g, g2, whi, wlo, bias_col, grp_of_tile):
    t_all = x1_all.shape[0]
    tm = ROW_TILE
    fixed2 = lambda i: (0, 0)
    col = lambda i: (0, i)
    upper = (lax.broadcasted_iota(jnp.int32, (tm, tm), 0) < lax.broadcasted_iota(jnp.int32, (tm, tm), 1)).astype(BF16)
    return pl.pallas_call(
        _route_kernel,
        grid=(t_all // tm,),
        in_specs=[pl.BlockSpec((tm, D_MODEL), lambda i: (i, 0)),
                  _mod_spec(grp_of_tile, MOD_SHIFT2),
                  _mod_spec(grp_of_tile, MOD_SCALE2),
                  pl.BlockSpec((1, D_MODEL), fixed2),
                  pl.BlockSpec(whi.shape, fixed2),
                  pl.BlockSpec(wlo.shape, fixed2),
                  pl.BlockSpec(bias_col.shape, fixed2),
                  pl.BlockSpec((tm, tm), fixed2)],
        out_specs=[pl.BlockSpec((TOP_K, tm), col), pl.BlockSpec((TOP_K, tm), col), pl.BlockSpec((TOP_K, tm), col),
                   pl.BlockSpec((N_EXPERTS, LANES), fixed2)],
        out_shape=[jax.ShapeDtypeStruct((TOP_K, t_all), jnp.int32),
                   jax.ShapeDtypeStruct((TOP_K, t_all), F32),
                   jax.ShapeDtypeStruct((TOP_K, t_all), jnp.int32),
                   jax.ShapeDtypeStruct((N_EXPERTS, LANES), F32)],
        scratch_shapes=[pltpu.VMEM((N_EXPERTS, LANES), F32)],
        compiler_params=_params(("arbitrary",)),
        name="route",
    )(x1_all, mod_g, mod_g, g2, whi, wlo, bias_col, upper)


def _moe_kernel(be_ref, nv_ref, next_ref, *refs):
    x_refs = refs[:SC_SPLIT]
    wgu_hbm, wdn_hbm = refs[SC_SPLIT:SC_SPLIT + 2]
    y_refs = refs[SC_SPLIT + 2:2 * SC_SPLIT + 2]
    gu_f32, dn_f32, gu_bf, dn_bf, sem, run_ref = refs[2 * SC_SPLIT + 2:]
    i = pl.program_id(0)
    expert = be_ref[i]
    prev = be_ref[jnp.maximum(i - 1, 0)]

    def weight_copies(e, slot):
        return (pltpu.make_async_copy(wgu_hbm.at[e], gu_f32.at[slot], sem.at[0, slot]),
                pltpu.make_async_copy(wdn_hbm.at[e], dn_f32.at[slot], sem.at[1, slot]))

    @pl.when(i == 0)
    def _():
        run_ref[0] = 0
        for cp in weight_copies(expert, 0):
            cp.start(priority=1)

    @pl.when(i < nv_ref[0])
    def _():
        @pl.when((i == 0) | (expert != prev))
        def _():
            run = run_ref[0] + jnp.where(i == 0, 0, 1)
            run_ref[0] = run
            slot = run % 2
            for cp in weight_copies(expert, slot):
                cp.wait()
            following = next_ref[expert]

            @pl.when(following < N_EXPERTS)
            def _():
                for cp in weight_copies(following, 1 - slot):
                    cp.start(priority=1)

            gu_bf[...] = gu_f32[slot].astype(BF16)
            dn_bf[...] = dn_f32[slot].astype(BF16)

        subs = [slice(s * MOE_SUB, (s + 1) * MOE_SUB) for s in range(MOE_BLOCK // MOE_SUB)]
        gus = [jnp.dot(_unpack_rows([r[rows, :] for r in x_refs]).astype(BF16), gu_bf[...],
                       preferred_element_type=F32) for rows in subs]
        hids = [(_silu(gu[:, :EXPERT_DIM]) * gu[:, EXPERT_DIM:]).astype(BF16) for gu in gus]
        ys = [jnp.dot(hid, dn_bf[...], preferred_element_type=F32) for hid in hids]
        for rows, y in zip(subs, ys):
            for ref, piece in zip(y_refs, _pack_rows(y)):
                ref[rows, :] = piece


def _moe(block_e, n_valid, next_expert, x_sorted, w_gu, w_dn):
    n_slots = x_sorted[0].shape[0]
    blk = MOE_BLOCK
    n_blocks = n_slots // blk
    rows = lambda i, be, nv, nx: (jnp.minimum(i, nv[0] - 1), 0)
    return pl.pallas_call(
        _moe_kernel,
        grid_spec=pltpu.PrefetchScalarGridSpec(
            num_scalar_prefetch=3,
            grid=(n_blocks,),
            in_specs=[pl.BlockSpec((blk, PIECE), rows)] * SC_SPLIT
            + [pl.BlockSpec(memory_space=pl.ANY), pl.BlockSpec(memory_space=pl.ANY)],
            out_specs=[pl.BlockSpec((blk, PIECE), rows)] * SC_SPLIT,
            scratch_shapes=[pltpu.VMEM((2, D_MODEL, 2 * EXPERT_DIM), F32),
                            pltpu.VMEM((2, EXPERT_DIM, D_MODEL), F32),
                            pltpu.VMEM((D_MODEL, 2 * EXPERT_DIM), BF16),
                            pltpu.VMEM((EXPERT_DIM, D_MODEL), BF16),
                            pltpu.SemaphoreType.DMA((2, 2)),
                            pltpu.SMEM((1,), jnp.int32)]),
        out_shape=[jax.ShapeDtypeStruct((n_slots, PIECE), jnp.int32)] * SC_SPLIT,
        compiler_params=_params(("arbitrary",)),
        name="moe",
    )(block_e, n_valid, next_expert, *x_sorted, w_gu, w_dn)


def _dest_kernel(e_ref, rank_ref, start_ref, dest_ref):
    e = e_ref[...]
    rows = lax.broadcasted_iota(jnp.int32, (N_EXPERTS, e.shape[1]), 0)
    start = jnp.sum(jnp.where(rows == e, start_ref[...], 0), axis=0, keepdims=True)
    dest_ref[...] = start + rank_ref[...]


def _dest(e_flat, rank, pad_start_col):
    n_assign = e_flat.shape[1]
    n = RANK_TILE
    tile = pl.BlockSpec((1, n), lambda i: (0, i))
    return pl.pallas_call(
        _dest_kernel,
        grid=(n_assign // n,),
        in_specs=[tile, tile, pl.BlockSpec((N_EXPERTS, 1), lambda i: (0, 0))],
        out_specs=tile,
        out_shape=jax.ShapeDtypeStruct((1, n_assign), jnp.int32),
        compiler_params=_params(("parallel",)),
        name="dest",
    )(e_flat, rank, pad_start_col)


def _sc_mesh():
    return plsc.VectorSubcoreMesh(core_axis_name="core", subcore_axis_name="subcore")


def _sc_scatter_rows(src, dest, n_slots):
    n_src, width = src.shape
    n_assign = dest.shape[1]
    src_blocks = n_src // SC_WINDOW

    @functools.partial(pl.kernel, out_type=jax.ShapeDtypeStruct((n_slots, width), src.dtype),
                       mesh=_sc_mesh(), scratch_types=[])
    def scatter(src_hbm, dest_hbm, out_hbm):
        def body(rows_vmem, dest_vmem):
            pltpu.sync_copy(rows_vmem, out_hbm.at[dest_vmem.at[0]])

        pltpu.emit_pipeline(
            body,
            grid=(n_assign // SC_WINDOW,),
            in_specs=[pl.BlockSpec((SC_WINDOW, width), lambda i: (i % src_blocks, 0)),
                      pl.BlockSpec((1, SC_WINDOW), lambda i: (0, i))],
            out_specs=[],
            core_axis_name=("core", "subcore"),
            dimension_semantics=(pltpu.PARALLEL,),
        )(src_hbm, dest_hbm)

    return scatter(src, dest)


def _sc_gather_rows(table, idx):
    width = table.shape[1]
    n_assign = idx.shape[1]

    @functools.partial(pl.kernel, out_type=jax.ShapeDtypeStruct((n_assign, width), table.dtype),
                       mesh=_sc_mesh(), scratch_types=[])
    def gather(table_hbm, idx_hbm, out_hbm):
        def body(idx_vmem, rows_vmem):
            pltpu.sync_copy(table_hbm.at[idx_vmem.at[0]], rows_vmem)

        pltpu.emit_pipeline(
            body,
            grid=(n_assign // SC_WINDOW,),
            in_specs=[pl.BlockSpec((1, SC_WINDOW), lambda i: (0, i))],
            out_specs=[pl.BlockSpec((SC_WINDOW, width), lambda i: (i, 0))],
            core_axis_name=("core", "subcore"),
            dimension_semantics=(pltpu.PARALLEL,),
        )(idx_hbm, out_hbm)

    return gather(table, idx)


def _final_kernel(n_prompt_tiles, x1_ref, *refs):
    h2_refs = refs[:SC_SPLIT]
    yg_refs = refs[SC_SPLIT:2 * SC_SPLIT]
    wt_ref, gt2_ref, wsg_ref, wsd_ref, gf_ref, op_ref, os_ref = refs[2 * SC_SPLIT:]
    is_prompt = pl.program_id(0) < n_prompt_tiles
    h2 = _unpack_rows([r[...] for r in h2_refs]).astype(BF16)
    gu = jnp.dot(h2, wsg_ref[...], preferred_element_type=F32)
    hid = _silu(gu[:, :SHARED_DIM]) * gu[:, SHARED_DIM:]
    y = jnp.dot(hid.astype(BF16), wsd_ref[...], preferred_element_type=F32)
    wt = wt_ref[...]
    for k in range(TOP_K):
        y = y + wt[:, k:k + 1] * _unpack_rows([r[k] for r in yg_refs])
    x2 = x1_ref[...] + gt2_ref[...] * y
    out = _rms(x2, gf_ref[...])

    @pl.when(is_prompt)
    def _():
        op_ref[...] = out

    @pl.when(jnp.logical_not(is_prompt))
    def _():
        os_ref[...] = out


def _final(x1_all, h2_all, y_gath, wt_rows, mod_g, wsg, wsd, gf, grp_of_tile, n_prompt_tiles):
    t_all = x1_all.shape[0]
    tm = ROW_TILE
    t_p = n_prompt_tiles * tm
    row = lambda i: (i, 0)
    fixed2 = lambda i: (0, 0)
    return pl.pallas_call(
        functools.partial(_final_kernel, n_prompt_tiles),
        grid=(t_all // tm,),
        in_specs=[pl.BlockSpec((tm, D_MODEL), row)]
        + [pl.BlockSpec((tm, PIECE), row)] * SC_SPLIT
        + [pl.BlockSpec((TOP_K, tm, PIECE), lambda i: (0, i, 0))] * SC_SPLIT
        + [pl.BlockSpec((tm, TOP_K), row),
           _mod_spec(grp_of_tile, MOD_GATE2),
           pl.BlockSpec(wsg.shape, fixed2),
           pl.BlockSpec(wsd.shape, fixed2),
           pl.BlockSpec((1, D_MODEL), fixed2)],
        out_specs=[pl.BlockSpec((tm, D_MODEL), lambda i: (jnp.minimum(i, n_prompt_tiles - 1), 0)),
                   pl.BlockSpec((tm, D_MODEL), lambda i: (jnp.maximum(i - n_prompt_tiles, 0), 0))],
        out_shape=[jax.ShapeDtypeStruct((t_p, D_MODEL), F32),
                   jax.ShapeDtypeStruct((t_all - t_p, D_MODEL), F32)],
        compiler_params=_params(("arbitrary",)),
        name="final",
    )(x1_all, *h2_all, *y_gath, wt_rows, mod_g, wsg, wsd, gf)


def _rope_tables(pos):
    freqs = ROPE_THETA ** (-jnp.arange(ROPE_HALF, dtype=F32) / ROPE_HALF)
    ang = pos.astype(F32)[:, None] * freqs
    return jnp.cos(ang), jnp.sin(ang)


def kernel(x_prompt, x_sample, cache_ckv, cache_krope, page_table, c_prompt, c_sample, w_ada, b_ada, norm1_g, norm2_g, w_in, a_vnorm_g, w_spatial, b_spatial, w_out_a, q_norm_g, w_uq, kv_norm_g, w_uk, w_uv, w_out_b, w_o, w_router, router_bias, w_exp_gu, w_exp_down, w_sh_gu, w_sh_down, final_norm_g):
    batch, seq, _ = x_prompt.shape
    dec_b, dec_s, _ = x_sample.shape
    n_pages = page_table.shape[1]
    n_past = n_pages * CHUNK
    t_p, t_s = batch * seq, dec_b * dec_s
    t_all = t_p + t_s
    tm = ROW_TILE
    assert seq % ATTN_Q_STEP == 0 and t_s % tm == 0 and CHUNK % dec_s == 0 and n_pages % PAGES_PER_STEP == 0
    assert (t_all * TOP_K) % RANK_TILE == 0 and w_ada.shape[0] == 1
    n_prompt_tiles = t_p // tm
    tiles_per_b = seq // tm

    def grp_of_tile(i):
        return jnp.minimum(i // tiles_per_b, batch) + jnp.maximum(i - n_prompt_tiles, 0)

    n_c = batch + dec_b
    c_rows = -(-n_c // 8) * 8
    c_all = jnp.concatenate([c_prompt, c_sample, jnp.zeros((c_rows - n_c, D_MODEL), F32)], axis=0)
    mod = _ada(c_all, w_ada[0].astype(BF16), b_ada[0][None, :])
    mod_p = jnp.broadcast_to(mod[:batch, None, :], (batch, tm, 6 * D_MODEL))
    mod_s = jnp.repeat(mod[batch:n_c], dec_s, axis=0).reshape(t_s // tm, tm, 6 * D_MODEL)
    mod_g = jnp.concatenate([mod_p, mod_s], axis=0)

    o_u, o_v, o_cq, o_ckv, o_kr = A_WIDTH, 2 * A_WIDTH, 2 * A_WIDTH + Q_LORA, 2 * A_WIDTH + Q_LORA + KV_LORA, \
        2 * A_WIDTH + Q_LORA + KV_LORA + QK_ROPE
    win = w_in[0]
    w5 = jnp.concatenate([win[:, :o_kr], jnp.zeros((D_MODEL, IN_END - IN_KR - QK_ROPE), F32)], axis=1).astype(BF16)
    wg = win[:, o_kr:].astype(BF16)
    row1 = lambda v: v.reshape(1, -1)
    tri = jnp.tril(jnp.ones((CHUNK, CHUNK), F32))
    wc_p = w_spatial[0] * tri
    per = CHUNK // dec_s
    small = (w_spatial[0] * tri)[:, :dec_s, :dec_s]
    wc_s = jnp.einsum('ab,gts->gatbs', jnp.eye(per, dtype=F32), small).reshape(A_GROUPS, CHUNK, CHUNK)
    wc = jnp.stack([wc_p, wc_s]).astype(BF16)
    bias_p = jnp.repeat(b_spatial[0].T, A_GROUP_DIM, axis=1)
    bias_s = jnp.tile(bias_p[:dec_s], (per, 1))
    bc = jnp.stack([bias_p, bias_s])

    pos_p = jnp.arange(seq)
    pos_s = n_past + jnp.arange(dec_s)
    cos_p, sin_p = _rope_tables(pos_p)
    cos_s, sin_s = _rope_tables(pos_s)
    cos_rows = jnp.concatenate([jnp.tile(cos_p, (batch, 1)), jnp.tile(cos_s, (dec_b, 1))], axis=0)
    sin_rows = jnp.concatenate([jnp.tile(sin_p, (batch, 1)), jnp.tile(sin_s, (dec_b, 1))], axis=0)
    lane_pad = jnp.zeros((t_all, LANES - QK_ROPE), F32)
    cos_kr = jnp.concatenate([cos_rows, cos_rows, lane_pad], axis=1)
    sin_kr = jnp.concatenate([-sin_rows, sin_rows, lane_pad], axis=1)

    x_p, x_s = x_prompt.reshape(t_p, D_MODEL), x_sample.reshape(t_s, D_MODEL)
    g1, g2 = row1(norm1_g[0]), row1(norm2_g[0])
    a_all, cq_all, ckv_p, kr_p, ckv_s, kr_s, v_s = _in_proj(
        x_p, x_s, mod_g, g1, w5, row1(a_vnorm_g[0]), row1(q_norm_g[0]), row1(kv_norm_g[0]),
        cos_kr, sin_kr, wc, bc, grp_of_tile, n_prompt_tiles)

    wuq = w_uq[0]
    wuqt = jnp.pad(wuq, ((0, 0), (0, 0), (0, HEAD_PAD - QK_NOPE - QK_ROPE))).reshape(Q_LORA, -1).T.astype(BF16)
    wukp = jnp.pad(w_uk[0], ((0, 0), (0, 0), (0, HEAD_PAD - QK_NOPE))).reshape(KV_LORA, -1).astype(BF16)
    place_h = jnp.pad(jnp.eye(QK_ROPE, dtype=F32), ((0, 0), (QK_NOPE, HEAD_PAD - QK_NOPE - QK_ROPE)))
    place = jnp.tile(place_h, (1, N_HEADS)).astype(BF16)
    wuvt = w_uv[0].reshape(KV_LORA, -1).T.astype(BF16)
    qt, k, vt = _qkv(cq_all, ckv_p, kr_p, wuqt, wukp, place, wuvt, cos_p.T, sin_p.T, batch, seq)
    ot = _attn(qt, k, vt)

    wq_s = jnp.concatenate([wuq[:, :, :QK_NOPE].reshape(Q_LORA, -1),
                            wuq[:, :, QK_NOPE:QK_NOPE + ROPE_HALF].reshape(Q_LORA, -1),
                            wuq[:, :, QK_NOPE + ROPE_HALF:].reshape(Q_LORA, -1)], axis=1).astype(BF16)
    eye_h = jnp.eye(N_HEADS, dtype=F32)
    wuk_blk = jnp.einsum('rhd,hg->hdgr', w_uk[0], eye_h).reshape(N_HEADS * QK_NOPE, N_HEADS * KV_LORA).astype(BF16)
    wuv_blk = jnp.einsum('rhd,hg->hrgd', w_uv[0], eye_h).reshape(N_HEADS * KV_LORA, N_HEADS * V_HEAD).astype(BF16)
    cos_sq = jnp.tile(jnp.tile(cos_s, (1, N_HEADS)), (dec_b, 1))
    sin_sq = jnp.tile(jnp.tile(sin_s, (1, N_HEADS)), (dec_b, 1))
    ql, r1, r2 = _sample_q(cq_all[t_p:], wq_s, wuk_blk, cos_sq, sin_sq)
    rows = dec_s * N_HEADS
    ql3 = ql.reshape(dec_b, rows, KV_LORA)
    qr3 = jnp.concatenate([r1.reshape(dec_b, dec_s, N_HEADS, ROPE_HALF),
                           r2.reshape(dec_b, dec_s, N_HEADS, ROPE_HALF)], axis=-1).reshape(dec_b, rows, QK_ROPE)
    new_pad = ((0, 0), (0, 8 - dec_s), (0, 0))
    ckv_new = jnp.pad(ckv_s.reshape(dec_b, dec_s, KV_LORA), new_pad)
    kr_new_t = jnp.swapaxes(jnp.pad(kr_s.reshape(dec_b, dec_s, QK_ROPE), new_pad), 1, 2)
    cache_kr_t = jnp.swapaxes(cache_krope[0], 1, 2)
    o_lat = _sample_attn(page_table, cache_ckv[0], cache_kr_t, ql3, qr3, ckv_new, kr_new_t)
    o_s = _matmul(o_lat.reshape(t_s, N_HEADS * KV_LORA), wuv_blk, BF16)

    x1_all, *h2_all = _merge(x_p, x_s, mod_g, g1, g2, a_all, ot, o_s, wg,
                            w_out_a[0].astype(BF16), w_out_b[0].astype(BF16), w_o[0].astype(BF16),
                            grp_of_tile, n_prompt_tiles)

    wr_t = w_router[0].T
    whi = wr_t.astype(BF16)
    wlo = (wr_t - whi.astype(F32)).astype(BF16)
    idx_t, wt_t, rank_t, counts = _route(x1_all, mod_g, g2, whi, wlo, router_bias[0].reshape(-1, 1),
                                         grp_of_tile)
    n_assign = t_all * TOP_K
    e_flat = idx_t.reshape(1, n_assign)
    counts = counts[:, 0].astype(jnp.int32)
    blk = MOE_BLOCK
    padded = (counts + blk - 1) // blk * blk
    pad_end = jnp.cumsum(padded)
    pad_start = pad_end - padded
    dest = _dest(e_flat, rank_t.reshape(1, n_assign), pad_start.reshape(-1, 1))
    n_blocks = -(-n_assign // blk) + N_EXPERTS
    n_slots = n_blocks * blk
    first_row = jnp.arange(n_blocks, dtype=jnp.int32) * blk
    block_e = jnp.minimum(jnp.sum(pad_end[None, :] <= first_row[:, None], axis=1), N_EXPERTS - 1).astype(jnp.int32)
    n_valid = (pad_end[-1] // blk).astype(jnp.int32).reshape(1)
    experts = jnp.arange(N_EXPERTS, dtype=jnp.int32)
    owners = jnp.where(padded > 0, experts, N_EXPERTS)
    later_owner = lax.cummin(owners, reverse=True)
    next_expert = jnp.concatenate([later_owner[1:], jnp.full((1,), N_EXPERTS, jnp.int32)])
    x_sorted = [_sc_scatter_rows(h, dest, n_slots) for h in h2_all]
    y_sorted = _moe(block_e, n_valid, next_expert, x_sorted, w_exp_gu[0], w_exp_down[0])
    y_gath = [_sc_gather_rows(y, dest).reshape(TOP_K, t_all, PIECE) for y in y_sorted]

    y_p, y_s = _final(x1_all, h2_all, y_gath, wt_t.T, mod_g, w_sh_gu[0].astype(BF16), w_sh_down[0].astype(BF16),
                      row1(final_norm_g), grp_of_tile, n_prompt_tiles)

    y_prompt = y_p.reshape(batch, seq, D_MODEL)
    y_sample = y_s.reshape(dec_b, dec_s, D_MODEL)
    new_ckv_prompt = ckv_p.reshape(1, batch, seq, KV_LORA)
    new_krope_prompt = kr_p.reshape(1, batch, seq, QK_ROPE)
    new_ckv_sample = ckv_s.reshape(1, dec_b, dec_s, KV_LORA)
    new_krope_sample = kr_s.reshape(1, dec_b, dec_s, QK_ROPE)
    new_chunk_v_sample = v_s.reshape(1, dec_b, dec_s, A_WIDTH)
    return (y_prompt, y_sample, new_ckv_prompt, new_krope_prompt, new_ckv_sample, new_krope_sample,
            new_chunk_v_sample)
```

```python
import functools

import jax
import jax.numpy as jnp
from jax import lax
from jax.experimental import pallas as pl
from jax.experimental.pallas import tpu as pltpu
from jax.experimental.pallas import tpu_sc as plsc

F32 = jnp.float32
BF16 = jnp.bfloat16

D_MODEL = 1024
A_WIDTH = D_MODEL // 2
A_GROUPS = 8
A_GROUP_DIM = A_WIDTH // A_GROUPS
CHUNK = 128
N_HEADS = 8
QK_NOPE = 64
QK_ROPE = 32
ROPE_HALF = QK_ROPE // 2
V_HEAD = 64
Q_LORA = 384
KV_LORA = 256
ROPE_THETA = 10000.0
ATTN_SCALE = (QK_NOPE + QK_ROPE) ** -0.5
N_EXPERTS = 256
TOP_K = 8
N_GROUPS = 8
GROUP_SIZE = N_EXPERTS // N_GROUPS
TOPK_GROUPS = 4
EXPERT_DIM = 256
SHARED_DIM = 256
ROUTED_SCALE = 2.5
EPS = 1e-6

LANES = 128
HEAD_PAD = 128
ROW_TILE = 256
ATTN_KV_TILE = 256
ATTN_Q_STEP = 512
ATTN_HEADS_PER_STEP = 4
LOG2_E = 1.4426950408889634
RANK_TILE = 512
MOE_BLOCK = 256
MOE_SUB = 128
PAGES_PER_STEP = 32
PAGE_GROUP = 8
V_ROWS = V_HEAD + 16
SC_WINDOW = 128
SC_SPLIT = 2
PIECE = D_MODEL // 2 // SC_SPLIT
VMEM_LIMIT = 48 * 1024 * 1024

IN_U, IN_V, IN_CQ, IN_CKV, IN_KR, IN_END = 0, 512, 1024, 1408, 1664, 1792

NT_DIMS = (((1,), (1,)), ((), ()))
TN_DIMS = (((0,), (0,)), ((), ()))


def _params(sem, vmem=VMEM_LIMIT):
    return pltpu.CompilerParams(dimension_semantics=sem, vmem_limit_bytes=vmem)


def _rms(x, g):
    return x * lax.rsqrt(jnp.mean(x * x, axis=-1, keepdims=True) + EPS) * g


def _gelu(x):
    return 0.5 * x * (1.0 + jnp.tanh(0.7978845608028654 * (x + 0.044715 * (x * x * x))))


def _sigmoid(x):
    return 1.0 / (1.0 + jnp.exp(-x))


def _silu(x):
    return x * _sigmoid(x)


def _pack_rows(x):
    pieces = []
    for c in range(SC_SPLIT):
        lo = x[:, 2 * c * PIECE:(2 * c + 1) * PIECE].astype(BF16).astype(F32)
        hi = x[:, (2 * c + 1) * PIECE:(2 * c + 2) * PIECE].astype(BF16).astype(F32)
        pieces.append(lax.bitcast_convert_type(hi, jnp.int32)
                      | lax.shift_right_logical(lax.bitcast_convert_type(lo, jnp.int32), 16))
    return pieces


def _unpack_rows(pieces):
    cols = []
    for p in pieces:
        cols.append(lax.bitcast_convert_type(lax.shift_left(p, 16), F32))
        cols.append(lax.bitcast_convert_type(p & jnp.int32(-65536), F32))
    return jnp.concatenate(cols, axis=1)


def _ada_kernel(c_ref, w_ref, b_ref, o_ref):
    s = _silu(c_ref[...]).astype(BF16)
    o_ref[...] = jnp.dot(s, w_ref[...].astype(BF16), preferred_element_type=F32) + b_ref[...]


def _ada(c, w, b):
    rows, n = c.shape[0], w.shape[1]
    tn = 1536
    return pl.pallas_call(
        _ada_kernel,
        grid=(n // tn,),
        in_specs=[pl.BlockSpec((rows, D_MODEL), lambda j: (0, 0)),
                  pl.BlockSpec((D_MODEL, tn), lambda j: (0, j)),
                  pl.BlockSpec((1, tn), lambda j: (0, j))],
        out_specs=pl.BlockSpec((rows, tn), lambda j: (0, j)),
        out_shape=jax.ShapeDtypeStruct((rows, n), F32),
        compiler_params=_params(("parallel",)),
        name="ada",
    )(c, w, b)


def _in_kernel(n_prompt_tiles, xp_ref, xs_ref, sh_ref, sc_ref, g1_ref, w_ref, vg_ref, qg_ref, kg_ref,
               cos_ref, sin_ref, wc_ref, bc_ref,
               a_ref, cq_ref, ckvp_ref, krp_ref, ckvs_ref, krs_ref, vs_ref):
    tm = xp_ref.shape[0]
    is_prompt = pl.program_id(0) < n_prompt_tiles
    x = jnp.where(is_prompt, xp_ref[...], xs_ref[...])
    h = _rms(x, g1_ref[...]) * (1.0 + sc_ref[...]) + sh_ref[...]
    z = jnp.dot(h.astype(BF16), w_ref[...], preferred_element_type=F32)
    u = _gelu(z[:, IN_U:IN_V])
    v = _rms(_gelu(z[:, IN_V:IN_CQ]), vg_ref[...])
    cq_ref[...] = _rms(z[:, IN_CQ:IN_CKV], qg_ref[...]).astype(BF16)
    ckv = _rms(z[:, IN_CKV:IN_KR], kg_ref[...])
    kr = z[:, IN_KR:IN_END]
    lane = lax.broadcasted_iota(jnp.int32, kr.shape, 1)
    swapped = jnp.where(lane < ROPE_HALF, pltpu.roll(kr, LANES - ROPE_HALF, 1), pltpu.roll(kr, ROPE_HALF, 1))
    kr_rot = (kr * cos_ref[...] + swapped * sin_ref[...])[:, :QK_ROPE]

    @pl.when(is_prompt)
    def _():
        ckvp_ref[...] = ckv
        krp_ref[...] = kr_rot

    @pl.when(jnp.logical_not(is_prompt))
    def _():
        ckvs_ref[...] = ckv
        krs_ref[...] = kr_rot
        vs_ref[...] = v

    vb = v.astype(BF16)
    half = A_WIDTH // 2
    grp_of_lane = lax.broadcasted_iota(jnp.int32, (CHUNK, half), 1) // A_GROUP_DIM
    for ci in range(tm // CHUNK):
        rows = slice(ci * CHUNK, (ci + 1) * CHUNK)
        parts = []
        for q in range(2):
            vq = vb[rows, q * half:(q + 1) * half]
            acc = None
            for gg in range(A_GROUPS // 2):
                vm = jnp.where(grp_of_lane == gg, vq, jnp.zeros_like(vq))
                part = jnp.dot(wc_ref[q * (A_GROUPS // 2) + gg], vm, preferred_element_type=F32)
                acc = part if acc is None else acc + part
            parts.append(acc)
        sp = jnp.concatenate(parts, axis=1) + bc_ref[...]
        a_ref[rows, :] = (u[rows, :] * sp).astype(BF16)


MOD_SHIFT1, MOD_SCALE1, MOD_GATE1, MOD_SHIFT2, MOD_SCALE2, MOD_GATE2 = range(6)


def _mod_spec(grp_of_tile, component):
    return pl.BlockSpec((None, ROW_TILE, D_MODEL), lambda i: (grp_of_tile(i), 0, component))


def _in_proj(x_p, x_s, mod_g, g1, w5, vg, qg, kg, cos_kr, sin_kr, wc, bc, grp_of_tile, n_prompt_tiles):
    t_p, t_s = x_p.shape[0], x_s.shape[0]
    t_all = t_p + t_s
    tm = ROW_TILE
    n_tiles = t_all // tm
    row = lambda i: (i, 0)
    prow = lambda i: (jnp.minimum(i, n_prompt_tiles - 1), 0)
    srow = lambda i: (jnp.maximum(i - n_prompt_tiles, 0), 0)
    fixed2 = lambda i: (0, 0)
    kind = lambda i: (jnp.where(i < n_prompt_tiles, 0, 1), 0, 0)
    kind4 = lambda i: (jnp.where(i < n_prompt_tiles, 0, 1), 0, 0, 0)
    return pl.pallas_call(
        functools.partial(_in_kernel, n_prompt_tiles),
        grid=(n_tiles,),
        in_specs=[pl.BlockSpec((tm, D_MODEL), prow),
                  pl.BlockSpec((tm, D_MODEL), srow),
                  _mod_spec(grp_of_tile, MOD_SHIFT1),
                  _mod_spec(grp_of_tile, MOD_SCALE1),
                  pl.BlockSpec((1, D_MODEL), fixed2),
                  pl.BlockSpec((D_MODEL, IN_END), fixed2),
                  pl.BlockSpec((1, A_WIDTH), fixed2),
                  pl.BlockSpec((1, Q_LORA), fixed2),
                  pl.BlockSpec((1, KV_LORA), fixed2),
                  pl.BlockSpec((tm, LANES), row),
                  pl.BlockSpec((tm, LANES), row),
                  pl.BlockSpec((None, A_GROUPS, CHUNK, CHUNK), kind4),
                  pl.BlockSpec((None, CHUNK, A_WIDTH), kind)],
        out_specs=[pl.BlockSpec((tm, A_WIDTH), row),
                   pl.BlockSpec((tm, Q_LORA), row),
                   pl.BlockSpec((tm, KV_LORA), prow),
                   pl.BlockSpec((tm, QK_ROPE), prow),
                   pl.BlockSpec((tm, KV_LORA), srow),
                   pl.BlockSpec((tm, QK_ROPE), srow),
                   pl.BlockSpec((tm, A_WIDTH), srow)],
        out_shape=[jax.ShapeDtypeStruct((t_all, A_WIDTH), BF16),
                   jax.ShapeDtypeStruct((t_all, Q_LORA), BF16),
                   jax.ShapeDtypeStruct((t_p, KV_LORA), F32),
                   jax.ShapeDtypeStruct((t_p, QK_ROPE), F32),
                   jax.ShapeDtypeStruct((t_s, KV_LORA), F32),
                   jax.ShapeDtypeStruct((t_s, QK_ROPE), F32),
                   jax.ShapeDtypeStruct((t_s, A_WIDTH), F32)],
        compiler_params=_params(("arbitrary",)),
        name="in_proj",
    )(x_p, x_s, mod_g, mod_g, g1, w5, vg, qg, kg, cos_kr, sin_kr, wc, bc)


def _qkv_kernel(cq_ref, ckv_ref, kr_ref, wuqt_ref, wukp_ref, place_ref, wuvt_ref, cos_ref, sin_ref,
                qt_ref, k_ref, vt_ref):
    tm = cq_ref.shape[0]
    ckv = ckv_ref[...].astype(BF16)
    kr = kr_ref[...].astype(BF16)
    qt = lax.dot_general(wuqt_ref[...], cq_ref[...], NT_DIMS, preferred_element_type=F32)
    c = cos_ref[...]
    s = sin_ref[...]
    pad = jnp.zeros((HEAD_PAD - QK_NOPE - QK_ROPE, tm), F32)
    for h in range(N_HEADS):
        blk = qt[h * HEAD_PAD:(h + 1) * HEAD_PAD]
        x1 = blk[QK_NOPE:QK_NOPE + ROPE_HALF]
        x2 = blk[QK_NOPE + ROPE_HALF:QK_NOPE + QK_ROPE]
        full = jnp.concatenate([blk[:QK_NOPE], x1 * c - x2 * s, x1 * s + x2 * c, pad], axis=0)
        qt_ref[h] = (full * (ATTN_SCALE * LOG2_E)).astype(BF16)
    k = (jnp.dot(ckv, wukp_ref[...], preferred_element_type=F32)
         + jnp.dot(kr, place_ref[...], preferred_element_type=F32))
    for h in range(N_HEADS):
        k_ref[h] = k[:, h * HEAD_PAD:(h + 1) * HEAD_PAD].astype(BF16)
    vt = lax.dot_general(wuvt_ref[...], ckv, NT_DIMS, preferred_element_type=F32)
    extra = V_ROWS - V_HEAD
    ones_row = jnp.where(lax.broadcasted_iota(jnp.int32, (extra, tm), 0) == 0, 1.0, 0.0)
    for h in range(N_HEADS):
        vt_ref[h] = jnp.concatenate([vt[h * V_HEAD:(h + 1) * V_HEAD], ones_row], axis=0).astype(BF16)


def _qkv(cq_all, ckv_all, kr_all, wuqt, wukp, place, wuvt, cos_t, sin_t, batch, seq):
    tm = ATTN_KV_TILE
    nk = seq // tm
    row = lambda b, j: (b * nk + j, 0)
    fixed2 = lambda b, j: (0, 0)
    tab = lambda b, j: (0, j)
    return pl.pallas_call(
        _qkv_kernel,
        grid=(batch, nk),
        in_specs=[pl.BlockSpec((tm, Q_LORA), row),
                  pl.BlockSpec((tm, KV_LORA), row),
                  pl.BlockSpec((tm, QK_ROPE), row),
                  pl.BlockSpec(wuqt.shape, fixed2),
                  pl.BlockSpec(wukp.shape, fixed2),
                  pl.BlockSpec(place.shape, fixed2),
                  pl.BlockSpec(wuvt.shape, fixed2),
                  pl.BlockSpec((ROPE_HALF, tm), tab),
                  pl.BlockSpec((ROPE_HALF, tm), tab)],
        out_specs=[pl.BlockSpec((None, N_HEADS, HEAD_PAD, tm), lambda b, j: (b, 0, 0, j)),
                   pl.BlockSpec((None, N_HEADS, tm, HEAD_PAD), lambda b, j: (b, 0, j, 0)),
                   pl.BlockSpec((None, N_HEADS, None, V_ROWS, tm), lambda b, j: (b, 0, j, 0, 0))],
        out_shape=[jax.ShapeDtypeStruct((batch, N_HEADS, HEAD_PAD, seq), BF16),
                   jax.ShapeDtypeStruct((batch, N_HEADS, seq, HEAD_PAD), BF16),
                   jax.ShapeDtypeStruct((batch, N_HEADS, nk, V_ROWS, tm), BF16)],
        compiler_params=_params(("parallel", "parallel")),
        name="qkv",
    )(cq_all, ckv_all, kr_all, wuqt, wukp, place, wuvt, cos_t, sin_t)


def _attn_kernel(qt_ref, k_ref, vt_ref, o_ref):
    tk = ATTN_KV_TILE
    n_heads = qt_ref.shape[0]
    n_sub = qt_ref.shape[2] // tk
    qi = pl.program_id(2)
    j0 = qi * n_sub
    chains = [(h, sub) for h in range(n_heads) for sub in range(n_sub)]

    def tiles(j, active, carries, diagonal_sub):
        scores = [jnp.dot(k_ref[h, j], qt_ref[h, :, sub * tk:(sub + 1) * tk], preferred_element_type=F32)
                  for h, sub in active]
        stats = []
        for (h, sub), s in zip(active, scores):
            m, _ = carries[h * n_sub + sub]
            if sub == diagonal_sub:
                key = lax.broadcasted_iota(jnp.int32, s.shape, 0)
                qry = lax.broadcasted_iota(jnp.int32, s.shape, 1)
                s = jnp.where(key <= qry, s, -jnp.inf)
            m_new = jnp.maximum(m, jnp.max(s, axis=0, keepdims=True))
            stats.append((m_new, jnp.exp2(m - m_new), jnp.exp2(s - m_new).astype(BF16)))
        out = list(carries)
        for (h, sub), (m_new, alpha, p) in zip(active, stats):
            c = h * n_sub + sub
            out[c] = (m_new, alpha * carries[c][1] + jnp.dot(vt_ref[h, j], p, preferred_element_type=F32))
        return out

    init = tuple((jnp.full((1, tk), -jnp.inf, F32), jnp.zeros((V_ROWS, tk), F32)) for _ in chains)
    carries = list(lax.fori_loop(0, j0, lambda j, c: tuple(tiles(j, chains, c, None)), init))
    for jj in range(n_sub):
        carries = tiles(j0 + jj, [(h, sub) for h, sub in chains if sub >= jj], carries, jj)
    for h, sub in chains:
        _, acc = carries[h * n_sub + sub]
        o_ref[h * V_HEAD:(h + 1) * V_HEAD, sub * tk:(sub + 1) * tk] = (
            acc[:V_HEAD] / acc[V_HEAD:V_HEAD + 1]).astype(BF16)


def _attn(qt, k, vt):
    batch, _, _, seq = qt.shape
    tk = ATTN_KV_TILE
    nk = seq // tk
    hp = ATTN_HEADS_PER_STEP
    k5 = k.reshape(batch, N_HEADS, nk, tk, HEAD_PAD)
    return pl.pallas_call(
        _attn_kernel,
        grid=(batch, N_HEADS // hp, seq // ATTN_Q_STEP),
        in_specs=[pl.BlockSpec((None, hp, HEAD_PAD, ATTN_Q_STEP), lambda b, h, i: (b, h, 0, i)),
                  pl.BlockSpec((None, hp, nk, tk, HEAD_PAD), lambda b, h, i: (b, h, 0, 0, 0)),
                  pl.BlockSpec((None, hp, nk, V_ROWS, tk), lambda b, h, i: (b, h, 0, 0, 0))],
        out_specs=pl.BlockSpec((None, hp * V_HEAD, ATTN_Q_STEP), lambda b, h, i: (b, h, i)),
        out_shape=jax.ShapeDtypeStruct((batch, N_HEADS * V_HEAD, seq), BF16),
        compiler_params=_params(("parallel", "parallel", "parallel")),
        name="attn",
    )(qt, k5, vt)


def _sq_kernel(cq_ref, wq_ref, wukb_ref, cos_ref, sin_ref, ql_ref, r1_ref, r2_ref):
    n_nope = N_HEADS * QK_NOPE
    q = jnp.dot(cq_ref[...], wq_ref[...], preferred_element_type=F32)
    x1 = q[:, n_nope:n_nope + LANES]
    x2 = q[:, n_nope + LANES:]
    c = cos_ref[...]
    s = sin_ref[...]
    r1_ref[...] = ((x1 * c - x2 * s) * ATTN_SCALE).astype(BF16)
    r2_ref[...] = ((x1 * s + x2 * c) * ATTN_SCALE).astype(BF16)
    ql = jnp.dot(q[:, :n_nope].astype(BF16), wukb_ref[...], preferred_element_type=F32)
    ql_ref[...] = (ql * ATTN_SCALE).astype(BF16)


def _sample_q(cq_s, wq_s, wuk_blk, cos_s, sin_s):
    ts = cq_s.shape[0]
    full = lambda shape: pl.BlockSpec(shape, lambda i: (0,) * len(shape))
    return pl.pallas_call(
        _sq_kernel,
        grid=(1,),
        in_specs=[full(cq_s.shape), full(wq_s.shape), full(wuk_blk.shape), full(cos_s.shape), full(sin_s.shape)],
        out_specs=[full((ts, N_HEADS * KV_LORA)), full((ts, LANES)), full((ts, LANES))],
        out_shape=[jax.ShapeDtypeStruct((ts, N_HEADS * KV_LORA), BF16),
                   jax.ShapeDtypeStruct((ts, LANES), BF16),
                   jax.ShapeDtypeStruct((ts, LANES), BF16)],
        compiler_params=_params(("arbitrary",)),
        name="sample_q",
    )(cq_s, wq_s, wuk_blk, cos_s, sin_s)


def _sattn_kernel(n_pages, pt_ref, ckv_hbm, kr_hbm, ql_ref, qr_ref, cnew_ref, knew_ref, o_ref,
                  cbuf, rbuf, sem_c, sem_r):
    npg = PAGES_PER_STEP
    n_chunks = n_pages // npg
    PAGE_SLOTS = cbuf.shape[0]
    PAGE_LOOKAHEAD = PAGE_SLOTS - 1
    b = pl.program_id(0)

    def page_copies(bb, chunk, slot, p):
        page = pt_ref[bb * n_pages + chunk * npg + p]
        return (pltpu.make_async_copy(ckv_hbm.at[page], cbuf.at[slot, p], sem_c.at[slot]),
                pltpu.make_async_copy(kr_hbm.at[page], rbuf.at[slot, p], sem_r.at[slot]))

    def start_chunk(bb, chunk, slot):
        for p in range(npg):
            for cp in page_copies(bb, chunk, slot, p):
                cp.start()

    def wait_chunk(bb, chunk, slot):
        for p in range(npg):
            for cp in page_copies(bb, chunk, slot, p):
                cp.wait()

    @pl.when(b == 0)
    def _():
        for chunk in range(PAGE_LOOKAHEAD):
            start_chunk(0, chunk, chunk)

    ql = ql_ref[...]
    qr = qr_ref[...]

    def partial_softmax(blocks):
        scores = [lax.dot_general(ql, keys, NT_DIMS, preferred_element_type=F32)
                  + jnp.dot(qr, rope_t, preferred_element_type=F32) for keys, rope_t, _ in blocks]
        probs = []
        for s, (_, _, mask) in zip(scores, blocks):
            if mask is not None:
                s = jnp.where(mask(s.shape), s, -jnp.inf)
            m = jnp.max(s, axis=-1, keepdims=True)
            p = jnp.exp(s - m)
            probs.append((m, jnp.sum(p, axis=-1, keepdims=True), p.astype(BF16)))
        return [(m, l, jnp.dot(p, blk[0], preferred_element_type=F32)) for (m, l, p), blk in zip(probs, blocks)]

    def merge(state, parts):
        m_old, l_old, acc_old = state
        m_new = m_old
        for m, _, _ in parts:
            m_new = jnp.maximum(m_new, m)
        alpha = jnp.exp(m_old - m_new)
        l = alpha * l_old
        acc = alpha * acc_old
        for m, lp, op in parts:
            w = jnp.exp(m - m_new)
            l = l + w * lp
            acc = acc + w * op
        return m_new, l, acc

    def chunk_step(chunk, state, extra_blocks=()):
        slot = chunk % PAGE_SLOTS
        wait_chunk(b, chunk, slot)
        ahead = chunk + PAGE_LOOKAHEAD
        ahead_b = jnp.where(ahead >= n_chunks, b + 1, b)
        ahead_b = jnp.where(ahead_b >= pl.num_programs(0), 0, ahead_b)
        start_chunk(ahead_b, ahead % n_chunks, ahead % PAGE_SLOTS)

        blocks = []
        for g in range(npg // PAGE_GROUP):
            pages = range(g * PAGE_GROUP, (g + 1) * PAGE_GROUP)
            blocks.append((jnp.concatenate([cbuf[slot, p].astype(BF16) for p in pages], axis=0),
                           jnp.concatenate([rbuf[slot, p].astype(BF16) for p in pages], axis=1), None))
        return merge(state, partial_softmax(blocks + list(extra_blocks)))

    def causal(shape):
        q_pos = lax.broadcasted_iota(jnp.int32, shape, 0) // N_HEADS
        return lax.broadcasted_iota(jnp.int32, shape, 1) <= q_pos

    rows = ql.shape[0]
    state = (jnp.full((rows, 1), -jnp.inf, F32), jnp.zeros((rows, 1), F32), jnp.zeros((rows, KV_LORA), F32))
    state = lax.fori_loop(0, n_chunks - 1, chunk_step, state)
    new_rows = (cnew_ref[...].astype(BF16), knew_ref[...].astype(BF16), causal)
    _, l, acc = chunk_step(n_chunks - 1, state, [new_rows])
    o_ref[...] = acc / l

    @pl.when(b == pl.num_programs(0) - 1)
    def _():
        for chunk in range(PAGE_LOOKAHEAD):
            wait_chunk(0, chunk, chunk % PAGE_SLOTS)


def _sample_attn(page_table, cache_ckv, cache_kr_t, ql, qr, cnew, knew_t):
    dec_b, n_pages = page_table.shape
    npg = PAGES_PER_STEP
    n_chunks = n_pages // npg
    slots = 4 if n_chunks % 4 == 0 else 2
    assert n_chunks % slots == 0
    rows = ql.shape[1]
    n_new = cnew.shape[1]
    per_b = lambda b, pt: (b, 0, 0)
    return pl.pallas_call(
        functools.partial(_sattn_kernel, n_pages),
        grid_spec=pltpu.PrefetchScalarGridSpec(
            num_scalar_prefetch=1,
            grid=(dec_b,),
            in_specs=[pl.BlockSpec(memory_space=pl.ANY),
                      pl.BlockSpec(memory_space=pl.ANY),
                      pl.BlockSpec((None, rows, KV_LORA), per_b),
                      pl.BlockSpec((None, rows, QK_ROPE), per_b),
                      pl.BlockSpec((None, n_new, KV_LORA), per_b),
                      pl.BlockSpec((None, QK_ROPE, n_new), per_b)],
            out_specs=pl.BlockSpec((None, rows, KV_LORA), per_b),
            scratch_shapes=[pltpu.VMEM((slots, npg, CHUNK, KV_LORA), F32),
                            pltpu.VMEM((slots, npg, QK_ROPE, CHUNK), F32),
                            pltpu.SemaphoreType.DMA((slots,)),
                            pltpu.SemaphoreType.DMA((slots,))]),
        out_shape=jax.ShapeDtypeStruct((dec_b, rows, KV_LORA), F32),
        compiler_params=_params(("arbitrary",)),
        name="sample_attn",
    )(page_table.reshape(-1), cache_ckv, cache_kr_t, ql, qr, cnew, knew_t)


def _mm_kernel(x_ref, w_ref, o_ref):
    o_ref[...] = jnp.dot(x_ref[...].astype(BF16), w_ref[...], preferred_element_type=F32).astype(o_ref.dtype)


def _matmul(x, w, out_dtype):
    m, n = x.shape[0], w.shape[1]
    full = lambda shape: pl.BlockSpec(shape, lambda i: (0,) * len(shape))
    return pl.pallas_call(
        _mm_kernel,
        grid=(1,),
        in_specs=[full(x.shape), full(w.shape)],
        out_specs=full((m, n)),
        out_shape=jax.ShapeDtypeStruct((m, n), out_dtype),
        compiler_params=_params(("arbitrary",)),
        name="matmul",
    )(x, w)


def _merge_kernel(n_prompt_tiles, xp_ref, xs_ref, sh1_ref, sc1_ref, gt1_ref, sh2_ref, sc2_ref, g1_ref, g2_ref,
                  a_ref, ot_ref, os_ref, wg_ref, woa_ref, wob_ref, wo_ref, x1_ref, *rest):
    h2_refs, yb_ref = rest[:SC_SPLIT], rest[SC_SPLIT]
    i = pl.program_id(0)
    x = jnp.where(i < n_prompt_tiles, xp_ref[...], xs_ref[...])
    h = _rms(x, g1_ref[...]) * (1.0 + sc1_ref[...]) + sh1_ref[...]
    gates = _sigmoid(jnp.dot(h.astype(BF16), wg_ref[...], preferred_element_type=F32))
    y_a = jnp.dot(a_ref[...], woa_ref[...], preferred_element_type=F32)

    @pl.when(i < n_prompt_tiles)
    def _():
        yb_ref[...] = lax.dot_general(ot_ref[...], wob_ref[...], TN_DIMS, preferred_element_type=F32)

    @pl.when(i >= n_prompt_tiles)
    def _():
        yb_ref[...] = jnp.dot(os_ref[...], wob_ref[...], preferred_element_type=F32)

    z = gates[:, :D_MODEL] * y_a + gates[:, D_MODEL:] * yb_ref[...]
    y = jnp.dot(z.astype(BF16), wo_ref[...], preferred_element_type=F32)
    x1 = x + gt1_ref[...] * y
    x1_ref[...] = x1
    for ref, piece in zip(h2_refs, _pack_rows(_rms(x1, g2_ref[...]) * (1.0 + sc2_ref[...]) + sh2_ref[...])):
        ref[...] = piece


def _merge(x_p, x_s, mod_g, g1, g2, a_all, ot, o_s, wg, woa, wob, wo, grp_of_tile, n_prompt_tiles):
    t_all = x_p.shape[0] + x_s.shape[0]
    tm = ROW_TILE
    seq = ot.shape[2]
    tpb = seq // tm
    row = lambda i: (i, 0)
    fixed2 = lambda i: (0, 0)
    mod_comps = (MOD_SHIFT1, MOD_SCALE1, MOD_GATE1, MOD_SHIFT2, MOD_SCALE2)

    def ot_map(i):
        ic = jnp.minimum(i, n_prompt_tiles - 1)
        return (ic // tpb, 0, ic % tpb)

    os_map = lambda i: (jnp.maximum(i - n_prompt_tiles, 0), 0)
    return pl.pallas_call(
        functools.partial(_merge_kernel, n_prompt_tiles),
        grid=(t_all // tm,),
        in_specs=[pl.BlockSpec((tm, D_MODEL), lambda i: (jnp.minimum(i, n_prompt_tiles - 1), 0)),
                  pl.BlockSpec((tm, D_MODEL), os_map)]
        + [_mod_spec(grp_of_tile, c) for c in mod_comps]
        + [pl.BlockSpec((1, D_MODEL), fixed2),
                  pl.BlockSpec((1, D_MODEL), fixed2),
                  pl.BlockSpec((tm, A_WIDTH), row),
                  pl.BlockSpec((None, N_HEADS * V_HEAD, tm), ot_map),
                  pl.BlockSpec((tm, N_HEADS * V_HEAD), os_map),
                  pl.BlockSpec(wg.shape, fixed2),
                  pl.BlockSpec(woa.shape, fixed2),
                  pl.BlockSpec(wob.shape, fixed2),
                  pl.BlockSpec(wo.shape, fixed2)],
        out_specs=[pl.BlockSpec((tm, D_MODEL), row)] + [pl.BlockSpec((tm, PIECE), row)] * SC_SPLIT,
        out_shape=[jax.ShapeDtypeStruct((t_all, D_MODEL), F32)]
        + [jax.ShapeDtypeStruct((t_all, PIECE), jnp.int32)] * SC_SPLIT,
        scratch_shapes=[pltpu.VMEM((tm, D_MODEL), F32)],
        compiler_params=_params(("parallel",)),
        name="merge",
    )(x_p, x_s, *([mod_g] * len(mod_comps)), g1, g2, a_all, ot, o_s, wg, woa, wob, wo)


def _first_argmax(v, rows):
    mx = jnp.max(v, axis=0, keepdims=True)
    idx = jnp.min(jnp.where(v == mx, rows, v.shape[0]), axis=0, keepdims=True)
    return mx, idx


def _route_kernel(x1_ref, sh2_ref, sc2_ref, g2_ref, whi_ref, wlo_ref, bias_ref, upper_ref,
                  idx_ref, wt_ref, rank_ref, count_ref, run_ref):
    h2 = _rms(x1_ref[...], g2_ref[...]) * (1.0 + sc2_ref[...]) + sh2_ref[...]
    hi = h2.astype(BF16)
    lo = (h2 - hi.astype(F32)).astype(BF16)
    whi = whi_ref[...]
    logits = (lax.dot_general(whi, hi, NT_DIMS, preferred_element_type=F32)
              + lax.dot_general(whi, lo, NT_DIMS, preferred_element_type=F32)
              + lax.dot_general(wlo_ref[...], hi, NT_DIMS, preferred_element_type=F32))
    scores = _sigmoid(logits)
    sel = scores + bias_ref[...]
    tm = sel.shape[1]
    neg = -jnp.inf
    rows_g = lax.broadcasted_iota(jnp.int32, (GROUP_SIZE, tm), 0)
    gscore = []
    for g in range(N_GROUPS):
        blk = sel[g * GROUP_SIZE:(g + 1) * GROUP_SIZE]
        m1, i1 = _first_argmax(blk, rows_g)
        m2 = jnp.max(jnp.where(rows_g == i1, neg, blk), axis=0, keepdims=True)
        gscore.append(m1 + m2)
    gs = jnp.concatenate(gscore, axis=0)
    rows_8 = lax.broadcasted_iota(jnp.int32, gs.shape, 0)
    chosen = jnp.zeros(gs.shape, jnp.int32)
    for _ in range(TOPK_GROUPS):
        _, gi = _first_argmax(gs, rows_8)
        hit = rows_8 == gi
        chosen = jnp.where(hit, 1, chosen)
        gs = jnp.where(hit, neg, gs)
    cand = jnp.concatenate(
        [jnp.where(chosen[g:g + 1] > 0, sel[g * GROUP_SIZE:(g + 1) * GROUP_SIZE], neg) for g in range(N_GROUPS)],
        axis=0)
    rows_e = lax.broadcasted_iota(jnp.int32, cand.shape, 0)
    idxs, wts, hits = [], [], []
    for _ in range(TOP_K):
        _, ei = _first_argmax(cand, rows_e)
        hit = rows_e == ei
        idxs.append(ei)
        hits.append(hit)
        wts.append(jnp.sum(jnp.where(hit, scores, 0.0), axis=0, keepdims=True))
        cand = jnp.where(hit, neg, cand)
    w = jnp.concatenate(wts, axis=0)
    idx_ref[...] = jnp.concatenate(idxs, axis=0)
    wt_ref[...] = w / jnp.sum(w, axis=0, keepdims=True) * ROUTED_SCALE

    @pl.when(pl.program_id(0) == 0)
    def _():
        run_ref[...] = jnp.zeros(run_ref.shape, F32)

    run = run_ref[...]
    ones = jnp.ones((tm, LANES), BF16)
    onehots = [jnp.where(hit, 1.0, 0.0).astype(BF16) for hit in hits]
    befores = [jnp.dot(oh, upper_ref[...], preferred_element_type=F32) for oh in onehots]
    totals = [jnp.dot(oh, ones, preferred_element_type=F32) for oh in onehots]
    ranks = []
    for hit, before, total in zip(hits, befores, totals):
        prior = jnp.concatenate([run] * (tm // LANES), axis=1)
        ranks.append(jnp.sum(jnp.where(hit, before + prior, 0.0), axis=0, keepdims=True))
        run = run + total
    rank_ref[...] = jnp.concatenate(ranks, axis=0).astype(jnp.int32)
    run_ref[...] = run
    count_ref[...] = run


def _route(x1_all, mod_g, g2, whi, wlo, bias_col, grp_of_tile):
    t_all = x1_all.shape[0]
    tm = ROW_TILE
    fixed2 = lambda i: (0, 0)
    col = lambda i: (0, i)
    upper = (lax.broadcasted_iota(jnp.int32, (tm, tm), 0) < lax.broadcasted_iota(jnp.int32, (tm, tm), 1)).astype(BF16)
    return pl.pallas_call(
        _route_kernel,
        grid=(t_all // tm,),
        in_specs=[pl.BlockSpec((tm, D_MODEL), lambda i: (i, 0)),
                  _mod_spec(grp_of_tile, MOD_SHIFT2),
                  _mod_spec(grp_of_tile, MOD_SCALE2),
                  pl.BlockSpec((1, D_MODEL), fixed2),
                  pl.BlockSpec(whi.shape, fixed2),
                  pl.BlockSpec(wlo.shape, fixed2),
                  pl.BlockSpec(bias_col.shape, fixed2),
                  pl.BlockSpec((tm, tm), fixed2)],
        out_specs=[pl.BlockSpec((TOP_K, tm), col), pl.BlockSpec((TOP_K, tm), col), pl.BlockSpec((TOP_K, tm), col),
                   pl.BlockSpec((N_EXPERTS, LANES), fixed2)],
        out_shape=[jax.ShapeDtypeStruct((TOP_K, t_all), jnp.int32),
                   jax.ShapeDtypeStruct((TOP_K, t_all), F32),
                   jax.ShapeDtypeStruct((TOP_K, t_all), jnp.int32),
                   jax.ShapeDtypeStruct((N_EXPERTS, LANES), F32)],
        scratch_shapes=[pltpu.VMEM((N_EXPERTS, LANES), F32)],
        compiler_params=_params(("arbitrary",)),
        name="route",
    )(x1_all, mod_g, mod_g, g2, whi, wlo, bias_col, upper)


def _moe_kernel(be_ref, nv_ref, next_ref, *refs):
    x_refs = refs[:SC_SPLIT]
    wgu_hbm, wdn_hbm = refs[SC_SPLIT:SC_SPLIT + 2]
    y_refs = refs[SC_SPLIT + 2:2 * SC_SPLIT + 2]
    gu_f32, dn_f32, gu_bf, dn_bf, sem, run_ref = refs[2 * SC_SPLIT + 2:]
    i = pl.program_id(0)
    expert = be_ref[i]
    prev = be_ref[jnp.maximum(i - 1, 0)]

    def weight_copies(e, slot):
        return (pltpu.make_async_copy(wgu_hbm.at[e], gu_f32.at[slot], sem.at[0, slot]),
                pltpu.make_async_copy(wdn_hbm.at[e], dn_f32.at[slot], sem.at[1, slot]))

    @pl.when(i == 0)
    def _():
        run_ref[0] = 0
        for cp in weight_copies(expert, 0):
            cp.start(priority=1)

    @pl.when(i < nv_ref[0])
    def _():
        @pl.when((i == 0) | (expert != prev))
        def _():
            run = run_ref[0] + jnp.where(i == 0, 0, 1)
            run_ref[0] = run
            slot = run % 2
            for cp in weight_copies(expert, slot):
                cp.wait()
            following = next_ref[expert]

            @pl.when(following < N_EXPERTS)
            def _():
                for cp in weight_copies(following, 1 - slot):
                    cp.start(priority=1)

            gu_bf[...] = gu_f32[slot].astype(BF16)
            dn_bf[...] = dn_f32[slot].astype(BF16)

        subs = [slice(s * MOE_SUB, (s + 1) * MOE_SUB) for s in range(MOE_BLOCK // MOE_SUB)]
        gus = [jnp.dot(_unpack_rows([r[rows, :] for r in x_refs]).astype(BF16), gu_bf[...],
                       preferred_element_type=F32) for rows in subs]
        hids = [(_silu(gu[:, :EXPERT_DIM]) * gu[:, EXPERT_DIM:]).astype(BF16) for gu in gus]
        ys = [jnp.dot(hid, dn_bf[...], preferred_element_type=F32) for hid in hids]
        for rows, y in zip(subs, ys):
            for ref, piece in zip(y_refs, _pack_rows(y)):
                ref[rows, :] = piece


def _moe(block_e, n_valid, next_expert, x_sorted, w_gu, w_dn):
    n_slots = x_sorted[0].shape[0]
    blk = MOE_BLOCK
    n_blocks = n_slots // blk
    rows = lambda i, be, nv, nx: (jnp.minimum(i, nv[0] - 1), 0)
    return pl.pallas_call(
        _moe_kernel,
        grid_spec=pltpu.PrefetchScalarGridSpec(
            num_scalar_prefetch=3,
            grid=(n_blocks,),
            in_specs=[pl.BlockSpec((blk, PIECE), rows)] * SC_SPLIT
            + [pl.BlockSpec(memory_space=pl.ANY), pl.BlockSpec(memory_space=pl.ANY)],
            out_specs=[pl.BlockSpec((blk, PIECE), rows)] * SC_SPLIT,
            scratch_shapes=[pltpu.VMEM((2, D_MODEL, 2 * EXPERT_DIM), F32),
                            pltpu.VMEM((2, EXPERT_DIM, D_MODEL), F32),
                            pltpu.VMEM((D_MODEL, 2 * EXPERT_DIM), BF16),
                            pltpu.VMEM((EXPERT_DIM, D_MODEL), BF16),
                            pltpu.SemaphoreType.DMA((2, 2)),
                            pltpu.SMEM((1,), jnp.int32)]),
        out_shape=[jax.ShapeDtypeStruct((n_slots, PIECE), jnp.int32)] * SC_SPLIT,
        compiler_params=_params(("arbitrary",)),
        name="moe",
    )(block_e, n_valid, next_expert, *x_sorted, w_gu, w_dn)


def _dest_kernel(e_ref, rank_ref, start_ref, dest_ref):
    e = e_ref[...]
    rows = lax.broadcasted_iota(jnp.int32, (N_EXPERTS, e.shape[1]), 0)
    start = jnp.sum(jnp.where(rows == e, start_ref[...], 0), axis=0, keepdims=True)
    dest_ref[...] = start + rank_ref[...]


def _dest(e_flat, rank, pad_start_col):
    n_assign = e_flat.shape[1]
    n = RANK_TILE
    tile = pl.BlockSpec((1, n), lambda i: (0, i))
    return pl.pallas_call(
        _dest_kernel,
        grid=(n_assign // n,),
        in_specs=[tile, tile, pl.BlockSpec((N_EXPERTS, 1), lambda i: (0, 0))],
        out_specs=tile,
        out_shape=jax.ShapeDtypeStruct((1, n_assign), jnp.int32),
        compiler_params=_params(("parallel",)),
        name="dest",
    )(e_flat, rank, pad_start_col)


def _sc_mesh():
    return plsc.VectorSubcoreMesh(core_axis_name="core", subcore_axis_name="subcore")


def _sc_scatter_rows(srcs, dest, n_slots):
    n_src, width = srcs[0].shape
    n_assign = dest.shape[1]
    src_blocks = n_src // SC_WINDOW
    n_pieces = len(srcs)

    @functools.partial(pl.kernel, out_type=[jax.ShapeDtypeStruct((n_slots, width), s.dtype) for s in srcs],
                       mesh=_sc_mesh(), scratch_types=[])
    def scatter(*refs):
        src_hbms, dest_hbm, out_hbms = refs[:n_pieces], refs[n_pieces], refs[n_pieces + 1:]
        for src_hbm, out_hbm in zip(src_hbms, out_hbms):
            def body(rows_vmem, dest_vmem, out_hbm=out_hbm):
                pltpu.sync_copy(rows_vmem, out_hbm.at[dest_vmem.at[0]])

            pltpu.emit_pipeline(
                body,
                grid=(n_assign // SC_WINDOW,),
                in_specs=[pl.BlockSpec((SC_WINDOW, width), lambda i: (i % src_blocks, 0)),
                          pl.BlockSpec((1, SC_WINDOW), lambda i: (0, i))],
                out_specs=[],
                core_axis_name=("core", "subcore"),
                dimension_semantics=(pltpu.PARALLEL,),
            )(src_hbm, dest_hbm)

    return scatter(*srcs, dest)


def _sc_gather_rows(tables, idx):
    width = tables[0].shape[1]
    n_assign = idx.shape[1]
    n_pieces = len(tables)

    @functools.partial(pl.kernel, out_type=[jax.ShapeDtypeStruct((n_assign, width), t.dtype) for t in tables],
                       mesh=_sc_mesh(), scratch_types=[])
    def gather(*refs):
        table_hbms, idx_hbm, out_hbms = refs[:n_pieces], refs[n_pieces], refs[n_pieces + 1:]
        for table_hbm, out_hbm in zip(table_hbms, out_hbms):
            def body(idx_vmem, rows_vmem, table_hbm=table_hbm):
                pltpu.sync_copy(table_hbm.at[idx_vmem.at[0]], rows_vmem)

            pltpu.emit_pipeline(
                body,
                grid=(n_assign // SC_WINDOW,),
                in_specs=[pl.BlockSpec((1, SC_WINDOW), lambda i: (0, i))],
                out_specs=[pl.BlockSpec((SC_WINDOW, width), lambda i: (i, 0))],
                core_axis_name=("core", "subcore"),
                dimension_semantics=(pltpu.PARALLEL,),
            )(idx_hbm, out_hbm)

    return gather(*tables, idx)


def _final_kernel(n_prompt_tiles, x1_ref, *refs):
    h2_refs = refs[:SC_SPLIT]
    yg_refs = refs[SC_SPLIT:2 * SC_SPLIT]
    wt_ref, gt2_ref, wsg_ref, wsd_ref, gf_ref, op_ref, os_ref = refs[2 * SC_SPLIT:]
    is_prompt = pl.program_id(0) < n_prompt_tiles
    h2 = _unpack_rows([r[...] for r in h2_refs]).astype(BF16)
    gu = jnp.dot(h2, wsg_ref[...], preferred_element_type=F32)
    hid = _silu(gu[:, :SHARED_DIM]) * gu[:, SHARED_DIM:]
    y = jnp.dot(hid.astype(BF16), wsd_ref[...], preferred_element_type=F32)
    wt = wt_ref[...]
    for k in range(TOP_K):
        y = y + wt[:, k:k + 1] * _unpack_rows([r[k] for r in yg_refs])
    x2 = x1_ref[...] + gt2_ref[...] * y
    out = _rms(x2, gf_ref[...])

    @pl.when(is_prompt)
    def _():
        op_ref[...] = out

    @pl.when(jnp.logical_not(is_prompt))
    def _():
        os_ref[...] = out


def _final(x1_all, h2_all, y_gath, wt_rows, mod_g, wsg, wsd, gf, grp_of_tile, n_prompt_tiles):
    t_all = x1_all.shape[0]
    tm = ROW_TILE
    t_p = n_prompt_tiles * tm
    row = lambda i: (i, 0)
    fixed2 = lambda i: (0, 0)
    return pl.pallas_call(
        functools.partial(_final_kernel, n_prompt_tiles),
        grid=(t_all // tm,),
        in_specs=[pl.BlockSpec((tm, D_MODEL), row)]
        + [pl.BlockSpec((tm, PIECE), row)] * SC_SPLIT
        + [pl.BlockSpec((TOP_K, tm, PIECE), lambda i: (0, i, 0))] * SC_SPLIT
        + [pl.BlockSpec((tm, TOP_K), row),
           _mod_spec(grp_of_tile, MOD_GATE2),
           pl.BlockSpec(wsg.shape, fixed2),
           pl.BlockSpec(wsd.shape, fixed2),
           pl.BlockSpec((1, D_MODEL), fixed2)],
        out_specs=[pl.BlockSpec((tm, D_MODEL), lambda i: (jnp.minimum(i, n_prompt_tiles - 1), 0)),
                   pl.BlockSpec((tm, D_MODEL), lambda i: (jnp.maximum(i - n_prompt_tiles, 0), 0))],
        out_shape=[jax.ShapeDtypeStruct((t_p, D_MODEL), F32),
                   jax.ShapeDtypeStruct((t_all - t_p, D_MODEL), F32)],
        compiler_params=_params(("arbitrary",)),
        name="final",
    )(x1_all, *h2_all, *y_gath, wt_rows, mod_g, wsg, wsd, gf)


def _rope_tables(pos):
    freqs = ROPE_THETA ** (-jnp.arange(ROPE_HALF, dtype=F32) / ROPE_HALF)
    ang = pos.astype(F32)[:, None] * freqs
    return jnp.cos(ang), jnp.sin(ang)


def kernel(x_prompt, x_sample, cache_ckv, cache_krope, page_table, c_prompt, c_sample, w_ada, b_ada, norm1_g, norm2_g, w_in, a_vnorm_g, w_spatial, b_spatial, w_out_a, q_norm_g, w_uq, kv_norm_g, w_uk, w_uv, w_out_b, w_o, w_router, router_bias, w_exp_gu, w_exp_down, w_sh_gu, w_sh_down, final_norm_g):
    batch, seq, _ = x_prompt.shape
    dec_b, dec_s, _ = x_sample.shape
    n_pages = page_table.shape[1]
    n_past = n_pages * CHUNK
    t_p, t_s = batch * seq, dec_b * dec_s
    t_all = t_p + t_s
    tm = ROW_TILE
    assert seq % ATTN_Q_STEP == 0 and t_s % tm == 0 and CHUNK % dec_s == 0 and n_pages % PAGES_PER_STEP == 0
    assert (t_all * TOP_K) % RANK_TILE == 0 and w_ada.shape[0] == 1
    n_prompt_tiles = t_p // tm
    tiles_per_b = seq // tm

    def grp_of_tile(i):
        return jnp.minimum(i // tiles_per_b, batch) + jnp.maximum(i - n_prompt_tiles, 0)

    n_c = batch + dec_b
    c_rows = -(-n_c // 8) * 8
    c_all = jnp.concatenate([c_prompt, c_sample, jnp.zeros((c_rows - n_c, D_MODEL), F32)], axis=0)
    mod = _ada(c_all, w_ada[0], b_ada[0][None, :])
    mod_p = jnp.broadcast_to(mod[:batch, None, :], (batch, tm, 6 * D_MODEL))
    mod_s = jnp.repeat(mod[batch:n_c], dec_s, axis=0).reshape(t_s // tm, tm, 6 * D_MODEL)
    mod_g = jnp.concatenate([mod_p, mod_s], axis=0)

    o_u, o_v, o_cq, o_ckv, o_kr = A_WIDTH, 2 * A_WIDTH, 2 * A_WIDTH + Q_LORA, 2 * A_WIDTH + Q_LORA + KV_LORA, \
        2 * A_WIDTH + Q_LORA + KV_LORA + QK_ROPE
    win = w_in[0]
    w5 = jnp.concatenate([win[:, :o_kr], jnp.zeros((D_MODEL, IN_END - IN_KR - QK_ROPE), F32)], axis=1).astype(BF16)
    wg = win[:, o_kr:].astype(BF16)
    row1 = lambda v: v.reshape(1, -1)
    tri = jnp.tril(jnp.ones((CHUNK, CHUNK), F32))
    wc_p = w_spatial[0] * tri
    per = CHUNK // dec_s
    small = (w_spatial[0] * tri)[:, :dec_s, :dec_s]
    wc_s = jnp.einsum('ab,gts->gatbs', jnp.eye(per, dtype=F32), small).reshape(A_GROUPS, CHUNK, CHUNK)
    wc = jnp.stack([wc_p, wc_s]).astype(BF16)
    bias_p = jnp.repeat(b_spatial[0].T, A_GROUP_DIM, axis=1)
    bias_s = jnp.tile(bias_p[:dec_s], (per, 1))
    bc = jnp.stack([bias_p, bias_s])

    pos_p = jnp.arange(seq)
    pos_s = n_past + jnp.arange(dec_s)
    cos_p, sin_p = _rope_tables(pos_p)
    cos_s, sin_s = _rope_tables(pos_s)
    cos_rows = jnp.concatenate([jnp.tile(cos_p, (batch, 1)), jnp.tile(cos_s, (dec_b, 1))], axis=0)
    sin_rows = jnp.concatenate([jnp.tile(sin_p, (batch, 1)), jnp.tile(sin_s, (dec_b, 1))], axis=0)
    lane_pad = jnp.zeros((t_all, LANES - QK_ROPE), F32)
    cos_kr = jnp.concatenate([cos_rows, cos_rows, lane_pad], axis=1)
    sin_kr = jnp.concatenate([-sin_rows, sin_rows, lane_pad], axis=1)

    x_p, x_s = x_prompt.reshape(t_p, D_MODEL), x_sample.reshape(t_s, D_MODEL)
    g1, g2 = row1(norm1_g[0]), row1(norm2_g[0])
    a_all, cq_all, ckv_p, kr_p, ckv_s, kr_s, v_s = _in_proj(
        x_p, x_s, mod_g, g1, w5, row1(a_vnorm_g[0]), row1(q_norm_g[0]), row1(kv_norm_g[0]),
        cos_kr, sin_kr, wc, bc, grp_of_tile, n_prompt_tiles)

    wuq = w_uq[0]
    wuqt = jnp.pad(wuq, ((0, 0), (0, 0), (0, HEAD_PAD - QK_NOPE - QK_ROPE))).reshape(Q_LORA, -1).T.astype(BF16)
    wukp = jnp.pad(w_uk[0], ((0, 0), (0, 0), (0, HEAD_PAD - QK_NOPE))).reshape(KV_LORA, -1).astype(BF16)
    place_h = jnp.pad(jnp.eye(QK_ROPE, dtype=F32), ((0, 0), (QK_NOPE, HEAD_PAD - QK_NOPE - QK_ROPE)))
    place = jnp.tile(place_h, (1, N_HEADS)).astype(BF16)
    wuvt = w_uv[0].reshape(KV_LORA, -1).T.astype(BF16)
    qt, k, vt = _qkv(cq_all, ckv_p, kr_p, wuqt, wukp, place, wuvt, cos_p.T, sin_p.T, batch, seq)
    ot = _attn(qt, k, vt)

    wq_s = jnp.concatenate([wuq[:, :, :QK_NOPE].reshape(Q_LORA, -1),
                            wuq[:, :, QK_NOPE:QK_NOPE + ROPE_HALF].reshape(Q_LORA, -1),
                            wuq[:, :, QK_NOPE + ROPE_HALF:].reshape(Q_LORA, -1)], axis=1).astype(BF16)
    eye_h = jnp.eye(N_HEADS, dtype=F32)
    wuk_blk = jnp.einsum('rhd,hg->hdgr', w_uk[0], eye_h).reshape(N_HEADS * QK_NOPE, N_HEADS * KV_LORA).astype(BF16)
    wuv_blk = jnp.einsum('rhd,hg->hrgd', w_uv[0], eye_h).reshape(N_HEADS * KV_LORA, N_HEADS * V_HEAD).astype(BF16)
    cos_sq = jnp.tile(jnp.tile(cos_s, (1, N_HEADS)), (dec_b, 1))
    sin_sq = jnp.tile(jnp.tile(sin_s, (1, N_HEADS)), (dec_b, 1))
    ql, r1, r2 = _sample_q(cq_all[t_p:], wq_s, wuk_blk, cos_sq, sin_sq)
    rows = dec_s * N_HEADS
    ql3 = ql.reshape(dec_b, rows, KV_LORA)
    qr3 = jnp.concatenate([r1.reshape(dec_b, dec_s, N_HEADS, ROPE_HALF),
                           r2.reshape(dec_b, dec_s, N_HEADS, ROPE_HALF)], axis=-1).reshape(dec_b, rows, QK_ROPE)
    new_pad = ((0, 0), (0, 8 - dec_s), (0, 0))
    ckv_new = jnp.pad(ckv_s.reshape(dec_b, dec_s, KV_LORA), new_pad)
    kr_new_t = jnp.swapaxes(jnp.pad(kr_s.reshape(dec_b, dec_s, QK_ROPE), new_pad), 1, 2)
    cache_kr_t = jnp.swapaxes(cache_krope[0], 1, 2)
    o_lat = _sample_attn(page_table, cache_ckv[0], cache_kr_t, ql3, qr3, ckv_new, kr_new_t)
    o_s = _matmul(o_lat.reshape(t_s, N_HEADS * KV_LORA), wuv_blk, BF16)

    x1_all, *h2_all = _merge(x_p, x_s, mod_g, g1, g2, a_all, ot, o_s, wg,
                            w_out_a[0].astype(BF16), w_out_b[0].astype(BF16), w_o[0].astype(BF16),
                            grp_of_tile, n_prompt_tiles)

    wr_t = w_router[0].T
    whi = wr_t.astype(BF16)
    wlo = (wr_t - whi.astype(F32)).astype(BF16)
    idx_t, wt_t, rank_t, counts = _route(x1_all, mod_g, g2, whi, wlo, router_bias[0].reshape(-1, 1),
                                         grp_of_tile)
    n_assign = t_all * TOP_K
    e_flat = idx_t.reshape(1, n_assign)
    counts = counts[:, 0].astype(jnp.int32)
    blk = MOE_BLOCK
    padded = (counts + blk - 1) // blk * blk
    pad_end = jnp.cumsum(padded)
    pad_start = pad_end - padded
    dest = _dest(e_flat, rank_t.reshape(1, n_assign), pad_start.reshape(-1, 1))
    n_blocks = -(-n_assign // blk) + N_EXPERTS
    n_slots = n_blocks * blk
    first_row = jnp.arange(n_blocks, dtype=jnp.int32) * blk
    block_e = jnp.minimum(jnp.sum(pad_end[None, :] <= first_row[:, None], axis=1), N_EXPERTS - 1).astype(jnp.int32)
    n_valid = (pad_end[-1] // blk).astype(jnp.int32).reshape(1)
    experts = jnp.arange(N_EXPERTS, dtype=jnp.int32)
    owners = jnp.where(padded > 0, experts, N_EXPERTS)
    later_owner = lax.cummin(owners, reverse=True)
    next_expert = jnp.concatenate([later_owner[1:], jnp.full((1,), N_EXPERTS, jnp.int32)])
    x_sorted = _sc_scatter_rows(h2_all, dest, n_slots)
    y_sorted = _moe(block_e, n_valid, next_expert, x_sorted, w_exp_gu[0], w_exp_down[0])
    y_gath = [y.reshape(TOP_K, t_all, PIECE) for y in _sc_gather_rows(y_sorted, dest)]

    y_p, y_s = _final(x1_all, h2_all, y_gath, wt_t.T, mod_g, w_sh_gu[0].astype(BF16), w_sh_down[0].astype(BF16),
                      row1(final_norm_g), grp_of_tile, n_prompt_tiles)

    y_prompt = y_p.reshape(batch, seq, D_MODEL)
    y_sample = y_s.reshape(dec_b, dec_s, D_MODEL)
    new_ckv_prompt = ckv_p.reshape(1, batch, seq, KV_LORA)
    new_krope_prompt = kr_p.reshape(1, batch, seq, QK_ROPE)
    new_ckv_sample = ckv_s.reshape(1, dec_b, dec_s, KV_LORA)
    new_krope_sample = kr_s.reshape(1, dec_b, dec_s, QK_ROPE)
    new_chunk_v_sample = v_s.reshape(1, dec_b, dec_s, A_WIDTH)
    return (y_prompt, y_sample, new_ckv_prompt, new_krope_prompt, new_ckv_sample, new_krope_sample,
            new_chunk_v_sample)
```

```python
import functools

import jax
import jax.numpy as jnp
from jax import lax
from jax.experimental import pallas as pl
from jax.experimental.pallas import tpu as pltpu
from jax.experimental.pallas import tpu_sc as plsc

F32 = jnp.float32
BF16 = jnp.bfloat16

D_MODEL = 1024
A_WIDTH = D_MODEL // 2
A_GROUPS = 8
A_GROUP_DIM = A_WIDTH // A_GROUPS
CHUNK = 128
N_HEADS = 8
QK_NOPE = 64
QK_ROPE = 32
ROPE_HALF = QK_ROPE // 2
V_HEAD = 64
Q_LORA = 384
KV_LORA = 256
ROPE_THETA = 10000.0
ATTN_SCALE = (QK_NOPE + QK_ROPE) ** -0.5
N_EXPERTS = 256
TOP_K = 8
N_GROUPS = 8
GROUP_SIZE = N_EXPERTS // N_GROUPS
TOPK_GROUPS = 4
EXPERT_DIM = 256
SHARED_DIM = 256
ROUTED_SCALE = 2.5
EPS = 1e-6

LANES = 128
HEAD_PAD = 128
ROW_TILE = 256
ATTN_KV_TILE = 256
ATTN_Q_STEP = 512
ATTN_HEADS_PER_STEP = 4
LOG2_E = 1.4426950408889634
RANK_TILE = 512
MOE_BLOCK = 256
MOE_SUB = 128
MOE_WEIGHT_SLOTS = 3
PAGES_PER_STEP = 32
PAGE_GROUP = 8
V_ROWS = V_HEAD + 16
SC_WINDOW = 128
SC_SPLIT = 2
PIECE = D_MODEL // 2 // SC_SPLIT
VMEM_LIMIT = 48 * 1024 * 1024

IN_U, IN_V, IN_CQ, IN_CKV, IN_KR, IN_END = 0, 512, 1024, 1408, 1664, 1792

NT_DIMS = (((1,), (1,)), ((), ()))
TN_DIMS = (((0,), (0,)), ((), ()))


def _params(sem, vmem=VMEM_LIMIT):
    return pltpu.CompilerParams(dimension_semantics=sem, vmem_limit_bytes=vmem)


def _rms(x, g):
    return x * lax.rsqrt(jnp.mean(x * x, axis=-1, keepdims=True) + EPS) * g


def _gelu(x):
    return 0.5 * x * (1.0 + jnp.tanh(0.7978845608028654 * (x + 0.044715 * (x * x * x))))


def _sigmoid(x):
    return 1.0 / (1.0 + jnp.exp(-x))


def _silu(x):
    return x * _sigmoid(x)


def _pack_rows(x):
    pieces = []
    for c in range(SC_SPLIT):
        lo = x[:, 2 * c * PIECE:(2 * c + 1) * PIECE].astype(BF16).astype(F32)
        hi = x[:, (2 * c + 1) * PIECE:(2 * c + 2) * PIECE].astype(BF16).astype(F32)
        pieces.append(lax.bitcast_convert_type(hi, jnp.int32)
                      | lax.shift_right_logical(lax.bitcast_convert_type(lo, jnp.int32), 16))
    return pieces


def _unpack_rows(pieces):
    cols = []
    for p in pieces:
        cols.append(lax.bitcast_convert_type(lax.shift_left(p, 16), F32))
        cols.append(lax.bitcast_convert_type(p & jnp.int32(-65536), F32))
    return jnp.concatenate(cols, axis=1)


def _ada_kernel(c_ref, w_ref, b_ref, o_ref):
    s = _silu(c_ref[...]).astype(BF16)
    o_ref[...] = jnp.dot(s, w_ref[...].astype(BF16), preferred_element_type=F32) + b_ref[...]


def _ada(c, w, b):
    rows, n = c.shape[0], w.shape[1]
    tn = 1536
    return pl.pallas_call(
        _ada_kernel,
        grid=(n // tn,),
        in_specs=[pl.BlockSpec((rows, D_MODEL), lambda j: (0, 0)),
                  pl.BlockSpec((D_MODEL, tn), lambda j: (0, j)),
                  pl.BlockSpec((1, tn), lambda j: (0, j))],
        out_specs=pl.BlockSpec((rows, tn), lambda j: (0, j)),
        out_shape=jax.ShapeDtypeStruct((rows, n), F32),
        compiler_params=_params(("parallel",)),
        name="ada",
    )(c, w, b)


def _in_kernel(n_prompt_tiles, xp_ref, xs_ref, sh_ref, sc_ref, g1_ref, w_ref, vg_ref, qg_ref, kg_ref,
               cos_ref, sin_ref, wc_ref, bc_ref,
               a_ref, cq_ref, ckvp_ref, krp_ref, ckvs_ref, krs_ref, vs_ref):
    tm = xp_ref.shape[0]
    is_prompt = pl.program_id(0) < n_prompt_tiles
    x = jnp.where(is_prompt, xp_ref[...], xs_ref[...])
    h = _rms(x, g1_ref[...]) * (1.0 + sc_ref[...]) + sh_ref[...]
    z = jnp.dot(h.astype(BF16), w_ref[...], preferred_element_type=F32)
    u = _gelu(z[:, IN_U:IN_V])
    v = _rms(_gelu(z[:, IN_V:IN_CQ]), vg_ref[...])
    cq_ref[...] = _rms(z[:, IN_CQ:IN_CKV], qg_ref[...]).astype(BF16)
    ckv = _rms(z[:, IN_CKV:IN_KR], kg_ref[...])
    kr = z[:, IN_KR:IN_END]
    lane = lax.broadcasted_iota(jnp.int32, kr.shape, 1)
    swapped = jnp.where(lane < ROPE_HALF, pltpu.roll(kr, LANES - ROPE_HALF, 1), pltpu.roll(kr, ROPE_HALF, 1))
    kr_rot = (kr * cos_ref[...] + swapped * sin_ref[...])[:, :QK_ROPE]

    @pl.when(is_prompt)
    def _():
        ckvp_ref[...] = ckv
        krp_ref[...] = kr_rot

    @pl.when(jnp.logical_not(is_prompt))
    def _():
        ckvs_ref[...] = ckv
        krs_ref[...] = kr_rot
        vs_ref[...] = v

    vb = v.astype(BF16)
    half = A_WIDTH // 2
    grp_of_lane = lax.broadcasted_iota(jnp.int32, (CHUNK, half), 1) // A_GROUP_DIM
    for ci in range(tm // CHUNK):
        rows = slice(ci * CHUNK, (ci + 1) * CHUNK)
        parts = []
        for q in range(2):
            vq = vb[rows, q * half:(q + 1) * half]
            acc = None
            for gg in range(A_GROUPS // 2):
                vm = jnp.where(grp_of_lane == gg, vq, jnp.zeros_like(vq))
                part = jnp.dot(wc_ref[q * (A_GROUPS // 2) + gg], vm, preferred_element_type=F32)
                acc = part if acc is None else acc + part
            parts.append(acc)
        sp = jnp.concatenate(parts, axis=1) + bc_ref[...]
        a_ref[rows, :] = (u[rows, :] * sp).astype(BF16)


MOD_SHIFT1, MOD_SCALE1, MOD_GATE1, MOD_SHIFT2, MOD_SCALE2, MOD_GATE2 = range(6)


def _mod_spec(grp_of_tile, component):
    return pl.BlockSpec((None, ROW_TILE, D_MODEL), lambda i: (grp_of_tile(i), 0, component))


def _in_proj(x_p, x_s, mod_g, g1, w5, vg, qg, kg, cos_kr, sin_kr, wc, bc, grp_of_tile, n_prompt_tiles):
    t_p, t_s = x_p.shape[0], x_s.shape[0]
    t_all = t_p + t_s
    tm = ROW_TILE
    n_tiles = t_all // tm
    row = lambda i: (i, 0)
    prow = lambda i: (jnp.minimum(i, n_prompt_tiles - 1), 0)
    srow = lambda i: (jnp.maximum(i - n_prompt_tiles, 0), 0)
    fixed2 = lambda i: (0, 0)
    kind = lambda i: (jnp.where(i < n_prompt_tiles, 0, 1), 0, 0)
    kind4 = lambda i: (jnp.where(i < n_prompt_tiles, 0, 1), 0, 0, 0)
    return pl.pallas_call(
        functools.partial(_in_kernel, n_prompt_tiles),
        grid=(n_tiles,),
        in_specs=[pl.BlockSpec((tm, D_MODEL), prow),
                  pl.BlockSpec((tm, D_MODEL), srow),
                  _mod_spec(grp_of_tile, MOD_SHIFT1),
                  _mod_spec(grp_of_tile, MOD_SCALE1),
                  pl.BlockSpec((1, D_MODEL), fixed2),
                  pl.BlockSpec((D_MODEL, IN_END), fixed2),
                  pl.BlockSpec((1, A_WIDTH), fixed2),
                  pl.BlockSpec((1, Q_LORA), fixed2),
                  pl.BlockSpec((1, KV_LORA), fixed2),
                  pl.BlockSpec((tm, LANES), row),
                  pl.BlockSpec((tm, LANES), row),
                  pl.BlockSpec((None, A_GROUPS, CHUNK, CHUNK), kind4),
                  pl.BlockSpec((None, CHUNK, A_WIDTH), kind)],
        out_specs=[pl.BlockSpec((tm, A_WIDTH), row),
                   pl.BlockSpec((tm, Q_LORA), row),
                   pl.BlockSpec((tm, KV_LORA), prow),
                   pl.BlockSpec((tm, QK_ROPE), prow),
                   pl.BlockSpec((tm, KV_LORA), srow),
                   pl.BlockSpec((tm, QK_ROPE), srow),
                   pl.BlockSpec((tm, A_WIDTH), srow)],
        out_shape=[jax.ShapeDtypeStruct((t_all, A_WIDTH), BF16),
                   jax.ShapeDtypeStruct((t_all, Q_LORA), BF16),
                   jax.ShapeDtypeStruct((t_p, KV_LORA), F32),
                   jax.ShapeDtypeStruct((t_p, QK_ROPE), F32),
                   jax.ShapeDtypeStruct((t_s, KV_LORA), F32),
                   jax.ShapeDtypeStruct((t_s, QK_ROPE), F32),
                   jax.ShapeDtypeStruct((t_s, A_WIDTH), F32)],
        compiler_params=_params(("arbitrary",)),
        name="in_proj",
    )(x_p, x_s, mod_g, mod_g, g1, w5, vg, qg, kg, cos_kr, sin_kr, wc, bc)


def _qkv_kernel(cq_ref, ckv_ref, kr_ref, wuqt_ref, wukp_ref, place_ref, wuvt_ref, cos_ref, sin_ref,
                qt_ref, k_ref, vt_ref):
    tm = cq_ref.shape[0]
    ckv = ckv_ref[...].astype(BF16)
    kr = kr_ref[...].astype(BF16)
    qt = lax.dot_general(wuqt_ref[...], cq_ref[...], NT_DIMS, preferred_element_type=F32)
    c = cos_ref[...]
    s = sin_ref[...]
    pad = jnp.zeros((HEAD_PAD - QK_NOPE - QK_ROPE, tm), F32)
    for h in range(N_HEADS):
        blk = qt[h * HEAD_PAD:(h + 1) * HEAD_PAD]
        x1 = blk[QK_NOPE:QK_NOPE + ROPE_HALF]
        x2 = blk[QK_NOPE + ROPE_HALF:QK_NOPE + QK_ROPE]
        full = jnp.concatenate([blk[:QK_NOPE], x1 * c - x2 * s, x1 * s + x2 * c, pad], axis=0)
        qt_ref[h] = (full * (ATTN_SCALE * LOG2_E)).astype(BF16)
    k = (jnp.dot(ckv, wukp_ref[...], preferred_element_type=F32)
         + jnp.dot(kr, place_ref[...], preferred_element_type=F32))
    for h in range(N_HEADS):
        k_ref[h] = k[:, h * HEAD_PAD:(h + 1) * HEAD_PAD].astype(BF16)
    vt = lax.dot_general(wuvt_ref[...], ckv, NT_DIMS, preferred_element_type=F32)
    extra = V_ROWS - V_HEAD
    ones_row = jnp.where(lax.broadcasted_iota(jnp.int32, (extra, tm), 0) == 0, 1.0, 0.0)
    for h in range(N_HEADS):
        vt_ref[h] = jnp.concatenate([vt[h * V_HEAD:(h + 1) * V_HEAD], ones_row], axis=0).astype(BF16)


def _qkv(cq_all, ckv_all, kr_all, wuqt, wukp, place, wuvt, cos_t, sin_t, batch, seq):
    tm = ATTN_KV_TILE
    nk = seq // tm
    row = lambda b, j: (b * nk + j, 0)
    fixed2 = lambda b, j: (0, 0)
    tab = lambda b, j: (0, j)
    return pl.pallas_call(
        _qkv_kernel,
        grid=(batch, nk),
        in_specs=[pl.BlockSpec((tm, Q_LORA), row),
                  pl.BlockSpec((tm, KV_LORA), row),
                  pl.BlockSpec((tm, QK_ROPE), row),
                  pl.BlockSpec(wuqt.shape, fixed2),
                  pl.BlockSpec(wukp.shape, fixed2),
                  pl.BlockSpec(place.shape, fixed2),
                  pl.BlockSpec(wuvt.shape, fixed2),
                  pl.BlockSpec((ROPE_HALF, tm), tab),
                  pl.BlockSpec((ROPE_HALF, tm), tab)],
        out_specs=[pl.BlockSpec((None, N_HEADS, HEAD_PAD, tm), lambda b, j: (b, 0, 0, j)),
                   pl.BlockSpec((None, N_HEADS, tm, HEAD_PAD), lambda b, j: (b, 0, j, 0)),
                   pl.BlockSpec((None, N_HEADS, None, V_ROWS, tm), lambda b, j: (b, 0, j, 0, 0))],
        out_shape=[jax.ShapeDtypeStruct((batch, N_HEADS, HEAD_PAD, seq), BF16),
                   jax.ShapeDtypeStruct((batch, N_HEADS, seq, HEAD_PAD), BF16),
                   jax.ShapeDtypeStruct((batch, N_HEADS, nk, V_ROWS, tm), BF16)],
        compiler_params=_params(("parallel", "parallel")),
        name="qkv",
    )(cq_all, ckv_all, kr_all, wuqt, wukp, place, wuvt, cos_t, sin_t)


def _attn_kernel(qt_ref, k_ref, vt_ref, o_ref):
    tk = ATTN_KV_TILE
    n_heads = qt_ref.shape[0]
    n_sub = qt_ref.shape[2] // tk
    qi = pl.program_id(2)
    j0 = qi * n_sub
    chains = [(h, sub) for h in range(n_heads) for sub in range(n_sub)]

    def tiles(j, active, carries, diagonal_sub):
        scores = [jnp.dot(k_ref[h, j], qt_ref[h, :, sub * tk:(sub + 1) * tk], preferred_element_type=F32)
                  for h, sub in active]
        stats = []
        for (h, sub), s in zip(active, scores):
            m, _ = carries[h * n_sub + sub]
            if sub == diagonal_sub:
                key = lax.broadcasted_iota(jnp.int32, s.shape, 0)
                qry = lax.broadcasted_iota(jnp.int32, s.shape, 1)
                s = jnp.where(key <= qry, s, -jnp.inf)
            m_new = jnp.maximum(m, jnp.max(s, axis=0, keepdims=True))
            stats.append((m_new, jnp.exp2(m - m_new), jnp.exp2(s - m_new).astype(BF16)))
        out = list(carries)
        for (h, sub), (m_new, alpha, p) in zip(active, stats):
            c = h * n_sub + sub
            out[c] = (m_new, alpha * carries[c][1] + jnp.dot(vt_ref[h, j], p, preferred_element_type=F32))
        return out

    init = tuple((jnp.full((1, tk), -jnp.inf, F32), jnp.zeros((V_ROWS, tk), F32)) for _ in chains)
    carries = list(lax.fori_loop(0, j0, lambda j, c: tuple(tiles(j, chains, c, None)), init))
    for jj in range(n_sub):
        carries = tiles(j0 + jj, [(h, sub) for h, sub in chains if sub >= jj], carries, jj)
    for h, sub in chains:
        _, acc = carries[h * n_sub + sub]
        o_ref[h * V_HEAD:(h + 1) * V_HEAD, sub * tk:(sub + 1) * tk] = (
            acc[:V_HEAD] / acc[V_HEAD:V_HEAD + 1]).astype(BF16)


def _attn(qt, k, vt):
    batch, _, _, seq = qt.shape
    tk = ATTN_KV_TILE
    nk = seq // tk
    hp = ATTN_HEADS_PER_STEP
    k5 = k.reshape(batch, N_HEADS, nk, tk, HEAD_PAD)
    return pl.pallas_call(
        _attn_kernel,
        grid=(batch, N_HEADS // hp, seq // ATTN_Q_STEP),
        in_specs=[pl.BlockSpec((None, hp, HEAD_PAD, ATTN_Q_STEP), lambda b, h, i: (b, h, 0, i)),
                  pl.BlockSpec((None, hp, nk, tk, HEAD_PAD), lambda b, h, i: (b, h, 0, 0, 0)),
                  pl.BlockSpec((None, hp, nk, V_ROWS, tk), lambda b, h, i: (b, h, 0, 0, 0))],
        out_specs=pl.BlockSpec((None, hp * V_HEAD, ATTN_Q_STEP), lambda b, h, i: (b, h, i)),
        out_shape=jax.ShapeDtypeStruct((batch, N_HEADS * V_HEAD, seq), BF16),
        compiler_params=_params(("parallel", "parallel", "parallel")),
        name="attn",
    )(qt, k5, vt)


def _sq_kernel(cq_ref, wq_ref, wukb_ref, cos_ref, sin_ref, ql_ref, r1_ref, r2_ref):
    n_nope = N_HEADS * QK_NOPE
    q = jnp.dot(cq_ref[...], wq_ref[...], preferred_element_type=F32)
    x1 = q[:, n_nope:n_nope + LANES]
    x2 = q[:, n_nope + LANES:]
    c = cos_ref[...]
    s = sin_ref[...]
    r1_ref[...] = ((x1 * c - x2 * s) * ATTN_SCALE).astype(BF16)
    r2_ref[...] = ((x1 * s + x2 * c) * ATTN_SCALE).astype(BF16)
    ql = jnp.dot(q[:, :n_nope].astype(BF16), wukb_ref[...], preferred_element_type=F32)
    ql_ref[...] = (ql * ATTN_SCALE).astype(BF16)


def _sample_q(cq_s, wq_s, wuk_blk, cos_s, sin_s):
    ts = cq_s.shape[0]
    full = lambda shape: pl.BlockSpec(shape, lambda i: (0,) * len(shape))
    return pl.pallas_call(
        _sq_kernel,
        grid=(1,),
        in_specs=[full(cq_s.shape), full(wq_s.shape), full(wuk_blk.shape), full(cos_s.shape), full(sin_s.shape)],
        out_specs=[full((ts, N_HEADS * KV_LORA)), full((ts, LANES)), full((ts, LANES))],
        out_shape=[jax.ShapeDtypeStruct((ts, N_HEADS * KV_LORA), BF16),
                   jax.ShapeDtypeStruct((ts, LANES), BF16),
                   jax.ShapeDtypeStruct((ts, LANES), BF16)],
        compiler_params=_params(("arbitrary",)),
        name="sample_q",
    )(cq_s, wq_s, wuk_blk, cos_s, sin_s)


def _sattn_kernel(n_pages, pt_ref, ckv_hbm, kr_hbm, ql_ref, qr_ref, cnew_ref, knew_ref, o_ref,
                  cbuf, rbuf, sem_c, sem_r):
    npg = PAGES_PER_STEP
    n_chunks = n_pages // npg
    PAGE_SLOTS = cbuf.shape[0]
    PAGE_LOOKAHEAD = PAGE_SLOTS - 1
    b = pl.program_id(0)

    def page_copies(bb, chunk, slot, p):
        page = pt_ref[bb * n_pages + chunk * npg + p]
        return (pltpu.make_async_copy(ckv_hbm.at[page], cbuf.at[slot, p], sem_c.at[slot]),
                pltpu.make_async_copy(kr_hbm.at[page], rbuf.at[slot, p], sem_r.at[slot]))

    def start_chunk(bb, chunk, slot):
        for p in range(npg):
            for cp in page_copies(bb, chunk, slot, p):
                cp.start()

    def wait_chunk(bb, chunk, slot):
        for p in range(npg):
            for cp in page_copies(bb, chunk, slot, p):
                cp.wait()

    @pl.when(b == 0)
    def _():
        for chunk in range(PAGE_LOOKAHEAD):
            start_chunk(0, chunk, chunk)

    ql = ql_ref[...]
    qr = qr_ref[...]

    def partial_softmax(blocks):
        scores = [lax.dot_general(ql, keys, NT_DIMS, preferred_element_type=F32)
                  + jnp.dot(qr, rope_t, preferred_element_type=F32) for keys, rope_t, _ in blocks]
        probs = []
        for s, (_, _, mask) in zip(scores, blocks):
            if mask is not None:
                s = jnp.where(mask(s.shape), s, -jnp.inf)
            m = jnp.max(s, axis=-1, keepdims=True)
            p = jnp.exp(s - m)
            probs.append((m, jnp.sum(p, axis=-1, keepdims=True), p.astype(BF16)))
        return [(m, l, jnp.dot(p, blk[0], preferred_element_type=F32)) for (m, l, p), blk in zip(probs, blocks)]

    def merge(state, parts):
        m_old, l_old, acc_old = state
        m_new = m_old
        for m, _, _ in parts:
            m_new = jnp.maximum(m_new, m)
        alpha = jnp.exp(m_old - m_new)
        l = alpha * l_old
        acc = alpha * acc_old
        for m, lp, op in parts:
            w = jnp.exp(m - m_new)
            l = l + w * lp
            acc = acc + w * op
        return m_new, l, acc

    def chunk_step(chunk, state, extra_blocks=()):
        slot = chunk % PAGE_SLOTS
        wait_chunk(b, chunk, slot)
        ahead = chunk + PAGE_LOOKAHEAD
        ahead_b = jnp.where(ahead >= n_chunks, b + 1, b)
        ahead_b = jnp.where(ahead_b >= pl.num_programs(0), 0, ahead_b)
        start_chunk(ahead_b, ahead % n_chunks, ahead % PAGE_SLOTS)

        blocks = []
        for g in range(npg // PAGE_GROUP):
            pages = range(g * PAGE_GROUP, (g + 1) * PAGE_GROUP)
            blocks.append((jnp.concatenate([cbuf[slot, p].astype(BF16) for p in pages], axis=0),
                           jnp.concatenate([rbuf[slot, p].astype(BF16) for p in pages], axis=1), None))
        return merge(state, partial_softmax(blocks + list(extra_blocks)))

    def causal(shape):
        q_pos = lax.broadcasted_iota(jnp.int32, shape, 0) // N_HEADS
        return lax.broadcasted_iota(jnp.int32, shape, 1) <= q_pos

    rows = ql.shape[0]
    state = (jnp.full((rows, 1), -jnp.inf, F32), jnp.zeros((rows, 1), F32), jnp.zeros((rows, KV_LORA), F32))
    state = lax.fori_loop(0, n_chunks - 1, chunk_step, state)
    new_rows = (cnew_ref[...].astype(BF16), knew_ref[...].astype(BF16), causal)
    _, l, acc = chunk_step(n_chunks - 1, state, [new_rows])
    o_ref[...] = acc / l

    @pl.when(b == pl.num_programs(0) - 1)
    def _():
        for chunk in range(PAGE_LOOKAHEAD):
            wait_chunk(0, chunk, chunk % PAGE_SLOTS)


def _sample_attn(page_table, cache_ckv, cache_kr_t, ql, qr, cnew, knew_t):
    dec_b, n_pages = page_table.shape
    npg = PAGES_PER_STEP
    n_chunks = n_pages // npg
    slots = 4 if n_chunks % 4 == 0 else 2
    assert n_chunks % slots == 0
    rows = ql.shape[1]
    n_new = cnew.shape[1]
    per_b = lambda b, pt: (b, 0, 0)
    return pl.pallas_call(
        functools.partial(_sattn_kernel, n_pages),
        grid_spec=pltpu.PrefetchScalarGridSpec(
            num_scalar_prefetch=1,
            grid=(dec_b,),
            in_specs=[pl.BlockSpec(memory_space=pl.ANY),
                      pl.BlockSpec(memory_space=pl.ANY),
                      pl.BlockSpec((None, rows, KV_LORA), per_b),
                      pl.BlockSpec((None, rows, QK_ROPE), per_b),
                      pl.BlockSpec((None, n_new, KV_LORA), per_b),
                      pl.BlockSpec((None, QK_ROPE, n_new), per_b)],
            out_specs=pl.BlockSpec((None, rows, KV_LORA), per_b),
            scratch_shapes=[pltpu.VMEM((slots, npg, CHUNK, KV_LORA), F32),
                            pltpu.VMEM((slots, npg, QK_ROPE, CHUNK), F32),
                            pltpu.SemaphoreType.DMA((slots,)),
                            pltpu.SemaphoreType.DMA((slots,))]),
        out_shape=jax.ShapeDtypeStruct((dec_b, rows, KV_LORA), F32),
        compiler_params=_params(("arbitrary",)),
        name="sample_attn",
    )(page_table.reshape(-1), cache_ckv, cache_kr_t, ql, qr, cnew, knew_t)


def _mm_kernel(x_ref, w_ref, o_ref):
    o_ref[...] = jnp.dot(x_ref[...].astype(BF16), w_ref[...], preferred_element_type=F32).astype(o_ref.dtype)


def _matmul(x, w, out_dtype):
    m, n = x.shape[0], w.shape[1]
    full = lambda shape: pl.BlockSpec(shape, lambda i: (0,) * len(shape))
    return pl.pallas_call(
        _mm_kernel,
        grid=(1,),
        in_specs=[full(x.shape), full(w.shape)],
        out_specs=full((m, n)),
        out_shape=jax.ShapeDtypeStruct((m, n), out_dtype),
        compiler_params=_params(("arbitrary",)),
        name="matmul",
    )(x, w)


def _merge_kernel(n_prompt_tiles, xp_ref, xs_ref, sh1_ref, sc1_ref, gt1_ref, sh2_ref, sc2_ref, g1_ref, g2_ref,
                  a_ref, ot_ref, os_ref, wg_ref, woa_ref, wob_ref, wo_ref, x1_ref, *rest):
    h2_refs, yb_ref = rest[:SC_SPLIT], rest[SC_SPLIT]
    i = pl.program_id(0)
    x = jnp.where(i < n_prompt_tiles, xp_ref[...], xs_ref[...])
    h = _rms(x, g1_ref[...]) * (1.0 + sc1_ref[...]) + sh1_ref[...]
    gates = _sigmoid(jnp.dot(h.astype(BF16), wg_ref[...], preferred_element_type=F32))
    y_a = jnp.dot(a_ref[...], woa_ref[...], preferred_element_type=F32)

    @pl.when(i < n_prompt_tiles)
    def _():
        yb_ref[...] = lax.dot_general(ot_ref[...], wob_ref[...], TN_DIMS, preferred_element_type=F32)

    @pl.when(i >= n_prompt_tiles)
    def _():
        yb_ref[...] = jnp.dot(os_ref[...], wob_ref[...], preferred_element_type=F32)

    z = gates[:, :D_MODEL] * y_a + gates[:, D_MODEL:] * yb_ref[...]
    y = jnp.dot(z.astype(BF16), wo_ref[...], preferred_element_type=F32)
    x1 = x + gt1_ref[...] * y
    x1_ref[...] = x1
    for ref, piece in zip(h2_refs, _pack_rows(_rms(x1, g2_ref[...]) * (1.0 + sc2_ref[...]) + sh2_ref[...])):
        ref[...] = piece


def _merge(x_p, x_s, mod_g, g1, g2, a_all, ot, o_s, wg, woa, wob, wo, grp_of_tile, n_prompt_tiles):
    t_all = x_p.shape[0] + x_s.shape[0]
    tm = ROW_TILE
    seq = ot.shape[2]
    tpb = seq // tm
    row = lambda i: (i, 0)
    fixed2 = lambda i: (0, 0)
    mod_comps = (MOD_SHIFT1, MOD_SCALE1, MOD_GATE1, MOD_SHIFT2, MOD_SCALE2)

    def ot_map(i):
        ic = jnp.minimum(i, n_prompt_tiles - 1)
        return (ic // tpb, 0, ic % tpb)

    os_map = lambda i: (jnp.maximum(i - n_prompt_tiles, 0), 0)
    return pl.pallas_call(
        functools.partial(_merge_kernel, n_prompt_tiles),
        grid=(t_all // tm,),
        in_specs=[pl.BlockSpec((tm, D_MODEL), lambda i: (jnp.minimum(i, n_prompt_tiles - 1), 0)),
                  pl.BlockSpec((tm, D_MODEL), os_map)]
        + [_mod_spec(grp_of_tile, c) for c in mod_comps]
        + [pl.BlockSpec((1, D_MODEL), fixed2),
                  pl.BlockSpec((1, D_MODEL), fixed2),
                  pl.BlockSpec((tm, A_WIDTH), row),
                  pl.BlockSpec((None, N_HEADS * V_HEAD, tm), ot_map),
                  pl.BlockSpec((tm, N_HEADS * V_HEAD), os_map),
                  pl.BlockSpec(wg.shape, fixed2),
                  pl.BlockSpec(woa.shape, fixed2),
                  pl.BlockSpec(wob.shape, fixed2),
                  pl.BlockSpec(wo.shape, fixed2)],
        out_specs=[pl.BlockSpec((tm, D_MODEL), row)] + [pl.BlockSpec((tm, PIECE), row)] * SC_SPLIT,
        out_shape=[jax.ShapeDtypeStruct((t_all, D_MODEL), F32)]
        + [jax.ShapeDtypeStruct((t_all, PIECE), jnp.int32)] * SC_SPLIT,
        scratch_shapes=[pltpu.VMEM((tm, D_MODEL), F32)],
        compiler_params=_params(("parallel",)),
        name="merge",
    )(x_p, x_s, *([mod_g] * len(mod_comps)), g1, g2, a_all, ot, o_s, wg, woa, wob, wo)


def _first_argmax(v, rows):
    mx = jnp.max(v, axis=0, keepdims=True)
    idx = jnp.min(jnp.where(v == mx, rows, v.shape[0]), axis=0, keepdims=True)
    return mx, idx


def _route_kernel(x1_ref, sh2_ref, sc2_ref, g2_ref, whi_ref, wlo_ref, bias_ref, upper_ref,
                  idx_ref, wt_ref, rank_ref, count_ref, run_ref):
    h2 = _rms(x1_ref[...], g2_ref[...]) * (1.0 + sc2_ref[...]) + sh2_ref[...]
    hi = h2.astype(BF16)
    lo = (h2 - hi.astype(F32)).astype(BF16)
    whi = whi_ref[...]
    logits = (lax.dot_general(whi, hi, NT_DIMS, preferred_element_type=F32)
              + lax.dot_general(whi, lo, NT_DIMS, preferred_element_type=F32)
              + lax.dot_general(wlo_ref[...], hi, NT_DIMS, preferred_element_type=F32))
    scores = _sigmoid(logits)
    sel = scores + bias_ref[...]
    tm = sel.shape[1]
    neg = -jnp.inf
    rows_g = lax.broadcasted_iota(jnp.int32, (GROUP_SIZE, tm), 0)
    gscore = []
    for g in range(N_GROUPS):
        blk = sel[g * GROUP_SIZE:(g + 1) * GROUP_SIZE]
        m1, i1 = _first_argmax(blk, rows_g)
        m2 = jnp.max(jnp.where(rows_g == i1, neg, blk), axis=0, keepdims=True)
        gscore.append(m1 + m2)
    gs = jnp.concatenate(gscore, axis=0)
    rows_8 = lax.broadcasted_iota(jnp.int32, gs.shape, 0)
    chosen = jnp.zeros(gs.shape, jnp.int32)
    for _ in range(TOPK_GROUPS):
        _, gi = _first_argmax(gs, rows_8)
        hit = rows_8 == gi
        chosen = jnp.where(hit, 1, chosen)
        gs = jnp.where(hit, neg, gs)
    cand = jnp.concatenate(
        [jnp.where(chosen[g:g + 1] > 0, sel[g * GROUP_SIZE:(g + 1) * GROUP_SIZE], neg) for g in range(N_GROUPS)],
        axis=0)
    rows_e = lax.broadcasted_iota(jnp.int32, cand.shape, 0)
    idxs, wts, hits = [], [], []
    for _ in range(TOP_K):
        _, ei = _first_argmax(cand, rows_e)
        hit = rows_e == ei
        idxs.append(ei)
        hits.append(hit)
        wts.append(jnp.sum(jnp.where(hit, scores, 0.0), axis=0, keepdims=True))
        cand = jnp.where(hit, neg, cand)
    w = jnp.concatenate(wts, axis=0)
    idx_ref[...] = jnp.concatenate(idxs, axis=0)
    wt_ref[...] = w / jnp.sum(w, axis=0, keepdims=True) * ROUTED_SCALE

    @pl.when(pl.program_id(0) == 0)
    def _():
        run_ref[...] = jnp.zeros(run_ref.shape, F32)

    run = run_ref[...]
    ones = jnp.ones((tm, LANES), BF16)
    onehots = [jnp.where(hit, 1.0, 0.0).astype(BF16) for hit in hits]
    befores = [jnp.dot(oh, upper_ref[...], preferred_element_type=F32) for oh in onehots]
    totals = [jnp.dot(oh, ones, preferred_element_type=F32) for oh in onehots]
    ranks = []
    for hit, before, total in zip(hits, befores, totals):
        prior = jnp.concatenate([run] * (tm // LANES), axis=1)
        ranks.append(jnp.sum(jnp.where(hit, before + prior, 0.0), axis=0, keepdims=True))
        run = run + total
    rank_ref[...] = jnp.concatenate(ranks, axis=0).astype(jnp.int32)
    run_ref[...] = run
    count_ref[...] = run


def _route(x1_all, mod_g, g2, whi, wlo, bias_col, grp_of_tile):
    t_all = x1_all.shape[0]
    tm = ROW_TILE
    fixed2 = lambda i: (0, 0)
    col = lambda i: (0, i)
    upper = (lax.broadcasted_iota(jnp.int32, (tm, tm), 0) < lax.broadcasted_iota(jnp.int32, (tm, tm), 1)).astype(BF16)
    return pl.pallas_call(
        _route_kernel,
        grid=(t_all // tm,),
        in_specs=[pl.BlockSpec((tm, D_MODEL), lambda i: (i, 0)),
                  _mod_spec(grp_of_tile, MOD_SHIFT2),
                  _mod_spec(grp_of_tile, MOD_SCALE2),
                  pl.BlockSpec((1, D_MODEL), fixed2),
                  pl.BlockSpec(whi.shape, fixed2),
                  pl.BlockSpec(wlo.shape, fixed2),
                  pl.BlockSpec(bias_col.shape, fixed2),
                  pl.BlockSpec((tm, tm), fixed2)],
        out_specs=[pl.BlockSpec((TOP_K, tm), col), pl.BlockSpec((TOP_K, tm), col), pl.BlockSpec((TOP_K, tm), col),
                   pl.BlockSpec((N_EXPERTS, LANES), fixed2)],
        out_shape=[jax.ShapeDtypeStruct((TOP_K, t_all), jnp.int32),
                   jax.ShapeDtypeStruct((TOP_K, t_all), F32),
                   jax.ShapeDtypeStruct((TOP_K, t_all), jnp.int32),
                   jax.ShapeDtypeStruct((N_EXPERTS, LANES), F32)],
        scratch_shapes=[pltpu.VMEM((N_EXPERTS, LANES), F32)],
        compiler_params=_params(("arbitrary",)),
        name="route",
    )(x1_all, mod_g, mod_g, g2, whi, wlo, bias_col, upper)


def _moe_kernel(be_ref, nv_ref, next_ref, *refs):
    x_refs = refs[:SC_SPLIT]
    wgu_hbm, wdn_hbm = refs[SC_SPLIT:SC_SPLIT + 2]
    y_refs = refs[SC_SPLIT + 2:2 * SC_SPLIT + 2]
    gu_f32, dn_f32, gu_bf, dn_bf, sem, run_ref = refs[2 * SC_SPLIT + 2:]
    i = pl.program_id(0)
    expert = be_ref[i]
    prev = be_ref[jnp.maximum(i - 1, 0)]

    def weight_copies(e, slot):
        return (pltpu.make_async_copy(wgu_hbm.at[e], gu_f32.at[slot], sem.at[0, slot]),
                pltpu.make_async_copy(wdn_hbm.at[e], dn_f32.at[slot], sem.at[1, slot]))

    n_wslots = gu_f32.shape[0]

    def owner_after(e):
        return next_ref[jnp.minimum(e, N_EXPERTS - 1)]

    @pl.when(i == 0)
    def _():
        run_ref[0] = 0
        for cp in weight_copies(expert, 0):
            cp.start(priority=1)

        @pl.when(owner_after(expert) < N_EXPERTS)
        def _():
            for cp in weight_copies(owner_after(expert), 1):
                cp.start(priority=1)

    @pl.when(i < nv_ref[0])
    def _():
        @pl.when((i == 0) | (expert != prev))
        def _():
            run = run_ref[0] + jnp.where(i == 0, 0, 1)
            run_ref[0] = run
            slot = run % n_wslots
            for cp in weight_copies(expert, slot):
                cp.wait()
            following = owner_after(expert)
            second = owner_after(following)

            @pl.when((following < N_EXPERTS) & (second < N_EXPERTS))
            def _():
                for cp in weight_copies(second, (run + 2) % n_wslots):
                    cp.start(priority=1)

            gu_bf[...] = gu_f32[slot].astype(BF16)
            dn_bf[...] = dn_f32[slot].astype(BF16)

        subs = [slice(s * MOE_SUB, (s + 1) * MOE_SUB) for s in range(MOE_BLOCK // MOE_SUB)]
        gus = [jnp.dot(_unpack_rows([r[rows, :] for r in x_refs]).astype(BF16), gu_bf[...],
                       preferred_element_type=F32) for rows in subs]
        hids = [(_silu(gu[:, :EXPERT_DIM]) * gu[:, EXPERT_DIM:]).astype(BF16) for gu in gus]
        ys = [jnp.dot(hid, dn_bf[...], preferred_element_type=F32) for hid in hids]
        for rows, y in zip(subs, ys):
            for ref, piece in zip(y_refs, _pack_rows(y)):
                ref[rows, :] = piece


def _moe(block_e, n_valid, next_expert, x_sorted, w_gu, w_dn):
    n_slots = x_sorted[0].shape[0]
    blk = MOE_BLOCK
    n_blocks = n_slots // blk
    rows = lambda i, be, nv, nx: (jnp.minimum(i, nv[0] - 1), 0)
    return pl.pallas_call(
        _moe_kernel,
        grid_spec=pltpu.PrefetchScalarGridSpec(
            num_scalar_prefetch=3,
            grid=(n_blocks,),
            in_specs=[pl.BlockSpec((blk, PIECE), rows)] * SC_SPLIT
            + [pl.BlockSpec(memory_space=pl.ANY), pl.BlockSpec(memory_space=pl.ANY)],
            out_specs=[pl.BlockSpec((blk, PIECE), rows)] * SC_SPLIT,
            scratch_shapes=[pltpu.VMEM((MOE_WEIGHT_SLOTS, D_MODEL, 2 * EXPERT_DIM), F32),
                            pltpu.VMEM((MOE_WEIGHT_SLOTS, EXPERT_DIM, D_MODEL), F32),
                            pltpu.VMEM((D_MODEL, 2 * EXPERT_DIM), BF16),
                            pltpu.VMEM((EXPERT_DIM, D_MODEL), BF16),
                            pltpu.SemaphoreType.DMA((2, MOE_WEIGHT_SLOTS)),
                            pltpu.SMEM((1,), jnp.int32)]),
        out_shape=[jax.ShapeDtypeStruct((n_slots, PIECE), jnp.int32)] * SC_SPLIT,
        compiler_params=_params(("arbitrary",)),
        name="moe",
    )(block_e, n_valid, next_expert, *x_sorted, w_gu, w_dn)


def _dest_kernel(e_ref, rank_ref, start_ref, dest_ref):
    e = e_ref[...]
    rows = lax.broadcasted_iota(jnp.int32, (N_EXPERTS, e.shape[1]), 0)
    start = jnp.sum(jnp.where(rows == e, start_ref[...], 0), axis=0, keepdims=True)
    dest_ref[...] = start + rank_ref[...]


def _dest(e_flat, rank, pad_start_col):
    n_assign = e_flat.shape[1]
    n = RANK_TILE
    tile = pl.BlockSpec((1, n), lambda i: (0, i))
    return pl.pallas_call(
        _dest_kernel,
        grid=(n_assign // n,),
        in_specs=[tile, tile, pl.BlockSpec((N_EXPERTS, 1), lambda i: (0, 0))],
        out_specs=tile,
        out_shape=jax.ShapeDtypeStruct((1, n_assign), jnp.int32),
        compiler_params=_params(("parallel",)),
        name="dest",
    )(e_flat, rank, pad_start_col)


def _sc_mesh():
    return plsc.VectorSubcoreMesh(core_axis_name="core", subcore_axis_name="subcore")


def _sc_scatter_rows(srcs, dest, n_slots):
    n_src, width = srcs[0].shape
    n_assign = dest.shape[1]
    src_blocks = n_src // SC_WINDOW
    n_pieces = len(srcs)

    @functools.partial(pl.kernel, out_type=[jax.ShapeDtypeStruct((n_slots, width), s.dtype) for s in srcs],
                       mesh=_sc_mesh(), scratch_types=[])
    def scatter(*refs):
        src_hbms, dest_hbm, out_hbms = refs[:n_pieces], refs[n_pieces], refs[n_pieces + 1:]
        for src_hbm, out_hbm in zip(src_hbms, out_hbms):
            def body(rows_vmem, dest_vmem, out_hbm=out_hbm):
                pltpu.sync_copy(rows_vmem, out_hbm.at[dest_vmem.at[0]])

            pltpu.emit_pipeline(
                body,
                grid=(n_assign // SC_WINDOW,),
                in_specs=[pl.BlockSpec((SC_WINDOW, width), lambda i: (i % src_blocks, 0)),
                          pl.BlockSpec((1, SC_WINDOW), lambda i: (0, i))],
                out_specs=[],
                core_axis_name=("core", "subcore"),
                dimension_semantics=(pltpu.PARALLEL,),
            )(src_hbm, dest_hbm)

    return scatter(*srcs, dest)


def _sc_gather_rows(tables, idx):
    width = tables[0].shape[1]
    n_assign = idx.shape[1]
    n_pieces = len(tables)

    @functools.partial(pl.kernel, out_type=[jax.ShapeDtypeStruct((n_assign, width), t.dtype) for t in tables],
                       mesh=_sc_mesh(), scratch_types=[])
    def gather(*refs):
        table_hbms, idx_hbm, out_hbms = refs[:n_pieces], refs[n_pieces], refs[n_pieces + 1:]
        for table_hbm, out_hbm in zip(table_hbms, out_hbms):
            def body(idx_vmem, rows_vmem, table_hbm=table_hbm):
                pltpu.sync_copy(table_hbm.at[idx_vmem.at[0]], rows_vmem)

            pltpu.emit_pipeline(
                body,
                grid=(n_assign // SC_WINDOW,),
                in_specs=[pl.BlockSpec((1, SC_WINDOW), lambda i: (0, i))],
                out_specs=[pl.BlockSpec((SC_WINDOW, width), lambda i: (i, 0))],
                core_axis_name=("core", "subcore"),
                dimension_semantics=(pltpu.PARALLEL,),
            )(idx_hbm, out_hbm)

    return gather(*tables, idx)


def _final_kernel(n_prompt_tiles, x1_ref, *refs):
    h2_refs = refs[:SC_SPLIT]
    yg_refs = refs[SC_SPLIT:2 * SC_SPLIT]
    wt_ref, gt2_ref, wsg_ref, wsd_ref, gf_ref, op_ref, os_ref = refs[2 * SC_SPLIT:]
    is_prompt = pl.program_id(0) < n_prompt_tiles
    h2 = _unpack_rows([r[...] for r in h2_refs]).astype(BF16)
    gu = jnp.dot(h2, wsg_ref[...], preferred_element_type=F32)
    hid = _silu(gu[:, :SHARED_DIM]) * gu[:, SHARED_DIM:]
    y = jnp.dot(hid.astype(BF16), wsd_ref[...], preferred_element_type=F32)
    wt = wt_ref[...]
    for k in range(TOP_K):
        y = y + wt[:, k:k + 1] * _unpack_rows([r[k] for r in yg_refs])
    x2 = x1_ref[...] + gt2_ref[...] * y
    out = _rms(x2, gf_ref[...])

    @pl.when(is_prompt)
    def _():
        op_ref[...] = out

    @pl.when(jnp.logical_not(is_prompt))
    def _():
        os_ref[...] = out


def _final(x1_all, h2_all, y_gath, wt_rows, mod_g, wsg, wsd, gf, grp_of_tile, n_prompt_tiles):
    t_all = x1_all.shape[0]
    tm = ROW_TILE
    t_p = n_prompt_tiles * tm
    row = lambda i: (i, 0)
    fixed2 = lambda i: (0, 0)
    return pl.pallas_call(
        functools.partial(_final_kernel, n_prompt_tiles),
        grid=(t_all // tm,),
        in_specs=[pl.BlockSpec((tm, D_MODEL), row)]
        + [pl.BlockSpec((tm, PIECE), row)] * SC_SPLIT
        + [pl.BlockSpec((TOP_K, tm, PIECE), lambda i: (0, i, 0))] * SC_SPLIT
        + [pl.BlockSpec((tm, TOP_K), row),
           _mod_spec(grp_of_tile, MOD_GATE2),
           pl.BlockSpec(wsg.shape, fixed2),
           pl.BlockSpec(wsd.shape, fixed2),
           pl.BlockSpec((1, D_MODEL), fixed2)],
        out_specs=[pl.BlockSpec((tm, D_MODEL), lambda i: (jnp.minimum(i, n_prompt_tiles - 1), 0)),
                   pl.BlockSpec((tm, D_MODEL), lambda i: (jnp.maximum(i - n_prompt_tiles, 0), 0))],
        out_shape=[jax.ShapeDtypeStruct((t_p, D_MODEL), F32),
                   jax.ShapeDtypeStruct((t_all - t_p, D_MODEL), F32)],
        compiler_params=_params(("arbitrary",)),
        name="final",
    )(x1_all, *h2_all, *y_gath, wt_rows, mod_g, wsg, wsd, gf)


def _rope_tables(pos):
    freqs = ROPE_THETA ** (-jnp.arange(ROPE_HALF, dtype=F32) / ROPE_HALF)
    ang = pos.astype(F32)[:, None] * freqs
    return jnp.cos(ang), jnp.sin(ang)


def kernel(x_prompt, x_sample, cache_ckv, cache_krope, page_table, c_prompt, c_sample, w_ada, b_ada, norm1_g, norm2_g, w_in, a_vnorm_g, w_spatial, b_spatial, w_out_a, q_norm_g, w_uq, kv_norm_g, w_uk, w_uv, w_out_b, w_o, w_router, router_bias, w_exp_gu, w_exp_down, w_sh_gu, w_sh_down, final_norm_g):
    batch, seq, _ = x_prompt.shape
    dec_b, dec_s, _ = x_sample.shape
    n_pages = page_table.shape[1]
    n_past = n_pages * CHUNK
    t_p, t_s = batch * seq, dec_b * dec_s
    t_all = t_p + t_s
    tm = ROW_TILE
    assert seq % ATTN_Q_STEP == 0 and t_s % tm == 0 and CHUNK % dec_s == 0 and n_pages % PAGES_PER_STEP == 0
    assert (t_all * TOP_K) % RANK_TILE == 0 and w_ada.shape[0] == 1
    n_prompt_tiles = t_p // tm
    tiles_per_b = seq // tm

    def grp_of_tile(i):
        return jnp.minimum(i // tiles_per_b, batch) + jnp.maximum(i - n_prompt_tiles, 0)

    n_c = batch + dec_b
    c_rows = -(-n_c // 8) * 8
    c_all = jnp.concatenate([c_prompt, c_sample, jnp.zeros((c_rows - n_c, D_MODEL), F32)], axis=0)
    mod = _ada(c_all, w_ada[0], b_ada[0][None, :])
    mod_p = jnp.broadcast_to(mod[:batch, None, :], (batch, tm, 6 * D_MODEL))
    mod_s = jnp.repeat(mod[batch:n_c], dec_s, axis=0).reshape(t_s // tm, tm, 6 * D_MODEL)
    mod_g = jnp.concatenate([mod_p, mod_s], axis=0)

    o_u, o_v, o_cq, o_ckv, o_kr = A_WIDTH, 2 * A_WIDTH, 2 * A_WIDTH + Q_LORA, 2 * A_WIDTH + Q_LORA + KV_LORA, \
        2 * A_WIDTH + Q_LORA + KV_LORA + QK_ROPE
    win = w_in[0]
    w5 = jnp.concatenate([win[:, :o_kr], jnp.zeros((D_MODEL, IN_END - IN_KR - QK_ROPE), F32)], axis=1).astype(BF16)
    wg = win[:, o_kr:].astype(BF16)
    row1 = lambda v: v.reshape(1, -1)
    tri = jnp.tril(jnp.ones((CHUNK, CHUNK), F32))
    wc_p = w_spatial[0] * tri
    per = CHUNK // dec_s
    small = (w_spatial[0] * tri)[:, :dec_s, :dec_s]
    wc_s = jnp.einsum('ab,gts->gatbs', jnp.eye(per, dtype=F32), small).reshape(A_GROUPS, CHUNK, CHUNK)
    wc = jnp.stack([wc_p, wc_s]).astype(BF16)
    bias_p = jnp.repeat(b_spatial[0].T, A_GROUP_DIM, axis=1)
    bias_s = jnp.tile(bias_p[:dec_s], (per, 1))
    bc = jnp.stack([bias_p, bias_s])

    pos_p = jnp.arange(seq)
    pos_s = n_past + jnp.arange(dec_s)
    cos_p, sin_p = _rope_tables(pos_p)
    cos_s, sin_s = _rope_tables(pos_s)
    cos_rows = jnp.concatenate([jnp.tile(cos_p, (batch, 1)), jnp.tile(cos_s, (dec_b, 1))], axis=0)
    sin_rows = jnp.concatenate([jnp.tile(sin_p, (batch, 1)), jnp.tile(sin_s, (dec_b, 1))], axis=0)
    lane_pad = jnp.zeros((t_all, LANES - QK_ROPE), F32)
    cos_kr = jnp.concatenate([cos_rows, cos_rows, lane_pad], axis=1)
    sin_kr = jnp.concatenate([-sin_rows, sin_rows, lane_pad], axis=1)

    x_p, x_s = x_prompt.reshape(t_p, D_MODEL), x_sample.reshape(t_s, D_MODEL)
    g1, g2 = row1(norm1_g[0]), row1(norm2_g[0])
    a_all, cq_all, ckv_p, kr_p, ckv_s, kr_s, v_s = _in_proj(
        x_p, x_s, mod_g, g1, w5, row1(a_vnorm_g[0]), row1(q_norm_g[0]), row1(kv_norm_g[0]),
        cos_kr, sin_kr, wc, bc, grp_of_tile, n_prompt_tiles)

    wuq = w_uq[0]
    wuqt = jnp.pad(wuq, ((0, 0), (0, 0), (0, HEAD_PAD - QK_NOPE - QK_ROPE))).reshape(Q_LORA, -1).T.astype(BF16)
    wukp = jnp.pad(w_uk[0], ((0, 0), (0, 0), (0, HEAD_PAD - QK_NOPE))).reshape(KV_LORA, -1).astype(BF16)
    place_h = jnp.pad(jnp.eye(QK_ROPE, dtype=F32), ((0, 0), (QK_NOPE, HEAD_PAD - QK_NOPE - QK_ROPE)))
    place = jnp.tile(place_h, (1, N_HEADS)).astype(BF16)
    wuvt = w_uv[0].reshape(KV_LORA, -1).T.astype(BF16)
    qt, k, vt = _qkv(cq_all, ckv_p, kr_p, wuqt, wukp, place, wuvt, cos_p.T, sin_p.T, batch, seq)
    ot = _attn(qt, k, vt)

    wq_s = jnp.concatenate([wuq[:, :, :QK_NOPE].reshape(Q_LORA, -1),
                            wuq[:, :, QK_NOPE:QK_NOPE + ROPE_HALF].reshape(Q_LORA, -1),
                            wuq[:, :, QK_NOPE + ROPE_HALF:].reshape(Q_LORA, -1)], axis=1).astype(BF16)
    eye_h = jnp.eye(N_HEADS, dtype=F32)
    wuk_blk = jnp.einsum('rhd,hg->hdgr', w_uk[0], eye_h).reshape(N_HEADS * QK_NOPE, N_HEADS * KV_LORA).astype(BF16)
    wuv_blk = jnp.einsum('rhd,hg->hrgd', w_uv[0], eye_h).reshape(N_HEADS * KV_LORA, N_HEADS * V_HEAD).astype(BF16)
    cos_sq = jnp.tile(jnp.tile(cos_s, (1, N_HEADS)), (dec_b, 1))
    sin_sq = jnp.tile(jnp.tile(sin_s, (1, N_HEADS)), (dec_b, 1))
    ql, r1, r2 = _sample_q(cq_all[t_p:], wq_s, wuk_blk, cos_sq, sin_sq)
    rows = dec_s * N_HEADS
    ql3 = ql.reshape(dec_b, rows, KV_LORA)
    qr3 = jnp.concatenate([r1.reshape(dec_b, dec_s, N_HEADS, ROPE_HALF),
                           r2.reshape(dec_b, dec_s, N_HEADS, ROPE_HALF)], axis=-1).reshape(dec_b, rows, QK_ROPE)
    new_pad = ((0, 0), (0, 8 - dec_s), (0, 0))
    ckv_new = jnp.pad(ckv_s.reshape(dec_b, dec_s, KV_LORA), new_pad)
    kr_new_t = jnp.swapaxes(jnp.pad(kr_s.reshape(dec_b, dec_s, QK_ROPE), new_pad), 1, 2)
    cache_kr_t = jnp.swapaxes(cache_krope[0], 1, 2)
    o_lat = _sample_attn(page_table, cache_ckv[0], cache_kr_t, ql3, qr3, ckv_new, kr_new_t)
    o_s = _matmul(o_lat.reshape(t_s, N_HEADS * KV_LORA), wuv_blk, BF16)

    x1_all, *h2_all = _merge(x_p, x_s, mod_g, g1, g2, a_all, ot, o_s, wg,
                            w_out_a[0].astype(BF16), w_out_b[0].astype(BF16), w_o[0].astype(BF16),
                            grp_of_tile, n_prompt_tiles)

    wr_t = w_router[0].T
    whi = wr_t.astype(BF16)
    wlo = (wr_t - whi.astype(F32)).astype(BF16)
    idx_t, wt_t, rank_t, counts = _route(x1_all, mod_g, g2, whi, wlo, router_bias[0].reshape(-1, 1),
                                         grp_of_tile)
    n_assign = t_all * TOP_K
    e_flat = idx_t.reshape(1, n_assign)
    counts = counts[:, 0].astype(jnp.int32)
    blk = MOE_BLOCK
    padded = (counts + blk - 1) // blk * blk
    pad_end = jnp.cumsum(padded)
    pad_start = pad_end - padded
    dest = _dest(e_flat, rank_t.reshape(1, n_assign), pad_start.reshape(-1, 1))
    n_blocks = -(-n_assign // blk) + N_EXPERTS
    n_slots = n_blocks * blk
    first_row = jnp.arange(n_blocks, dtype=jnp.int32) * blk
    block_e = jnp.minimum(jnp.sum(pad_end[None, :] <= first_row[:, None], axis=1), N_EXPERTS - 1).astype(jnp.int32)
    n_valid = (pad_end[-1] // blk).astype(jnp.int32).reshape(1)
    experts = jnp.arange(N_EXPERTS, dtype=jnp.int32)
    owners = jnp.where(padded > 0, experts, N_EXPERTS)
    later_owner = lax.cummin(owners, reverse=True)
    next_expert = jnp.concatenate([later_owner[1:], jnp.full((1,), N_EXPERTS, jnp.int32)])
    x_sorted = _sc_scatter_rows(h2_all, dest, n_slots)
    y_sorted = _moe(block_e, n_valid, next_expert, x_sorted, w_exp_gu[0], w_exp_down[0])
    y_gath = [y.reshape(TOP_K, t_all, PIECE) for y in _sc_gather_rows(y_sorted, dest)]

    y_p, y_s = _final(x1_all, h2_all, y_gath, wt_t.T, mod_g, w_sh_gu[0].astype(BF16), w_sh_down[0].astype(BF16),
                      row1(final_norm_g), grp_of_tile, n_prompt_tiles)

    y_prompt = y_p.reshape(batch, seq, D_MODEL)
    y_sample = y_s.reshape(dec_b, dec_s, D_MODEL)
    new_ckv_prompt = ckv_p.reshape(1, batch, seq, KV_LORA)
    new_krope_prompt = kr_p.reshape(1, batch, seq, QK_ROPE)
    new_ckv_sample = ckv_s.reshape(1, dec_b, dec_s, KV_LORA)
    new_krope_sample = kr_s.reshape(1, dec_b, dec_s, QK_ROPE)
    new_chunk_v_sample = v_s.reshape(1, dec_b, dec_s, A_WIDTH)
    return (y_prompt, y_sample, new_ckv_prompt, new_krope_prompt, new_ckv_sample, new_krope_sample,
            new_chunk_v_sample)
```

```python
import functools

import jax
import jax.numpy as jnp
from jax import lax
from jax.experimental import pallas as pl
from jax.experimental.pallas import tpu as pltpu
from jax.experimental.pallas import tpu_sc as plsc

F32 = jnp.float32
BF16 = jnp.bfloat16

D_MODEL = 1024
A_WIDTH = D_MODEL // 2
A_GROUPS = 8
A_GROUP_DIM = A_WIDTH // A_GROUPS
CHUNK = 128
N_HEADS = 8
QK_NOPE = 64
QK_ROPE = 32
ROPE_HALF = QK_ROPE // 2
V_HEAD = 64
Q_LORA = 384
KV_LORA = 256
ROPE_THETA = 10000.0
ATTN_SCALE = (QK_NOPE + QK_ROPE) ** -0.5
N_EXPERTS = 256
TOP_K = 8
N_GROUPS = 8
GROUP_SIZE = N_EXPERTS // N_GROUPS
TOPK_GROUPS = 4
EXPERT_DIM = 256
SHARED_DIM = 256
ROUTED_SCALE = 2.5
EPS = 1e-6

LANES = 128
HEAD_PAD = 128
ROW_TILE = 256
ATTN_KV_TILE = 256
ATTN_Q_STEP = 512
ATTN_HEADS_PER_STEP = 4
LOG2_E = 1.4426950408889634
RANK_TILE = 512
MOE_BLOCK = 256
MOE_SUB = 128
MOE_WEIGHT_SLOTS = 3
PAGES_PER_STEP = 32
PAGE_GROUP = 8
V_ROWS = V_HEAD + 16
SC_WINDOW = 128
SC_SPLIT = 2
PIECE = D_MODEL // 2 // SC_SPLIT
VMEM_LIMIT = 48 * 1024 * 1024

IN_U, IN_V, IN_CQ, IN_CKV, IN_KR, IN_END = 0, 512, 1024, 1408, 1664, 1792

NT_DIMS = (((1,), (1,)), ((), ()))
TN_DIMS = (((0,), (0,)), ((), ()))


def _params(sem, vmem=VMEM_LIMIT):
    return pltpu.CompilerParams(dimension_semantics=sem, vmem_limit_bytes=vmem)


def _rms(x, g):
    return x * lax.rsqrt(jnp.mean(x * x, axis=-1, keepdims=True) + EPS) * g


def _gelu(x):
    return 0.5 * x * (1.0 + jnp.tanh(0.7978845608028654 * (x + 0.044715 * (x * x * x))))


def _sigmoid(x):
    return 1.0 / (1.0 + jnp.exp(-x))


def _silu(x):
    return x * _sigmoid(x)


def _pack_rows(x):
    pieces = []
    for c in range(SC_SPLIT):
        lo = x[:, 2 * c * PIECE:(2 * c + 1) * PIECE].astype(BF16).astype(F32)
        hi = x[:, (2 * c + 1) * PIECE:(2 * c + 2) * PIECE].astype(BF16).astype(F32)
        pieces.append(lax.bitcast_convert_type(hi, jnp.int32)
                      | lax.shift_right_logical(lax.bitcast_convert_type(lo, jnp.int32), 16))
    return pieces


def _unpack_rows(pieces):
    cols = []
    for p in pieces:
        cols.append(lax.bitcast_convert_type(lax.shift_left(p, 16), F32))
        cols.append(lax.bitcast_convert_type(p & jnp.int32(-65536), F32))
    return jnp.concatenate(cols, axis=1)


def _ada_kernel(c_ref, w_ref, b_ref, o_ref):
    s = _silu(c_ref[...]).astype(BF16)
    o_ref[...] = jnp.dot(s, w_ref[...].astype(BF16), preferred_element_type=F32) + b_ref[...]


def _ada(c, w, b):
    rows, n = c.shape[0], w.shape[1]
    tn = 1536
    return pl.pallas_call(
        _ada_kernel,
        grid=(n // tn,),
        in_specs=[pl.BlockSpec((rows, D_MODEL), lambda j: (0, 0)),
                  pl.BlockSpec((D_MODEL, tn), lambda j: (0, j)),
                  pl.BlockSpec((1, tn), lambda j: (0, j))],
        out_specs=pl.BlockSpec((rows, tn), lambda j: (0, j)),
        out_shape=jax.ShapeDtypeStruct((rows, n), F32),
        compiler_params=_params(("parallel",)),
        name="ada",
    )(c, w, b)


def _in_kernel(n_prompt_tiles, xp_ref, xs_ref, sh_ref, sc_ref, g1_ref, w_ref, vg_ref, qg_ref, kg_ref,
               cos_ref, sin_ref, wc_ref, bc_ref,
               a_ref, cq_ref, ckvp_ref, krp_ref, ckvs_ref, krs_ref, vs_ref):
    tm = xp_ref.shape[0]
    is_prompt = pl.program_id(0) < n_prompt_tiles
    x = jnp.where(is_prompt, xp_ref[...], xs_ref[...])
    h = _rms(x, g1_ref[...]) * (1.0 + sc_ref[...]) + sh_ref[...]
    z = jnp.dot(h.astype(BF16), w_ref[...], preferred_element_type=F32)
    u = _gelu(z[:, IN_U:IN_V])
    v = _rms(_gelu(z[:, IN_V:IN_CQ]), vg_ref[...])
    cq_ref[...] = _rms(z[:, IN_CQ:IN_CKV], qg_ref[...]).astype(BF16)
    ckv = _rms(z[:, IN_CKV:IN_KR], kg_ref[...])
    kr = z[:, IN_KR:IN_END]
    lane = lax.broadcasted_iota(jnp.int32, kr.shape, 1)
    swapped = jnp.where(lane < ROPE_HALF, pltpu.roll(kr, LANES - ROPE_HALF, 1), pltpu.roll(kr, ROPE_HALF, 1))
    kr_rot = (kr * cos_ref[...] + swapped * sin_ref[...])[:, :QK_ROPE]

    @pl.when(is_prompt)
    def _():
        ckvp_ref[...] = ckv
        krp_ref[...] = kr_rot

    @pl.when(jnp.logical_not(is_prompt))
    def _():
        ckvs_ref[...] = ckv
        krs_ref[...] = kr_rot
        vs_ref[...] = v

    vb = v.astype(BF16)
    half = A_WIDTH // 2
    grp_of_lane = lax.broadcasted_iota(jnp.int32, (CHUNK, half), 1) // A_GROUP_DIM
    for ci in range(tm // CHUNK):
        rows = slice(ci * CHUNK, (ci + 1) * CHUNK)
        parts = []
        for q in range(2):
            vq = vb[rows, q * half:(q + 1) * half]
            acc = None
            for gg in range(A_GROUPS // 2):
                vm = jnp.where(grp_of_lane == gg, vq, jnp.zeros_like(vq))
                part = jnp.dot(wc_ref[q * (A_GROUPS // 2) + gg], vm, preferred_element_type=F32)
                acc = part if acc is None else acc + part
            parts.append(acc)
        sp = jnp.concatenate(parts, axis=1) + bc_ref[...]
        a_ref[rows, :] = (u[rows, :] * sp).astype(BF16)


MOD_SHIFT1, MOD_SCALE1, MOD_GATE1, MOD_SHIFT2, MOD_SCALE2, MOD_GATE2 = range(6)


def _mod_spec(grp_of_tile, component):
    return pl.BlockSpec((None, ROW_TILE, D_MODEL), lambda i: (grp_of_tile(i), 0, component))


def _in_proj(x_p, x_s, mod_g, g1, w5, vg, qg, kg, cos_kr, sin_kr, wc, bc, grp_of_tile, n_prompt_tiles):
    t_p, t_s = x_p.shape[0], x_s.shape[0]
    t_all = t_p + t_s
    tm = ROW_TILE
    n_tiles = t_all // tm
    row = lambda i: (i, 0)
    prow = lambda i: (jnp.minimum(i, n_prompt_tiles - 1), 0)
    srow = lambda i: (jnp.maximum(i - n_prompt_tiles, 0), 0)
    fixed2 = lambda i: (0, 0)
    kind = lambda i: (jnp.where(i < n_prompt_tiles, 0, 1), 0, 0)
    kind4 = lambda i: (jnp.where(i < n_prompt_tiles, 0, 1), 0, 0, 0)
    return pl.pallas_call(
        functools.partial(_in_kernel, n_prompt_tiles),
        grid=(n_tiles,),
        in_specs=[pl.BlockSpec((tm, D_MODEL), prow),
                  pl.BlockSpec((tm, D_MODEL), srow),
                  _mod_spec(grp_of_tile, MOD_SHIFT1),
                  _mod_spec(grp_of_tile, MOD_SCALE1),
                  pl.BlockSpec((1, D_MODEL), fixed2),
                  pl.BlockSpec((D_MODEL, IN_END), fixed2),
                  pl.BlockSpec((1, A_WIDTH), fixed2),
                  pl.BlockSpec((1, Q_LORA), fixed2),
                  pl.BlockSpec((1, KV_LORA), fixed2),
                  pl.BlockSpec((tm, LANES), row),
                  pl.BlockSpec((tm, LANES), row),
                  pl.BlockSpec((None, A_GROUPS, CHUNK, CHUNK), kind4),
                  pl.BlockSpec((None, CHUNK, A_WIDTH), kind)],
        out_specs=[pl.BlockSpec((tm, A_WIDTH), row),
                   pl.BlockSpec((tm, Q_LORA), row),
                   pl.BlockSpec((tm, KV_LORA), prow),
                   pl.BlockSpec((tm, QK_ROPE), prow),
                   pl.BlockSpec((tm, KV_LORA), srow),
                   pl.BlockSpec((tm, QK_ROPE), srow),
                   pl.BlockSpec((tm, A_WIDTH), srow)],
        out_shape=[jax.ShapeDtypeStruct((t_all, A_WIDTH), BF16),
                   jax.ShapeDtypeStruct((t_all, Q_LORA), BF16),
                   jax.ShapeDtypeStruct((t_p, KV_LORA), F32),
                   jax.ShapeDtypeStruct((t_p, QK_ROPE), F32),
                   jax.ShapeDtypeStruct((t_s, KV_LORA), F32),
                   jax.ShapeDtypeStruct((t_s, QK_ROPE), F32),
                   jax.ShapeDtypeStruct((t_s, A_WIDTH), F32)],
        compiler_params=_params(("arbitrary",)),
        name="in_proj",
    )(x_p, x_s, mod_g, mod_g, g1, w5, vg, qg, kg, cos_kr, sin_kr, wc, bc)


def _qkv_kernel(cq_ref, ckv_ref, kr_ref, wuqt_ref, wukp_ref, place_ref, wuvt_ref, cos_ref, sin_ref,
                qt_ref, k_ref, vt_ref):
    tm = cq_ref.shape[0]
    ckv = ckv_ref[...].astype(BF16)
    kr = kr_ref[...].astype(BF16)
    qt = lax.dot_general(wuqt_ref[...], cq_ref[...], NT_DIMS, preferred_element_type=F32)
    c = cos_ref[...]
    s = sin_ref[...]
    pad = jnp.zeros((HEAD_PAD - QK_NOPE - QK_ROPE, tm), F32)
    for h in range(N_HEADS):
        blk = qt[h * HEAD_PAD:(h + 1) * HEAD_PAD]
        x1 = blk[QK_NOPE:QK_NOPE + ROPE_HALF]
        x2 = blk[QK_NOPE + ROPE_HALF:QK_NOPE + QK_ROPE]
        full = jnp.concatenate([blk[:QK_NOPE], x1 * c - x2 * s, x1 * s + x2 * c, pad], axis=0)
        qt_ref[h] = (full * (ATTN_SCALE * LOG2_E)).astype(BF16)
    k = (jnp.dot(ckv, wukp_ref[...], preferred_element_type=F32)
         + jnp.dot(kr, place_ref[...], preferred_element_type=F32))
    for h in range(N_HEADS):
        k_ref[h] = k[:, h * HEAD_PAD:(h + 1) * HEAD_PAD].astype(BF16)
    vt = lax.dot_general(wuvt_ref[...], ckv, NT_DIMS, preferred_element_type=F32)
    extra = V_ROWS - V_HEAD
    ones_row = jnp.where(lax.broadcasted_iota(jnp.int32, (extra, tm), 0) == 0, 1.0, 0.0)
    for h in range(N_HEADS):
        vt_ref[h] = jnp.concatenate([vt[h * V_HEAD:(h + 1) * V_HEAD], ones_row], axis=0).astype(BF16)


def _qkv(cq_all, ckv_all, kr_all, wuqt, wukp, place, wuvt, cos_t, sin_t, batch, seq):
    tm = ATTN_KV_TILE
    nk = seq // tm
    row = lambda b, j: (b * nk + j, 0)
    fixed2 = lambda b, j: (0, 0)
    tab = lambda b, j: (0, j)
    return pl.pallas_call(
        _qkv_kernel,
        grid=(batch, nk),
        in_specs=[pl.BlockSpec((tm, Q_LORA), row),
                  pl.BlockSpec((tm, KV_LORA), row),
                  pl.BlockSpec((tm, QK_ROPE), row),
                  pl.BlockSpec(wuqt.shape, fixed2),
                  pl.BlockSpec(wukp.shape, fixed2),
                  pl.BlockSpec(place.shape, fixed2),
                  pl.BlockSpec(wuvt.shape, fixed2),
                  pl.BlockSpec((ROPE_HALF, tm), tab),
                  pl.BlockSpec((ROPE_HALF, tm), tab)],
        out_specs=[pl.BlockSpec((None, N_HEADS, HEAD_PAD, tm), lambda b, j: (b, 0, 0, j)),
                   pl.BlockSpec((None, N_HEADS, tm, HEAD_PAD), lambda b, j: (b, 0, j, 0)),
                   pl.BlockSpec((None, N_HEADS, None, V_ROWS, tm), lambda b, j: (b, 0, j, 0, 0))],
        out_shape=[jax.ShapeDtypeStruct((batch, N_HEADS, HEAD_PAD, seq), BF16),
                   jax.ShapeDtypeStruct((batch, N_HEADS, seq, HEAD_PAD), BF16),
                   jax.ShapeDtypeStruct((batch, N_HEADS, nk, V_ROWS, tm), BF16)],
        compiler_params=_params(("parallel", "parallel")),
        name="qkv",
    )(cq_all, ckv_all, kr_all, wuqt, wukp, place, wuvt, cos_t, sin_t)


def _attn_kernel(qt_ref, k_ref, vt_ref, o_ref):
    tk = ATTN_KV_TILE
    n_heads = qt_ref.shape[0]
    n_sub = qt_ref.shape[2] // tk
    qi = pl.program_id(2)
    j0 = qi * n_sub
    chains = [(h, sub) for h in range(n_heads) for sub in range(n_sub)]

    def tiles(j, active, carries, diagonal_sub):
        scores = [jnp.dot(k_ref[h, j], qt_ref[h, :, sub * tk:(sub + 1) * tk], preferred_element_type=F32)
                  for h, sub in active]
        stats = []
        for (h, sub), s in zip(active, scores):
            m, _ = carries[h * n_sub + sub]
            if sub == diagonal_sub:
                key = lax.broadcasted_iota(jnp.int32, s.shape, 0)
                qry = lax.broadcasted_iota(jnp.int32, s.shape, 1)
                s = jnp.where(key <= qry, s, -jnp.inf)
            m_new = jnp.maximum(m, jnp.max(s, axis=0, keepdims=True))
            stats.append((m_new, jnp.exp2(m - m_new), jnp.exp2(s - m_new).astype(BF16)))
        out = list(carries)
        for (h, sub), (m_new, alpha, p) in zip(active, stats):
            c = h * n_sub + sub
            out[c] = (m_new, alpha * carries[c][1] + jnp.dot(vt_ref[h, j], p, preferred_element_type=F32))
        return out

    init = tuple((jnp.full((1, tk), -jnp.inf, F32), jnp.zeros((V_ROWS, tk), F32)) for _ in chains)
    carries = list(lax.fori_loop(0, j0, lambda j, c: tuple(tiles(j, chains, c, None)), init))
    for jj in range(n_sub):
        carries = tiles(j0 + jj, [(h, sub) for h, sub in chains if sub >= jj], carries, jj)
    for h, sub in chains:
        _, acc = carries[h * n_sub + sub]
        o_ref[h * V_HEAD:(h + 1) * V_HEAD, sub * tk:(sub + 1) * tk] = (
            acc[:V_HEAD] / acc[V_HEAD:V_HEAD + 1]).astype(BF16)


def _attn(qt, k, vt):
    batch, _, _, seq = qt.shape
    tk = ATTN_KV_TILE
    nk = seq // tk
    hp = ATTN_HEADS_PER_STEP
    k5 = k.reshape(batch, N_HEADS, nk, tk, HEAD_PAD)
    return pl.pallas_call(
        _attn_kernel,
        grid=(batch, N_HEADS // hp, seq // ATTN_Q_STEP),
        in_specs=[pl.BlockSpec((None, hp, HEAD_PAD, ATTN_Q_STEP), lambda b, h, i: (b, h, 0, i)),
                  pl.BlockSpec((None, hp, nk, tk, HEAD_PAD), lambda b, h, i: (b, h, 0, 0, 0)),
                  pl.BlockSpec((None, hp, nk, V_ROWS, tk), lambda b, h, i: (b, h, 0, 0, 0))],
        out_specs=pl.BlockSpec((None, hp * V_HEAD, ATTN_Q_STEP), lambda b, h, i: (b, h, i)),
        out_shape=jax.ShapeDtypeStruct((batch, N_HEADS * V_HEAD, seq), BF16),
        compiler_params=_params(("parallel", "parallel", "parallel")),
        name="attn",
    )(qt, k5, vt)


def _sq_kernel(cq_ref, wq_ref, wukb_ref, cos_ref, sin_ref, ql_ref, r1_ref, r2_ref):
    n_nope = N_HEADS * QK_NOPE
    q = jnp.dot(cq_ref[...], wq_ref[...], preferred_element_type=F32)
    x1 = q[:, n_nope:n_nope + LANES]
    x2 = q[:, n_nope + LANES:]
    c = cos_ref[...]
    s = sin_ref[...]
    r1_ref[...] = ((x1 * c - x2 * s) * ATTN_SCALE).astype(BF16)
    r2_ref[...] = ((x1 * s + x2 * c) * ATTN_SCALE).astype(BF16)
    ql = jnp.dot(q[:, :n_nope].astype(BF16), wukb_ref[...], preferred_element_type=F32)
    ql_ref[...] = (ql * ATTN_SCALE).astype(BF16)


def _sample_q(cq_s, wq_s, wuk_blk, cos_s, sin_s):
    ts = cq_s.shape[0]
    full = lambda shape: pl.BlockSpec(shape, lambda i: (0,) * len(shape))
    return pl.pallas_call(
        _sq_kernel,
        grid=(1,),
        in_specs=[full(cq_s.shape), full(wq_s.shape), full(wuk_blk.shape), full(cos_s.shape), full(sin_s.shape)],
        out_specs=[full((ts, N_HEADS * KV_LORA)), full((ts, LANES)), full((ts, LANES))],
        out_shape=[jax.ShapeDtypeStruct((ts, N_HEADS * KV_LORA), BF16),
                   jax.ShapeDtypeStruct((ts, LANES), BF16),
                   jax.ShapeDtypeStruct((ts, LANES), BF16)],
        compiler_params=_params(("arbitrary",)),
        name="sample_q",
    )(cq_s, wq_s, wuk_blk, cos_s, sin_s)


def _sattn_kernel(n_pages, pt_ref, ckv_hbm, kr_hbm, ql_ref, qr_ref, cnew_ref, knew_ref, o_ref,
                  cbuf, rbuf, sem_c, sem_r):
    npg = PAGES_PER_STEP
    n_chunks = n_pages // npg
    PAGE_SLOTS = cbuf.shape[0]
    PAGE_LOOKAHEAD = PAGE_SLOTS - 1
    b = pl.program_id(0)

    def page_copies(bb, chunk, slot, p):
        page = pt_ref[bb * n_pages + chunk * npg + p]
        return (pltpu.make_async_copy(ckv_hbm.at[page], cbuf.at[slot, p], sem_c.at[slot]),
                pltpu.make_async_copy(kr_hbm.at[page], rbuf.at[slot, p], sem_r.at[slot]))

    def start_chunk(bb, chunk, slot):
        for p in range(npg):
            latent_cp, rope_cp = page_copies(bb, chunk, slot, p)
            latent_cp.start(priority=0)
            rope_cp.start(priority=1)

    def wait_chunk(bb, chunk, slot):
        for p in range(npg):
            for cp in page_copies(bb, chunk, slot, p):
                cp.wait()

    @pl.when(b == 0)
    def _():
        for chunk in range(PAGE_LOOKAHEAD):
            start_chunk(0, chunk, chunk)

    ql = ql_ref[...]
    qr = qr_ref[...]

    def partial_softmax(blocks):
        scores = [lax.dot_general(ql, keys, NT_DIMS, preferred_element_type=F32)
                  + jnp.dot(qr, rope_t, preferred_element_type=F32) for keys, rope_t, _ in blocks]
        probs = []
        for s, (_, _, mask) in zip(scores, blocks):
            if mask is not None:
                s = jnp.where(mask(s.shape), s, -jnp.inf)
            m = jnp.max(s, axis=-1, keepdims=True)
            p = jnp.exp(s - m)
            probs.append((m, jnp.sum(p, axis=-1, keepdims=True), p.astype(BF16)))
        return [(m, l, jnp.dot(p, blk[0], preferred_element_type=F32)) for (m, l, p), blk in zip(probs, blocks)]

    def merge(state, parts):
        m_old, l_old, acc_old = state
        m_new = m_old
        for m, _, _ in parts:
            m_new = jnp.maximum(m_new, m)
        alpha = jnp.exp(m_old - m_new)
        l = alpha * l_old
        acc = alpha * acc_old
        for m, lp, op in parts:
            w = jnp.exp(m - m_new)
            l = l + w * lp
            acc = acc + w * op
        return m_new, l, acc

    def chunk_step(chunk, state, extra_blocks=()):
        slot = chunk % PAGE_SLOTS
        wait_chunk(b, chunk, slot)
        ahead = chunk + PAGE_LOOKAHEAD
        ahead_b = jnp.where(ahead >= n_chunks, b + 1, b)
        ahead_b = jnp.where(ahead_b >= pl.num_programs(0), 0, ahead_b)
        start_chunk(ahead_b, ahead % n_chunks, ahead % PAGE_SLOTS)

        blocks = []
        for g in range(npg // PAGE_GROUP):
            pages = range(g * PAGE_GROUP, (g + 1) * PAGE_GROUP)
            blocks.append((jnp.concatenate([cbuf[slot, p].astype(BF16) for p in pages], axis=0),
                           jnp.concatenate([rbuf[slot, p].astype(BF16) for p in pages], axis=1), None))
        return merge(state, partial_softmax(blocks + list(extra_blocks)))

    def causal(shape):
        q_pos = lax.broadcasted_iota(jnp.int32, shape, 0) // N_HEADS
        return lax.broadcasted_iota(jnp.int32, shape, 1) <= q_pos

    rows = ql.shape[0]
    state = (jnp.full((rows, 1), -jnp.inf, F32), jnp.zeros((rows, 1), F32), jnp.zeros((rows, KV_LORA), F32))
    state = lax.fori_loop(0, n_chunks - 1, chunk_step, state)
    new_rows = (cnew_ref[...].astype(BF16), knew_ref[...].astype(BF16), causal)
    _, l, acc = chunk_step(n_chunks - 1, state, [new_rows])
    o_ref[...] = acc / l

    @pl.when(b == pl.num_programs(0) - 1)
    def _():
        for chunk in range(PAGE_LOOKAHEAD):
            wait_chunk(0, chunk, chunk % PAGE_SLOTS)


def _sample_attn(page_table, cache_ckv, cache_kr_t, ql, qr, cnew, knew_t):
    dec_b, n_pages = page_table.shape
    npg = PAGES_PER_STEP
    n_chunks = n_pages // npg
    slots = 4 if n_chunks % 4 == 0 else 2
    assert n_chunks % slots == 0
    rows = ql.shape[1]
    n_new = cnew.shape[1]
    per_b = lambda b, pt: (b, 0, 0)
    return pl.pallas_call(
        functools.partial(_sattn_kernel, n_pages),
        grid_spec=pltpu.PrefetchScalarGridSpec(
            num_scalar_prefetch=1,
            grid=(dec_b,),
            in_specs=[pl.BlockSpec(memory_space=pl.ANY),
                      pl.BlockSpec(memory_space=pl.ANY),
                      pl.BlockSpec((None, rows, KV_LORA), per_b),
                      pl.BlockSpec((None, rows, QK_ROPE), per_b),
                      pl.BlockSpec((None, n_new, KV_LORA), per_b),
                      pl.BlockSpec((None, QK_ROPE, n_new), per_b)],
            out_specs=pl.BlockSpec((None, rows, KV_LORA), per_b),
            scratch_shapes=[pltpu.VMEM((slots, npg, CHUNK, KV_LORA), F32),
                            pltpu.VMEM((slots, npg, QK_ROPE, CHUNK), F32),
                            pltpu.SemaphoreType.DMA((slots,)),
                            pltpu.SemaphoreType.DMA((slots,))]),
        out_shape=jax.ShapeDtypeStruct((dec_b, rows, KV_LORA), F32),
        compiler_params=_params(("arbitrary",)),
        name="sample_attn",
    )(page_table.reshape(-1), cache_ckv, cache_kr_t, ql, qr, cnew, knew_t)


def _mm_kernel(x_ref, w_ref, o_ref):
    o_ref[...] = jnp.dot(x_ref[...].astype(BF16), w_ref[...], preferred_element_type=F32).astype(o_ref.dtype)


def _matmul(x, w, out_dtype):
    m, n = x.shape[0], w.shape[1]
    full = lambda shape: pl.BlockSpec(shape, lambda i: (0,) * len(shape))
    return pl.pallas_call(
        _mm_kernel,
        grid=(1,),
        in_specs=[full(x.shape), full(w.shape)],
        out_specs=full((m, n)),
        out_shape=jax.ShapeDtypeStruct((m, n), out_dtype),
        compiler_params=_params(("arbitrary",)),
        name="matmul",
    )(x, w)


def _merge_kernel(n_prompt_tiles, xp_ref, xs_ref, sh1_ref, sc1_ref, gt1_ref, sh2_ref, sc2_ref, g1_ref, g2_ref,
                  a_ref, ot_ref, os_ref, wg_ref, woa_ref, wob_ref, wo_ref, x1_ref, *rest):
    h2_refs, yb_ref = rest[:SC_SPLIT], rest[SC_SPLIT]
    i = pl.program_id(0)
    x = jnp.where(i < n_prompt_tiles, xp_ref[...], xs_ref[...])
    h = _rms(x, g1_ref[...]) * (1.0 + sc1_ref[...]) + sh1_ref[...]
    gates = _sigmoid(jnp.dot(h.astype(BF16), wg_ref[...], preferred_element_type=F32))
    y_a = jnp.dot(a_ref[...], woa_ref[...], preferred_element_type=F32)

    @pl.when(i < n_prompt_tiles)
    def _():
        yb_ref[...] = lax.dot_general(ot_ref[...], wob_ref[...], TN_DIMS, preferred_element_type=F32)

    @pl.when(i >= n_prompt_tiles)
    def _():
        yb_ref[...] = jnp.dot(os_ref[...], wob_ref[...], preferred_element_type=F32)

    z = gates[:, :D_MODEL] * y_a + gates[:, D_MODEL:] * yb_ref[...]
    y = jnp.dot(z.astype(BF16), wo_ref[...], preferred_element_type=F32)
    x1 = x + gt1_ref[...] * y
    x1_ref[...] = x1
    for ref, piece in zip(h2_refs, _pack_rows(_rms(x1, g2_ref[...]) * (1.0 + sc2_ref[...]) + sh2_ref[...])):
        ref[...] = piece


def _merge(x_p, x_s, mod_g, g1, g2, a_all, ot, o_s, wg, woa, wob, wo, grp_of_tile, n_prompt_tiles):
    t_all = x_p.shape[0] + x_s.shape[0]
    tm = ROW_TILE
    seq = ot.shape[2]
    tpb = seq // tm
    row = lambda i: (i, 0)
    fixed2 = lambda i: (0, 0)
    mod_comps = (MOD_SHIFT1, MOD_SCALE1, MOD_GATE1, MOD_SHIFT2, MOD_SCALE2)

    def ot_map(i):
        ic = jnp.minimum(i, n_prompt_tiles - 1)
        return (ic // tpb, 0, ic % tpb)

    os_map = lambda i: (jnp.maximum(i - n_prompt_tiles, 0), 0)
    return pl.pallas_call(
        functools.partial(_merge_kernel, n_prompt_tiles),
        grid=(t_all // tm,),
        in_specs=[pl.BlockSpec((tm, D_MODEL), lambda i: (jnp.minimum(i, n_prompt_tiles - 1), 0)),
                  pl.BlockSpec((tm, D_MODEL), os_map)]
        + [_mod_spec(grp_of_tile, c) for c in mod_comps]
        + [pl.BlockSpec((1, D_MODEL), fixed2),
                  pl.BlockSpec((1, D_MODEL), fixed2),
                  pl.BlockSpec((tm, A_WIDTH), row),
                  pl.BlockSpec((None, N_HEADS * V_HEAD, tm), ot_map),
                  pl.BlockSpec((tm, N_HEADS * V_HEAD), os_map),
                  pl.BlockSpec(wg.shape, fixed2),
                  pl.BlockSpec(woa.shape, fixed2),
                  pl.BlockSpec(wob.shape, fixed2),
                  pl.BlockSpec(wo.shape, fixed2)],
        out_specs=[pl.BlockSpec((tm, D_MODEL), row)] + [pl.BlockSpec((tm, PIECE), row)] * SC_SPLIT,
        out_shape=[jax.ShapeDtypeStruct((t_all, D_MODEL), F32)]
        + [jax.ShapeDtypeStruct((t_all, PIECE), jnp.int32)] * SC_SPLIT,
        scratch_shapes=[pltpu.VMEM((tm, D_MODEL), F32)],
        compiler_params=_params(("parallel",)),
        name="merge",
    )(x_p, x_s, *([mod_g] * len(mod_comps)), g1, g2, a_all, ot, o_s, wg, woa, wob, wo)


def _first_argmax(v, rows):
    mx = jnp.max(v, axis=0, keepdims=True)
    idx = jnp.min(jnp.where(v == mx, rows, v.shape[0]), axis=0, keepdims=True)
    return mx, idx


def _route_kernel(x1_ref, sh2_ref, sc2_ref, g2_ref, whi_ref, wlo_ref, bias_ref, upper_ref,
                  idx_ref, wt_ref, rank_ref, count_ref, run_ref):
    h2 = _rms(x1_ref[...], g2_ref[...]) * (1.0 + sc2_ref[...]) + sh2_ref[...]
    hi = h2.astype(BF16)
    lo = (h2 - hi.astype(F32)).astype(BF16)
    whi = whi_ref[...]
    logits = (lax.dot_general(whi, hi, NT_DIMS, preferred_element_type=F32)
              + lax.dot_general(whi, lo, NT_DIMS, preferred_element_type=F32)
              + lax.dot_general(wlo_ref[...], hi, NT_DIMS, preferred_element_type=F32))
    scores = _sigmoid(logits)
    sel = scores + bias_ref[...]
    tm = sel.shape[1]
    neg = -jnp.inf
    rows_g = lax.broadcasted_iota(jnp.int32, (GROUP_SIZE, tm), 0)
    gscore = []
    for g in range(N_GROUPS):
        blk = sel[g * GROUP_SIZE:(g + 1) * GROUP_SIZE]
        m1, i1 = _first_argmax(blk, rows_g)
        m2 = jnp.max(jnp.where(rows_g == i1, neg, blk), axis=0, keepdims=True)
        gscore.append(m1 + m2)
    gs = jnp.concatenate(gscore, axis=0)
    rows_8 = lax.broadcasted_iota(jnp.int32, gs.shape, 0)
    chosen = jnp.zeros(gs.shape, jnp.int32)
    for _ in range(TOPK_GROUPS):
        _, gi = _first_argmax(gs, rows_8)
        hit = rows_8 == gi
        chosen = jnp.where(hit, 1, chosen)
        gs = jnp.where(hit, neg, gs)
    cand = jnp.concatenate(
        [jnp.where(chosen[g:g + 1] > 0, sel[g * GROUP_SIZE:(g + 1) * GROUP_SIZE], neg) for g in range(N_GROUPS)],
        axis=0)
    rows_e = lax.broadcasted_iota(jnp.int32, cand.shape, 0)
    idxs, wts, hits = [], [], []
    for _ in range(TOP_K):
        _, ei = _first_argmax(cand, rows_e)
        hit = rows_e == ei
        idxs.append(ei)
        hits.append(hit)
        wts.append(jnp.sum(jnp.where(hit, scores, 0.0), axis=0, keepdims=True))
        cand = jnp.where(hit, neg, cand)
    w = jnp.concatenate(wts, axis=0)
    idx_ref[...] = jnp.concatenate(idxs, axis=0)
    wt_ref[...] = w / jnp.sum(w, axis=0, keepdims=True) * ROUTED_SCALE

    @pl.when(pl.program_id(0) == 0)
    def _():
        run_ref[...] = jnp.zeros(run_ref.shape, F32)

    run = run_ref[...]
    ones = jnp.ones((tm, LANES), BF16)
    onehots = [jnp.where(hit, 1.0, 0.0).astype(BF16) for hit in hits]
    befores = [jnp.dot(oh, upper_ref[...], preferred_element_type=F32) for oh in onehots]
    totals = [jnp.dot(oh, ones, preferred_element_type=F32) for oh in onehots]
    ranks = []
    for hit, before, total in zip(hits, befores, totals):
        prior = jnp.concatenate([run] * (tm // LANES), axis=1)
        ranks.append(jnp.sum(jnp.where(hit, before + prior, 0.0), axis=0, keepdims=True))
        run = run + total
    rank_ref[...] = jnp.concatenate(ranks, axis=0).astype(jnp.int32)
    run_ref[...] = run
    count_ref[...] = run


def _route(x1_all, mod_g, g2, whi, wlo, bias_col, grp_of_tile):
    t_all = x1_all.shape[0]
    tm = ROW_TILE
    fixed2 = lambda i: (0, 0)
    col = lambda i: (0, i)
    upper = (lax.broadcasted_iota(jnp.int32, (tm, tm), 0) < lax.broadcasted_iota(jnp.int32, (tm, tm), 1)).astype(BF16)
    return pl.pallas_call(
        _route_kernel,
        grid=(t_all // tm,),
        in_specs=[pl.BlockSpec((tm, D_MODEL), lambda i: (i, 0)),
                  _mod_spec(grp_of_tile, MOD_SHIFT2),
                  _mod_spec(grp_of_tile, MOD_SCALE2),
                  pl.BlockSpec((1, D_MODEL), fixed2),
                  pl.BlockSpec(whi.shape, fixed2),
                  pl.BlockSpec(wlo.shape, fixed2),
                  pl.BlockSpec(bias_col.shape, fixed2),
                  pl.BlockSpec((tm, tm), fixed2)],
        out_specs=[pl.BlockSpec((TOP_K, tm), col), pl.BlockSpec((TOP_K, tm), col), pl.BlockSpec((TOP_K, tm), col),
                   pl.BlockSpec((N_EXPERTS, LANES), fixed2)],
        out_shape=[jax.ShapeDtypeStruct((TOP_K, t_all), jnp.int32),
                   jax.ShapeDtypeStruct((TOP_K, t_all), F32),
                   jax.ShapeDtypeStruct((TOP_K, t_all), jnp.int32),
                   jax.ShapeDtypeStruct((N_EXPERTS, LANES), F32)],
        scratch_shapes=[pltpu.VMEM((N_EXPERTS, LANES), F32)],
        compiler_params=_params(("arbitrary",)),
        name="route",
    )(x1_all, mod_g, mod_g, g2, whi, wlo, bias_col, upper)


def _moe_kernel(be_ref, nv_ref, next_ref, *refs):
    x_refs = refs[:SC_SPLIT]
    wgu_hbm, wdn_hbm = refs[SC_SPLIT:SC_SPLIT + 2]
    y_refs = refs[SC_SPLIT + 2:2 * SC_SPLIT + 2]
    gu_f32, dn_f32, gu_bf, dn_bf, sem, run_ref = refs[2 * SC_SPLIT + 2:]
    i = pl.program_id(0)
    expert = be_ref[i]
    prev = be_ref[jnp.maximum(i - 1, 0)]

    def weight_copies(e, slot):
        return (pltpu.make_async_copy(wgu_hbm.at[e], gu_f32.at[slot], sem.at[0, slot]),
                pltpu.make_async_copy(wdn_hbm.at[e], dn_f32.at[slot], sem.at[1, slot]))

    n_wslots = gu_f32.shape[0]

    def owner_after(e):
        return next_ref[jnp.minimum(e, N_EXPERTS - 1)]

    @pl.when(i == 0)
    def _():
        run_ref[0] = 0
        for cp in weight_copies(expert, 0):
            cp.start(priority=1)

        @pl.when(owner_after(expert) < N_EXPERTS)
        def _():
            for cp in weight_copies(owner_after(expert), 1):
                cp.start(priority=1)

    @pl.when(i < nv_ref[0])
    def _():
        @pl.when((i == 0) | (expert != prev))
        def _():
            run = run_ref[0] + jnp.where(i == 0, 0, 1)
            run_ref[0] = run
            slot = run % n_wslots
            for cp in weight_copies(expert, slot):
                cp.wait()
            following = owner_after(expert)
            second = owner_after(following)

            @pl.when((following < N_EXPERTS) & (second < N_EXPERTS))
            def _():
                for cp in weight_copies(second, (run + 2) % n_wslots):
                    cp.start(priority=1)

            gu_bf[...] = gu_f32[slot].astype(BF16)
            dn_bf[...] = dn_f32[slot].astype(BF16)

        subs = [slice(s * MOE_SUB, (s + 1) * MOE_SUB) for s in range(MOE_BLOCK // MOE_SUB)]
        gus = [jnp.dot(_unpack_rows([r[rows, :] for r in x_refs]).astype(BF16), gu_bf[...],
                       preferred_element_type=F32) for rows in subs]
        hids = [(_silu(gu[:, :EXPERT_DIM]) * gu[:, EXPERT_DIM:]).astype(BF16) for gu in gus]
        ys = [jnp.dot(hid, dn_bf[...], preferred_element_type=F32) for hid in hids]
        for rows, y in zip(subs, ys):
            for ref, piece in zip(y_refs, _pack_rows(y)):
                ref[rows, :] = piece


def _moe(block_e, n_valid, next_expert, x_sorted, w_gu, w_dn):
    n_slots = x_sorted[0].shape[0]
    blk = MOE_BLOCK
    n_blocks = n_slots // blk
    rows = lambda i, be, nv, nx: (jnp.minimum(i, nv[0] - 1), 0)
    return pl.pallas_call(
        _moe_kernel,
        grid_spec=pltpu.PrefetchScalarGridSpec(
            num_scalar_prefetch=3,
            grid=(n_blocks,),
            in_specs=[pl.BlockSpec((blk, PIECE), rows)] * SC_SPLIT
            + [pl.BlockSpec(memory_space=pl.ANY), pl.BlockSpec(memory_space=pl.ANY)],
            out_specs=[pl.BlockSpec((blk, PIECE), rows)] * SC_SPLIT,
            scratch_shapes=[pltpu.VMEM((MOE_WEIGHT_SLOTS, D_MODEL, 2 * EXPERT_DIM), F32),
                            pltpu.VMEM((MOE_WEIGHT_SLOTS, EXPERT_DIM, D_MODEL), F32),
                            pltpu.VMEM((D_MODEL, 2 * EXPERT_DIM), BF16),
                            pltpu.VMEM((EXPERT_DIM, D_MODEL), BF16),
                            pltpu.SemaphoreType.DMA((2, MOE_WEIGHT_SLOTS)),
                            pltpu.SMEM((1,), jnp.int32)]),
        out_shape=[jax.ShapeDtypeStruct((n_slots, PIECE), jnp.int32)] * SC_SPLIT,
        compiler_params=_params(("arbitrary",)),
        name="moe",
    )(block_e, n_valid, next_expert, *x_sorted, w_gu, w_dn)


def _dest_kernel(e_ref, rank_ref, start_ref, dest_ref):
    e = e_ref[...]
    rows = lax.broadcasted_iota(jnp.int32, (N_EXPERTS, e.shape[1]), 0)
    start = jnp.sum(jnp.where(rows == e, start_ref[...], 0), axis=0, keepdims=True)
    dest_ref[...] = start + rank_ref[...]


def _dest(e_flat, rank, pad_start_col):
    n_assign = e_flat.shape[1]
    n = RANK_TILE
    tile = pl.BlockSpec((1, n), lambda i: (0, i))
    return pl.pallas_call(
        _dest_kernel,
        grid=(n_assign // n,),
        in_specs=[tile, tile, pl.BlockSpec((N_EXPERTS, 1), lambda i: (0, 0))],
        out_specs=tile,
        out_shape=jax.ShapeDtypeStruct((1, n_assign), jnp.int32),
        compiler_params=_params(("parallel",)),
        name="dest",
    )(e_flat, rank, pad_start_col)


def _sc_mesh():
    return plsc.VectorSubcoreMesh(core_axis_name="core", subcore_axis_name="subcore")


def _sc_scatter_rows(srcs, dest, n_slots):
    n_src, width = srcs[0].shape
    n_assign = dest.shape[1]
    src_blocks = n_src // SC_WINDOW
    n_pieces = len(srcs)

    @functools.partial(pl.kernel, out_type=[jax.ShapeDtypeStruct((n_slots, width), s.dtype) for s in srcs],
                       mesh=_sc_mesh(), scratch_types=[])
    def scatter(*refs):
        src_hbms, dest_hbm, out_hbms = refs[:n_pieces], refs[n_pieces], refs[n_pieces + 1:]
        for src_hbm, out_hbm in zip(src_hbms, out_hbms):
            def body(rows_vmem, dest_vmem, out_hbm=out_hbm):
                pltpu.sync_copy(rows_vmem, out_hbm.at[dest_vmem.at[0]])

            pltpu.emit_pipeline(
                body,
                grid=(n_assign // SC_WINDOW,),
                in_specs=[pl.BlockSpec((SC_WINDOW, width), lambda i: (i % src_blocks, 0)),
                          pl.BlockSpec((1, SC_WINDOW), lambda i: (0, i))],
                out_specs=[],
                core_axis_name=("core", "subcore"),
                dimension_semantics=(pltpu.PARALLEL,),
            )(src_hbm, dest_hbm)

    return scatter(*srcs, dest)


def _sc_gather_rows(tables, idx):
    width = tables[0].shape[1]
    n_assign = idx.shape[1]
    n_pieces = len(tables)

    @functools.partial(pl.kernel, out_type=[jax.ShapeDtypeStruct((n_assign, width), t.dtype) for t in tables],
                       mesh=_sc_mesh(), scratch_types=[])
    def gather(*refs):
        table_hbms, idx_hbm, out_hbms = refs[:n_pieces], refs[n_pieces], refs[n_pieces + 1:]
        for table_hbm, out_hbm in zip(table_hbms, out_hbms):
            def body(idx_vmem, rows_vmem, table_hbm=table_hbm):
                pltpu.sync_copy(table_hbm.at[idx_vmem.at[0]], rows_vmem)

            pltpu.emit_pipeline(
                body,
                grid=(n_assign // SC_WINDOW,),
                in_specs=[pl.BlockSpec((1, SC_WINDOW), lambda i: (0, i))],
                out_specs=[pl.BlockSpec((SC_WINDOW, width), lambda i: (i, 0))],
                core_axis_name=("core", "subcore"),
                dimension_semantics=(pltpu.PARALLEL,),
            )(idx_hbm, out_hbm)

    return gather(*tables, idx)


def _final_kernel(n_prompt_tiles, x1_ref, *refs):
    h2_refs = refs[:SC_SPLIT]
    yg_refs = refs[SC_SPLIT:2 * SC_SPLIT]
    wt_ref, gt2_ref, wsg_ref, wsd_ref, gf_ref, op_ref, os_ref = refs[2 * SC_SPLIT:]
    is_prompt = pl.program_id(0) < n_prompt_tiles
    h2 = _unpack_rows([r[...] for r in h2_refs]).astype(BF16)
    gu = jnp.dot(h2, wsg_ref[...], preferred_element_type=F32)
    hid = _silu(gu[:, :SHARED_DIM]) * gu[:, SHARED_DIM:]
    y = jnp.dot(hid.astype(BF16), wsd_ref[...], preferred_element_type=F32)
    wt = wt_ref[...]
    for k in range(TOP_K):
        y = y + wt[:, k:k + 1] * _unpack_rows([r[k] for r in yg_refs])
    x2 = x1_ref[...] + gt2_ref[...] * y
    out = _rms(x2, gf_ref[...])

    @pl.when(is_prompt)
    def _():
        op_ref[...] = out

    @pl.when(jnp.logical_not(is_prompt))
    def _():
        os_ref[...] = out


def _final(x1_all, h2_all, y_gath, wt_rows, mod_g, wsg, wsd, gf, grp_of_tile, n_prompt_tiles):
    t_all = x1_all.shape[0]
    tm = ROW_TILE
    t_p = n_prompt_tiles * tm
    row = lambda i: (i, 0)
    fixed2 = lambda i: (0, 0)
    return pl.pallas_call(
        functools.partial(_final_kernel, n_prompt_tiles),
        grid=(t_all // tm,),
        in_specs=[pl.BlockSpec((tm, D_MODEL), row)]
        + [pl.BlockSpec((tm, PIECE), row)] * SC_SPLIT
        + [pl.BlockSpec((TOP_K, tm, PIECE), lambda i: (0, i, 0))] * SC_SPLIT
        + [pl.BlockSpec((tm, TOP_K), row),
           _mod_spec(grp_of_tile, MOD_GATE2),
           pl.BlockSpec(wsg.shape, fixed2),
           pl.BlockSpec(wsd.shape, fixed2),
           pl.BlockSpec((1, D_MODEL), fixed2)],
        out_specs=[pl.BlockSpec((tm, D_MODEL), lambda i: (jnp.minimum(i, n_prompt_tiles - 1), 0)),
                   pl.BlockSpec((tm, D_MODEL), lambda i: (jnp.maximum(i - n_prompt_tiles, 0), 0))],
        out_shape=[jax.ShapeDtypeStruct((t_p, D_MODEL), F32),
                   jax.ShapeDtypeStruct((t_all - t_p, D_MODEL), F32)],
        compiler_params=_params(("arbitrary",)),
        name="final",
    )(x1_all, *h2_all, *y_gath, wt_rows, mod_g, wsg, wsd, gf)


def _rope_tables(pos):
    freqs = ROPE_THETA ** (-jnp.arange(ROPE_HALF, dtype=F32) / ROPE_HALF)
    ang = pos.astype(F32)[:, None] * freqs
    return jnp.cos(ang), jnp.sin(ang)


def kernel(x_prompt, x_sample, cache_ckv, cache_krope, page_table, c_prompt, c_sample, w_ada, b_ada, norm1_g, norm2_g, w_in, a_vnorm_g, w_spatial, b_spatial, w_out_a, q_norm_g, w_uq, kv_norm_g, w_uk, w_uv, w_out_b, w_o, w_router, router_bias, w_exp_gu, w_exp_down, w_sh_gu, w_sh_down, final_norm_g):
    batch, seq, _ = x_prompt.shape
    dec_b, dec_s, _ = x_sample.shape
    n_pages = page_table.shape[1]
    n_past = n_pages * CHUNK
    t_p, t_s = batch * seq, dec_b * dec_s
    t_all = t_p + t_s
    tm = ROW_TILE
    assert seq % ATTN_Q_STEP == 0 and t_s % tm == 0 and CHUNK % dec_s == 0 and n_pages % PAGES_PER_STEP == 0
    assert (t_all * TOP_K) % RANK_TILE == 0 and w_ada.shape[0] == 1
    n_prompt_tiles = t_p // tm
    tiles_per_b = seq // tm

    def grp_of_tile(i):
        return jnp.minimum(i // tiles_per_b, batch) + jnp.maximum(i - n_prompt_tiles, 0)

    n_c = batch + dec_b
    c_rows = -(-n_c // 8) * 8
    c_all = jnp.concatenate([c_prompt, c_sample, jnp.zeros((c_rows - n_c, D_MODEL), F32)], axis=0)
    mod = _ada(c_all, w_ada[0], b_ada[0][None, :])
    mod_p = jnp.broadcast_to(mod[:batch, None, :], (batch, tm, 6 * D_MODEL))
    mod_s = jnp.repeat(mod[batch:n_c], dec_s, axis=0).reshape(t_s // tm, tm, 6 * D_MODEL)
    mod_g = jnp.concatenate([mod_p, mod_s], axis=0)

    o_u, o_v, o_cq, o_ckv, o_kr = A_WIDTH, 2 * A_WIDTH, 2 * A_WIDTH + Q_LORA, 2 * A_WIDTH + Q_LORA + KV_LORA, \
        2 * A_WIDTH + Q_LORA + KV_LORA + QK_ROPE
    win = w_in[0]
    w5 = jnp.concatenate([win[:, :o_kr], jnp.zeros((D_MODEL, IN_END - IN_KR - QK_ROPE), F32)], axis=1).astype(BF16)
    wg = win[:, o_kr:].astype(BF16)
    row1 = lambda v: v.reshape(1, -1)
    tri = jnp.tril(jnp.ones((CHUNK, CHUNK), F32))
    wc_p = w_spatial[0] * tri
    per = CHUNK // dec_s
    small = (w_spatial[0] * tri)[:, :dec_s, :dec_s]
    wc_s = jnp.einsum('ab,gts->gatbs', jnp.eye(per, dtype=F32), small).reshape(A_GROUPS, CHUNK, CHUNK)
    wc = jnp.stack([wc_p, wc_s]).astype(BF16)
    bias_p = jnp.repeat(b_spatial[0].T, A_GROUP_DIM, axis=1)
    bias_s = jnp.tile(bias_p[:dec_s], (per, 1))
    bc = jnp.stack([bias_p, bias_s])

    pos_p = jnp.arange(seq)
    pos_s = n_past + jnp.arange(dec_s)
    cos_p, sin_p = _rope_tables(pos_p)
    cos_s, sin_s = _rope_tables(pos_s)
    cos_rows = jnp.concatenate([jnp.tile(cos_p, (batch, 1)), jnp.tile(cos_s, (dec_b, 1))], axis=0)
    sin_rows = jnp.concatenate([jnp.tile(sin_p, (batch, 1)), jnp.tile(sin_s, (dec_b, 1))], axis=0)
    lane_pad = jnp.zeros((t_all, LANES - QK_ROPE), F32)
    cos_kr = jnp.concatenate([cos_rows, cos_rows, lane_pad], axis=1)
    sin_kr = jnp.concatenate([-sin_rows, sin_rows, lane_pad], axis=1)

    x_p, x_s = x_prompt.reshape(t_p, D_MODEL), x_sample.reshape(t_s, D_MODEL)
    g1, g2 = row1(norm1_g[0]), row1(norm2_g[0])
    a_all, cq_all, ckv_p, kr_p, ckv_s, kr_s, v_s = _in_proj(
        x_p, x_s, mod_g, g1, w5, row1(a_vnorm_g[0]), row1(q_norm_g[0]), row1(kv_norm_g[0]),
        cos_kr, sin_kr, wc, bc, grp_of_tile, n_prompt_tiles)

    wuq = w_uq[0]
    wuqt = jnp.pad(wuq, ((0, 0), (0, 0), (0, HEAD_PAD - QK_NOPE - QK_ROPE))).reshape(Q_LORA, -1).T.astype(BF16)
    wukp = jnp.pad(w_uk[0], ((0, 0), (0, 0), (0, HEAD_PAD - QK_NOPE))).reshape(KV_LORA, -1).astype(BF16)
    place_h = jnp.pad(jnp.eye(QK_ROPE, dtype=F32), ((0, 0), (QK_NOPE, HEAD_PAD - QK_NOPE - QK_ROPE)))
    place = jnp.tile(place_h, (1, N_HEADS)).astype(BF16)
    wuvt = w_uv[0].reshape(KV_LORA, -1).T.astype(BF16)
    qt, k, vt = _qkv(cq_all, ckv_p, kr_p, wuqt, wukp, place, wuvt, cos_p.T, sin_p.T, batch, seq)
    ot = _attn(qt, k, vt)

    wq_s = jnp.concatenate([wuq[:, :, :QK_NOPE].reshape(Q_LORA, -1),
                            wuq[:, :, QK_NOPE:QK_NOPE + ROPE_HALF].reshape(Q_LORA, -1),
                            wuq[:, :, QK_NOPE + ROPE_HALF:].reshape(Q_LORA, -1)], axis=1).astype(BF16)
    eye_h = jnp.eye(N_HEADS, dtype=F32)
    wuk_blk = jnp.einsum('rhd,hg->hdgr', w_uk[0], eye_h).reshape(N_HEADS * QK_NOPE, N_HEADS * KV_LORA).astype(BF16)
    wuv_blk = jnp.einsum('rhd,hg->hrgd', w_uv[0], eye_h).reshape(N_HEADS * KV_LORA, N_HEADS * V_HEAD).astype(BF16)
    cos_sq = jnp.tile(jnp.tile(cos_s, (1, N_HEADS)), (dec_b, 1))
    sin_sq = jnp.tile(jnp.tile(sin_s, (1, N_HEADS)), (dec_b, 1))
    ql, r1, r2 = _sample_q(cq_all[t_p:], wq_s, wuk_blk, cos_sq, sin_sq)
    rows = dec_s * N_HEADS
    ql3 = ql.reshape(dec_b, rows, KV_LORA)
    qr3 = jnp.concatenate([r1.reshape(dec_b, dec_s, N_HEADS, ROPE_HALF),
                           r2.reshape(dec_b, dec_s, N_HEADS, ROPE_HALF)], axis=-1).reshape(dec_b, rows, QK_ROPE)
    new_pad = ((0, 0), (0, 8 - dec_s), (0, 0))
    ckv_new = jnp.pad(ckv_s.reshape(dec_b, dec_s, KV_LORA), new_pad)
    kr_new_t = jnp.swapaxes(jnp.pad(kr_s.reshape(dec_b, dec_s, QK_ROPE), new_pad), 1, 2)
    cache_kr_t = jnp.swapaxes(cache_krope[0], 1, 2)
    o_lat = _sample_attn(page_table, cache_ckv[0], cache_kr_t, ql3, qr3, ckv_new, kr_new_t)
    o_s = _matmul(o_lat.reshape(t_s, N_HEADS * KV_LORA), wuv_blk, BF16)

    x1_all, *h2_all = _merge(x_p, x_s, mod_g, g1, g2, a_all, ot, o_s, wg,
                            w_out_a[0].astype(BF16), w_out_b[0].astype(BF16), w_o[0].astype(BF16),
                            grp_of_tile, n_prompt_tiles)

    wr_t = w_router[0].T
    whi = wr_t.astype(BF16)
    wlo = (wr_t - whi.astype(F32)).astype(BF16)
    idx_t, wt_t, rank_t, counts = _route(x1_all, mod_g, g2, whi, wlo, router_bias[0].reshape(-1, 1),
                                         grp_of_tile)
    n_assign = t_all * TOP_K
    e_flat = idx_t.reshape(1, n_assign)
    counts = counts[:, 0].astype(jnp.int32)
    blk = MOE_BLOCK
    padded = (counts + blk - 1) // blk * blk
    pad_end = jnp.cumsum(padded)
    pad_start = pad_end - padded
    dest = _dest(e_flat, rank_t.reshape(1, n_assign), pad_start.reshape(-1, 1))
    n_blocks = -(-n_assign // blk) + N_EXPERTS
    n_slots = n_blocks * blk
    first_row = jnp.arange(n_blocks, dtype=jnp.int32) * blk
    block_e = jnp.minimum(jnp.sum(pad_end[None, :] <= first_row[:, None], axis=1), N_EXPERTS - 1).astype(jnp.int32)
    n_valid = (pad_end[-1] // blk).astype(jnp.int32).reshape(1)
    experts = jnp.arange(N_EXPERTS, dtype=jnp.int32)
    owners = jnp.where(padded > 0, experts, N_EXPERTS)
    later_owner = lax.cummin(owners, reverse=True)
    next_expert = jnp.concatenate([later_owner[1:], jnp.full((1,), N_EXPERTS, jnp.int32)])
    x_sorted = _sc_scatter_rows(h2_all, dest, n_slots)
    y_sorted = _moe(block_e, n_valid, next_expert, x_sorted, w_exp_gu[0], w_exp_down[0])
    y_gath = [y.reshape(TOP_K, t_all, PIECE) for y in _sc_gather_rows(y_sorted, dest)]

    y_p, y_s = _final(x1_all, h2_all, y_gath, wt_t.T, mod_g, w_sh_gu[0].astype(BF16), w_sh_down[0].astype(BF16),
                      row1(final_norm_g), grp_of_tile, n_prompt_tiles)

    y_prompt = y_p.reshape(batch, seq, D_MODEL)
    y_sample = y_s.reshape(dec_b, dec_s, D_MODEL)
    new_ckv_prompt = ckv_p.reshape(1, batch, seq, KV_LORA)
    new_krope_prompt = kr_p.reshape(1, batch, seq, QK_ROPE)
    new_ckv_sample = ckv_s.reshape(1, dec_b, dec_s, KV_LORA)
    new_krope_sample = kr_s.reshape(1, dec_b, dec_s, QK_ROPE)
    new_chunk_v_sample = v_s.reshape(1, dec_b, dec_s, A_WIDTH)
    return (y_prompt, y_sample, new_ckv_prompt, new_krope_prompt, new_ckv_sample, new_krope_sample,
            new_chunk_v_sample)
```
